```python
import math
import jax, jax.numpy as jnp
from jax import lax
import numpy as np

D_MODEL = 1024
BATCH = 8
SEQ = 2048
DEPTH = 2
DEC_BATCH = 128
DEC_SEQ = 1
PAST_LEN = 16384
PAGE_SIZE = 128

D_MIX = D_MODEL
BRANCH_W = D_MIX // 4
RET_HEADS = 4
RET_HEAD_DIM = BRANCH_W // RET_HEADS
HGRN_HEADS = 4
HGRN_HEAD_DIM = BRANCH_W // HGRN_HEADS
S5_CH_PER_GROUP = 16
S5_GROUPS = BRANCH_W // S5_CH_PER_GROUP
S5_STATE = 64
M2_HEAD_DIM = 64
M2_HEADS = BRANCH_W // M2_HEAD_DIM
M2_GROUPS = 2
M2_STATE = 64
M2_CONV = 4
M2_CONV_CH = BRANCH_W + 2 * M2_GROUPS * M2_STATE
CHUNK = 64
ROPE_BASE = 10000.0
EPS = 1e-6
EXP_CLIP = 60.0
PROJ_SIZES = (BRANCH_W, BRANCH_W, BRANCH_W, BRANCH_W,
              BRANCH_W, BRANCH_W, BRANCH_W, BRANCH_W,
              BRANCH_W, BRANCH_W,
              BRANCH_W, M2_CONV_CH, M2_HEADS)
PROJ_TOTAL = sum(PROJ_SIZES)

kernel_name = "hybrid_ret_hgrn2_s5_ssd_step"


def rms_norm(x, w):
    xf = x.astype(jnp.float32)
    y = xf * lax.rsqrt(jnp.mean(xf * xf, axis=-1, keepdims=True) + EPS)
    return (y * w.astype(jnp.float32)).astype(x.dtype)


def head_rms(o, w):
    o = o * lax.rsqrt(jnp.mean(o * o, axis=-1, keepdims=True) + EPS)
    return o.reshape(o.shape[:2] + (-1,)) * w.astype(jnp.float32)


def rotary(x, pos):
    half = x.shape[-1] // 2
    inv = 1.0 / (ROPE_BASE ** (jnp.arange(half, dtype=jnp.float32) / half))
    ang = pos[:, None] * inv[None, :]
    cos = jnp.cos(ang)[None, :, None, :]
    sin = jnp.sin(ang)[None, :, None, :]
    x1, x2 = x[..., :half], x[..., half:]
    return jnp.concatenate([x1 * cos - x2 * sin, x1 * sin + x2 * cos], axis=-1)


def chunked_linear_recurrence(q, k, v, log_a, h0, chunk):
    bsz, L, H, K = q.shape
    V = v.shape[-1]
    C = min(chunk, L)
    n = -(-L // C)
    pad = n * C - L
    per_channel = log_a.ndim == 4

    def prep(t):
        t = jnp.pad(t, [(0, 0), (0, pad)] + [(0, 0)] * (t.ndim - 2))
        return jnp.moveaxis(t.reshape((bsz, n, C) + t.shape[2:]), 1, 0)

    causal = jnp.tril(jnp.ones((C, C), dtype=bool))

    def masked_exp(diff, mask):
        return jnp.where(mask, jnp.exp(jnp.where(mask, diff, 0.0)), 0.0)

    def step(h, blk):
        qc, kc, vc, la = blk
        cum = jnp.cumsum(la, axis=1)
        total = cum[:, -1]
        diff = cum[:, :, None] - cum[:, None, :]
        if per_channel:
            o_inter = jnp.einsum('bchk,bhkv->bchv', qc * jnp.exp(cum), h)
            decay = masked_exp(diff, causal[None, :, :, None, None])
            scores = jnp.einsum('bthk,bshk,btshk->bhts', qc, kc, decay)
            k_end = kc * jnp.exp(total[:, None] - cum)
            h_new = jnp.exp(total)[..., None] * h + jnp.einsum('bshk,bshv->bhkv', k_end, vc)
        else:
            o_inter = jnp.einsum('bchk,bhkv->bchv', qc, h) * jnp.exp(cum)[..., None]
            decay = masked_exp(diff, causal[None, :, :, None])
            scores = jnp.einsum('bthk,bshk->bhts', qc, kc) * jnp.moveaxis(decay, 3, 1)
            k_end = kc * jnp.exp(total[:, None] - cum)[..., None]
            h_new = jnp.exp(total)[..., None, None] * h + jnp.einsum('bshk,bshv->bhkv', k_end, vc)
        o = o_inter + jnp.einsum('bhts,bshv->bthv', scores, vc)
        return h_new, o

    h_last, o = lax.scan(step, h0, (prep(q), prep(k), prep(v), prep(log_a)))
    o = jnp.moveaxis(o, 0, 1).reshape(bsz, n * C, H, V)[:, :L]
    return o, h_last


def s5_scan(u, h0_re, h0_im, A_re, A_im, log_dt, B_re, B_im, C_re, C_im):
    f32 = jnp.float32
    A_re, A_im = A_re.astype(f32), A_im.astype(f32)
    dt = jnp.exp(log_dt.astype(f32))[:, None]
    mag = jnp.exp(A_re * dt)
    ab_re, ab_im = mag * jnp.cos(A_im * dt), mag * jnp.sin(A_im * dt)
    nr, ni = ab_re - 1.0, ab_im
    den = A_re * A_re + A_im * A_im
    f_re = (nr * A_re + ni * A_im) / den
    f_im = (ni * A_re - nr * A_im) / den
    B_re, B_im = B_re.astype(f32), B_im.astype(f32)
    bb_re = f_re[..., None] * B_re - f_im[..., None] * B_im
    bb_im = f_re[..., None] * B_im + f_im[..., None] * B_re
    bu_re = jnp.einsum('blgc,gpc->blgp', u, bb_re)
    bu_im = jnp.einsum('blgc,gpc->blgp', u, bb_im)
    bu_re = bu_re.at[:, 0].add(ab_re * h0_re - ab_im * h0_im)
    bu_im = bu_im.at[:, 0].add(ab_re * h0_im + ab_im * h0_re)
    a_re = jnp.broadcast_to(ab_re, bu_re.shape)
    a_im = jnp.broadcast_to(ab_im, bu_im.shape)

    def combine(e1, e2):
        a1r, a1i, b1r, b1i = e1
        a2r, a2i, b2r, b2i = e2
        return (a2r * a1r - a2i * a1i, a2r * a1i + a2i * a1r,
                a2r * b1r - a2i * b1i + b2r, a2r * b1i + a2i * b1r + b2i)

    _, _, h_re, h_im = lax.associative_scan(combine, (a_re, a_im, bu_re, bu_im), axis=1)
    y = (jnp.einsum('blgp,gcp->blgc', h_re, C_re.astype(f32))
         - jnp.einsum('blgp,gcp->blgc', h_im, C_im.astype(f32)))
    return y, h_re[:, -1], h_im[:, -1]


def causal_conv(xbc, buf, w, b):
    L = xbc.shape[1]
    full = jnp.concatenate([buf, xbc], axis=1)
    out = b.astype(jnp.float32) + sum(full[:, i:i + L] * w[i].astype(jnp.float32)
                                      for i in range(M2_CONV))
    return jax.nn.silu(out), full[:, -(M2_CONV - 1):]


def hybrid_layer(x, pos0, ret_h0, hgrn_h0, s5_h0_re, s5_h0_im, m2_h0, m2_buf0, lb,
                 norm_w, w_in, ret_norm_w, hgrn_norm_w,
                 s5_A_re, s5_A_im, s5_log_dt, s5_B_re, s5_B_im, s5_C_re, s5_C_im,
                 s5_D, s5_glu_w, s5_glu_b,
                 m2_conv_w, m2_conv_b, m2_dt_bias, m2_A_log, m2_D, m2_norm_w, w_out):
    f32 = jnp.float32
    bsz, L, _ = x.shape
    hn = rms_norm(x, norm_w)
    proj = (hn @ w_in).astype(f32)
    points = np.cumsum(PROJ_SIZES)[:-1].tolist()
    (r_q, r_k, r_v, r_g, g_q, g_f, g_i, g_g, s_u, s_g, m_z, m_xbc, m_dt) = jnp.split(proj, points, axis=-1)

    pos = pos0 + jnp.arange(L, dtype=f32)
    rq = rotary(r_q.reshape(bsz, L, RET_HEADS, RET_HEAD_DIM), pos)
    rk = rotary(r_k.reshape(bsz, L, RET_HEADS, RET_HEAD_DIM), pos) * (RET_HEAD_DIM ** -0.5)
    rv = r_v.reshape(bsz, L, RET_HEADS, RET_HEAD_DIM)
    log_gamma = jnp.log1p(-(2.0 ** (-5.0 - jnp.arange(RET_HEADS, dtype=f32))))
    r_la = jnp.broadcast_to(log_gamma, (bsz, L, RET_HEADS))
    r_o, ret_h = chunked_linear_recurrence(rq, rk, rv, r_la, ret_h0.astype(f32), CHUNK)
    o_ret = head_rms(r_o, ret_norm_w) * jax.nn.silu(r_g)

    hq = jax.nn.silu(g_q).reshape(bsz, L, HGRN_HEADS, HGRN_HEAD_DIM)
    fr = g_f.reshape(bsz, L, HGRN_HEADS, HGRN_HEAD_DIM)
    lb_h = lb.reshape(HGRN_HEADS, HGRN_HEAD_DIM)
    log_f = jax.nn.log_sigmoid(fr) + jnp.log1p(lb_h * jnp.exp(jnp.minimum(-fr, EXP_CLIP)))
    hk = (1.0 - lb_h) * jax.nn.sigmoid(-fr)
    hv = g_i.reshape(bsz, L, HGRN_HEADS, HGRN_HEAD_DIM)
    h_o, hgrn_h = chunked_linear_recurrence(hq, hk, hv, log_f, hgrn_h0.astype(f32), CHUNK)
    o_hgrn = head_rms(h_o, hgrn_norm_w) * jax.nn.silu(g_g)

    u = s_u.reshape(bsz, L, S5_GROUPS, S5_CH_PER_GROUP)
    sy, s5_re, s5_im = s5_scan(u, s5_h0_re.astype(f32), s5_h0_im.astype(f32), s5_A_re, s5_A_im,
                               s5_log_dt, s5_B_re, s5_B_im, s5_C_re, s5_C_im)
    sy = sy + s5_D.astype(f32).reshape(S5_GROUPS, S5_CH_PER_GROUP) * u
    gy = jax.nn.gelu(sy.reshape(bsz, L, BRANCH_W))
    o_s5 = gy * jax.nn.sigmoid(gy @ s5_glu_w.astype(f32) + s5_glu_b.astype(f32)) * jax.nn.silu(s_g)

    xbc, m2_buf = causal_conv(m_xbc, m2_buf0.astype(f32), m2_conv_w, m2_conv_b)
    gn = M2_GROUPS * M2_STATE
    xm = xbc[..., :BRANCH_W].reshape(bsz, L, M2_HEADS, M2_HEAD_DIM)
    Bm = jnp.repeat(xbc[..., BRANCH_W:BRANCH_W + gn].reshape(bsz, L, M2_GROUPS, M2_STATE),
                    M2_HEADS // M2_GROUPS, axis=2)
    Cm = jnp.repeat(xbc[..., BRANCH_W + gn:].reshape(bsz, L, M2_GROUPS, M2_STATE),
                    M2_HEADS // M2_GROUPS, axis=2)
    dt = jax.nn.softplus(m_dt + m2_dt_bias.astype(f32))
    A = -jnp.exp(m2_A_log.astype(f32))
    m_o, m2_h = chunked_linear_recurrence(Cm, Bm, xm * dt[..., None], dt * A,
                                          m2_h0.astype(f32), CHUNK)
    my = (m_o + m2_D.astype(f32)[:, None] * xm).reshape(bsz, L, BRANCH_W) * jax.nn.silu(m_z)
    o_m2 = my * lax.rsqrt(jnp.mean(my * my, axis=-1, keepdims=True) + EPS) * m2_norm_w.astype(f32)

    mixed = jnp.concatenate([o_ret, o_hgrn, o_s5, o_m2], axis=-1).astype(x.dtype)
    x_out = x + mixed @ w_out
    return x_out, (ret_h, hgrn_h, s5_re, s5_im, m2_h, m2_buf)


def setup_inputs(seed: int = 0) -> dict:
    key = jax.random.key(seed)
    ks = iter(jax.random.split(key, 48))
    f32 = jnp.float32

    def nrm(shape, s):
        return s * jax.random.normal(next(ks), shape, f32)

    def unif(shape, lo, hi):
        return jax.random.uniform(next(ks), shape, f32, lo, hi)

    inp = {}
    inp["x_prompt"] = nrm((BATCH, SEQ, D_MODEL), 1.0)
    inp["x_sample"] = nrm((DEC_BATCH, DEC_SEQ, D_MODEL), 1.0)
    inp["state_ret"] = nrm((DEPTH, DEC_BATCH, RET_HEADS, RET_HEAD_DIM, RET_HEAD_DIM), 0.3)
    inp["state_hgrn"] = nrm((DEPTH, DEC_BATCH, HGRN_HEADS, HGRN_HEAD_DIM, HGRN_HEAD_DIM), 0.5)
    inp["state_s5_re"] = nrm((DEPTH, DEC_BATCH, S5_GROUPS, S5_STATE), 0.1)
    inp["state_s5_im"] = nrm((DEPTH, DEC_BATCH, S5_GROUPS, S5_STATE), 0.1)
    inp["state_m2_ssm"] = nrm((DEPTH, DEC_BATCH, M2_HEADS, M2_STATE, M2_HEAD_DIM), 0.3)
    inp["state_m2_conv"] = nrm((DEPTH, DEC_BATCH, M2_CONV - 1, M2_CONV_CH), 1.0)
    inp["norm_w"] = 1.0 + nrm((DEPTH, D_MODEL), 0.01)
    inp["w_in"] = nrm((DEPTH, D_MODEL, PROJ_TOTAL), D_MODEL ** -0.5)
    inp["ret_norm_w"] = 1.0 + nrm((DEPTH, BRANCH_W), 0.01)
    inp["hgrn_lb_logits"] = 1.0 + nrm((DEPTH, BRANCH_W), 0.1)
    inp["hgrn_norm_w"] = 1.0 + nrm((DEPTH, BRANCH_W), 0.01)
    inp["s5_A_re"] = -0.5 + nrm((DEPTH, S5_GROUPS, S5_STATE), 0.01)
    inp["s5_A_im"] = jnp.pi * jnp.arange(S5_STATE, dtype=f32) + nrm((DEPTH, S5_GROUPS, S5_STATE), 0.01)
    inp["s5_log_dt"] = unif((DEPTH, S5_GROUPS), math.log(1e-3), math.log(1e-1))
    inp["s5_B_re"] = nrm((DEPTH, S5_GROUPS, S5_STATE, S5_CH_PER_GROUP), (2 * S5_CH_PER_GROUP) ** -0.5)
    inp["s5_B_im"] = nrm((DEPTH, S5_GROUPS, S5_STATE, S5_CH_PER_GROUP), (2 * S5_CH_PER_GROUP) ** -0.5)
    inp["s5_C_re"] = nrm((DEPTH, S5_GROUPS, S5_CH_PER_GROUP, S5_STATE), (2 * S5_STATE) ** -0.5)
    inp["s5_C_im"] = nrm((DEPTH, S5_GROUPS, S5_CH_PER_GROUP, S5_STATE), (2 * S5_STATE) ** -0.5)
    inp["s5_D"] = nrm((DEPTH, BRANCH_W), 1.0)
    inp["s5_glu_w"] = nrm((DEPTH, BRANCH_W, BRANCH_W), BRANCH_W ** -0.5)
    inp["s5_glu_b"] = nrm((DEPTH, BRANCH_W), 0.01)
    inp["m2_conv_w"] = nrm((DEPTH, M2_CONV, M2_CONV_CH), M2_CONV ** -0.5)
    inp["m2_conv_b"] = nrm((DEPTH, M2_CONV_CH), 0.01)
    dt0 = jnp.exp(unif((DEPTH, M2_HEADS), math.log(1e-3), math.log(1e-1)))
    inp["m2_dt_bias"] = dt0 + jnp.log(-jnp.expm1(-dt0))
    inp["m2_A_log"] = jnp.log(unif((DEPTH, M2_HEADS), 1.0, 16.0))
    inp["m2_D"] = 1.0 + nrm((DEPTH, M2_HEADS), 0.1)
    inp["m2_norm_w"] = 1.0 + nrm((DEPTH, BRANCH_W), 0.01)
    inp["w_out"] = nrm((DEPTH, D_MIX, D_MODEL), 0.5 * D_MIX ** -0.5)
    inp["final_norm_w"] = 1.0 + nrm((D_MODEL,), 0.01)
    return inp


def reference(x_prompt, x_sample, state_ret, state_hgrn, state_s5_re, state_s5_im,
              state_m2_ssm, state_m2_conv, norm_w, w_in, ret_norm_w, hgrn_lb_logits,
              hgrn_norm_w, s5_A_re, s5_A_im, s5_log_dt, s5_B_re, s5_B_im, s5_C_re,
              s5_C_im, s5_D, s5_glu_w, s5_glu_b, m2_conv_w, m2_conv_b, m2_dt_bias,
              m2_A_log, m2_D, m2_norm_w, w_out, final_norm_w):
    f32 = jnp.float32
    lb_sm = jax.nn.softmax(hgrn_lb_logits.astype(f32), axis=0)
    lb_all = jnp.clip(jnp.cumsum(lb_sm, axis=0) - lb_sm[0], 0.0, 1.0)

    def zero_states(b):
        return (jnp.zeros((b, RET_HEADS, RET_HEAD_DIM, RET_HEAD_DIM), f32),
                jnp.zeros((b, HGRN_HEADS, HGRN_HEAD_DIM, HGRN_HEAD_DIM), f32),
                jnp.zeros((b, S5_GROUPS, S5_STATE), f32),
                jnp.zeros((b, S5_GROUPS, S5_STATE), f32),
                jnp.zeros((b, M2_HEADS, M2_STATE, M2_HEAD_DIM), f32),
                jnp.zeros((b, M2_CONV - 1, M2_CONV_CH), f32))

    xp, xs = x_prompt, x_sample
    new_p, new_s = [], []
    for l in range(DEPTH):
        lw = (norm_w[l], w_in[l], ret_norm_w[l], hgrn_norm_w[l],
              s5_A_re[l], s5_A_im[l], s5_log_dt[l], s5_B_re[l], s5_B_im[l], s5_C_re[l],
              s5_C_im[l], s5_D[l], s5_glu_w[l], s5_glu_b[l],
              m2_conv_w[l], m2_conv_b[l], m2_dt_bias[l], m2_A_log[l], m2_D[l], m2_norm_w[l],
              w_out[l])
        xp, sp = hybrid_layer(xp, 0, *zero_states(xp.shape[0]), lb_all[l], *lw)
        xs, ss = hybrid_layer(xs, PAST_LEN, state_ret[l], state_hgrn[l], state_s5_re[l],
                              state_s5_im[l], state_m2_ssm[l], state_m2_conv[l], lb_all[l], *lw)
        new_p.append(sp)
        new_s.append(ss)

    y_prompt = rms_norm(xp, final_norm_w)
    y_sample = rms_norm(xs, final_norm_w)
    stk = lambda lst, i: jnp.stack([s[i] for s in lst], axis=0)
    return (y_prompt, y_sample,
            stk(new_p, 0), stk(new_p, 1), stk(new_p, 2), stk(new_p, 3), stk(new_p, 4), stk(new_p, 5),
            stk(new_s, 0), stk(new_s, 1), stk(new_s, 2), stk(new_s, 3), stk(new_s, 4), stk(new_s, 5))
```

```python
import functools
import math

import numpy as np
import jax
import jax.numpy as jnp
from jax import lax
from jax.experimental import pallas as pl
from jax.experimental.pallas import tpu as pltpu

F32 = jnp.float32
BF16 = jnp.bfloat16
HI = lax.Precision.HIGHEST

D = 1024
W = 256
NH = 4
HD = 64
S5G = 16
S5C = 16
S5P = 64
S5N = S5G * S5P
NLT = 2 * S5N // 128
CONV_CH = 512
CONV_K = 4
TB = 64
SUB = 16
NSUB = TB // SUB
EPS = 1e-6
EXP_CLIP = 60.0
ROPE_BASE = 10000.0
PAST_LEN = 16384

C_RQ, C_RK, C_RV, C_RG = 0, 256, 512, 768
C_GQ, C_GF, C_GI, C_GG = 1024, 1280, 1536, 1792
C_SU, C_SG = 2048, 2304
C_MZ, C_XBC, C_DT = 2560, 2816, 3328
P_TOTAL = 3332
P_PAD = 3456
LANE = 128
VMEM_LIMIT = 56 * 1024 * 1024


def _silu(x):
    return x * jax.nn.sigmoid(x)


def _softplus(x):
    return jnp.maximum(x, 0.0) + jnp.log1p(jnp.exp(-jnp.abs(x)))


def _log_sigmoid(x):
    return jnp.minimum(x, 0.0) - jnp.log1p(jnp.exp(-jnp.abs(x)))


def _gelu_tanh(x):
    c = math.sqrt(2.0 / math.pi)
    return 0.5 * x * (1.0 + jnp.tanh(c * (x + 0.044715 * (x * x * x))))


def _dot(a, b):
    return jnp.dot(a.astype(BF16), b.astype(BF16), preferred_element_type=F32)


def _dot_nt(a, b):
    return lax.dot_general(a.astype(BF16), b.astype(BF16), (((1,), (1,)), ((), ())),
                           preferred_element_type=F32)


def _dot_tn(a, b):
    return lax.dot_general(a.astype(BF16), b.astype(BF16), (((0,), (0,)), ((), ())),
                           preferred_element_type=F32)


def _dot_hi(a, b):
    return jnp.dot(a, b, precision=HI, preferred_element_type=F32)


def _dot_tn_hi(a, b):
    return lax.dot_general(a, b, (((0,), (0,)), ((), ())), precision=HI,
                           preferred_element_type=F32)


def _rot_half_partner(x):
    lane = lax.broadcasted_iota(jnp.int32, x.shape, 1)
    first = (lane % HD) < (HD // 2)
    return jnp.where(first, pltpu.roll(x, LANE - HD // 2, 1), pltpu.roll(x, HD // 2, 1))


def _rotary(x, cos, sin_signed):
    parts = []
    for i in range(W // LANE):
        xi = x[:, i * LANE:(i + 1) * LANE]
        parts.append(xi * cos + _rot_half_partner(xi) * sin_signed)
    return jnp.concatenate(parts, axis=1)


def _head_rms(o):
    return o * lax.rsqrt(jnp.sum(o * o, axis=-1, keepdims=True) * (1.0 / o.shape[-1]) + EPS)


def _prompt_layer_body(last, *refs):
    (x_ref, cos_ref, sin_ref, normw_ref, win_ref, wout_ref,
     rdec_ref, recum_ref, rkdec_ref, retot_ref, retnw_ref,
     lb_ref, hnw_ref, sel_ref,
     sA_ref, sB_ref, sC_ref, sD_ref, gluw_ref, glub_ref,
     cw_ref, cb_ref, dtb_ref, alog_ref, md_ref, mnw_ref) = refs[:26]
    refs = refs[26:]
    if last:
        fnw_ref, xo_ref, y_ref = refs[:3]
        refs = refs[3:]
    else:
        xo_ref = refs[0]
        refs = refs[1:]
    (ret_ref, hg_ref, s5_ref, m2_ref, m2buf_ref,
     hn_s, proj_s, mixed_s, bu_s, cv_s, p_s, dg_s, hq_s, hk_s, cum_s) = refs
    j = pl.program_id(0)
    nb = x_ref.shape[0]

    @pl.when(j == 0)
    def _init():
        ret_ref[...] = jnp.zeros(ret_ref.shape, F32)
        hg_ref[...] = jnp.zeros(hg_ref.shape, F32)
        s5_ref[...] = jnp.zeros(s5_ref.shape, F32)
        m2_ref[...] = jnp.zeros(m2_ref.shape, F32)
        cv_s[...] = jnp.zeros(cv_s.shape, F32)

    ti = lax.broadcasted_iota(jnp.int32, (TB, TB), 0)
    si = lax.broadcasted_iota(jnp.int32, (TB, TB), 1)
    causal = si <= ti
    tri_l = causal.astype(F32)
    tri_u = (ti <= si).astype(F32)
    blockdiag = (ti // SUB) == (si // SUB)
    ones_tv = jnp.ones((TB, HD), F32)

    def norm_body(b, c):
        xb = x_ref[b]
        hn = xb * lax.rsqrt(jnp.mean(xb * xb, axis=-1, keepdims=True) + EPS) * normw_ref[...]
        hn_s[pl.ds(pl.multiple_of(b * TB, TB), TB), :] = hn.astype(BF16)
        return c
    lax.fori_loop(0, nb, norm_body, 0)
    NCH = 3
    cw = P_PAD // NCH
    for c in range(NCH):
        proj_s[:, c * cw:(c + 1) * cw] = jnp.dot(hn_s[...], win_ref[:, c * cw:(c + 1) * cw],
                                                 preferred_element_type=F32)

    cos = cos_ref[...]
    sin = sin_ref[...]

    def phase1(b, c):
        r0 = pl.multiple_of(b * TB, TB)

        def pj(c0, w):
            return proj_s[pl.ds(r0, TB), c0:c0 + w]

        rq = _rotary(pj(C_RQ, W), cos, sin)
        rk = _rotary(pj(C_RK, W), cos, sin) * (HD ** -0.5)
        rv = pj(C_RV, W)
        rg = pj(C_RG, W)
        for h in range(NH):
            sl = slice(h * HD, (h + 1) * HD)
            qh, kh, vh = rq[:, sl], rk[:, sl], rv[:, sl]
            hst = ret_ref[b, h]
            s = _dot_nt(qh, kh) * rdec_ref[h]
            o = _dot(s, vh) + _dot(qh, hst) * recum_ref[h]
            ret_ref[b, h] = retot_ref[h] * hst + _dot_tn(kh * rkdec_ref[h], vh)
            o = _head_rms(o) * retnw_ref[:, sl] * _silu(rg[:, sl])
            mixed_s[pl.ds(r0, TB), 0 * W + h * HD:0 * W + (h + 1) * HD] = o.astype(BF16)

        cv_s[b, 8:8 + TB, :] = pj(C_XBC, CONV_CH)
        acc = cb_ref[...] + cv_s[b, 5:5 + TB, :] * cw_ref[0:1, :]
        for i in range(1, CONV_K):
            acc = acc + cv_s[b, 5 + i:5 + i + TB, :] * cw_ref[i:i + 1, :]
        tail = cv_s[b, TB + 5:TB + 8, :]
        cv_s[b, 5:8, :] = tail
        m2buf_ref[b] = tail
        xbc = _silu(acc)
        xm = xbc[:, 0:W]
        bm = xbc[:, W:W + 2 * HD]
        cm = xbc[:, W + 2 * HD:W + 4 * HD]
        dt = _softplus(pj(C_DT, LANE) + dtb_ref[...])
        la = dt * (-jnp.exp(alog_ref[...]))
        cumc = _dot_hi(tri_l, la)
        cumr = _dot_tn_hi(la, tri_u)
        totr = cumc[TB - 1:TB, :]
        mz = pj(C_MZ, W)
        ys = []
        for h in range(NH):
            g = h // 2
            sl = slice(h * HD, (h + 1) * HD)
            gs = slice(g * HD, (g + 1) * HD)
            ch, bh, xh = cm[:, gs], bm[:, gs], xm[:, sl]
            xdt = xh * dt[:, h:h + 1]
            cc = cumc[:, h:h + 1]
            decay = jnp.where(causal, jnp.exp(jnp.minimum(cc - cumr[h:h + 1, :], 0.0)), 0.0)
            hst = m2_ref[b, h]
            s = _dot_nt(ch, bh) * decay
            o = _dot(s, xdt) + _dot(ch, hst) * jnp.exp(cc)
            tot = totr[:, h:h + 1]
            kend = bh * jnp.exp(tot - cc)
            m2_ref[b, h] = jnp.exp(tot) * hst + _dot_tn(kend, xdt)
            ys.append(o + md_ref[:, h:h + 1] * xh)
        my = jnp.concatenate(ys, axis=1) * _silu(mz)
        om = my * lax.rsqrt(jnp.mean(my * my, axis=-1, keepdims=True) + EPS) * mnw_ref[...]
        mixed_s[pl.ds(r0, TB), 3 * W:4 * W] = om.astype(BF16)

        bu = _dot(pj(C_SU, W), sB_ref[...])
        for c in range(NLT):
            bu_s[c, pl.ds(b, TB, stride=nb), :] = bu[:, c * LANE:(c + 1) * LANE]

        fr = pj(C_GF, W)
        lb = lb_ref[...]
        logf = _log_sigmoid(fr) + jnp.log1p(lb * jnp.exp(jnp.minimum(-fr, EXP_CLIP)))
        hq = _silu(pj(C_GQ, W))
        hk = (1.0 - lb) * jax.nn.sigmoid(-fr)
        cum = _dot_hi(tri_l, logf)
        hq_s[b] = hq
        hk_s[b] = hk
        cum_s[b] = cum
        tl = lax.broadcasted_iota(jnp.int32, (TB, LANE), 0) % SUB
        for p in range(W // LANE):
            ls = slice(p * LANE, (p + 1) * LANE)
            hq_p = hq[:, ls]
            cum_p = cum[:, ls]
            for s_ in range(SUB):
                kb = jnp.concatenate(
                    [jnp.broadcast_to(hk_s[b, i * SUB + s_:i * SUB + s_ + 1, ls], (SUB, LANE))
                     for i in range(NSUB)], axis=0)
                cb = jnp.concatenate(
                    [jnp.broadcast_to(cum_s[b, i * SUB + s_:i * SUB + s_ + 1, ls], (SUB, LANE))
                     for i in range(NSUB)], axis=0)
                pv = jnp.where(tl >= s_, hq_p * kb * jnp.exp(jnp.minimum(cum_p - cb, 0.0)), 0.0)
                p_s[p, pl.ds(r0, TB), s_ * LANE:(s_ + 1) * LANE] = pv.astype(BF16)
        return c
    lax.fori_loop(0, nb, phase1, 0)

    for p in range(W // LANE):
        dg_s[p] = jnp.dot(p_s[p], sel_ref[...], preferred_element_type=F32)

    ar = jnp.broadcast_to(sA_ref[0:1, :], (nb, S5N))
    ai = jnp.broadcast_to(sA_ref[1:2, :], (nb, S5N))

    def scan_body(t, carry):
        hr, hi = carry
        row = pl.multiple_of(t * nb, nb)
        half = NLT // 2
        br = jnp.concatenate([bu_s[c, pl.ds(row, nb), :] for c in range(half)], axis=1)
        bi = jnp.concatenate([bu_s[half + c, pl.ds(row, nb), :] for c in range(half)], axis=1)
        nr = ar * hr - ai * hi + br
        ni = ar * hi + ai * hr + bi
        for c in range(half):
            bu_s[c, pl.ds(row, nb), :] = nr[:, c * LANE:(c + 1) * LANE]
            bu_s[half + c, pl.ds(row, nb), :] = ni[:, c * LANE:(c + 1) * LANE]
        return nr, ni
    hr, hi = lax.fori_loop(0, TB, scan_body, (s5_ref[:, 0:S5N], s5_ref[:, S5N:2 * S5N]))
    s5_ref[:, 0:S5N] = hr
    s5_ref[:, S5N:2 * S5N] = hi

    def phase2(b, c):
        r0 = pl.multiple_of(b * TB, TB)

        def pj(c0, w):
            return proj_s[pl.ds(r0, TB), c0:c0 + w]

        hs = jnp.concatenate([bu_s[c, pl.ds(b, TB, stride=nb), :] for c in range(NLT)], axis=1)
        u = pj(C_SU, W)
        sy = _dot(hs, sC_ref[...]) + sD_ref[...] * u
        gy = _gelu_tanh(sy)
        os5 = gy * jax.nn.sigmoid(_dot(gy, gluw_ref[...]) + glub_ref[...]) * _silu(pj(C_SG, W))
        mixed_s[pl.ds(r0, TB), 2 * W:3 * W] = os5.astype(BF16)

        cum = cum_s[b]
        rr = jnp.concatenate(
            [jnp.zeros((SUB, W), F32)]
            + [jnp.broadcast_to(cum_s[b, i * SUB - 1:i * SUB, :], (SUB, W))
               for i in range(1, NSUB)], axis=0)
        ee = jnp.concatenate(
            [jnp.broadcast_to(cum_s[b, i * SUB + SUB - 1:i * SUB + SUB, :], (SUB, W))
             for i in range(NSUB)], axis=0)
        lastrow = lax.broadcasted_iota(jnp.int32, (TB, W), 0) == TB - 1
        totc = _dot_tn_hi(jnp.where(lastrow, cum, 0.0), ones_tv)
        hq = hq_s[b]
        hk = hk_s[b]
        hv = pj(C_GI, W)
        gg = pj(C_GG, W)
        qt = hq * jnp.exp(cum - rr)
        kh_ = hk * jnp.exp(ee - cum)
        qe = hq * jnp.exp(cum)
        tot = cum[TB - 1:TB, :]
        kend = kh_ * jnp.exp(tot - ee)
        trow = lax.broadcasted_iota(jnp.int32, (TB, W), 0) // SUB
        qx, kx = [], []
        for jb in range(NSUB - 1):
            eb = jnp.broadcast_to(ee[jb * SUB:jb * SUB + 1, :], (TB, W))
            qx.append(jnp.where(trow > jb, qt * jnp.exp(jnp.minimum(rr - eb, 0.0)), 0.0))
            kx.append(jnp.where(trow == jb, kh_, 0.0))
        for h in range(NH):
            sl = slice(h * HD, (h + 1) * HD)
            p, h2 = divmod(h, 2)
            dgt = dg_s[p, pl.ds(r0, TB), h2 * HD:(h2 + 1) * HD]
            s = jnp.where(blockdiag, dgt, 0.0)
            for jb in range(NSUB - 1):
                s = s + _dot_nt(qx[jb][:, sl], kx[jb][:, sl])
            hst = hg_ref[b, h]
            o = _dot(s, hv[:, sl]) + _dot(qe[:, sl], hst)
            hg_ref[b, h] = jnp.exp(totc[sl, :]) * hst + _dot_tn(kend[:, sl], hv[:, sl])
            o = _head_rms(o) * hnw_ref[:, sl] * _silu(gg[:, sl])
            mixed_s[pl.ds(r0, TB), 1 * W + h * HD:1 * W + (h + 1) * HD] = o.astype(BF16)
        return c
    lax.fori_loop(0, nb, phase2, 0)

    res = jnp.dot(mixed_s[...], wout_ref[...], preferred_element_type=F32)
    for b in range(nb):
        xo = x_ref[b] + res[b * TB:(b + 1) * TB]
        xo_ref[b] = xo
        if last:
            y_ref[b] = xo * lax.rsqrt(jnp.mean(xo * xo, axis=-1, keepdims=True) + EPS) * fnw_ref[...]


def _const_spec(shape):
    nd = len(shape)
    return pl.BlockSpec(shape, lambda j: (0,) * nd)


def _prompt_layer(x, last, tabs, lw):
    nb, L, _ = x.shape
    nblk = L // TB
    rows = nb * TB
    xspec = pl.BlockSpec((nb, TB, D), lambda j: (0, j, 0))
    tspec = pl.BlockSpec((TB, LANE), lambda j: (j, 0))
    small = [tabs["rdec"], tabs["recum"], tabs["rkdec"], tabs["retot"], lw["ret_norm_w"],
             lw["lb"], lw["hgrn_norm_w"], tabs["sel"],
             lw["s5_A"], lw["s5_Bblk"], lw["s5_Cblk"], lw["s5_D"], lw["s5_glu_w"], lw["s5_glu_b"],
             lw["m2_conv_w"], lw["m2_conv_b"], lw["m2_dt_bias"], lw["m2_A_log"], lw["m2_D"],
             lw["m2_norm_w"]]
    if last:
        small.append(lw["final_norm_w"])
    in_specs = ([xspec, tspec, tspec, _const_spec(lw["norm_w"].shape),
                 _const_spec(lw["w_in"].shape), _const_spec(lw["w_out"].shape)]
                + [_const_spec(a.shape) for a in small])
    state_shapes = [jax.ShapeDtypeStruct((nb, NH, HD, HD), F32),
                    jax.ShapeDtypeStruct((nb, NH, HD, HD), F32),
                    jax.ShapeDtypeStruct((nb, 2 * S5N), F32),
                    jax.ShapeDtypeStruct((nb, NH, HD, HD), F32),
                    jax.ShapeDtypeStruct((nb, CONV_K - 1, CONV_CH), F32)]
    n_act = 2 if last else 1
    out_shape = [jax.ShapeDtypeStruct((nb, L, D), F32)] * n_act + state_shapes
    out_specs = [xspec] * n_act + [_const_spec(s.shape) for s in state_shapes]
    scratch = [pltpu.VMEM((rows, D), BF16),
               pltpu.VMEM((rows, P_PAD), F32),
               pltpu.VMEM((rows, D), BF16),
               pltpu.VMEM((NLT, rows, LANE), F32),
               pltpu.VMEM((nb, TB + 8, CONV_CH), F32),
               pltpu.VMEM((W // LANE, rows, SUB * LANE), BF16),
               pltpu.VMEM((W // LANE, rows, LANE), F32),
               pltpu.VMEM((nb, TB, W), F32),
               pltpu.VMEM((nb, TB, W), F32),
               pltpu.VMEM((nb, TB, W), F32)]
    outs = pl.pallas_call(
        functools.partial(_prompt_layer_body, last),
        grid=(nblk,),
        in_specs=in_specs,
        out_specs=out_specs,
        out_shape=out_shape,
        scratch_shapes=scratch,
        compiler_params=pltpu.CompilerParams(dimension_semantics=("arbitrary",),
                                             vmem_limit_bytes=VMEM_LIMIT),
        name="prompt_layer",
    )(x, tabs["cos_p"], tabs["sin_p"], lw["norm_w"], lw["w_in"], lw["w_out"], *small)
    return outs


def _rope_tables(pos):
    half = HD // 2
    inv = 1.0 / (ROPE_BASE ** (jnp.arange(half, dtype=F32) / half))
    ang = pos[:, None] * inv[None, :]
    cos = jnp.cos(ang)
    sin = jnp.sin(ang)
    cos_t = jnp.tile(cos, (1, LANE // half))
    sin_t = jnp.tile(jnp.concatenate([-sin, sin], axis=1), (1, LANE // HD))
    return cos_t, sin_t


def _retention_tables():
    log_gamma = jnp.log1p(-(2.0 ** (-5.0 - jnp.arange(NH, dtype=F32))))
    cum = jnp.cumsum(jnp.broadcast_to(log_gamma, (TB, NH)), axis=0)
    total = cum[-1]
    causal = jnp.tril(jnp.ones((TB, TB), dtype=bool))
    diff = cum[:, None, :] - cum[None, :, :]
    dec = jnp.where(causal[:, :, None], jnp.exp(jnp.where(causal[:, :, None], diff, 0.0)), 0.0)
    rdec = jnp.moveaxis(dec, 2, 0)
    recum = jnp.broadcast_to(jnp.exp(cum).T[:, :, None], (NH, TB, HD))
    rkdec = jnp.broadcast_to(jnp.exp(total[None, :] - cum).T[:, :, None], (NH, TB, HD))
    retot = jnp.broadcast_to(jnp.exp(total)[:, None, None], (NH, HD, HD))
    return rdec, recum, rkdec, retot, log_gamma


def _sel_matrix():
    sel = np.zeros((SUB, 2, HD, 2, TB), np.float32)
    for s_ in range(SUB):
        for h2 in range(2):
            sel[s_, h2, :, h2, s_::SUB] = 1.0
    return jnp.asarray(sel.reshape(SUB * LANE, LANE), dtype=BF16)


def _row(v, width=None):
    v = v.astype(F32).reshape(1, -1)
    if width is not None and v.shape[1] < width:
        v = jnp.pad(v, ((0, 0), (0, width - v.shape[1])))
    return v


def _block_diag(blocks):
    g, r, c = blocks.shape
    eye = jnp.eye(g, dtype=blocks.dtype)
    return jnp.einsum('grc,gh->grhc', blocks, eye).reshape(g * r, g * c)


def _s5_discretise(A_re, A_im, log_dt, B_re, B_im):
    A_re, A_im = A_re.astype(F32), A_im.astype(F32)
    dt = jnp.exp(log_dt.astype(F32))[:, None]
    mag = jnp.exp(A_re * dt)
    ab_re, ab_im = mag * jnp.cos(A_im * dt), mag * jnp.sin(A_im * dt)
    nr, ni = ab_re - 1.0, ab_im
    den = A_re * A_re + A_im * A_im
    f_re = (nr * A_re + ni * A_im) / den
    f_im = (ni * A_re - nr * A_im) / den
    B_re, B_im = B_re.astype(F32), B_im.astype(F32)
    bb_re = f_re[..., None] * B_re - f_im[..., None] * B_im
    bb_im = f_re[..., None] * B_im + f_im[..., None] * B_re
    return ab_re, ab_im, bb_re, bb_im


def _layer_weights(l, lb_all, p):
    ab_re, ab_im, bb_re, bb_im = _s5_discretise(p["s5_A_re"][l], p["s5_A_im"][l], p["s5_log_dt"][l],
                                                p["s5_B_re"][l], p["s5_B_im"][l])
    bblk = jnp.concatenate([_block_diag(jnp.swapaxes(bb_re, 1, 2)),
                            _block_diag(jnp.swapaxes(bb_im, 1, 2))], axis=1)
    cblk = jnp.concatenate([_block_diag(jnp.swapaxes(p["s5_C_re"][l].astype(F32), 1, 2)),
                            _block_diag(jnp.swapaxes(-p["s5_C_im"][l].astype(F32), 1, 2))], axis=0)
    return dict(
        norm_w=_row(p["norm_w"][l]),
        w_in=jnp.pad(p["w_in"][l], ((0, 0), (0, P_PAD - P_TOTAL))).astype(BF16),
        w_out=p["w_out"][l].astype(BF16),
        ret_norm_w=_row(p["ret_norm_w"][l]),
        lb=_row(lb_all[l]),
        hgrn_norm_w=_row(p["hgrn_norm_w"][l]),
        s5_A=jnp.stack([ab_re.reshape(-1), ab_im.reshape(-1)], axis=0),
        s5_Bblk=bblk.astype(BF16),
        s5_Cblk=cblk.astype(BF16),
        s5_D=_row(p["s5_D"][l]),
        s5_glu_w=p["s5_glu_w"][l].astype(BF16),
        s5_glu_b=_row(p["s5_glu_b"][l]),
        m2_conv_w=p["m2_conv_w"][l].astype(F32),
        m2_conv_b=_row(p["m2_conv_b"][l]),
        m2_dt_bias=_row(p["m2_dt_bias"][l], LANE),
        m2_A_log=_row(p["m2_A_log"][l], LANE),
        m2_D=_row(p["m2_D"][l], LANE),
        m2_norm_w=_row(p["m2_norm_w"][l]),
        final_norm_w=_row(p["final_norm_w"]),
    )


def _make_tables(prompt_len):
    cos_p, sin_p = _rope_tables(jnp.arange(prompt_len, dtype=F32))
    rdec, recum, rkdec, retot, log_gamma = _retention_tables()
    cos_s, sin_s = _rope_tables(PAST_LEN + jnp.arange(1, dtype=F32))
    return dict(cos_p=cos_p, sin_p=sin_p, cos_s=cos_s, sin_s=sin_s,
                rdec=rdec, recum=recum, rkdec=rkdec, retot=retot,
                rgam=_row(jnp.exp(log_gamma), LANE), sel=_sel_matrix())


SB = 8
NPCH = 3


def _sample_proj_body(x_ref, normw_ref, win_ref, proj_ref):
    x = x_ref[...]
    hn = x * lax.rsqrt(jnp.mean(x * x, axis=-1, keepdims=True) + EPS) * normw_ref[...]
    proj_ref[...] = jnp.dot(hn.astype(BF16), win_ref[...], preferred_element_type=F32)


def _sample_proj(x, lw):
    n = x.shape[0]
    cw = P_PAD // NPCH
    return pl.pallas_call(
        _sample_proj_body,
        grid=(NPCH,),
        in_specs=[pl.BlockSpec((n, D), lambda c: (0, 0)),
                  pl.BlockSpec((1, D), lambda c: (0, 0)),
                  pl.BlockSpec((D, cw), lambda c: (0, c))],
        out_specs=pl.BlockSpec((n, cw), lambda c: (0, c)),
        out_shape=jax.ShapeDtypeStruct((n, P_PAD), F32),
        compiler_params=pltpu.CompilerParams(dimension_semantics=("arbitrary",),
                                             vmem_limit_bytes=VMEM_LIMIT),
        name="sample_proj",
    )(x, lw["norm_w"], lw["w_in"])


def _cols(x):
    return jnp.concatenate([x] * (LANE // SB), axis=0).T


def _state_step(st_in, st_out, o_s, col0, decay, kcols, krow, v, qcols):
    for j in range(SB):
        for h in range(NH):
            vs = slice(h * HD, (h + 1) * HD)
            ks = krow(h)
            s_new = decay(j, h) * st_in[j, h] + kcols[ks, j:j + 1] * v[j:j + 1, vs]
            st_out[j, h] = s_new
            o_s[j:j + 1, col0 + h * HD:col0 + (h + 1) * HD] = jnp.sum(
                qcols[ks, j:j + 1] * s_new, axis=0, keepdims=True)


def _sample_mix_body(proj_ref, cos_ref, sin_ref, rgam_ref, retnw_ref, lb_ref, hnw_ref,
                     sA_ref, sB_ref, sC_ref, sD_ref, gluw_ref, glub_ref,
                     cw_ref, cb_ref, dtb_ref, alog_ref, md_ref, mnw_ref,
                     ret_in, hg_in, s5_in, m2_in, buf_in,
                     mixed_ref, ret_out, hg_out, s5_out, m2_out, buf_out, o_s):
    def pj(c0, w):
        return proj_ref[:, c0:c0 + w]

    cos = cos_ref[...]
    sin = sin_ref[...]

    def head_rows(h):
        return slice(h * HD, (h + 1) * HD)

    rq = _rotary(pj(C_RQ, W), cos, sin)
    rk = _rotary(pj(C_RK, W), cos, sin) * (HD ** -0.5)
    rv = pj(C_RV, W)
    rg = pj(C_RG, W)
    _state_step(ret_in, ret_out, o_s, 0 * W, lambda j, h: rgam_ref[:, h:h + 1],
                _cols(rk), head_rows, rv, _cols(rq))
    for h in range(NH):
        sl = head_rows(h)
        o = _head_rms(o_s[:, 0 * W + h * HD:0 * W + (h + 1) * HD]) * retnw_ref[:, sl] * _silu(rg[:, sl])
        mixed_ref[:, 0 * W + h * HD:0 * W + (h + 1) * HD] = o

    fr = pj(C_GF, W)
    lb = lb_ref[...]
    logf = _log_sigmoid(fr) + jnp.log1p(lb * jnp.exp(jnp.minimum(-fr, EXP_CLIP)))
    fcols = _cols(jnp.exp(logf))
    hq = _silu(pj(C_GQ, W))
    hk = (1.0 - lb) * jax.nn.sigmoid(-fr)
    hv = pj(C_GI, W)
    gg = pj(C_GG, W)
    _state_step(hg_in, hg_out, o_s, 1 * W, lambda j, h: fcols[head_rows(h), j:j + 1],
                _cols(hk), head_rows, hv, _cols(hq))
    for h in range(NH):
        sl = head_rows(h)
        o = _head_rms(o_s[:, 1 * W + h * HD:1 * W + (h + 1) * HD]) * hnw_ref[:, sl] * _silu(gg[:, sl])
        mixed_ref[:, 1 * W + h * HD:1 * W + (h + 1) * HD] = o

    u = pj(C_SU, W)
    bu = _dot(u, sB_ref[...])
    hr, hi = s5_in[:, 0:S5N], s5_in[:, S5N:2 * S5N]
    ar, ai = sA_ref[0:1, :], sA_ref[1:2, :]
    nr = ar * hr - ai * hi + bu[:, 0:S5N]
    ni = ar * hi + ai * hr + bu[:, S5N:2 * S5N]
    s5_out[:, 0:S5N] = nr
    s5_out[:, S5N:2 * S5N] = ni
    sy = _dot(jnp.concatenate([nr, ni], axis=1), sC_ref[...]) + sD_ref[...] * u
    gy = _gelu_tanh(sy)
    os5 = gy * jax.nn.sigmoid(_dot(gy, gluw_ref[...]) + glub_ref[...]) * _silu(pj(C_SG, W))
    mixed_ref[:, 2 * W:3 * W] = os5

    xnew = pj(C_XBC, CONV_CH)
    acc = cb_ref[...] + xnew * cw_ref[CONV_K - 1:CONV_K, :]
    for i in range(CONV_K - 1):
        acc = acc + buf_in[:, i * CONV_CH:(i + 1) * CONV_CH] * cw_ref[i:i + 1, :]
    for i in range(CONV_K - 2):
        buf_out[:, i * CONV_CH:(i + 1) * CONV_CH] = buf_in[:, (i + 1) * CONV_CH:(i + 2) * CONV_CH]
    buf_out[:, (CONV_K - 2) * CONV_CH:(CONV_K - 1) * CONV_CH] = xnew
    xbc = _silu(acc)
    xm = xbc[:, 0:W]
    bm = xbc[:, W:W + 2 * HD]
    cm = xbc[:, W + 2 * HD:W + 4 * HD]
    dt = _softplus(pj(C_DT, LANE) + dtb_ref[...])
    adec = jnp.exp(dt * (-jnp.exp(alog_ref[...])))
    mz = pj(C_MZ, W)
    xdt = jnp.concatenate([xm[:, head_rows(h)] * dt[:, h:h + 1] for h in range(NH)], axis=1)
    _state_step(m2_in, m2_out, o_s, 3 * W, lambda j, h: adec[j:j + 1, h:h + 1],
                _cols(bm), lambda h: head_rows(h // 2), xdt, _cols(cm))
    ys = [o_s[:, 3 * W + h * HD:3 * W + (h + 1) * HD] + md_ref[:, h:h + 1] * xm[:, head_rows(h)]
          for h in range(NH)]
    my = jnp.concatenate(ys, axis=1) * _silu(mz)
    om = my * lax.rsqrt(jnp.mean(my * my, axis=-1, keepdims=True) + EPS) * mnw_ref[...]
    mixed_ref[:, 3 * W:4 * W] = om


def _sample_mix(proj, tabs, lw, ret, hg, s5, m2, buf):
    n = proj.shape[0]
    params = [tabs["cos_s"], tabs["sin_s"], tabs["rgam"], lw["ret_norm_w"], lw["lb"], lw["hgrn_norm_w"],
              lw["s5_A"], lw["s5_Bblk"], lw["s5_Cblk"], lw["s5_D"], lw["s5_glu_w"], lw["s5_glu_b"],
              lw["m2_conv_w"], lw["m2_conv_b"], lw["m2_dt_bias"], lw["m2_A_log"], lw["m2_D"],
              lw["m2_norm_w"]]
    st_spec = pl.BlockSpec((SB, NH, HD, HD), lambda i: (i, 0, 0, 0))
    s5_spec = pl.BlockSpec((SB, 2 * S5N), lambda i: (i, 0))
    buf_spec = pl.BlockSpec((SB, (CONV_K - 1) * CONV_CH), lambda i: (i, 0))
    in_specs = ([pl.BlockSpec((SB, P_PAD), lambda i: (i, 0))]
                + [_const_spec(a.shape) for a in params]
                + [st_spec, st_spec, s5_spec, st_spec, buf_spec])
    out_specs = [pl.BlockSpec((SB, D), lambda i: (i, 0)), st_spec, st_spec, s5_spec, st_spec, buf_spec]
    out_shape = [jax.ShapeDtypeStruct((n, D), F32),
                 jax.ShapeDtypeStruct(ret.shape, F32), jax.ShapeDtypeStruct(hg.shape, F32),
                 jax.ShapeDtypeStruct(s5.shape, F32), jax.ShapeDtypeStruct(m2.shape, F32),
                 jax.ShapeDtypeStruct(buf.shape, F32)]
    return pl.pallas_call(
        _sample_mix_body,
        grid=(n // SB,),
        in_specs=in_specs,
        out_specs=out_specs,
        out_shape=out_shape,
        scratch_shapes=[pltpu.VMEM((SB, D), F32)],
        compiler_params=pltpu.CompilerParams(dimension_semantics=("arbitrary",),
                                             vmem_limit_bytes=VMEM_LIMIT),
        name="sample_mix",
    )(proj, *params, ret, hg, s5, m2, buf)


def _sample_out_body(last, *refs):
    if last:
        x_ref, mixed_ref, wout_ref, fnw_ref, xo_ref, y_ref = refs
    else:
        x_ref, mixed_ref, wout_ref, xo_ref = refs
    xo = x_ref[...] + jnp.dot(mixed_ref[...].astype(BF16), wout_ref[...], preferred_element_type=F32)
    xo_ref[...] = xo
    if last:
        y_ref[...] = xo * lax.rsqrt(jnp.mean(xo * xo, axis=-1, keepdims=True) + EPS) * fnw_ref[...]


def _sample_out(x, mixed, last, lw):
    n = x.shape[0]
    args = [x, mixed, lw["w_out"]] + ([lw["final_norm_w"]] if last else [])
    return pl.pallas_call(
        functools.partial(_sample_out_body, last),
        out_shape=[jax.ShapeDtypeStruct((n, D), F32)] * (2 if last else 1),
        compiler_params=pltpu.CompilerParams(vmem_limit_bytes=VMEM_LIMIT),
        name="sample_out",
    )(*args)


def kernel(x_prompt, x_sample, state_ret, state_hgrn, state_s5_re, state_s5_im, state_m2_ssm,
           state_m2_conv, norm_w, w_in, ret_norm_w, hgrn_lb_logits, hgrn_norm_w, s5_A_re, s5_A_im,
           s5_log_dt, s5_B_re, s5_B_im, s5_C_re, s5_C_im, s5_D, s5_glu_w, s5_glu_b, m2_conv_w,
           m2_conv_b, m2_dt_bias, m2_A_log, m2_D, m2_norm_w, w_out, final_norm_w):
    p = dict(norm_w=norm_w, w_in=w_in, ret_norm_w=ret_norm_w, hgrn_norm_w=hgrn_norm_w,
             s5_A_re=s5_A_re, s5_A_im=s5_A_im, s5_log_dt=s5_log_dt, s5_B_re=s5_B_re, s5_B_im=s5_B_im,
             s5_C_re=s5_C_re, s5_C_im=s5_C_im, s5_D=s5_D, s5_glu_w=s5_glu_w, s5_glu_b=s5_glu_b,
             m2_conv_w=m2_conv_w, m2_conv_b=m2_conv_b, m2_dt_bias=m2_dt_bias, m2_A_log=m2_A_log,
             m2_D=m2_D, m2_norm_w=m2_norm_w, w_out=w_out, final_norm_w=final_norm_w)
    depth = norm_w.shape[0]
    nbp, lp, _ = x_prompt.shape
    nbs = x_sample.shape[0]

    lb_sm = jax.nn.softmax(hgrn_lb_logits.astype(F32), axis=0)
    lb_all = jnp.clip(jnp.cumsum(lb_sm, axis=0) - lb_sm[0], 0.0, 1.0)

    tabs = _make_tables(lp)
    xp = x_prompt
    xs = x_sample.reshape(nbs, D)
    yp = ys = None
    pst, sst = [], []
    for l in range(depth):
        last = l == depth - 1
        lw = _layer_weights(l, lb_all, p)

        outs = _prompt_layer(xp, last, tabs, lw)
        xp = outs[0]
        if last:
            yp = outs[1]
        ret, hg, s5, m2, buf = outs[-5:]
        pst.append((ret, hg, s5[:, :S5N].reshape(nbp, S5G, S5P), s5[:, S5N:].reshape(nbp, S5G, S5P),
                    m2, buf))

        proj = _sample_proj(xs, lw)
        s5_in = jnp.concatenate([state_s5_re[l].reshape(nbs, S5N), state_s5_im[l].reshape(nbs, S5N)],
                                axis=1).astype(F32)
        mixed, ret, hg, s5, m2, buf = _sample_mix(
            proj, tabs, lw,
            state_ret[l].astype(F32), state_hgrn[l].astype(F32), s5_in, state_m2_ssm[l].astype(F32),
            state_m2_conv[l].astype(F32).reshape(nbs, (CONV_K - 1) * CONV_CH))
        souts = _sample_out(xs, mixed, last, lw)
        xs = souts[0]
        if last:
            ys = souts[1]
        sst.append((ret, hg, s5[:, :S5N].reshape(nbs, S5G, S5P), s5[:, S5N:].reshape(nbs, S5G, S5P),
                    m2, buf.reshape(nbs, CONV_K - 1, CONV_CH)))

    stk = lambda lst, i: jnp.stack([s[i] for s in lst], axis=0)
    return (yp, ys.reshape(nbs, 1, D),
            stk(pst, 0), stk(pst, 1), stk(pst, 2), stk(pst, 3), stk(pst, 4), stk(pst, 5),
            stk(sst, 0), stk(sst, 1), stk(sst, 2), stk(sst, 3), stk(sst, 4), stk(sst, 5))
```

```python
import functools
import math

import numpy as np
import jax
import jax.numpy as jnp
from jax import lax
from jax.experimental import pallas as pl
from jax.experimental.pallas import tpu as pltpu

F32 = jnp.float32
BF16 = jnp.bfloat16
HI = lax.Precision.HIGHEST

D = 1024
W = 256
NH = 4
HD = 64
S5G = 16
S5C = 16
S5P = 64
S5N = S5G * S5P
CONV_CH = 512
CONV_K = 4
TB = 64
SUB = 16
NSUB = TB // SUB
EPS = 1e-6
EXP_CLIP = 60.0
ROPE_BASE = 10000.0
PAST_LEN = 16384

C_RQ, C_RK, C_RV, C_RG = 0, 256, 512, 768
C_GQ, C_GF, C_GI, C_GG = 1024, 1280, 1536, 1792
C_SU, C_SG = 2048, 2304
C_MZ, C_XBC, C_DT = 2560, 2816, 3328
P_TOTAL = 3332
P_PAD = 3456
LANE = 128
VMEM_LIMIT = 56 * 1024 * 1024


def _silu(x):
    return x * jax.nn.sigmoid(x)


def _softplus(x):
    return jnp.maximum(x, 0.0) + jnp.log(1.0 + jnp.exp(-jnp.abs(x)))


def _log_sigmoid(x):
    return jnp.minimum(x, 0.0) - jnp.log(1.0 + jnp.exp(-jnp.abs(x)))


def _round_robin(gens):
    gens = list(gens)
    while gens:
        alive = []
        for g in gens:
            try:
                next(g)
                alive.append(g)
            except StopIteration:
                pass
        gens = alive


def _gelu_tanh(x):
    c = math.sqrt(2.0 / math.pi)
    return 0.5 * x * (1.0 + jnp.tanh(c * (x + 0.044715 * (x * x * x))))


def _dot(a, b):
    return jnp.dot(a.astype(BF16), b.astype(BF16), preferred_element_type=F32)


def _dot_nt(a, b):
    return lax.dot_general(a.astype(BF16), b.astype(BF16), (((1,), (1,)), ((), ())),
                           preferred_element_type=F32)


def _dot_tn(a, b):
    return lax.dot_general(a.astype(BF16), b.astype(BF16), (((0,), (0,)), ((), ())),
                           preferred_element_type=F32)


def _dot_hi(a, b):
    return jnp.dot(a, b, precision=HI, preferred_element_type=F32)


def _dot_tn_hi(a, b):
    return lax.dot_general(a, b, (((0,), (0,)), ((), ())), precision=HI,
                           preferred_element_type=F32)


def _rot_half_partner(x):
    lane = lax.broadcasted_iota(jnp.int32, x.shape, 1)
    first = (lane % HD) < (HD // 2)
    return jnp.where(first, pltpu.roll(x, LANE - HD // 2, 1), pltpu.roll(x, HD // 2, 1))


def _rotary(x, cos, sin_signed):
    parts = []
    for i in range(W // LANE):
        xi = x[:, i * LANE:(i + 1) * LANE]
        parts.append(xi * cos + _rot_half_partner(xi) * sin_signed)
    return jnp.concatenate(parts, axis=1)


def _head_rms(o):
    return o * lax.rsqrt(jnp.sum(o * o, axis=-1, keepdims=True) * (1.0 / o.shape[-1]) + EPS)


def _prompt_layer_body(last, *refs):
    (x_ref, cos_ref, sin_ref, normw_ref, win_ref, wout_ref,
     rdec_ref, recum_ref, rkdec_ref, retot_ref, retnw_ref,
     lb_ref, hnw_ref, sel_ref,
     sA_ref, sB_ref, sC_ref, sD_ref, gluw_ref, glub_ref,
     cw_ref, cb_ref, dtb_ref, alog_ref, md_ref, mnw_ref) = refs[:26]
    refs = refs[26:]
    if last:
        fnw_ref, xo_ref, y_ref = refs[:3]
        refs = refs[3:]
    else:
        xo_ref = refs[0]
        refs = refs[1:]
    (ret_ref, hg_ref, s5_ref, m2_ref, m2buf_ref,
     hn_s, proj_s, mixed_s, bu_s, u_s, cv_s, p_s, dg_s, hq_s, hk_s, cum_s) = refs
    j = pl.program_id(0)
    nb = x_ref.shape[0]

    @pl.when(j == 0)
    def _init():
        ret_ref[...] = jnp.zeros(ret_ref.shape, F32)
        hg_ref[...] = jnp.zeros(hg_ref.shape, F32)
        s5_ref[...] = jnp.zeros(s5_ref.shape, F32)
        m2_ref[...] = jnp.zeros(m2_ref.shape, F32)
        cv_s[...] = jnp.zeros(cv_s.shape, F32)

    ti = lax.broadcasted_iota(jnp.int32, (TB, TB), 0)
    si = lax.broadcasted_iota(jnp.int32, (TB, TB), 1)
    causal = si <= ti
    tri_l = causal.astype(F32)
    tri_u = (ti <= si).astype(F32)
    blockdiag = (ti // SUB) == (si // SUB)
    ones_tv = jnp.ones((TB, HD), F32)

    def norm_body(b, c):
        xb = x_ref[b]
        hn = xb * lax.rsqrt(jnp.mean(xb * xb, axis=-1, keepdims=True) + EPS) * normw_ref[...]
        hn_s[pl.ds(pl.multiple_of(b * TB, TB), TB), :] = hn.astype(BF16)
        return c
    lax.fori_loop(0, nb, norm_body, 0)
    NCH = 3
    cw = P_PAD // NCH
    for c in range(NCH):
        proj_s[:, c * cw:(c + 1) * cw] = jnp.dot(hn_s[...], win_ref[:, c * cw:(c + 1) * cw],
                                                 preferred_element_type=F32)

    cos = cos_ref[...]
    sin = sin_ref[...]

    def phase1(b, c):
        r0 = pl.multiple_of(b * TB, TB)

        def pj(c0, w):
            return proj_s[pl.ds(r0, TB), c0:c0 + w]

        u = pj(C_SU, W)
        for c_ in range(W // LANE):
            u_s[c_, pl.ds(b, TB, stride=nb), :] = u[:, c_ * LANE:(c_ + 1) * LANE]

        rq = _rotary(pj(C_RQ, W), cos, sin)
        rk = _rotary(pj(C_RK, W), cos, sin) * (HD ** -0.5)
        rv = pj(C_RV, W)
        rg = pj(C_RG, W)

        def ret_head(h):
            sl = slice(h * HD, (h + 1) * HD)
            qh, kh, vh = rq[:, sl], rk[:, sl], rv[:, sl]
            hst = ret_ref[b, h]
            s_raw = _dot_nt(qh, kh)
            oi = _dot(qh, hst)
            kv = _dot_tn(kh * rkdec_ref[h], vh)
            yield
            o = _dot(s_raw * rdec_ref[h], vh)
            yield
            o = o + oi * recum_ref[h]
            ret_ref[b, h] = retot_ref[h] * hst + kv
            o = _head_rms(o) * retnw_ref[:, sl] * _silu(rg[:, sl])
            mixed_s[pl.ds(r0, TB), 0 * W + h * HD:0 * W + (h + 1) * HD] = o.astype(BF16)

        cv_s[b, 8:8 + TB, :] = pj(C_XBC, CONV_CH)
        acc = cb_ref[...] + cv_s[b, 5:5 + TB, :] * cw_ref[0:1, :]
        for i in range(1, CONV_K):
            acc = acc + cv_s[b, 5 + i:5 + i + TB, :] * cw_ref[i:i + 1, :]
        tail = cv_s[b, TB + 5:TB + 8, :]
        cv_s[b, 5:8, :] = tail
        m2buf_ref[b] = tail
        xbc = _silu(acc)
        xm = xbc[:, 0:W]
        bm = xbc[:, W:W + 2 * HD]
        cm = xbc[:, W + 2 * HD:W + 4 * HD]
        dt = _softplus(pj(C_DT, LANE) + dtb_ref[...])
        la = dt * (-jnp.exp(alog_ref[...]))
        cumc = _dot_hi(tri_l, la)
        cumr = _dot_tn_hi(la, tri_u)
        totr = cumc[TB - 1:TB, :]
        ys = [None] * NH

        def m2_head(h):
            g = h // 2
            sl = slice(h * HD, (h + 1) * HD)
            gs = slice(g * HD, (g + 1) * HD)
            ch, bh, xh = cm[:, gs], bm[:, gs], xm[:, sl]
            xdt = xh * dt[:, h:h + 1]
            cc = cumc[:, h:h + 1]
            tot = totr[:, h:h + 1]
            hst = m2_ref[b, h]
            s_raw = _dot_nt(ch, bh)
            oi = _dot(ch, hst)
            kv = _dot_tn(bh * jnp.exp(tot - cc), xdt)
            decay = jnp.where(causal, jnp.exp(jnp.minimum(cc - cumr[h:h + 1, :], 0.0)), 0.0)
            yield
            o = _dot(s_raw * decay, xdt)
            yield
            m2_ref[b, h] = jnp.exp(tot) * hst + kv
            ys[h] = o + oi * jnp.exp(cc) + md_ref[:, h:h + 1] * xh

        fr = pj(C_GF, W)
        lb = lb_ref[...]
        logf = _log_sigmoid(fr) + jnp.log(1.0 + lb * jnp.exp(jnp.minimum(-fr, EXP_CLIP)))
        hq = _silu(pj(C_GQ, W))
        hk = (1.0 - lb) * jax.nn.sigmoid(-fr)
        cum = _dot_hi(tri_l, logf)
        hq_s[b] = hq
        hk_s[b] = hk
        cum_s[b] = cum
        tl = lax.broadcasted_iota(jnp.int32, (TB, LANE), 0) % SUB

        def diag_products(p):
            ls = slice(p * LANE, (p + 1) * LANE)
            hq_p = hq[:, ls]
            cum_p = cum[:, ls]
            for s_ in range(SUB):
                kb = jnp.concatenate(
                    [jnp.broadcast_to(hk_s[b, i * SUB + s_:i * SUB + s_ + 1, ls], (SUB, LANE))
                     for i in range(NSUB)], axis=0)
                cb = jnp.concatenate(
                    [jnp.broadcast_to(cum_s[b, i * SUB + s_:i * SUB + s_ + 1, ls], (SUB, LANE))
                     for i in range(NSUB)], axis=0)
                pv = jnp.where(tl >= s_, hq_p * kb * jnp.exp(jnp.minimum(cum_p - cb, 0.0)), 0.0)
                p_s[p, pl.ds(r0, TB), s_ * LANE:(s_ + 1) * LANE] = pv.astype(BF16)
                if s_ % 2 == 1:
                    yield

        _round_robin([ret_head(h) for h in range(NH)] + [m2_head(h) for h in range(NH)]
                     + [diag_products(p) for p in range(W // LANE)])

        my = jnp.concatenate(ys, axis=1) * _silu(pj(C_MZ, W))
        om = my * lax.rsqrt(jnp.mean(my * my, axis=-1, keepdims=True) + EPS) * mnw_ref[...]
        mixed_s[pl.ds(r0, TB), 3 * W:4 * W] = om.astype(BF16)
        return c
    lax.fori_loop(0, nb, phase1, 0)

    for p in range(W // LANE):
        dg_s[p] = jnp.dot(p_s[p], sel_ref[...], preferred_element_type=F32)

    u_tb = jnp.concatenate([u_s[c_] for c_ in range(W // LANE)], axis=1)
    bu_s[...] = _dot(u_tb, sB_ref[...])
    ar = jnp.broadcast_to(sA_ref[0:1, :], (nb, S5N))
    ai = jnp.broadcast_to(sA_ref[1:2, :], (nb, S5N))

    def scan_body(t, carry):
        hr, hi = carry
        row = pl.multiple_of(t * nb, nb)
        nr = ar * hr - ai * hi + bu_s[pl.ds(row, nb), 0:S5N]
        ni = ar * hi + ai * hr + bu_s[pl.ds(row, nb), S5N:2 * S5N]
        bu_s[pl.ds(row, nb), 0:S5N] = nr
        bu_s[pl.ds(row, nb), S5N:2 * S5N] = ni
        return nr, ni
    hr, hi = lax.fori_loop(0, TB, scan_body, (s5_ref[:, 0:S5N], s5_ref[:, S5N:2 * S5N]))
    s5_ref[:, 0:S5N] = hr
    s5_ref[:, S5N:2 * S5N] = hi
    ch_tb = _dot(bu_s[...], sC_ref[...])
    for c_ in range(W // LANE):
        u_s[c_] = ch_tb[:, c_ * LANE:(c_ + 1) * LANE]

    def phase2(b, c):
        r0 = pl.multiple_of(b * TB, TB)

        def pj(c0, w):
            return proj_s[pl.ds(r0, TB), c0:c0 + w]

        def s5_out():
            chs = jnp.concatenate([u_s[c_, pl.ds(b, TB, stride=nb), :] for c_ in range(W // LANE)],
                                  axis=1)
            gy = _gelu_tanh(chs + sD_ref[...] * pj(C_SU, W))
            glu = _dot(gy, gluw_ref[...])
            yield
            os5 = gy * jax.nn.sigmoid(glu + glub_ref[...]) * _silu(pj(C_SG, W))
            mixed_s[pl.ds(r0, TB), 2 * W:3 * W] = os5.astype(BF16)

        cum = cum_s[b]
        rr = jnp.concatenate(
            [jnp.zeros((SUB, W), F32)]
            + [jnp.broadcast_to(cum_s[b, i * SUB - 1:i * SUB, :], (SUB, W))
               for i in range(1, NSUB)], axis=0)
        ee = jnp.concatenate(
            [jnp.broadcast_to(cum_s[b, i * SUB + SUB - 1:i * SUB + SUB, :], (SUB, W))
             for i in range(NSUB)], axis=0)
        lastrow = lax.broadcasted_iota(jnp.int32, (TB, W), 0) == TB - 1
        totc = _dot_tn_hi(jnp.where(lastrow, cum, 0.0), ones_tv)
        hq = hq_s[b]
        hk = hk_s[b]
        hv = pj(C_GI, W)
        gg = pj(C_GG, W)
        qt = hq * jnp.exp(cum - rr)
        kh_ = hk * jnp.exp(ee - cum)
        qe = hq * jnp.exp(cum)
        tot = cum[TB - 1:TB, :]
        kend = kh_ * jnp.exp(tot - ee)
        trow = lax.broadcasted_iota(jnp.int32, (TB, W), 0) // SUB
        qx, kx = [], []
        for jb in range(NSUB - 1):
            eb = jnp.broadcast_to(ee[jb * SUB:jb * SUB + 1, :], (TB, W))
            qx.append(jnp.where(trow > jb, qt * jnp.exp(jnp.minimum(rr - eb, 0.0)), 0.0))
            kx.append(jnp.where(trow == jb, kh_, 0.0))

        def hg_head(h):
            sl = slice(h * HD, (h + 1) * HD)
            p, h2 = divmod(h, 2)
            hst = hg_ref[b, h]
            off = _dot_nt(qx[0][:, sl], kx[0][:, sl])
            for jb in range(1, NSUB - 1):
                off = off + _dot_nt(qx[jb][:, sl], kx[jb][:, sl])
            oi = _dot(qe[:, sl], hst)
            kv = _dot_tn(kend[:, sl], hv[:, sl])
            yield
            dgt = dg_s[p, pl.ds(r0, TB), h2 * HD:(h2 + 1) * HD]
            o = _dot(jnp.where(blockdiag, dgt, 0.0) + off, hv[:, sl])
            yield
            hg_ref[b, h] = jnp.exp(totc[sl, :]) * hst + kv
            o = _head_rms(o + oi) * hnw_ref[:, sl] * _silu(gg[:, sl])
            mixed_s[pl.ds(r0, TB), 1 * W + h * HD:1 * W + (h + 1) * HD] = o.astype(BF16)

        _round_robin([s5_out()] + [hg_head(h) for h in range(NH)])
        return c
    lax.fori_loop(0, nb, phase2, 0)

    res = jnp.dot(mixed_s[...], wout_ref[...], preferred_element_type=F32)
    for b in range(nb):
        xo = x_ref[b] + res[b * TB:(b + 1) * TB]
        xo_ref[b] = xo
        if last:
            y_ref[b] = xo * lax.rsqrt(jnp.mean(xo * xo, axis=-1, keepdims=True) + EPS) * fnw_ref[...]


def _const_spec(shape):
    nd = len(shape)
    return pl.BlockSpec(shape, lambda j: (0,) * nd)


def _prompt_layer(x, last, tabs, lw):
    nb, L, _ = x.shape
    nblk = L // TB
    rows = nb * TB
    xspec = pl.BlockSpec((nb, TB, D), lambda j: (0, j, 0))
    tspec = pl.BlockSpec((TB, LANE), lambda j: (j, 0))
    small = [tabs["rdec"], tabs["recum"], tabs["rkdec"], tabs["retot"], lw["ret_norm_w"],
             lw["lb"], lw["hgrn_norm_w"], tabs["sel"],
             lw["s5_A"], lw["s5_Bblk"], lw["s5_Cblk"], lw["s5_D"], lw["s5_glu_w"], lw["s5_glu_b"],
             lw["m2_conv_w"], lw["m2_conv_b"], lw["m2_dt_bias"], lw["m2_A_log"], lw["m2_D"],
             lw["m2_norm_w"]]
    if last:
        small.append(lw["final_norm_w"])
    in_specs = ([xspec, tspec, tspec, _const_spec(lw["norm_w"].shape),
                 _const_spec(lw["w_in"].shape), _const_spec(lw["w_out"].shape)]
                + [_const_spec(a.shape) for a in small])
    state_shapes = [jax.ShapeDtypeStruct((nb, NH, HD, HD), F32),
                    jax.ShapeDtypeStruct((nb, NH, HD, HD), F32),
                    jax.ShapeDtypeStruct((nb, 2 * S5N), F32),
                    jax.ShapeDtypeStruct((nb, NH, HD, HD), F32),
                    jax.ShapeDtypeStruct((nb, CONV_K - 1, CONV_CH), F32)]
    n_act = 2 if last else 1
    out_shape = [jax.ShapeDtypeStruct((nb, L, D), F32)] * n_act + state_shapes
    out_specs = [xspec] * n_act + [_const_spec(s.shape) for s in state_shapes]
    scratch = [pltpu.VMEM((rows, D), BF16),
               pltpu.VMEM((rows, P_PAD), F32),
               pltpu.VMEM((rows, D), BF16),
               pltpu.VMEM((rows, 2 * S5N), F32),
               pltpu.VMEM((W // LANE, rows, LANE), F32),
               pltpu.VMEM((nb, TB + 8, CONV_CH), F32),
               pltpu.VMEM((W // LANE, rows, SUB * LANE), BF16),
               pltpu.VMEM((W // LANE, rows, LANE), F32),
               pltpu.VMEM((nb, TB, W), F32),
               pltpu.VMEM((nb, TB, W), F32),
               pltpu.VMEM((nb, TB, W), F32)]
    outs = pl.pallas_call(
        functools.partial(_prompt_layer_body, last),
        grid=(nblk,),
        in_specs=in_specs,
        out_specs=out_specs,
        out_shape=out_shape,
        scratch_shapes=scratch,
        compiler_params=pltpu.CompilerParams(dimension_semantics=("arbitrary",),
                                             vmem_limit_bytes=VMEM_LIMIT),
        name="prompt_layer",
    )(x, tabs["cos_p"], tabs["sin_p"], lw["norm_w"], lw["w_in"], lw["w_out"], *small)
    return outs


def _rope_tables(pos):
    half = HD // 2
    inv = 1.0 / (ROPE_BASE ** (jnp.arange(half, dtype=F32) / half))
    ang = pos[:, None] * inv[None, :]
    cos = jnp.cos(ang)
    sin = jnp.sin(ang)
    cos_t = jnp.tile(cos, (1, LANE // half))
    sin_t = jnp.tile(jnp.concatenate([-sin, sin], axis=1), (1, LANE // HD))
    return cos_t, sin_t


def _retention_tables():
    log_gamma = jnp.log1p(-(2.0 ** (-5.0 - jnp.arange(NH, dtype=F32))))
    cum = jnp.cumsum(jnp.broadcast_to(log_gamma, (TB, NH)), axis=0)
    total = cum[-1]
    causal = jnp.tril(jnp.ones((TB, TB), dtype=bool))
    diff = cum[:, None, :] - cum[None, :, :]
    dec = jnp.where(causal[:, :, None], jnp.exp(jnp.where(causal[:, :, None], diff, 0.0)), 0.0)
    rdec = jnp.moveaxis(dec, 2, 0)
    recum = jnp.broadcast_to(jnp.exp(cum).T[:, :, None], (NH, TB, HD))
    rkdec = jnp.broadcast_to(jnp.exp(total[None, :] - cum).T[:, :, None], (NH, TB, HD))
    retot = jnp.broadcast_to(jnp.exp(total)[:, None, None], (NH, HD, HD))
    return rdec, recum, rkdec, retot, log_gamma


def _sel_matrix():
    sel = np.zeros((SUB, 2, HD, 2, TB), np.float32)
    for s_ in range(SUB):
        for h2 in range(2):
            sel[s_, h2, :, h2, s_::SUB] = 1.0
    return jnp.asarray(sel.reshape(SUB * LANE, LANE), dtype=BF16)


def _row(v, width=None):
    v = v.astype(F32).reshape(1, -1)
    if width is not None and v.shape[1] < width:
        v = jnp.pad(v, ((0, 0), (0, width - v.shape[1])))
    return v


def _block_diag(blocks):
    g, r, c = blocks.shape
    eye = jnp.eye(g, dtype=blocks.dtype)
    return jnp.einsum('grc,gh->grhc', blocks, eye).reshape(g * r, g * c)


def _s5_discretise(A_re, A_im, log_dt, B_re, B_im):
    A_re, A_im = A_re.astype(F32), A_im.astype(F32)
    dt = jnp.exp(log_dt.astype(F32))[:, None]
    mag = jnp.exp(A_re * dt)
    ab_re, ab_im = mag * jnp.cos(A_im * dt), mag * jnp.sin(A_im * dt)
    nr, ni = ab_re - 1.0, ab_im
    den = A_re * A_re + A_im * A_im
    f_re = (nr * A_re + ni * A_im) / den
    f_im = (ni * A_re - nr * A_im) / den
    B_re, B_im = B_re.astype(F32), B_im.astype(F32)
    bb_re = f_re[..., None] * B_re - f_im[..., None] * B_im
    bb_im = f_re[..., None] * B_im + f_im[..., None] * B_re
    return ab_re, ab_im, bb_re, bb_im


def _layer_weights(l, lb_all, p):
    ab_re, ab_im, bb_re, bb_im = _s5_discretise(p["s5_A_re"][l], p["s5_A_im"][l], p["s5_log_dt"][l],
                                                p["s5_B_re"][l], p["s5_B_im"][l])
    bblk = jnp.concatenate([_block_diag(jnp.swapaxes(bb_re, 1, 2)),
                            _block_diag(jnp.swapaxes(bb_im, 1, 2))], axis=1)
    cblk = jnp.concatenate([_block_diag(jnp.swapaxes(p["s5_C_re"][l].astype(F32), 1, 2)),
                            _block_diag(jnp.swapaxes(-p["s5_C_im"][l].astype(F32), 1, 2))], axis=0)
    return dict(
        norm_w=_row(p["norm_w"][l]),
        w_in=jnp.pad(p["w_in"][l], ((0, 0), (0, P_PAD - P_TOTAL))).astype(BF16),
        w_out=p["w_out"][l].astype(BF16),
        ret_norm_w=_row(p["ret_norm_w"][l]),
        lb=_row(lb_all[l]),
        hgrn_norm_w=_row(p["hgrn_norm_w"][l]),
        s5_A=jnp.stack([ab_re.reshape(-1), ab_im.reshape(-1)], axis=0),
        s5_Bblk=bblk.astype(BF16),
        s5_Cblk=cblk.astype(BF16),
        s5_D=_row(p["s5_D"][l]),
        s5_glu_w=p["s5_glu_w"][l].astype(BF16),
        s5_glu_b=_row(p["s5_glu_b"][l]),
        m2_conv_w=p["m2_conv_w"][l].astype(F32),
        m2_conv_b=_row(p["m2_conv_b"][l]),
        m2_dt_bias=_row(p["m2_dt_bias"][l], LANE),
        m2_A_log=_row(p["m2_A_log"][l], LANE),
        m2_D=_row(p["m2_D"][l], LANE),
        m2_norm_w=_row(p["m2_norm_w"][l]),
        final_norm_w=_row(p["final_norm_w"]),
    )


def _make_tables(prompt_len):
    cos_p, sin_p = _rope_tables(jnp.arange(prompt_len, dtype=F32))
    rdec, recum, rkdec, retot, log_gamma = _retention_tables()
    cos_s, sin_s = _rope_tables(PAST_LEN + jnp.arange(1, dtype=F32))
    return dict(cos_p=cos_p, sin_p=sin_p, cos_s=cos_s, sin_s=sin_s,
                rdec=rdec, recum=recum, rkdec=rkdec, retot=retot,
                rgam=_row(jnp.exp(log_gamma), LANE), sel=_sel_matrix())


SB = 8
NPCH = 3


def _sample_proj_body(x_ref, normw_ref, win_ref, proj_ref):
    x = x_ref[...]
    hn = x * lax.rsqrt(jnp.mean(x * x, axis=-1, keepdims=True) + EPS) * normw_ref[...]
    proj_ref[...] = jnp.dot(hn.astype(BF16), win_ref[...], preferred_element_type=F32)


def _sample_proj(x, lw):
    n = x.shape[0]
    cw = P_PAD // NPCH
    return pl.pallas_call(
        _sample_proj_body,
        grid=(NPCH,),
        in_specs=[pl.BlockSpec((n, D), lambda c: (0, 0)),
                  pl.BlockSpec((1, D), lambda c: (0, 0)),
                  pl.BlockSpec((D, cw), lambda c: (0, c))],
        out_specs=pl.BlockSpec((n, cw), lambda c: (0, c)),
        out_shape=jax.ShapeDtypeStruct((n, P_PAD), F32),
        compiler_params=pltpu.CompilerParams(dimension_semantics=("arbitrary",),
                                             vmem_limit_bytes=VMEM_LIMIT),
        name="sample_proj",
    )(x, lw["norm_w"], lw["w_in"])


def _cols(x):
    return jnp.concatenate([x] * (LANE // SB), axis=0).T


def _state_step(st_in, st_out, o_s, col0, decay, kcols, krow, v, qcols):
    for j in range(SB):
        for h in range(NH):
            vs = slice(h * HD, (h + 1) * HD)
            ks = krow(h)
            s_new = decay(j, h) * st_in[j, h] + kcols[ks, j:j + 1] * v[j:j + 1, vs]
            st_out[j, h] = s_new
            o_s[j:j + 1, col0 + h * HD:col0 + (h + 1) * HD] = jnp.sum(
                qcols[ks, j:j + 1] * s_new, axis=0, keepdims=True)


def _sample_mix_body(proj_ref, cos_ref, sin_ref, rgam_ref, retnw_ref, lb_ref, hnw_ref,
                     sA_ref, sB_ref, sC_ref, sD_ref, gluw_ref, glub_ref,
                     cw_ref, cb_ref, dtb_ref, alog_ref, md_ref, mnw_ref,
                     ret_in, hg_in, s5_in, m2_in, buf_in,
                     mixed_ref, ret_out, hg_out, s5_out, m2_out, buf_out, o_s):
    def pj(c0, w):
        return proj_ref[:, c0:c0 + w]

    cos = cos_ref[...]
    sin = sin_ref[...]

    def head_rows(h):
        return slice(h * HD, (h + 1) * HD)

    rq = _rotary(pj(C_RQ, W), cos, sin)
    rk = _rotary(pj(C_RK, W), cos, sin) * (HD ** -0.5)
    rv = pj(C_RV, W)
    rg = pj(C_RG, W)
    _state_step(ret_in, ret_out, o_s, 0 * W, lambda j, h: rgam_ref[:, h:h + 1],
                _cols(rk), head_rows, rv, _cols(rq))
    for h in range(NH):
        sl = head_rows(h)
        o = _head_rms(o_s[:, 0 * W + h * HD:0 * W + (h + 1) * HD]) * retnw_ref[:, sl] * _silu(rg[:, sl])
        mixed_ref[:, 0 * W + h * HD:0 * W + (h + 1) * HD] = o

    fr = pj(C_GF, W)
    lb = lb_ref[...]
    logf = _log_sigmoid(fr) + jnp.log(1.0 + lb * jnp.exp(jnp.minimum(-fr, EXP_CLIP)))
    fcols = _cols(jnp.exp(logf))
    hq = _silu(pj(C_GQ, W))
    hk = (1.0 - lb) * jax.nn.sigmoid(-fr)
    hv = pj(C_GI, W)
    gg = pj(C_GG, W)
    _state_step(hg_in, hg_out, o_s, 1 * W, lambda j, h: fcols[head_rows(h), j:j + 1],
                _cols(hk), head_rows, hv, _cols(hq))
    for h in range(NH):
        sl = head_rows(h)
        o = _head_rms(o_s[:, 1 * W + h * HD:1 * W + (h + 1) * HD]) * hnw_ref[:, sl] * _silu(gg[:, sl])
        mixed_ref[:, 1 * W + h * HD:1 * W + (h + 1) * HD] = o

    u = pj(C_SU, W)
    bu = _dot(u, sB_ref[...])
    hr, hi = s5_in[:, 0:S5N], s5_in[:, S5N:2 * S5N]
    ar, ai = sA_ref[0:1, :], sA_ref[1:2, :]
    nr = ar * hr - ai * hi + bu[:, 0:S5N]
    ni = ar * hi + ai * hr + bu[:, S5N:2 * S5N]
    s5_out[:, 0:S5N] = nr
    s5_out[:, S5N:2 * S5N] = ni
    sy = _dot(jnp.concatenate([nr, ni], axis=1), sC_ref[...]) + sD_ref[...] * u
    gy = _gelu_tanh(sy)
    os5 = gy * jax.nn.sigmoid(_dot(gy, gluw_ref[...]) + glub_ref[...]) * _silu(pj(C_SG, W))
    mixed_ref[:, 2 * W:3 * W] = os5

    xnew = pj(C_XBC, CONV_CH)
    acc = cb_ref[...] + xnew * cw_ref[CONV_K - 1:CONV_K, :]
    for i in range(CONV_K - 1):
        acc = acc + buf_in[:, i * CONV_CH:(i + 1) * CONV_CH] * cw_ref[i:i + 1, :]
    for i in range(CONV_K - 2):
        buf_out[:, i * CONV_CH:(i + 1) * CONV_CH] = buf_in[:, (i + 1) * CONV_CH:(i + 2) * CONV_CH]
    buf_out[:, (CONV_K - 2) * CONV_CH:(CONV_K - 1) * CONV_CH] = xnew
    xbc = _silu(acc)
    xm = xbc[:, 0:W]
    bm = xbc[:, W:W + 2 * HD]
    cm = xbc[:, W + 2 * HD:W + 4 * HD]
    dt = _softplus(pj(C_DT, LANE) + dtb_ref[...])
    adec = jnp.exp(dt * (-jnp.exp(alog_ref[...])))
    mz = pj(C_MZ, W)
    xdt = jnp.concatenate([xm[:, head_rows(h)] * dt[:, h:h + 1] for h in range(NH)], axis=1)
    _state_step(m2_in, m2_out, o_s, 3 * W, lambda j, h: adec[j:j + 1, h:h + 1],
                _cols(bm), lambda h: head_rows(h // 2), xdt, _cols(cm))
    ys = [o_s[:, 3 * W + h * HD:3 * W + (h + 1) * HD] + md_ref[:, h:h + 1] * xm[:, head_rows(h)]
          for h in range(NH)]
    my = jnp.concatenate(ys, axis=1) * _silu(mz)
    om = my * lax.rsqrt(jnp.mean(my * my, axis=-1, keepdims=True) + EPS) * mnw_ref[...]
    mixed_ref[:, 3 * W:4 * W] = om


def _sample_mix(l, proj, tabs, lw, ret, hg, s5, m2, buf):
    n = proj.shape[0]
    params = [tabs["cos_s"], tabs["sin_s"], tabs["rgam"], lw["ret_norm_w"], lw["lb"], lw["hgrn_norm_w"],
              lw["s5_A"], lw["s5_Bblk"], lw["s5_Cblk"], lw["s5_D"], lw["s5_glu_w"], lw["s5_glu_b"],
              lw["m2_conv_w"], lw["m2_conv_b"], lw["m2_dt_bias"], lw["m2_A_log"], lw["m2_D"],
              lw["m2_norm_w"]]
    st_spec = pl.BlockSpec((SB, NH, HD, HD), lambda i: (i, 0, 0, 0))
    st_in_spec = pl.BlockSpec((None, SB, NH, HD, HD), lambda i: (l, i, 0, 0, 0))
    s5_spec = pl.BlockSpec((SB, 2 * S5N), lambda i: (i, 0))
    buf_spec = pl.BlockSpec((SB, (CONV_K - 1) * CONV_CH), lambda i: (i, 0))
    in_specs = ([pl.BlockSpec((SB, P_PAD), lambda i: (i, 0))]
                + [_const_spec(a.shape) for a in params]
                + [st_in_spec, st_in_spec, s5_spec, st_in_spec, buf_spec])
    out_specs = [pl.BlockSpec((SB, D), lambda i: (i, 0)), st_spec, st_spec, s5_spec, st_spec, buf_spec]
    out_shape = [jax.ShapeDtypeStruct((n, D), F32),
                 jax.ShapeDtypeStruct(ret.shape[1:], F32), jax.ShapeDtypeStruct(hg.shape[1:], F32),
                 jax.ShapeDtypeStruct(s5.shape, F32), jax.ShapeDtypeStruct(m2.shape[1:], F32),
                 jax.ShapeDtypeStruct(buf.shape, F32)]
    return pl.pallas_call(
        _sample_mix_body,
        grid=(n // SB,),
        in_specs=in_specs,
        out_specs=out_specs,
        out_shape=out_shape,
        scratch_shapes=[pltpu.VMEM((SB, D), F32)],
        compiler_params=pltpu.CompilerParams(dimension_semantics=("arbitrary",),
                                             vmem_limit_bytes=VMEM_LIMIT),
        name="sample_mix",
    )(proj, *params, ret, hg, s5, m2, buf)


def _sample_out_body(last, *refs):
    if last:
        x_ref, mixed_ref, wout_ref, fnw_ref, xo_ref, y_ref = refs
    else:
        x_ref, mixed_ref, wout_ref, xo_ref = refs
    xo = x_ref[...] + jnp.dot(mixed_ref[...].astype(BF16), wout_ref[...], preferred_element_type=F32)
    xo_ref[...] = xo
    if last:
        y_ref[...] = xo * lax.rsqrt(jnp.mean(xo * xo, axis=-1, keepdims=True) + EPS) * fnw_ref[...]


def _sample_out(x, mixed, last, lw):
    n = x.shape[0]
    args = [x, mixed, lw["w_out"]] + ([lw["final_norm_w"]] if last else [])
    return pl.pallas_call(
        functools.partial(_sample_out_body, last),
        out_shape=[jax.ShapeDtypeStruct((n, D), F32)] * (2 if last else 1),
        compiler_params=pltpu.CompilerParams(vmem_limit_bytes=VMEM_LIMIT),
        name="sample_out",
    )(*args)


def kernel(x_prompt, x_sample, state_ret, state_hgrn, state_s5_re, state_s5_im, state_m2_ssm,
           state_m2_conv, norm_w, w_in, ret_norm_w, hgrn_lb_logits, hgrn_norm_w, s5_A_re, s5_A_im,
           s5_log_dt, s5_B_re, s5_B_im, s5_C_re, s5_C_im, s5_D, s5_glu_w, s5_glu_b, m2_conv_w,
           m2_conv_b, m2_dt_bias, m2_A_log, m2_D, m2_norm_w, w_out, final_norm_w):
    p = dict(norm_w=norm_w, w_in=w_in, ret_norm_w=ret_norm_w, hgrn_norm_w=hgrn_norm_w,
             s5_A_re=s5_A_re, s5_A_im=s5_A_im, s5_log_dt=s5_log_dt, s5_B_re=s5_B_re, s5_B_im=s5_B_im,
             s5_C_re=s5_C_re, s5_C_im=s5_C_im, s5_D=s5_D, s5_glu_w=s5_glu_w, s5_glu_b=s5_glu_b,
             m2_conv_w=m2_conv_w, m2_conv_b=m2_conv_b, m2_dt_bias=m2_dt_bias, m2_A_log=m2_A_log,
             m2_D=m2_D, m2_norm_w=m2_norm_w, w_out=w_out, final_norm_w=final_norm_w)
    depth = norm_w.shape[0]
    nbp, lp, _ = x_prompt.shape
    nbs = x_sample.shape[0]

    lb_sm = jax.nn.softmax(hgrn_lb_logits.astype(F32), axis=0)
    lb_all = jnp.clip(jnp.cumsum(lb_sm, axis=0) - lb_sm[0], 0.0, 1.0)

    tabs = _make_tables(lp)
    xp = x_prompt
    xs = x_sample.reshape(nbs, D)
    yp = ys = None
    pst, sst = [], []
    for l in range(depth):
        last = l == depth - 1
        lw = _layer_weights(l, lb_all, p)

        outs = _prompt_layer(xp, last, tabs, lw)
        xp = outs[0]
        if last:
            yp = outs[1]
        ret, hg, s5, m2, buf = outs[-5:]
        pst.append((ret, hg, s5[:, :S5N].reshape(nbp, S5G, S5P), s5[:, S5N:].reshape(nbp, S5G, S5P),
                    m2, buf))

        proj = _sample_proj(xs, lw)
        s5_in = jnp.concatenate([state_s5_re[l].reshape(nbs, S5N), state_s5_im[l].reshape(nbs, S5N)],
                                axis=1).astype(F32)
        mixed, ret, hg, s5, m2, buf = _sample_mix(
            l, proj, tabs, lw,
            state_ret.astype(F32), state_hgrn.astype(F32), s5_in, state_m2_ssm.astype(F32),
            state_m2_conv[l].astype(F32).reshape(nbs, (CONV_K - 1) * CONV_CH))
        souts = _sample_out(xs, mixed, last, lw)
        xs = souts[0]
        if last:
            ys = souts[1]
        sst.append((ret, hg, s5[:, :S5N].reshape(nbs, S5G, S5P), s5[:, S5N:].reshape(nbs, S5G, S5P),
                    m2, buf.reshape(nbs, CONV_K - 1, CONV_CH)))

    stk = lambda lst, i: jnp.stack([s[i] for s in lst], axis=0)
    return (yp, ys.reshape(nbs, 1, D),
            stk(pst, 0), stk(pst, 1), stk(pst, 2), stk(pst, 3), stk(pst, 4), stk(pst, 5),
            stk(sst, 0), stk(sst, 1), stk(sst, 2), stk(sst, 3), stk(sst, 4), stk(sst, 5))
```

```python
import functools
import math

import numpy as np
import jax
import jax.numpy as jnp
from jax import lax
from jax.experimental import pallas as pl
from jax.experimental.pallas import tpu as pltpu

F32 = jnp.float32
BF16 = jnp.bfloat16
HI = lax.Precision.HIGHEST

D = 1024
W = 256
NH = 4
HD = 64
S5G = 16
S5C = 16
S5P = 64
S5N = S5G * S5P
CONV_CH = 512
CONV_K = 4
TB = 64
SUB = 16
NSUB = TB // SUB
EPS = 1e-6
EXP_CLIP = 60.0
ROPE_BASE = 10000.0
PAST_LEN = 16384

C_RQ, C_RK, C_RV, C_RG = 0, 256, 512, 768
C_GQ, C_GF, C_GI, C_GG = 1024, 1280, 1536, 1792
C_SU, C_SG = 2048, 2304
C_MZ, C_XBC, C_DT = 2560, 2816, 3328
P_TOTAL = 3332
P_PAD = 3456
LANE = 128
VMEM_LIMIT = 56 * 1024 * 1024


def _silu(x):
    return x * jax.nn.sigmoid(x)


def _softplus(x):
    return jnp.maximum(x, 0.0) + jnp.log(1.0 + jnp.exp(-jnp.abs(x)))


def _log_sigmoid(x):
    return jnp.minimum(x, 0.0) - jnp.log(1.0 + jnp.exp(-jnp.abs(x)))


def _round_robin(gens):
    gens = list(gens)
    while gens:
        alive = []
        for g in gens:
            try:
                next(g)
                alive.append(g)
            except StopIteration:
                pass
        gens = alive


def _gelu_tanh(x):
    c = math.sqrt(2.0 / math.pi)
    return 0.5 * x * (1.0 + jnp.tanh(c * (x + 0.044715 * (x * x * x))))


def _dot(a, b):
    return jnp.dot(a.astype(BF16), b.astype(BF16), preferred_element_type=F32)


def _dot_nt(a, b):
    return lax.dot_general(a.astype(BF16), b.astype(BF16), (((1,), (1,)), ((), ())),
                           preferred_element_type=F32)


def _dot_tn(a, b):
    return lax.dot_general(a.astype(BF16), b.astype(BF16), (((0,), (0,)), ((), ())),
                           preferred_element_type=F32)


def _dot_hi(a, b):
    return jnp.dot(a, b, precision=HI, preferred_element_type=F32)


def _dot_tn_hi(a, b):
    return lax.dot_general(a, b, (((0,), (0,)), ((), ())), precision=HI,
                           preferred_element_type=F32)


def _rot_half_partner(x):
    lane = lax.broadcasted_iota(jnp.int32, x.shape, 1)
    first = (lane % HD) < (HD // 2)
    return jnp.where(first, pltpu.roll(x, LANE - HD // 2, 1), pltpu.roll(x, HD // 2, 1))


def _rotary(x, cos, sin_signed):
    parts = []
    for i in range(W // LANE):
        xi = x[:, i * LANE:(i + 1) * LANE]
        parts.append(xi * cos + _rot_half_partner(xi) * sin_signed)
    return jnp.concatenate(parts, axis=1)


def _head_rms(o):
    return o * lax.rsqrt(jnp.sum(o * o, axis=-1, keepdims=True) * (1.0 / o.shape[-1]) + EPS)


def _prompt_layer_body(last, *refs):
    (x_ref, cos_ref, sin_ref, normw_ref, win_ref, wout_ref,
     rdec_ref, recum_ref, rkdec_ref, retot_ref, retnw_ref,
     lb_ref, hnw_ref, sel_ref,
     sA_ref, sB_ref, sC_ref, sD_ref, gluw_ref, glub_ref,
     cw_ref, cb_ref, dtb_ref, alog_ref, md_ref, mnw_ref) = refs[:26]
    refs = refs[26:]
    if last:
        fnw_ref, xo_ref, y_ref = refs[:3]
        refs = refs[3:]
    else:
        xo_ref = refs[0]
        refs = refs[1:]
    (ret_ref, hg_ref, s5_ref, m2_ref, m2buf_ref,
     hn_s, proj_s, mixed_s, bu_s, u_s, cv_s, p_s, dg_s, hq_s, hk_s, cum_s) = refs
    j = pl.program_id(0)
    nb = x_ref.shape[0]

    @pl.when(j == 0)
    def _init():
        ret_ref[...] = jnp.zeros(ret_ref.shape, F32)
        hg_ref[...] = jnp.zeros(hg_ref.shape, F32)
        s5_ref[...] = jnp.zeros(s5_ref.shape, F32)
        m2_ref[...] = jnp.zeros(m2_ref.shape, F32)
        cv_s[...] = jnp.zeros(cv_s.shape, F32)

    ti = lax.broadcasted_iota(jnp.int32, (TB, TB), 0)
    si = lax.broadcasted_iota(jnp.int32, (TB, TB), 1)
    causal = si <= ti
    tri_l = causal.astype(F32)
    tri_u = (ti <= si).astype(F32)
    blockdiag = (ti // SUB) == (si // SUB)
    ones_tv = jnp.ones((TB, HD), F32)

    def norm_body(b, c):
        xb = x_ref[b]
        hn = xb * lax.rsqrt(jnp.mean(xb * xb, axis=-1, keepdims=True) + EPS) * normw_ref[...]
        hn_s[pl.ds(pl.multiple_of(b * TB, TB), TB), :] = hn.astype(BF16)
        return c
    lax.fori_loop(0, nb, norm_body, 0)
    NCH = 3
    cw = P_PAD // NCH
    for c in range(NCH):
        proj_s[:, c * cw:(c + 1) * cw] = jnp.dot(hn_s[...], win_ref[:, c * cw:(c + 1) * cw],
                                                 preferred_element_type=F32)

    cos = cos_ref[...]
    sin = sin_ref[...]

    def phase1(b, c):
        r0 = pl.multiple_of(b * TB, TB)

        def pj(c0, w):
            return proj_s[pl.ds(r0, TB), c0:c0 + w]

        u = pj(C_SU, W)
        for c_ in range(W // LANE):
            u_s[c_, pl.ds(b, TB, stride=nb), :] = u[:, c_ * LANE:(c_ + 1) * LANE]

        rq = _rotary(pj(C_RQ, W), cos, sin)
        rk = _rotary(pj(C_RK, W), cos, sin) * (HD ** -0.5)
        rv = pj(C_RV, W)
        rg = pj(C_RG, W)

        def ret_head(h):
            sl = slice(h * HD, (h + 1) * HD)
            qh, kh, vh = rq[:, sl], rk[:, sl], rv[:, sl]
            hst = ret_ref[b, h]
            s_raw = _dot_nt(qh, kh)
            oi = _dot(qh, hst)
            kv = _dot_tn(kh * rkdec_ref[h], vh)
            yield
            o = _dot(s_raw * rdec_ref[h], vh)
            yield
            o = o + oi * recum_ref[h]
            ret_ref[b, h] = retot_ref[h] * hst + kv
            o = _head_rms(o) * retnw_ref[:, sl] * _silu(rg[:, sl])
            mixed_s[pl.ds(r0, TB), 0 * W + h * HD:0 * W + (h + 1) * HD] = o.astype(BF16)

        cv_s[b, 8:8 + TB, :] = pj(C_XBC, CONV_CH)
        acc = cb_ref[...] + cv_s[b, 5:5 + TB, :] * cw_ref[0:1, :]
        for i in range(1, CONV_K):
            acc = acc + cv_s[b, 5 + i:5 + i + TB, :] * cw_ref[i:i + 1, :]
        tail = cv_s[b, TB + 5:TB + 8, :]
        cv_s[b, 5:8, :] = tail
        m2buf_ref[b] = tail
        xbc = _silu(acc)
        xm = xbc[:, 0:W]
        bm = xbc[:, W:W + 2 * HD]
        cm = xbc[:, W + 2 * HD:W + 4 * HD]
        dt = _softplus(pj(C_DT, LANE) + dtb_ref[...])
        la = dt * (-jnp.exp(alog_ref[...]))
        cumc = _dot_hi(tri_l, la)
        cumr = _dot_tn_hi(la, tri_u)
        totr = cumc[TB - 1:TB, :]
        ys = [None] * NH

        def m2_head(h):
            g = h // 2
            sl = slice(h * HD, (h + 1) * HD)
            gs = slice(g * HD, (g + 1) * HD)
            ch, bh, xh = cm[:, gs], bm[:, gs], xm[:, sl]
            xdt = xh * dt[:, h:h + 1]
            cc = cumc[:, h:h + 1]
            tot = totr[:, h:h + 1]
            hst = m2_ref[b, h]
            s_raw = _dot_nt(ch, bh)
            oi = _dot(ch, hst)
            kv = _dot_tn(bh * jnp.exp(tot - cc), xdt)
            decay = jnp.where(causal, jnp.exp(jnp.minimum(cc - cumr[h:h + 1, :], 0.0)), 0.0)
            yield
            o = _dot(s_raw * decay, xdt)
            yield
            m2_ref[b, h] = jnp.exp(tot) * hst + kv
            ys[h] = o + oi * jnp.exp(cc) + md_ref[:, h:h + 1] * xh

        fr = pj(C_GF, W)
        lb = lb_ref[...]
        logf = _log_sigmoid(fr) + jnp.log(1.0 + lb * jnp.exp(jnp.minimum(-fr, EXP_CLIP)))
        hq = _silu(pj(C_GQ, W))
        hk = (1.0 - lb) * jax.nn.sigmoid(-fr)
        cum = _dot_hi(tri_l, logf)
        hq_s[b] = hq
        hk_s[b] = hk
        cum_s[b] = cum
        tl = lax.broadcasted_iota(jnp.int32, (TB, LANE), 0) % SUB

        def diag_products(p):
            ls = slice(p * LANE, (p + 1) * LANE)
            hq_p = hq[:, ls]
            cum_p = cum[:, ls]
            for s_ in range(SUB):
                kb = jnp.concatenate(
                    [jnp.broadcast_to(hk_s[b, i * SUB + s_:i * SUB + s_ + 1, ls], (SUB, LANE))
                     for i in range(NSUB)], axis=0)
                cb = jnp.concatenate(
                    [jnp.broadcast_to(cum_s[b, i * SUB + s_:i * SUB + s_ + 1, ls], (SUB, LANE))
                     for i in range(NSUB)], axis=0)
                pv = jnp.where(tl >= s_, hq_p * kb * jnp.exp(jnp.minimum(cum_p - cb, 0.0)), 0.0)
                p_s[p, pl.ds(r0, TB), s_ * LANE:(s_ + 1) * LANE] = pv.astype(BF16)
                if s_ % 2 == 1:
                    yield

        _round_robin([ret_head(h) for h in range(NH)] + [m2_head(h) for h in range(NH)]
                     + [diag_products(p) for p in range(W // LANE)])

        my = jnp.concatenate(ys, axis=1) * _silu(pj(C_MZ, W))
        om = my * lax.rsqrt(jnp.mean(my * my, axis=-1, keepdims=True) + EPS) * mnw_ref[...]
        mixed_s[pl.ds(r0, TB), 3 * W:4 * W] = om.astype(BF16)
        return c
    lax.fori_loop(0, nb, phase1, 0)

    for p in range(W // LANE):
        dg_s[p] = jnp.dot(p_s[p], sel_ref[...], preferred_element_type=F32)

    u_tb = jnp.concatenate([u_s[c_] for c_ in range(W // LANE)], axis=1)
    bu_s[...] = _dot(u_tb, sB_ref[...])
    ar = jnp.broadcast_to(sA_ref[0:1, :], (nb, S5N))
    ai = jnp.broadcast_to(sA_ref[1:2, :], (nb, S5N))

    def scan_body(t, carry):
        hr, hi = carry
        row = pl.multiple_of(t * nb, nb)
        nr = ar * hr - ai * hi + bu_s[pl.ds(row, nb), 0:S5N]
        ni = ar * hi + ai * hr + bu_s[pl.ds(row, nb), S5N:2 * S5N]
        bu_s[pl.ds(row, nb), 0:S5N] = nr
        bu_s[pl.ds(row, nb), S5N:2 * S5N] = ni
        return nr, ni
    hr, hi = lax.fori_loop(0, TB, scan_body, (s5_ref[:, 0:S5N], s5_ref[:, S5N:2 * S5N]))
    s5_ref[:, 0:S5N] = hr
    s5_ref[:, S5N:2 * S5N] = hi
    ch_tb = _dot(bu_s[...], sC_ref[...])
    for c_ in range(W // LANE):
        u_s[c_] = ch_tb[:, c_ * LANE:(c_ + 1) * LANE]

    def phase2(b, c):
        r0 = pl.multiple_of(b * TB, TB)

        def pj(c0, w):
            return proj_s[pl.ds(r0, TB), c0:c0 + w]

        def s5_out():
            chs = jnp.concatenate([u_s[c_, pl.ds(b, TB, stride=nb), :] for c_ in range(W // LANE)],
                                  axis=1)
            gy = _gelu_tanh(chs + sD_ref[...] * pj(C_SU, W))
            glu = _dot(gy, gluw_ref[...])
            yield
            os5 = gy * jax.nn.sigmoid(glu + glub_ref[...]) * _silu(pj(C_SG, W))
            mixed_s[pl.ds(r0, TB), 2 * W:3 * W] = os5.astype(BF16)

        cum = cum_s[b]
        rr = jnp.concatenate(
            [jnp.zeros((SUB, W), F32)]
            + [jnp.broadcast_to(cum_s[b, i * SUB - 1:i * SUB, :], (SUB, W))
               for i in range(1, NSUB)], axis=0)
        ee = jnp.concatenate(
            [jnp.broadcast_to(cum_s[b, i * SUB + SUB - 1:i * SUB + SUB, :], (SUB, W))
             for i in range(NSUB)], axis=0)
        lastrow = lax.broadcasted_iota(jnp.int32, (TB, W), 0) == TB - 1
        totc = _dot_tn_hi(jnp.where(lastrow, cum, 0.0), ones_tv)
        hq = hq_s[b]
        hk = hk_s[b]
        hv = pj(C_GI, W)
        gg = pj(C_GG, W)
        qt = hq * jnp.exp(cum - rr)
        kh_ = hk * jnp.exp(ee - cum)
        qe = hq * jnp.exp(cum)
        tot = cum[TB - 1:TB, :]
        kend = kh_ * jnp.exp(tot - ee)
        trow = lax.broadcasted_iota(jnp.int32, (TB, W), 0) // SUB
        qx, kx = [], []
        for jb in range(NSUB - 1):
            eb = jnp.broadcast_to(ee[jb * SUB:jb * SUB + 1, :], (TB, W))
            qx.append(jnp.where(trow > jb, qt * jnp.exp(jnp.minimum(rr - eb, 0.0)), 0.0))
            kx.append(jnp.where(trow == jb, kh_, 0.0))

        def hg_head(h):
            sl = slice(h * HD, (h + 1) * HD)
            p, h2 = divmod(h, 2)
            hst = hg_ref[b, h]
            off = _dot_nt(qx[0][:, sl], kx[0][:, sl])
            for jb in range(1, NSUB - 1):
                off = off + _dot_nt(qx[jb][:, sl], kx[jb][:, sl])
            oi = _dot(qe[:, sl], hst)
            kv = _dot_tn(kend[:, sl], hv[:, sl])
            yield
            dgt = dg_s[p, pl.ds(r0, TB), h2 * HD:(h2 + 1) * HD]
            o = _dot(jnp.where(blockdiag, dgt, 0.0) + off, hv[:, sl])
            yield
            hg_ref[b, h] = jnp.exp(totc[sl, :]) * hst + kv
            o = _head_rms(o + oi) * hnw_ref[:, sl] * _silu(gg[:, sl])
            mixed_s[pl.ds(r0, TB), 1 * W + h * HD:1 * W + (h + 1) * HD] = o.astype(BF16)

        _round_robin([s5_out()] + [hg_head(h) for h in range(NH)])
        return c
    lax.fori_loop(0, nb, phase2, 0)

    res = jnp.dot(mixed_s[...], wout_ref[...], preferred_element_type=F32)
    for b in range(nb):
        xo = x_ref[b] + res[b * TB:(b + 1) * TB]
        xo_ref[b] = xo
        if last:
            y_ref[b] = xo * lax.rsqrt(jnp.mean(xo * xo, axis=-1, keepdims=True) + EPS) * fnw_ref[...]


def _const_spec(shape):
    nd = len(shape)
    return pl.BlockSpec(shape, lambda j: (0,) * nd)


def _prompt_layer(x, l, last, w):
    nb, L, _ = x.shape
    nblk = L // TB
    rows = nb * TB
    xspec = pl.BlockSpec((nb, TB, D), lambda j: (0, j, 0))
    tspec = pl.BlockSpec((TB, LANE), lambda j: (j, 0))

    def per_layer(a):
        nd = a.ndim
        return pl.BlockSpec((None,) + a.shape[1:], lambda j: (l,) + (0,) * (nd - 1))

    stacked = [w[k] for k in ("norm_w", "w_in", "w_out")]
    consts = [w[k] for k in ("rdec", "recum", "rkdec", "retot")]
    ret_nw = [w["ret_norm_w"], w["lb"], w["hgrn_norm_w"]]
    s5 = [w[k] for k in ("s5_A", "s5_Bblk", "s5_Cblk", "s5_D", "s5_glu_w", "s5_glu_b")]
    m2 = [w[k] for k in ("m2_conv_w", "m2_conv_b", "m2_dt_bias", "m2_A_log", "m2_D", "m2_norm_w")]
    args = [x, w["cos_p"], w["sin_p"]] + stacked + consts + ret_nw + [w["sel"]] + s5 + m2
    in_specs = ([xspec, tspec, tspec] + [per_layer(a) for a in stacked] + [_const_spec(a.shape) for a in consts]
                + [per_layer(a) for a in ret_nw] + [_const_spec(w["sel"].shape)]
                + [per_layer(a) for a in s5 + m2])
    if last:
        args.append(w["final_norm_w"])
        in_specs.append(_const_spec(w["final_norm_w"].shape))
    state_shapes = [jax.ShapeDtypeStruct((nb, NH, HD, HD), F32),
                    jax.ShapeDtypeStruct((nb, NH, HD, HD), F32),
                    jax.ShapeDtypeStruct((nb, 2 * S5N), F32),
                    jax.ShapeDtypeStruct((nb, NH, HD, HD), F32),
                    jax.ShapeDtypeStruct((nb, CONV_K - 1, CONV_CH), F32)]
    n_act = 2 if last else 1
    out_shape = [jax.ShapeDtypeStruct((nb, L, D), F32)] * n_act + state_shapes
    out_specs = [xspec] * n_act + [_const_spec(s.shape) for s in state_shapes]
    scratch = [pltpu.VMEM((rows, D), BF16),
               pltpu.VMEM((rows, P_PAD), F32),
               pltpu.VMEM((rows, D), BF16),
               pltpu.VMEM((rows, 2 * S5N), F32),
               pltpu.VMEM((W // LANE, rows, LANE), F32),
               pltpu.VMEM((nb, TB + 8, CONV_CH), F32),
               pltpu.VMEM((W // LANE, rows, SUB * LANE), BF16),
               pltpu.VMEM((W // LANE, rows, LANE), F32),
               pltpu.VMEM((nb, TB, W), F32),
               pltpu.VMEM((nb, TB, W), F32),
               pltpu.VMEM((nb, TB, W), F32)]
    return pl.pallas_call(
        functools.partial(_prompt_layer_body, last),
        grid=(nblk,),
        in_specs=in_specs,
        out_specs=out_specs,
        out_shape=out_shape,
        scratch_shapes=scratch,
        compiler_params=pltpu.CompilerParams(dimension_semantics=("arbitrary",),
                                             vmem_limit_bytes=VMEM_LIMIT),
        name="prompt_layer",
    )(*args)


def _rope_tables(pos):
    half = HD // 2
    inv = 1.0 / (ROPE_BASE ** (jnp.arange(half, dtype=F32) / half))
    ang = pos[:, None] * inv[None, :]
    cos = jnp.cos(ang)
    sin = jnp.sin(ang)
    cos_t = jnp.tile(cos, (1, LANE // half))
    sin_t = jnp.tile(jnp.concatenate([-sin, sin], axis=1), (1, LANE // HD))
    return cos_t, sin_t


def _retention_tables():
    log_gamma = jnp.log1p(-(2.0 ** (-5.0 - jnp.arange(NH, dtype=F32))))
    cum = jnp.cumsum(jnp.broadcast_to(log_gamma, (TB, NH)), axis=0)
    total = cum[-1]
    causal = jnp.tril(jnp.ones((TB, TB), dtype=bool))
    diff = cum[:, None, :] - cum[None, :, :]
    dec = jnp.where(causal[:, :, None], jnp.exp(jnp.where(causal[:, :, None], diff, 0.0)), 0.0)
    rdec = jnp.moveaxis(dec, 2, 0)
    recum = jnp.broadcast_to(jnp.exp(cum).T[:, :, None], (NH, TB, HD))
    rkdec = jnp.broadcast_to(jnp.exp(total[None, :] - cum).T[:, :, None], (NH, TB, HD))
    retot = jnp.broadcast_to(jnp.exp(total)[:, None, None], (NH, HD, HD))
    return rdec, recum, rkdec, retot, log_gamma


def _sel_matrix():
    sel = np.zeros((SUB, 2, HD, 2, TB), np.float32)
    for s_ in range(SUB):
        for h2 in range(2):
            sel[s_, h2, :, h2, s_::SUB] = 1.0
    return jnp.asarray(sel.reshape(SUB * LANE, LANE), dtype=BF16)


def _rows(v, width=None):
    v = v.astype(F32)
    if width is not None and v.shape[-1] < width:
        v = jnp.pad(v, ((0, 0), (0, width - v.shape[-1])))
    return v[:, None, :]


def _block_diag(blocks):
    g, r, c = blocks.shape
    eye = jnp.eye(g, dtype=blocks.dtype)
    return jnp.einsum('grc,gh->grhc', blocks, eye).reshape(g * r, g * c)


def _s5_discretise(A_re, A_im, log_dt, B_re, B_im):
    A_re, A_im = A_re.astype(F32), A_im.astype(F32)
    dt = jnp.exp(log_dt.astype(F32))[:, None]
    mag = jnp.exp(A_re * dt)
    ab_re, ab_im = mag * jnp.cos(A_im * dt), mag * jnp.sin(A_im * dt)
    nr, ni = ab_re - 1.0, ab_im
    den = A_re * A_re + A_im * A_im
    f_re = (nr * A_re + ni * A_im) / den
    f_im = (ni * A_re - nr * A_im) / den
    B_re, B_im = B_re.astype(F32), B_im.astype(F32)
    bb_re = f_re[..., None] * B_re - f_im[..., None] * B_im
    bb_im = f_re[..., None] * B_im + f_im[..., None] * B_re
    return ab_re, ab_im, bb_re, bb_im


def _s5_matrices(A_re, A_im, log_dt, B_re, B_im, C_re, C_im):
    ab_re, ab_im, bb_re, bb_im = _s5_discretise(A_re, A_im, log_dt, B_re, B_im)
    bblk = jnp.concatenate([_block_diag(jnp.swapaxes(bb_re, 1, 2)),
                            _block_diag(jnp.swapaxes(bb_im, 1, 2))], axis=1)
    cblk = jnp.concatenate([_block_diag(jnp.swapaxes(C_re.astype(F32), 1, 2)),
                            _block_diag(jnp.swapaxes(-C_im.astype(F32), 1, 2))], axis=0)
    return jnp.stack([ab_re.reshape(-1), ab_im.reshape(-1)], axis=0), bblk, cblk


R_COS, R_SIN, R_RETNW, R_LB, R_HNW, R_S5D, R_GLUB, R_MNW = [i * W for i in range(8)]
R_AR = 8 * W
R_AI = R_AR + S5N
R_DTB = R_AI + S5N
R_ALOG = R_DTB + 8
R_MD = R_ALOG + 8
N_COLP = R_MD + 8


def _prepare(p, lb_all, prompt_len):
    depth = p["norm_w"].shape[0]
    s5_A, bblk, cblk = jax.vmap(_s5_matrices)(p["s5_A_re"], p["s5_A_im"], p["s5_log_dt"], p["s5_B_re"],
                                              p["s5_B_im"], p["s5_C_re"], p["s5_C_im"])
    cos_p, sin_p = _rope_tables(jnp.arange(prompt_len, dtype=F32))
    rdec, recum, rkdec, retot, log_gamma = _retention_tables()
    cos_s, sin_s = _rope_tables(PAST_LEN + jnp.arange(1, dtype=F32))
    w = dict(
        norm_w=_rows(p["norm_w"]),
        w_in=jnp.pad(p["w_in"], ((0, 0), (0, 0), (0, P_PAD - P_TOTAL))).astype(BF16),
        w_out=p["w_out"].astype(BF16),
        ret_norm_w=_rows(p["ret_norm_w"]),
        lb=_rows(lb_all),
        hgrn_norm_w=_rows(p["hgrn_norm_w"]),
        s5_A=s5_A,
        s5_Bblk=bblk.astype(BF16),
        s5_Cblk=cblk.astype(BF16),
        s5_D=_rows(p["s5_D"]),
        s5_glu_w=p["s5_glu_w"].astype(BF16),
        s5_glu_b=_rows(p["s5_glu_b"]),
        m2_conv_w=p["m2_conv_w"].astype(F32),
        m2_conv_b=_rows(p["m2_conv_b"]),
        m2_dt_bias=_rows(p["m2_dt_bias"], LANE),
        m2_A_log=_rows(p["m2_A_log"], LANE),
        m2_D=_rows(p["m2_D"], LANE),
        m2_norm_w=_rows(p["m2_norm_w"]),
        final_norm_w=p["final_norm_w"].astype(F32).reshape(1, D),
    )
    tile2 = lambda t: jnp.broadcast_to(jnp.tile(t[0], W // LANE), (depth, W))
    colp = jnp.concatenate(
        [tile2(cos_s), tile2(sin_s), w["ret_norm_w"][:, 0], w["lb"][:, 0], w["hgrn_norm_w"][:, 0],
         w["s5_D"][:, 0], w["s5_glu_b"][:, 0], w["m2_norm_w"][:, 0], s5_A[:, 0], s5_A[:, 1],
         w["m2_dt_bias"][:, 0, :8], w["m2_A_log"][:, 0, :8], w["m2_D"][:, 0, :8]], axis=1)
    w.update(
        colp=jnp.broadcast_to(colp[:, :, None], (depth, N_COLP, LANE)),
        s5_BblkT=jnp.swapaxes(w["s5_Bblk"], 1, 2),
        s5_CblkT=jnp.swapaxes(w["s5_Cblk"], 1, 2),
        s5_glu_wT=jnp.swapaxes(w["s5_glu_w"], 1, 2),
        rgam=jnp.broadcast_to(jnp.exp(log_gamma)[:, None, None], (NH, 8, LANE)),
        cos_p=cos_p, sin_p=sin_p, rdec=rdec, recum=recum, rkdec=rkdec, retot=retot, sel=_sel_matrix(),
    )
    return w


KC = 16
NKC = HD // KC
STEPS = NH * NKC


def _rotary_cols(x, cos, sin_signed):
    half = HD // 2
    parts = []
    for h in range(NH):
        parts += [x[h * HD + half:(h + 1) * HD], x[h * HD:h * HD + half]]
    return x * cos + jnp.concatenate(parts, axis=0) * sin_signed


def _expand_rows(dst, x):
    for c in range(x.shape[0]):
        dst[c] = jnp.broadcast_to(x[c:c + 1, :], (8, LANE))


def _sample_body(n_steps, x_ref, normw_ref, win_ref, wout_ref, fnw_ref, colp_ref, sbt_ref, sct_ref,
                 gluwt_ref, cw_ref, cb_ref, rgam_ref,
                 ret_in, hg_in, m2_in, s5re_in, s5im_in, buf_in,
                 y_ref, ret_out, hg_out, m2_out, s5re_out, s5im_out, buf_out,
                 xs_s, pt_s, vt_s, ot_s, mixt_s, o_s, hp_s,
                 kret_s, qret_s, khg_s, qhg_s, ahg_s, km2_s, qm2_s):
    i = pl.program_id(0)
    r = i % STEPS
    h = r // NKC
    kc = r % NKC

    def cp(r0, n=W):
        return colp_ref[r0:r0 + n, :]

    def pc(c0, w):
        return pt_s[c0:c0 + w, :]

    @pl.when(i == 0)
    def _load():
        xs_s[...] = x_ref[...]

    @pl.when(r == 0)
    def _prep():
        x = xs_s[...]
        hn = x * lax.rsqrt(jnp.mean(x * x, axis=-1, keepdims=True) + EPS) * normw_ref[...]
        proj = jnp.dot(hn.astype(BF16), win_ref[...], preferred_element_type=F32)

        xnew = proj[:, C_XBC:C_XBC + CONV_CH]
        acc = cb_ref[...] + xnew * cw_ref[CONV_K - 1:CONV_K, :]
        for t in range(CONV_K - 1):
            acc = acc + buf_in[t] * cw_ref[t:t + 1, :]
        for t in range(CONV_K - 2):
            buf_out[t] = buf_in[t + 1]
        buf_out[CONV_K - 2] = xnew
        xbc = _silu(acc)

        for t in range(P_PAD // LANE):
            c0 = t * LANE
            if C_XBC <= c0 < C_XBC + CONV_CH:
                tile = xbc[:, c0 - C_XBC:c0 - C_XBC + LANE]
            else:
                tile = proj[:, c0:c0 + LANE]
            pt_s[c0:c0 + LANE, :] = tile.T

        cos, sin = cp(R_COS), cp(R_SIN)
        _expand_rows(kret_s, _rotary_cols(pc(C_RK, W), cos, sin) * (HD ** -0.5))
        _expand_rows(qret_s, _rotary_cols(pc(C_RQ, W), cos, sin))
        vt_s[0] = pc(C_RV, W)

        fr = pc(C_GF, W)
        lb = cp(R_LB)
        logf = _log_sigmoid(fr) + jnp.log(1.0 + lb * jnp.exp(jnp.minimum(-fr, EXP_CLIP)))
        _expand_rows(ahg_s, jnp.exp(logf))
        _expand_rows(khg_s, (1.0 - lb) * jax.nn.sigmoid(-fr))
        _expand_rows(qhg_s, _silu(pc(C_GQ, W)))
        vt_s[1] = pc(C_GI, W)

        dt8 = _softplus(pc(C_DT, 8) + cp(R_DTB, 8))
        adec8 = jnp.exp(dt8 * (-jnp.exp(cp(R_ALOG, 8))))
        for hh in range(NH):
            hp_s[hh] = jnp.broadcast_to(adec8[hh:hh + 1, :], (8, LANE))
            vt_s[2, hh * HD:(hh + 1) * HD, :] = pc(C_XBC + hh * HD, HD) * dt8[hh:hh + 1, :]
        _expand_rows(km2_s, pc(C_XBC + W, 2 * HD))
        _expand_rows(qm2_s, pc(C_XBC + W + 2 * HD, 2 * HD))

        u = pc(C_SU, W)
        bu = jnp.dot(sbt_ref[...], u.astype(BF16), preferred_element_type=F32)
        hr, hi = s5re_in[...], s5im_in[...]
        ar, ai = cp(R_AR, S5N), cp(R_AI, S5N)
        nr = ar * hr - ai * hi + bu[0:S5N]
        ni = ar * hi + ai * hr + bu[S5N:2 * S5N]
        s5re_out[...] = nr
        s5im_out[...] = ni
        hcat = jnp.concatenate([nr, ni], axis=0).astype(BF16)
        sy = jnp.dot(sct_ref[...], hcat, preferred_element_type=F32) + cp(R_S5D) * u
        gy = _gelu_tanh(sy)
        glu = jnp.dot(gluwt_ref[...], gy.astype(BF16), preferred_element_type=F32) + cp(R_GLUB)
        mixt_s[2 * W:3 * W, :] = gy * jax.nn.sigmoid(glu) * _silu(pc(C_SG, W))

    @pl.when(kc == 0)
    def _zero():
        o_s[...] = jnp.zeros(o_s.shape, F32)

    hrow = pl.multiple_of(h * HD, HD)
    cbase = h * HD + kc * KC
    gbase = (h // 2) * HD + kc * KC

    def update(m, st_in, st_out, kx, qx, base, decay):
        v3 = vt_s[m, pl.ds(hrow, HD), :].reshape(HD // 8, 8, LANE)

        def body(kk, o):
            s_new = decay(kk) * st_in[kk].reshape(HD // 8, 8, LANE) + kx[base + kk] * v3
            st_out[kk] = s_new.reshape(HD, LANE)
            return o + qx[base + kk] * s_new
        o_s[m] = lax.fori_loop(0, KC, body, o_s[m], unroll=2)

    gam = rgam_ref[h]
    update(0, ret_in, ret_out, kret_s, qret_s, cbase, lambda kk: gam)
    update(1, hg_in, hg_out, khg_s, qhg_s, cbase, lambda kk: ahg_s[cbase + kk])
    adec = hp_s[h]
    update(2, m2_in, m2_out, km2_s, qm2_s, gbase, lambda kk: adec)

    @pl.when(kc == NKC - 1)
    def _head_done():
        for m in range(3):
            ot_s[m, pl.ds(hrow, HD), :] = o_s[m].reshape(HD, LANE)

    @pl.when(r == STEPS - 1)
    def _finish():
        def head_rms_cols(o):
            parts = []
            for hh in range(NH):
                seg = o[hh * HD:(hh + 1) * HD]
                parts.append(seg * lax.rsqrt(jnp.mean(seg * seg, axis=0, keepdims=True) + EPS))
            return jnp.concatenate(parts, axis=0)

        mixt_s[0:W, :] = head_rms_cols(ot_s[0]) * cp(R_RETNW) * _silu(pc(C_RG, W))
        mixt_s[W:2 * W, :] = head_rms_cols(ot_s[1]) * cp(R_HNW) * _silu(pc(C_GG, W))
        md8 = cp(R_MD, 8)
        ym = jnp.concatenate([ot_s[2, hh * HD:(hh + 1) * HD, :] + md8[hh:hh + 1, :] * pc(C_XBC + hh * HD, HD)
                              for hh in range(NH)], axis=0)
        my = ym * _silu(pc(C_MZ, W))
        mixt_s[3 * W:4 * W, :] = my * lax.rsqrt(jnp.mean(my * my, axis=0, keepdims=True) + EPS) * cp(R_MNW)
        mixed = jnp.concatenate([mixt_s[t * LANE:(t + 1) * LANE, :].T for t in range(D // LANE)], axis=1)
        xo = xs_s[...] + jnp.dot(mixed.astype(BF16), wout_ref[...], preferred_element_type=F32)
        xs_s[...] = xo

        @pl.when(i == n_steps - 1)
        def _final_norm():
            y_ref[...] = xo * lax.rsqrt(jnp.mean(xo * xo, axis=-1, keepdims=True) + EPS) * fnw_ref[...]


def _sample_step(x, w, ret, hg, m2, s5re, s5im, buf):
    depth = ret.shape[0]
    n = x.shape[0]
    n_steps = depth * STEPS
    lay = lambda i: i // STEPS

    def per_layer(a):
        nd = a.ndim
        return pl.BlockSpec((None,) + a.shape[1:], lambda i: (lay(i),) + (0,) * (nd - 1))

    st_spec = pl.BlockSpec((None, None, KC, HD, LANE),
                           lambda i: (lay(i), (i % STEPS) // NKC, i % NKC, 0, 0))
    weights = [w["norm_w"], w["w_in"], w["w_out"]]
    tables = [w["colp"], w["s5_BblkT"], w["s5_CblkT"], w["s5_glu_wT"], w["m2_conv_w"], w["m2_conv_b"]]
    in_specs = ([_const_spec(x.shape)] + [per_layer(a) for a in weights] + [_const_spec(w["final_norm_w"].shape)]
                + [per_layer(a) for a in tables] + [_const_spec(w["rgam"].shape)]
                + [st_spec, st_spec, st_spec, per_layer(s5re), per_layer(s5im), per_layer(buf)])
    out_shape = [jax.ShapeDtypeStruct((n, D), F32)] + [jax.ShapeDtypeStruct(a.shape, F32)
                                                       for a in (ret, hg, m2, s5re, s5im, buf)]
    out_specs = [_const_spec((n, D)), st_spec, st_spec, st_spec, per_layer(s5re), per_layer(s5im),
                 per_layer(buf)]
    expand = lambda c: pltpu.VMEM((c, 8, LANE), F32)
    scratch = [pltpu.VMEM((n, D), F32),
               pltpu.VMEM((P_PAD, LANE), F32),
               pltpu.VMEM((3, W, LANE), F32),
               pltpu.VMEM((3, W, LANE), F32),
               pltpu.VMEM((D, LANE), F32),
               pltpu.VMEM((3, HD // 8, 8, LANE), F32),
               pltpu.VMEM((NH, 8, LANE), F32),
               expand(W), expand(W), expand(W), expand(W), expand(W), expand(2 * HD), expand(2 * HD)]
    return pl.pallas_call(
        functools.partial(_sample_body, n_steps),
        grid=(n_steps,),
        in_specs=in_specs,
        out_specs=out_specs,
        out_shape=out_shape,
        scratch_shapes=scratch,
        compiler_params=pltpu.CompilerParams(dimension_semantics=("arbitrary",),
                                             vmem_limit_bytes=VMEM_LIMIT),
        name="sample_step",
    )(x, *weights, w["final_norm_w"], *tables, w["rgam"], ret, hg, m2, s5re, s5im, buf)


def kernel(x_prompt, x_sample, state_ret, state_hgrn, state_s5_re, state_s5_im, state_m2_ssm,
           state_m2_conv, norm_w, w_in, ret_norm_w, hgrn_lb_logits, hgrn_norm_w, s5_A_re, s5_A_im,
           s5_log_dt, s5_B_re, s5_B_im, s5_C_re, s5_C_im, s5_D, s5_glu_w, s5_glu_b, m2_conv_w,
           m2_conv_b, m2_dt_bias, m2_A_log, m2_D, m2_norm_w, w_out, final_norm_w):
    p = dict(norm_w=norm_w, w_in=w_in, ret_norm_w=ret_norm_w, hgrn_norm_w=hgrn_norm_w,
             s5_A_re=s5_A_re, s5_A_im=s5_A_im, s5_log_dt=s5_log_dt, s5_B_re=s5_B_re, s5_B_im=s5_B_im,
             s5_C_re=s5_C_re, s5_C_im=s5_C_im, s5_D=s5_D, s5_glu_w=s5_glu_w, s5_glu_b=s5_glu_b,
             m2_conv_w=m2_conv_w, m2_conv_b=m2_conv_b, m2_dt_bias=m2_dt_bias, m2_A_log=m2_A_log,
             m2_D=m2_D, m2_norm_w=m2_norm_w, w_out=w_out, final_norm_w=final_norm_w)
    depth = norm_w.shape[0]
    nbp, lp, _ = x_prompt.shape
    nbs = x_sample.shape[0]

    lb_sm = jax.nn.softmax(hgrn_lb_logits.astype(F32), axis=0)
    lb_all = jnp.clip(jnp.cumsum(lb_sm, axis=0) - lb_sm[0], 0.0, 1.0)

    w = _prepare(p, lb_all, lp)

    xp = x_prompt
    pst = []
    for l in range(depth):
        outs = _prompt_layer(xp, l, l == depth - 1, w)
        xp = outs[0]
        ret, hg, s5, m2, buf = outs[-5:]
        pst.append((ret, hg, s5[:, :S5N].reshape(nbp, S5G, S5P), s5[:, S5N:].reshape(nbp, S5G, S5P),
                    m2, buf))
    yp = outs[1]

    seq_last = lambda a: jnp.moveaxis(a.astype(F32), 1, -1)
    ys, ret, hg, m2, s5re, s5im, buf = _sample_step(
        x_sample.reshape(nbs, D), w,
        seq_last(state_ret), seq_last(state_hgrn), seq_last(state_m2_ssm),
        seq_last(state_s5_re).reshape(depth, S5N, nbs), seq_last(state_s5_im).reshape(depth, S5N, nbs),
        jnp.swapaxes(state_m2_conv.astype(F32), 1, 2))
    seq_second = lambda a: jnp.moveaxis(a, -1, 1)

    stk = lambda i: jnp.stack([s[i] for s in pst], axis=0)
    return (yp, ys.reshape(nbs, 1, D),
            stk(0), stk(1), stk(2), stk(3), stk(4), stk(5),
            seq_second(ret), seq_second(hg), seq_second(s5re.reshape(depth, S5G, S5P, nbs)),
            seq_second(s5im.reshape(depth, S5G, S5P, nbs)), seq_second(m2), jnp.swapaxes(buf, 1, 2))
```

```python
import functools
import math

import numpy as np
import jax
import jax.numpy as jnp
from jax import lax
from jax.experimental import pallas as pl
from jax.experimental.pallas import tpu as pltpu

F32 = jnp.float32
BF16 = jnp.bfloat16
HI = lax.Precision.HIGHEST

D = 1024
W = 256
NH = 4
HD = 64
S5G = 16
S5C = 16
S5P = 64
S5N = S5G * S5P
CONV_CH = 512
CONV_K = 4
TB = 64
SUB = 16
NSUB = TB // SUB
EPS = 1e-6
EXP_CLIP = 60.0
ROPE_BASE = 10000.0
PAST_LEN = 16384

C_RQ, C_RK, C_RV, C_RG = 0, 256, 512, 768
C_GQ, C_GF, C_GI, C_GG = 1024, 1280, 1536, 1792
C_SU, C_SG = 2048, 2304
C_MZ, C_XBC, C_DT = 2560, 2816, 3328
P_TOTAL = 3332
P_PAD = 3456
LANE = 128
VMEM_LIMIT = 56 * 1024 * 1024


def _silu(x):
    return x * jax.nn.sigmoid(x)


def _softplus(x):
    return jnp.maximum(x, 0.0) + jnp.log(1.0 + jnp.exp(-jnp.abs(x)))


def _log_sigmoid(x):
    return jnp.minimum(x, 0.0) - jnp.log(1.0 + jnp.exp(-jnp.abs(x)))


def _round_robin(gens):
    gens = list(gens)
    while gens:
        alive = []
        for g in gens:
            try:
                next(g)
                alive.append(g)
            except StopIteration:
                pass
        gens = alive


def _for_sequences(nb, parts, group):
    def body(i, c):
        built = [parts(i * group + k) for k in range(group)]
        _round_robin([g for gens, _ in built for g in gens])
        for _, finish in built:
            if finish is not None:
                finish()
        return c
    lax.fori_loop(0, nb // group, body, 0)


def _gelu_tanh(x):
    c = math.sqrt(2.0 / math.pi)
    return 0.5 * x * (1.0 + jnp.tanh(c * (x + 0.044715 * (x * x * x))))


def _dot(a, b):
    return jnp.dot(a.astype(BF16), b.astype(BF16), preferred_element_type=F32)


def _dot_nt(a, b):
    return lax.dot_general(a.astype(BF16), b.astype(BF16), (((1,), (1,)), ((), ())),
                           preferred_element_type=F32)


def _dot_tn(a, b):
    return lax.dot_general(a.astype(BF16), b.astype(BF16), (((0,), (0,)), ((), ())),
                           preferred_element_type=F32)


def _dot_hi(a, b):
    return jnp.dot(a, b, precision=HI, preferred_element_type=F32)


def _dot_tn_hi(a, b):
    return lax.dot_general(a, b, (((0,), (0,)), ((), ())), precision=HI,
                           preferred_element_type=F32)


def _rot_half_partner(x):
    lane = lax.broadcasted_iota(jnp.int32, x.shape, 1)
    first = (lane % HD) < (HD // 2)
    return jnp.where(first, pltpu.roll(x, LANE - HD // 2, 1), pltpu.roll(x, HD // 2, 1))


def _rotary(x, cos, sin_signed):
    parts = []
    for i in range(W // LANE):
        xi = x[:, i * LANE:(i + 1) * LANE]
        parts.append(xi * cos + _rot_half_partner(xi) * sin_signed)
    return jnp.concatenate(parts, axis=1)


def _head_rms(o):
    return o * lax.rsqrt(jnp.sum(o * o, axis=-1, keepdims=True) * (1.0 / o.shape[-1]) + EPS)


def _prompt_layer_body(last, *refs):
    (x_ref, cos_ref, sin_ref, normw_ref, win_ref, wout_ref,
     rdec_ref, recum_ref, rkdec_ref, retot_ref, retnw_ref,
     lb_ref, hnw_ref, sel_ref,
     sA_ref, sB_ref, sC_ref, sD_ref, gluw_ref, glub_ref,
     cw_ref, cb_ref, dtb_ref, alog_ref, md_ref, mnw_ref) = refs[:26]
    refs = refs[26:]
    if last:
        fnw_ref, xo_ref, y_ref = refs[:3]
        refs = refs[3:]
    else:
        xo_ref = refs[0]
        refs = refs[1:]
    (ret_ref, hg_ref, s5_ref, m2_ref, m2buf_ref,
     hn_s, proj_s, mixed_s, bu_s, u_s, cv_s, p_s, dg_s, hq_s, hk_s, cum_s) = refs
    j = pl.program_id(0)
    nb = x_ref.shape[0]

    @pl.when(j == 0)
    def _init():
        ret_ref[...] = jnp.zeros(ret_ref.shape, F32)
        hg_ref[...] = jnp.zeros(hg_ref.shape, F32)
        s5_ref[...] = jnp.zeros(s5_ref.shape, F32)
        m2_ref[...] = jnp.zeros(m2_ref.shape, F32)
        cv_s[...] = jnp.zeros(cv_s.shape, F32)

    ti = lax.broadcasted_iota(jnp.int32, (TB, TB), 0)
    si = lax.broadcasted_iota(jnp.int32, (TB, TB), 1)
    causal = si <= ti
    tri_l = causal.astype(F32)
    tri_u = (ti <= si).astype(F32)
    blockdiag = (ti // SUB) == (si // SUB)
    ones_tv = jnp.ones((TB, HD), F32)

    def norm_body(b, c):
        xb = x_ref[b]
        hn = xb * lax.rsqrt(jnp.mean(xb * xb, axis=-1, keepdims=True) + EPS) * normw_ref[...]
        hn_s[pl.ds(pl.multiple_of(b * TB, TB), TB), :] = hn.astype(BF16)
        return c
    lax.fori_loop(0, nb, norm_body, 0)
    NCH = 3
    cw = P_PAD // NCH
    for c in range(NCH):
        proj_s[:, c * cw:(c + 1) * cw] = jnp.dot(hn_s[...], win_ref[:, c * cw:(c + 1) * cw],
                                                 preferred_element_type=F32)

    cos = cos_ref[...]
    sin = sin_ref[...]

    def phase1_parts(b):
        r0 = pl.multiple_of(b * TB, TB)

        def pj(c0, w):
            return proj_s[pl.ds(r0, TB), c0:c0 + w]

        u = pj(C_SU, W)
        for c_ in range(W // LANE):
            u_s[c_, pl.ds(b, TB, stride=nb), :] = u[:, c_ * LANE:(c_ + 1) * LANE]

        rq = _rotary(pj(C_RQ, W), cos, sin)
        rk = _rotary(pj(C_RK, W), cos, sin) * (HD ** -0.5)
        rv = pj(C_RV, W)
        rg = pj(C_RG, W)

        def ret_head(h):
            sl = slice(h * HD, (h + 1) * HD)
            qh, kh, vh = rq[:, sl], rk[:, sl], rv[:, sl]
            hst = ret_ref[b, h]
            s_raw = _dot_nt(qh, kh)
            oi = _dot(qh, hst)
            kv = _dot_tn(kh * rkdec_ref[h], vh)
            yield
            o = _dot(s_raw * rdec_ref[h], vh)
            yield
            o = o + oi * recum_ref[h]
            ret_ref[b, h] = retot_ref[h] * hst + kv
            o = _head_rms(o) * retnw_ref[:, sl] * _silu(rg[:, sl])
            mixed_s[pl.ds(r0, TB), 0 * W + h * HD:0 * W + (h + 1) * HD] = o.astype(BF16)

        cv_s[b, 8:8 + TB, :] = pj(C_XBC, CONV_CH)
        acc = cb_ref[...] + cv_s[b, 5:5 + TB, :] * cw_ref[0:1, :]
        for i in range(1, CONV_K):
            acc = acc + cv_s[b, 5 + i:5 + i + TB, :] * cw_ref[i:i + 1, :]
        tail = cv_s[b, TB + 5:TB + 8, :]
        cv_s[b, 5:8, :] = tail
        m2buf_ref[b] = tail
        xbc = _silu(acc)
        xm = xbc[:, 0:W]
        bm = xbc[:, W:W + 2 * HD]
        cm = xbc[:, W + 2 * HD:W + 4 * HD]
        dt = _softplus(pj(C_DT, LANE) + dtb_ref[...])
        la = dt * (-jnp.exp(alog_ref[...]))
        cumc = _dot_hi(tri_l, la)
        cumr = _dot_tn_hi(la, tri_u)
        totr = cumc[TB - 1:TB, :]
        ys = [None] * NH

        def m2_head(h):
            g = h // 2
            sl = slice(h * HD, (h + 1) * HD)
            gs = slice(g * HD, (g + 1) * HD)
            ch, bh, xh = cm[:, gs], bm[:, gs], xm[:, sl]
            xdt = xh * dt[:, h:h + 1]
            cc = cumc[:, h:h + 1]
            tot = totr[:, h:h + 1]
            hst = m2_ref[b, h]
            s_raw = _dot_nt(ch, bh)
            oi = _dot(ch, hst)
            kv = _dot_tn(bh * jnp.exp(tot - cc), xdt)
            decay = jnp.where(causal, jnp.exp(jnp.minimum(cc - cumr[h:h + 1, :], 0.0)), 0.0)
            yield
            o = _dot(s_raw * decay, xdt)
            yield
            m2_ref[b, h] = jnp.exp(tot) * hst + kv
            ys[h] = o + oi * jnp.exp(cc) + md_ref[:, h:h + 1] * xh

        fr = pj(C_GF, W)
        lb = lb_ref[...]
        logf = _log_sigmoid(fr) + jnp.log(1.0 + lb * jnp.exp(jnp.minimum(-fr, EXP_CLIP)))
        hq = _silu(pj(C_GQ, W))
        hk = (1.0 - lb) * jax.nn.sigmoid(-fr)
        cum = _dot_hi(tri_l, logf)
        hq_s[b] = hq
        hk_s[b] = hk
        cum_s[b] = cum
        t8 = lax.broadcasted_iota(jnp.int32, (8, LANE), 0)

        def diag_products(p):
            ls = slice(p * LANE, (p + 1) * LANE)
            for s_ in range(SUB):
                pieces = []
                for i in range(NSUB):
                    kb = jnp.broadcast_to(hk_s[b, i * SUB + s_:i * SUB + s_ + 1, ls], (8, LANE))
                    cb = jnp.broadcast_to(cum_s[b, i * SUB + s_:i * SUB + s_ + 1, ls], (8, LANE))
                    for half in range(SUB // 8):
                        rows = slice(i * SUB + half * 8, i * SUB + half * 8 + 8)
                        if half * 8 + 7 < s_:
                            pieces.append(jnp.zeros((8, LANE), F32))
                        elif half * 8 >= s_:
                            pieces.append(hq[rows, ls] * kb * jnp.exp(cum[rows, ls] - cb))
                        else:
                            e = jnp.exp(jnp.minimum(cum[rows, ls] - cb, 0.0))
                            pieces.append(jnp.where(t8 + half * 8 >= s_, hq[rows, ls] * kb * e, 0.0))
                pv = jnp.concatenate(pieces, axis=0)
                p_s[p, pl.ds(r0, TB), s_ * LANE:(s_ + 1) * LANE] = pv.astype(BF16)
                if s_ % 2 == 1:
                    yield

        def finish():
            my = jnp.concatenate(ys, axis=1) * _silu(pj(C_MZ, W))
            om = my * lax.rsqrt(jnp.mean(my * my, axis=-1, keepdims=True) + EPS) * mnw_ref[...]
            mixed_s[pl.ds(r0, TB), 3 * W:4 * W] = om.astype(BF16)

        gens = ([ret_head(h) for h in range(NH)] + [m2_head(h) for h in range(NH)]
                + [diag_products(p) for p in range(W // LANE)])
        return gens, finish

    _for_sequences(nb, phase1_parts, 1)

    for p in range(W // LANE):
        dg_s[p] = jnp.dot(p_s[p], sel_ref[...], preferred_element_type=F32)

    u_tb = jnp.concatenate([u_s[c_] for c_ in range(W // LANE)], axis=1)
    bu_s[...] = _dot(u_tb, sB_ref[...])
    ar = jnp.broadcast_to(sA_ref[0:1, :], (nb, S5N))
    ai = jnp.broadcast_to(sA_ref[1:2, :], (nb, S5N))

    def scan_body(t, carry):
        hr, hi = carry
        row = pl.multiple_of(t * nb, nb)
        nr = ar * hr - ai * hi + bu_s[pl.ds(row, nb), 0:S5N]
        ni = ar * hi + ai * hr + bu_s[pl.ds(row, nb), S5N:2 * S5N]
        bu_s[pl.ds(row, nb), 0:S5N] = nr
        bu_s[pl.ds(row, nb), S5N:2 * S5N] = ni
        return nr, ni
    hr, hi = lax.fori_loop(0, TB, scan_body, (s5_ref[:, 0:S5N], s5_ref[:, S5N:2 * S5N]))
    s5_ref[:, 0:S5N] = hr
    s5_ref[:, S5N:2 * S5N] = hi
    ch_tb = _dot(bu_s[...], sC_ref[...])
    for c_ in range(W // LANE):
        u_s[c_] = ch_tb[:, c_ * LANE:(c_ + 1) * LANE]

    def phase2_parts(b):
        r0 = pl.multiple_of(b * TB, TB)

        def pj(c0, w):
            return proj_s[pl.ds(r0, TB), c0:c0 + w]

        def s5_out():
            chs = jnp.concatenate([u_s[c_, pl.ds(b, TB, stride=nb), :] for c_ in range(W // LANE)],
                                  axis=1)
            gy = _gelu_tanh(chs + sD_ref[...] * pj(C_SU, W))
            glu = _dot(gy, gluw_ref[...])
            yield
            os5 = gy * jax.nn.sigmoid(glu + glub_ref[...]) * _silu(pj(C_SG, W))
            mixed_s[pl.ds(r0, TB), 2 * W:3 * W] = os5.astype(BF16)

        cum = cum_s[b]
        rr = jnp.concatenate(
            [jnp.zeros((SUB, W), F32)]
            + [jnp.broadcast_to(cum_s[b, i * SUB - 1:i * SUB, :], (SUB, W))
               for i in range(1, NSUB)], axis=0)
        ee = jnp.concatenate(
            [jnp.broadcast_to(cum_s[b, i * SUB + SUB - 1:i * SUB + SUB, :], (SUB, W))
             for i in range(NSUB)], axis=0)
        lastrow = lax.broadcasted_iota(jnp.int32, (TB, W), 0) == TB - 1
        totc = _dot_tn_hi(jnp.where(lastrow, cum, 0.0), ones_tv)
        hq = hq_s[b]
        hk = hk_s[b]
        hv = pj(C_GI, W)
        gg = pj(C_GG, W)
        qt = hq * jnp.exp(cum - rr)
        kh_ = hk * jnp.exp(ee - cum)
        qe = hq * jnp.exp(cum)
        tot = cum[TB - 1:TB, :]
        kend = kh_ * jnp.exp(tot - ee)
        trow = lax.broadcasted_iota(jnp.int32, (TB, W), 0) // SUB
        qx, kx = [], []
        for jb in range(NSUB - 1):
            eb = jnp.broadcast_to(ee[jb * SUB:jb * SUB + 1, :], (TB, W))
            qx.append(jnp.where(trow > jb, qt * jnp.exp(jnp.minimum(rr - eb, 0.0)), 0.0))
            kx.append(jnp.where(trow == jb, kh_, 0.0))

        def hg_head(h):
            sl = slice(h * HD, (h + 1) * HD)
            p, h2 = divmod(h, 2)
            hst = hg_ref[b, h]
            off = _dot_nt(qx[0][:, sl], kx[0][:, sl])
            for jb in range(1, NSUB - 1):
                off = off + _dot_nt(qx[jb][:, sl], kx[jb][:, sl])
            oi = _dot(qe[:, sl], hst)
            kv = _dot_tn(kend[:, sl], hv[:, sl])
            yield
            dgt = dg_s[p, pl.ds(r0, TB), h2 * HD:(h2 + 1) * HD]
            o = _dot(jnp.where(blockdiag, dgt, 0.0) + off, hv[:, sl])
            yield
            hg_ref[b, h] = jnp.exp(totc[sl, :]) * hst + kv
            o = _head_rms(o + oi) * hnw_ref[:, sl] * _silu(gg[:, sl])
            mixed_s[pl.ds(r0, TB), 1 * W + h * HD:1 * W + (h + 1) * HD] = o.astype(BF16)

        return [s5_out()] + [hg_head(h) for h in range(NH)], None

    _for_sequences(nb, phase2_parts, 2)

    res = jnp.dot(mixed_s[...], wout_ref[...], preferred_element_type=F32)
    for b in range(nb):
        xo = x_ref[b] + res[b * TB:(b + 1) * TB]
        xo_ref[b] = xo
        if last:
            y_ref[b] = xo * lax.rsqrt(jnp.mean(xo * xo, axis=-1, keepdims=True) + EPS) * fnw_ref[...]


def _const_spec(shape):
    nd = len(shape)
    return pl.BlockSpec(shape, lambda j: (0,) * nd)


def _prompt_layer(x, l, last, w):
    nb, L, _ = x.shape
    nblk = L // TB
    rows = nb * TB
    xspec = pl.BlockSpec((nb, TB, D), lambda j: (0, j, 0))
    tspec = pl.BlockSpec((TB, LANE), lambda j: (j, 0))

    def per_layer(a):
        nd = a.ndim
        return pl.BlockSpec((None,) + a.shape[1:], lambda j: (l,) + (0,) * (nd - 1))

    stacked = [w[k] for k in ("norm_w", "w_in", "w_out")]
    consts = [w[k] for k in ("rdec", "recum", "rkdec", "retot")]
    ret_nw = [w["ret_norm_w"], w["lb"], w["hgrn_norm_w"]]
    s5 = [w[k] for k in ("s5_A", "s5_Bblk", "s5_Cblk", "s5_D", "s5_glu_w", "s5_glu_b")]
    m2 = [w[k] for k in ("m2_conv_w", "m2_conv_b", "m2_dt_bias", "m2_A_log", "m2_D", "m2_norm_w")]
    args = [x, w["cos_p"], w["sin_p"]] + stacked + consts + ret_nw + [w["sel"]] + s5 + m2
    in_specs = ([xspec, tspec, tspec] + [per_layer(a) for a in stacked] + [_const_spec(a.shape) for a in consts]
                + [per_layer(a) for a in ret_nw] + [_const_spec(w["sel"].shape)]
                + [per_layer(a) for a in s5 + m2])
    if last:
        args.append(w["final_norm_w"])
        in_specs.append(_const_spec(w["final_norm_w"].shape))
    state_shapes = [jax.ShapeDtypeStruct((nb, NH, HD, HD), F32),
                    jax.ShapeDtypeStruct((nb, NH, HD, HD), F32),
                    jax.ShapeDtypeStruct((nb, 2 * S5N), F32),
                    jax.ShapeDtypeStruct((nb, NH, HD, HD), F32),
                    jax.ShapeDtypeStruct((nb, CONV_K - 1, CONV_CH), F32)]
    n_act = 2 if last else 1
    out_shape = [jax.ShapeDtypeStruct((nb, L, D), F32)] * n_act + state_shapes
    out_specs = [xspec] * n_act + [_const_spec(s.shape) for s in state_shapes]
    scratch = [pltpu.VMEM((rows, D), BF16),
               pltpu.VMEM((rows, P_PAD), F32),
               pltpu.VMEM((rows, D), BF16),
               pltpu.VMEM((rows, 2 * S5N), F32),
               pltpu.VMEM((W // LANE, rows, LANE), F32),
               pltpu.VMEM((nb, TB + 8, CONV_CH), F32),
               pltpu.VMEM((W // LANE, rows, SUB * LANE), BF16),
               pltpu.VMEM((W // LANE, rows, LANE), F32),
               pltpu.VMEM((nb, TB, W), F32),
               pltpu.VMEM((nb, TB, W), F32),
               pltpu.VMEM((nb, TB, W), F32)]
    return pl.pallas_call(
        functools.partial(_prompt_layer_body, last),
        grid=(nblk,),
        in_specs=in_specs,
        out_specs=out_specs,
        out_shape=out_shape,
        scratch_shapes=scratch,
        compiler_params=pltpu.CompilerParams(dimension_semantics=("arbitrary",),
                                             vmem_limit_bytes=VMEM_LIMIT),
        name="prompt_layer",
    )(*args)


def _rope_tables(pos):
    half = HD // 2
    inv = 1.0 / (ROPE_BASE ** (jnp.arange(half, dtype=F32) / half))
    ang = pos[:, None] * inv[None, :]
    cos = jnp.cos(ang)
    sin = jnp.sin(ang)
    cos_t = jnp.tile(cos, (1, LANE // half))
    sin_t = jnp.tile(jnp.concatenate([-sin, sin], axis=1), (1, LANE // HD))
    return cos_t, sin_t


def _retention_tables():
    log_gamma = jnp.log1p(-(2.0 ** (-5.0 - jnp.arange(NH, dtype=F32))))
    cum = jnp.cumsum(jnp.broadcast_to(log_gamma, (TB, NH)), axis=0)
    total = cum[-1]
    causal = jnp.tril(jnp.ones((TB, TB), dtype=bool))
    diff = cum[:, None, :] - cum[None, :, :]
    dec = jnp.where(causal[:, :, None], jnp.exp(jnp.where(causal[:, :, None], diff, 0.0)), 0.0)
    rdec = jnp.moveaxis(dec, 2, 0)
    recum = jnp.broadcast_to(jnp.exp(cum).T[:, :, None], (NH, TB, HD))
    rkdec = jnp.broadcast_to(jnp.exp(total[None, :] - cum).T[:, :, None], (NH, TB, HD))
    retot = jnp.broadcast_to(jnp.exp(total)[:, None, None], (NH, HD, HD))
    return rdec, recum, rkdec, retot, log_gamma


def _sel_matrix():
    sel = np.zeros((SUB, 2, HD, 2, TB), np.float32)
    for s_ in range(SUB):
        for h2 in range(2):
            sel[s_, h2, :, h2, s_::SUB] = 1.0
    return jnp.asarray(sel.reshape(SUB * LANE, LANE), dtype=BF16)


def _rows(v, width=None):
    v = v.astype(F32)
    if width is not None and v.shape[-1] < width:
        v = jnp.pad(v, ((0, 0), (0, width - v.shape[-1])))
    return v[:, None, :]


def _block_diag(blocks):
    g, r, c = blocks.shape
    eye = jnp.eye(g, dtype=blocks.dtype)
    return jnp.einsum('grc,gh->grhc', blocks, eye).reshape(g * r, g * c)


def _s5_discretise(A_re, A_im, log_dt, B_re, B_im):
    A_re, A_im = A_re.astype(F32), A_im.astype(F32)
    dt = jnp.exp(log_dt.astype(F32))[:, None]
    mag = jnp.exp(A_re * dt)
    ab_re, ab_im = mag * jnp.cos(A_im * dt), mag * jnp.sin(A_im * dt)
    nr, ni = ab_re - 1.0, ab_im
    den = A_re * A_re + A_im * A_im
    f_re = (nr * A_re + ni * A_im) / den
    f_im = (ni * A_re - nr * A_im) / den
    B_re, B_im = B_re.astype(F32), B_im.astype(F32)
    bb_re = f_re[..., None] * B_re - f_im[..., None] * B_im
    bb_im = f_re[..., None] * B_im + f_im[..., None] * B_re
    return ab_re, ab_im, bb_re, bb_im


def _s5_matrices(A_re, A_im, log_dt, B_re, B_im, C_re, C_im):
    ab_re, ab_im, bb_re, bb_im = _s5_discretise(A_re, A_im, log_dt, B_re, B_im)
    bblk = jnp.concatenate([_block_diag(jnp.swapaxes(bb_re, 1, 2)),
                            _block_diag(jnp.swapaxes(bb_im, 1, 2))], axis=1)
    cblk = jnp.concatenate([_block_diag(jnp.swapaxes(C_re.astype(F32), 1, 2)),
                            _block_diag(jnp.swapaxes(-C_im.astype(F32), 1, 2))], axis=0)
    return jnp.stack([ab_re.reshape(-1), ab_im.reshape(-1)], axis=0), bblk, cblk


R_COS, R_SIN, R_RETNW, R_LB, R_HNW, R_S5D, R_GLUB, R_MNW = [i * W for i in range(8)]
R_AR = 8 * W
R_AI = R_AR + S5N
R_DTB = R_AI + S5N
R_ALOG = R_DTB + 8
R_MD = R_ALOG + 8
N_COLP = R_MD + 8


def _prepare(p, lb_all, prompt_len):
    depth = p["norm_w"].shape[0]
    s5_A, bblk, cblk = jax.vmap(_s5_matrices)(p["s5_A_re"], p["s5_A_im"], p["s5_log_dt"], p["s5_B_re"],
                                              p["s5_B_im"], p["s5_C_re"], p["s5_C_im"])
    cos_p, sin_p = _rope_tables(jnp.arange(prompt_len, dtype=F32))
    rdec, recum, rkdec, retot, log_gamma = _retention_tables()
    cos_s, sin_s = _rope_tables(PAST_LEN + jnp.arange(1, dtype=F32))
    w = dict(
        norm_w=_rows(p["norm_w"]),
        w_in=jnp.pad(p["w_in"], ((0, 0), (0, 0), (0, P_PAD - P_TOTAL))).astype(BF16),
        w_out=p["w_out"].astype(BF16),
        ret_norm_w=_rows(p["ret_norm_w"]),
        lb=_rows(lb_all),
        hgrn_norm_w=_rows(p["hgrn_norm_w"]),
        s5_A=s5_A,
        s5_Bblk=bblk.astype(BF16),
        s5_Cblk=cblk.astype(BF16),
        s5_D=_rows(p["s5_D"]),
        s5_glu_w=p["s5_glu_w"].astype(BF16),
        s5_glu_b=_rows(p["s5_glu_b"]),
        m2_conv_w=p["m2_conv_w"].astype(F32),
        m2_conv_b=_rows(p["m2_conv_b"]),
        m2_dt_bias=_rows(p["m2_dt_bias"], LANE),
        m2_A_log=_rows(p["m2_A_log"], LANE),
        m2_D=_rows(p["m2_D"], LANE),
        m2_norm_w=_rows(p["m2_norm_w"]),
        final_norm_w=p["final_norm_w"].astype(F32).reshape(1, D),
    )
    tile2 = lambda t: jnp.broadcast_to(jnp.tile(t[0], W // LANE), (depth, W))
    colp = jnp.concatenate(
        [tile2(cos_s), tile2(sin_s), w["ret_norm_w"][:, 0], w["lb"][:, 0], w["hgrn_norm_w"][:, 0],
         w["s5_D"][:, 0], w["s5_glu_b"][:, 0], w["m2_norm_w"][:, 0], s5_A[:, 0], s5_A[:, 1],
         w["m2_dt_bias"][:, 0, :8], w["m2_A_log"][:, 0, :8], w["m2_D"][:, 0, :8]], axis=1)
    w.update(
        colp=jnp.broadcast_to(colp[:, :, None], (depth, N_COLP, LANE)),
        s5_BblkT=jnp.swapaxes(w["s5_Bblk"], 1, 2),
        s5_CblkT=jnp.swapaxes(w["s5_Cblk"], 1, 2),
        s5_glu_wT=jnp.swapaxes(w["s5_glu_w"], 1, 2),
        rgam=jnp.broadcast_to(jnp.exp(log_gamma)[:, None, None], (NH, 8, LANE)),
        cos_p=cos_p, sin_p=sin_p, rdec=rdec, recum=recum, rkdec=rkdec, retot=retot, sel=_sel_matrix(),
    )
    return w


KC = 16
NKC = HD // KC
STEPS = NH * NKC


def _rotary_cols(x, cos, sin_signed):
    half = HD // 2
    parts = []
    for h in range(NH):
        parts += [x[h * HD + half:(h + 1) * HD], x[h * HD:h * HD + half]]
    return x * cos + jnp.concatenate(parts, axis=0) * sin_signed


def _expand_rows(dst, x):
    for c in range(x.shape[0]):
        dst[c] = jnp.broadcast_to(x[c:c + 1, :], (8, LANE))


def _sample_body(n_steps, x_ref, normw_ref, win_ref, wout_ref, fnw_ref, colp_ref, sbt_ref, sct_ref,
                 gluwt_ref, cw_ref, cb_ref, rgam_ref,
                 ret_in, hg_in, m2_in, s5re_in, s5im_in, buf_in,
                 y_ref, ret_out, hg_out, m2_out, s5re_out, s5im_out, buf_out,
                 xs_s, pt_s, vt_s, ot_s, mixt_s, o_s, hp_s,
                 kret_s, qret_s, khg_s, qhg_s, ahg_s, km2_s, qm2_s):
    i = pl.program_id(0)
    r = i % STEPS
    h = r // NKC
    kc = r % NKC

    def cp(r0, n=W):
        return colp_ref[r0:r0 + n, :]

    def pc(c0, w):
        return pt_s[c0:c0 + w, :]

    @pl.when(i == 0)
    def _load():
        xs_s[...] = x_ref[...]

    @pl.when(r == 0)
    def _prep():
        x = xs_s[...]
        hn = x * lax.rsqrt(jnp.mean(x * x, axis=-1, keepdims=True) + EPS) * normw_ref[...]
        proj = jnp.dot(hn.astype(BF16), win_ref[...], preferred_element_type=F32)

        xnew = proj[:, C_XBC:C_XBC + CONV_CH]
        acc = cb_ref[...] + xnew * cw_ref[CONV_K - 1:CONV_K, :]
        for t in range(CONV_K - 1):
            acc = acc + buf_in[t] * cw_ref[t:t + 1, :]
        for t in range(CONV_K - 2):
            buf_out[t] = buf_in[t + 1]
        buf_out[CONV_K - 2] = xnew
        xbc = _silu(acc)

        for t in range(P_PAD // LANE):
            c0 = t * LANE
            if C_XBC <= c0 < C_XBC + CONV_CH:
                tile = xbc[:, c0 - C_XBC:c0 - C_XBC + LANE]
            else:
                tile = proj[:, c0:c0 + LANE]
            pt_s[c0:c0 + LANE, :] = tile.T

        cos, sin = cp(R_COS), cp(R_SIN)
        _expand_rows(kret_s, _rotary_cols(pc(C_RK, W), cos, sin) * (HD ** -0.5))
        _expand_rows(qret_s, _rotary_cols(pc(C_RQ, W), cos, sin))
        vt_s[0] = pc(C_RV, W)

        fr = pc(C_GF, W)
        lb = cp(R_LB)
        logf = _log_sigmoid(fr) + jnp.log(1.0 + lb * jnp.exp(jnp.minimum(-fr, EXP_CLIP)))
        _expand_rows(ahg_s, jnp.exp(logf))
        _expand_rows(khg_s, (1.0 - lb) * jax.nn.sigmoid(-fr))
        _expand_rows(qhg_s, _silu(pc(C_GQ, W)))
        vt_s[1] = pc(C_GI, W)

        dt8 = _softplus(pc(C_DT, 8) + cp(R_DTB, 8))
        adec8 = jnp.exp(dt8 * (-jnp.exp(cp(R_ALOG, 8))))
        for hh in range(NH):
            hp_s[hh] = jnp.broadcast_to(adec8[hh:hh + 1, :], (8, LANE))
            vt_s[2, hh * HD:(hh + 1) * HD, :] = pc(C_XBC + hh * HD, HD) * dt8[hh:hh + 1, :]
        _expand_rows(km2_s, pc(C_XBC + W, 2 * HD))
        _expand_rows(qm2_s, pc(C_XBC + W + 2 * HD, 2 * HD))

        u = pc(C_SU, W)
        bu = jnp.dot(sbt_ref[...], u.astype(BF16), preferred_element_type=F32)
        hr, hi = s5re_in[...], s5im_in[...]
        ar, ai = cp(R_AR, S5N), cp(R_AI, S5N)
        nr = ar * hr - ai * hi + bu[0:S5N]
        ni = ar * hi + ai * hr + bu[S5N:2 * S5N]
        s5re_out[...] = nr
        s5im_out[...] = ni
        hcat = jnp.concatenate([nr, ni], axis=0).astype(BF16)
        sy = jnp.dot(sct_ref[...], hcat, preferred_element_type=F32) + cp(R_S5D) * u
        gy = _gelu_tanh(sy)
        glu = jnp.dot(gluwt_ref[...], gy.astype(BF16), preferred_element_type=F32) + cp(R_GLUB)
        mixt_s[2 * W:3 * W, :] = gy * jax.nn.sigmoid(glu) * _silu(pc(C_SG, W))

    @pl.when(kc == 0)
    def _zero():
        o_s[...] = jnp.zeros(o_s.shape, F32)

    hrow = pl.multiple_of(h * HD, HD)
    cbase = h * HD + kc * KC
    gbase = (h // 2) * HD + kc * KC

    def update(m, st_in, st_out, kx, qx, base, decay):
        v3 = vt_s[m, pl.ds(hrow, HD), :].reshape(HD // 8, 8, LANE)

        def body(kk, o):
            s_new = decay(kk) * st_in[kk].reshape(HD // 8, 8, LANE) + kx[base + kk] * v3
            st_out[kk] = s_new.reshape(HD, LANE)
            return o + qx[base + kk] * s_new
        o_s[m] = lax.fori_loop(0, KC, body, o_s[m], unroll=2)

    gam = rgam_ref[h]
    update(0, ret_in, ret_out, kret_s, qret_s, cbase, lambda kk: gam)
    update(1, hg_in, hg_out, khg_s, qhg_s, cbase, lambda kk: ahg_s[cbase + kk])
    adec = hp_s[h]
    update(2, m2_in, m2_out, km2_s, qm2_s, gbase, lambda kk: adec)

    @pl.when(kc == NKC - 1)
    def _head_done():
        for m in range(3):
            ot_s[m, pl.ds(hrow, HD), :] = o_s[m].reshape(HD, LANE)

    @pl.when(r == STEPS - 1)
    def _finish():
        def head_rms_cols(o):
            parts = []
            for hh in range(NH):
                seg = o[hh * HD:(hh + 1) * HD]
                parts.append(seg * lax.rsqrt(jnp.mean(seg * seg, axis=0, keepdims=True) + EPS))
            return jnp.concatenate(parts, axis=0)

        mixt_s[0:W, :] = head_rms_cols(ot_s[0]) * cp(R_RETNW) * _silu(pc(C_RG, W))
        mixt_s[W:2 * W, :] = head_rms_cols(ot_s[1]) * cp(R_HNW) * _silu(pc(C_GG, W))
        md8 = cp(R_MD, 8)
        ym = jnp.concatenate([ot_s[2, hh * HD:(hh + 1) * HD, :] + md8[hh:hh + 1, :] * pc(C_XBC + hh * HD, HD)
                              for hh in range(NH)], axis=0)
        my = ym * _silu(pc(C_MZ, W))
        mixt_s[3 * W:4 * W, :] = my * lax.rsqrt(jnp.mean(my * my, axis=0, keepdims=True) + EPS) * cp(R_MNW)
        mixed = jnp.concatenate([mixt_s[t * LANE:(t + 1) * LANE, :].T for t in range(D // LANE)], axis=1)
        xo = xs_s[...] + jnp.dot(mixed.astype(BF16), wout_ref[...], preferred_element_type=F32)
        xs_s[...] = xo

        @pl.when(i == n_steps - 1)
        def _final_norm():
            y_ref[...] = xo * lax.rsqrt(jnp.mean(xo * xo, axis=-1, keepdims=True) + EPS) * fnw_ref[...]


def _sample_step(x, w, ret, hg, m2, s5re, s5im, buf):
    depth = ret.shape[0]
    n = x.shape[0]
    n_steps = depth * STEPS
    lay = lambda i: i // STEPS

    def per_layer(a):
        nd = a.ndim
        return pl.BlockSpec((None,) + a.shape[1:], lambda i: (lay(i),) + (0,) * (nd - 1))

    st_spec = pl.BlockSpec((None, None, KC, HD, LANE),
                           lambda i: (lay(i), (i % STEPS) // NKC, i % NKC, 0, 0))
    weights = [w["norm_w"], w["w_in"], w["w_out"]]
    tables = [w["colp"], w["s5_BblkT"], w["s5_CblkT"], w["s5_glu_wT"], w["m2_conv_w"], w["m2_conv_b"]]
    in_specs = ([_const_spec(x.shape)] + [per_layer(a) for a in weights] + [_const_spec(w["final_norm_w"].shape)]
                + [per_layer(a) for a in tables] + [_const_spec(w["rgam"].shape)]
                + [st_spec, st_spec, st_spec, per_layer(s5re), per_layer(s5im), per_layer(buf)])
    out_shape = [jax.ShapeDtypeStruct((n, D), F32)] + [jax.ShapeDtypeStruct(a.shape, F32)
                                                       for a in (ret, hg, m2, s5re, s5im, buf)]
    out_specs = [_const_spec((n, D)), st_spec, st_spec, st_spec, per_layer(s5re), per_layer(s5im),
                 per_layer(buf)]
    expand = lambda c: pltpu.VMEM((c, 8, LANE), F32)
    scratch = [pltpu.VMEM((n, D), F32),
               pltpu.VMEM((P_PAD, LANE), F32),
               pltpu.VMEM((3, W, LANE), F32),
               pltpu.VMEM((3, W, LANE), F32),
               pltpu.VMEM((D, LANE), F32),
               pltpu.VMEM((3, HD // 8, 8, LANE), F32),
               pltpu.VMEM((NH, 8, LANE), F32),
               expand(W), expand(W), expand(W), expand(W), expand(W), expand(2 * HD), expand(2 * HD)]
    return pl.pallas_call(
        functools.partial(_sample_body, n_steps),
        grid=(n_steps,),
        in_specs=in_specs,
        out_specs=out_specs,
        out_shape=out_shape,
        scratch_shapes=scratch,
        compiler_params=pltpu.CompilerParams(dimension_semantics=("arbitrary",),
                                             vmem_limit_bytes=VMEM_LIMIT),
        name="sample_step",
    )(x, *weights, w["final_norm_w"], *tables, w["rgam"], ret, hg, m2, s5re, s5im, buf)


def kernel(x_prompt, x_sample, state_ret, state_hgrn, state_s5_re, state_s5_im, state_m2_ssm,
           state_m2_conv, norm_w, w_in, ret_norm_w, hgrn_lb_logits, hgrn_norm_w, s5_A_re, s5_A_im,
           s5_log_dt, s5_B_re, s5_B_im, s5_C_re, s5_C_im, s5_D, s5_glu_w, s5_glu_b, m2_conv_w,
           m2_conv_b, m2_dt_bias, m2_A_log, m2_D, m2_norm_w, w_out, final_norm_w):
    p = dict(norm_w=norm_w, w_in=w_in, ret_norm_w=ret_norm_w, hgrn_norm_w=hgrn_norm_w,
             s5_A_re=s5_A_re, s5_A_im=s5_A_im, s5_log_dt=s5_log_dt, s5_B_re=s5_B_re, s5_B_im=s5_B_im,
             s5_C_re=s5_C_re, s5_C_im=s5_C_im, s5_D=s5_D, s5_glu_w=s5_glu_w, s5_glu_b=s5_glu_b,
             m2_conv_w=m2_conv_w, m2_conv_b=m2_conv_b, m2_dt_bias=m2_dt_bias, m2_A_log=m2_A_log,
             m2_D=m2_D, m2_norm_w=m2_norm_w, w_out=w_out, final_norm_w=final_norm_w)
    depth = norm_w.shape[0]
    nbp, lp, _ = x_prompt.shape
    nbs = x_sample.shape[0]

    lb_sm = jax.nn.softmax(hgrn_lb_logits.astype(F32), axis=0)
    lb_all = jnp.clip(jnp.cumsum(lb_sm, axis=0) - lb_sm[0], 0.0, 1.0)

    w = _prepare(p, lb_all, lp)

    xp = x_prompt
    pst = []
    for l in range(depth):
        outs = _prompt_layer(xp, l, l == depth - 1, w)
        xp = outs[0]
        ret, hg, s5, m2, buf = outs[-5:]
        pst.append((ret, hg, s5[:, :S5N].reshape(nbp, S5G, S5P), s5[:, S5N:].reshape(nbp, S5G, S5P),
                    m2, buf))
    yp = outs[1]

    seq_last = lambda a: jnp.moveaxis(a.astype(F32), 1, -1)
    ys, ret, hg, m2, s5re, s5im, buf = _sample_step(
        x_sample.reshape(nbs, D), w,
        seq_last(state_ret), seq_last(state_hgrn), seq_last(state_m2_ssm),
        seq_last(state_s5_re).reshape(depth, S5N, nbs), seq_last(state_s5_im).reshape(depth, S5N, nbs),
        jnp.swapaxes(state_m2_conv.astype(F32), 1, 2))
    seq_second = lambda a: jnp.moveaxis(a, -1, 1)

    stk = lambda i: jnp.stack([s[i] for s in pst], axis=0)
    return (yp, ys.reshape(nbs, 1, D),
            stk(0), stk(1), stk(2), stk(3), stk(4), stk(5),
            seq_second(ret), seq_second(hg), seq_second(s5re.reshape(depth, S5G, S5P, nbs)),
            seq_second(s5im.reshape(depth, S5G, S5P, nbs)), seq_second(m2), jnp.swapaxes(buf, 1, 2))
```

```python
import functools
import math

import numpy as np
import jax
import jax.numpy as jnp
from jax import lax
from jax.experimental import pallas as pl
from jax.experimental.pallas import tpu as pltpu

F32 = jnp.float32
BF16 = jnp.bfloat16
HI = lax.Precision.HIGHEST

D = 1024
W = 256
NH = 4
HD = 64
S5G = 16
S5C = 16
S5P = 64
S5N = S5G * S5P
CONV_CH = 512
CONV_K = 4
TB = 64
SUB = 16
NSUB = TB // SUB
EPS = 1e-6
EXP_CLIP = 60.0
ROPE_BASE = 10000.0
PAST_LEN = 16384

C_RQ, C_RK, C_RV, C_RG = 0, 256, 512, 768
C_GQ, C_GF, C_GI, C_GG = 1024, 1280, 1536, 1792
C_SU, C_SG = 2048, 2304
C_MZ, C_XBC, C_DT = 2560, 2816, 3328
P_TOTAL = 3332
P_PAD = 3456
LANE = 128
VMEM_LIMIT = 56 * 1024 * 1024


def _silu(x):
    return x * jax.nn.sigmoid(x)


def _softplus(x):
    return jnp.maximum(x, 0.0) + jnp.log(1.0 + jnp.exp(-jnp.abs(x)))


def _log_sigmoid(x):
    return jnp.minimum(x, 0.0) - jnp.log(1.0 + jnp.exp(-jnp.abs(x)))


def _round_robin(gens):
    gens = list(gens)
    while gens:
        alive = []
        for g in gens:
            try:
                next(g)
                alive.append(g)
            except StopIteration:
                pass
        gens = alive


def _for_sequences(nb, parts, group):
    def body(i, c):
        built = [parts(i * group + k) for k in range(group)]
        _round_robin([g for gens, _ in built for g in gens])
        for _, finish in built:
            if finish is not None:
                finish()
        return c
    lax.fori_loop(0, nb // group, body, 0)


def _gelu_tanh(x):
    c = math.sqrt(2.0 / math.pi)
    return 0.5 * x * (1.0 + jnp.tanh(c * (x + 0.044715 * (x * x * x))))


def _dot(a, b):
    return jnp.dot(a.astype(BF16), b.astype(BF16), preferred_element_type=F32)


def _dot_nt(a, b):
    return lax.dot_general(a.astype(BF16), b.astype(BF16), (((1,), (1,)), ((), ())),
                           preferred_element_type=F32)


def _dot_tn(a, b):
    return lax.dot_general(a.astype(BF16), b.astype(BF16), (((0,), (0,)), ((), ())),
                           preferred_element_type=F32)


def _dot_hi(a, b):
    return jnp.dot(a, b, precision=HI, preferred_element_type=F32)


def _dot_tn_hi(a, b):
    return lax.dot_general(a, b, (((0,), (0,)), ((), ())), precision=HI,
                           preferred_element_type=F32)


def _rot_half_partner(x):
    lane = lax.broadcasted_iota(jnp.int32, x.shape, 1)
    first = (lane % HD) < (HD // 2)
    return jnp.where(first, pltpu.roll(x, LANE - HD // 2, 1), pltpu.roll(x, HD // 2, 1))


def _rotary(x, cos, sin_signed):
    parts = []
    for i in range(W // LANE):
        xi = x[:, i * LANE:(i + 1) * LANE]
        parts.append(xi * cos + _rot_half_partner(xi) * sin_signed)
    return jnp.concatenate(parts, axis=1)


def _head_rms(o):
    return o * lax.rsqrt(jnp.sum(o * o, axis=-1, keepdims=True) * (1.0 / o.shape[-1]) + EPS)


def _prompt_layer_body(last, *refs):
    (x_ref, cos_ref, sin_ref, normw_ref, win_ref, wout_ref,
     rdec_ref, recum_ref, rkdec_ref, retot_ref, retnw_ref,
     lb_ref, hnw_ref, sel_ref,
     sA_ref, sB_ref, sC_ref, sD_ref, gluw_ref, glub_ref,
     cw_ref, cb_ref, dtb_ref, alog_ref, md_ref, mnw_ref, bones_ref, hexp_ref) = refs[:28]
    refs = refs[28:]
    if last:
        fnw_ref, xo_ref, y_ref = refs[:3]
        refs = refs[3:]
    else:
        xo_ref = refs[0]
        refs = refs[1:]
    (ret_ref, hg_ref, s5_ref, m2_ref, m2buf_ref,
     hn_s, proj_s, mixed_s, bu_s, u_s, cv_s, p_s, dg_s, hq_s, hk_s, cum_s, hret_s, hm2_s) = refs
    j = pl.program_id(0)
    nb = x_ref.shape[0]

    @pl.when(j == 0)
    def _init():
        hret_s[...] = jnp.zeros(hret_s.shape, F32)
        hg_ref[...] = jnp.zeros(hg_ref.shape, F32)
        s5_ref[...] = jnp.zeros(s5_ref.shape, F32)
        hm2_s[...] = jnp.zeros(hm2_s.shape, F32)
        cv_s[...] = jnp.zeros(cv_s.shape, F32)

    ti = lax.broadcasted_iota(jnp.int32, (TB, TB), 0)
    si = lax.broadcasted_iota(jnp.int32, (TB, TB), 1)
    causal = si <= ti
    tri_l = causal.astype(F32)
    tri_u = (ti <= si).astype(F32)
    blockdiag = (ti // SUB) == (si // SUB)
    ones_tv = jnp.ones((TB, HD), F32)
    ones_tt = jnp.ones((TB, TB), F32)
    pr = lax.broadcasted_iota(jnp.int32, (LANE, LANE), 0)
    pc_ = lax.broadcasted_iota(jnp.int32, (LANE, LANE), 1)
    pairmask = (pr // HD) == (pc_ // HD)
    t2 = lax.broadcasted_iota(jnp.int32, (TB, LANE), 0)
    l2 = lax.broadcasted_iota(jnp.int32, (TB, LANE), 1)
    causal2 = (l2 % HD) <= t2
    tri_u2 = (t2 <= (l2 % HD)).astype(F32)
    first_head = l2 < HD

    def dup_t(x2):
        return jnp.concatenate([x2, x2], axis=0).T

    def blockdiag2(x2):
        return jnp.where(pairmask, jnp.concatenate([x2, x2], axis=0), 0.0)

    def norm_body(b, c):
        xb = x_ref[b]
        hn = xb * lax.rsqrt(jnp.mean(xb * xb, axis=-1, keepdims=True) + EPS) * normw_ref[...]
        hn_s[pl.ds(pl.multiple_of(b * TB, TB), TB), :] = hn.astype(BF16)
        return c
    lax.fori_loop(0, nb, norm_body, 0)
    NCH = 3
    cw = P_PAD // NCH
    for c in range(NCH):
        proj_s[:, c * cw:(c + 1) * cw] = jnp.dot(hn_s[...], win_ref[:, c * cw:(c + 1) * cw],
                                                 preferred_element_type=F32)

    cos = cos_ref[...]
    sin = sin_ref[...]

    def phase1_parts(b):
        r0 = pl.multiple_of(b * TB, TB)

        def pj(c0, w):
            return proj_s[pl.ds(r0, TB), c0:c0 + w]

        u = pj(C_SU, W)
        for c_ in range(W // LANE):
            u_s[c_, pl.ds(b, TB, stride=nb), :] = u[:, c_ * LANE:(c_ + 1) * LANE]

        rq = _rotary(pj(C_RQ, W), cos, sin)
        rk = _rotary(pj(C_RK, W), cos, sin) * (HD ** -0.5)
        rv = pj(C_RV, W)
        rg = pj(C_RG, W)

        def ret_pair(p):
            ls = slice(p * LANE, (p + 1) * LANE)
            q2, k2, v2 = rq[:, ls], rk[:, ls], rv[:, ls]
            hb = hret_s[b, p]
            kt = dup_t(k2)
            s_raw = _dot(q2, jnp.where(pairmask, kt, 0.0))
            oi = _dot(q2, hb)
            kv = _dot(kt[:, 0:HD], v2 * rkdec_ref[p])
            yield
            o = _dot(s_raw * rdec_ref[p], blockdiag2(v2))
            yield
            o = o + oi * recum_ref[p]
            hret_s[b, p] = jnp.where(pairmask, retot_ref[p] * hb + kv, 0.0)
            ms = _dot(o * o, bones_ref[...])
            yield
            o = o * lax.rsqrt(ms + EPS) * retnw_ref[:, ls] * _silu(rg[:, ls])
            mixed_s[pl.ds(r0, TB), 0 * W + p * LANE:0 * W + (p + 1) * LANE] = o.astype(BF16)

        cv_s[b, 8:8 + TB, :] = pj(C_XBC, CONV_CH)
        acc = cb_ref[...] + cv_s[b, 5:5 + TB, :] * cw_ref[0:1, :]
        for i in range(1, CONV_K):
            acc = acc + cv_s[b, 5 + i:5 + i + TB, :] * cw_ref[i:i + 1, :]
        tail = cv_s[b, TB + 5:TB + 8, :]
        cv_s[b, 5:8, :] = tail
        m2buf_ref[b] = tail
        xbc = _silu(acc)
        xm = xbc[:, 0:W]
        bm = xbc[:, W:W + 2 * HD]
        cm = xbc[:, W + 2 * HD:W + 4 * HD]
        bm_sw = pltpu.roll(bm, HD, 1)
        cm_sw = pltpu.roll(cm, HD, 1)
        dt_b = _softplus(_dot_hi(pj(C_DT, LANE), hexp_ref[...]) + dtb_ref[...])
        la_b = dt_b * (-jnp.exp(alog_ref[...]))
        cum_b = _dot_hi(tri_l, la_b)
        ys = [None] * (W // LANE)

        def m2_pair(p):
            ls = slice(p * LANE, (p + 1) * LANE)
            b2 = jnp.where(first_head, bm, bm_sw) if p == 0 else jnp.where(first_head, bm_sw, bm)
            c2 = jnp.where(first_head, cm, cm_sw) if p == 0 else jnp.where(first_head, cm_sw, cm)
            x2, dt2, cum2 = xm[:, ls], dt_b[:, ls], cum_b[:, ls]
            xdt2 = x2 * dt2
            r2 = _dot_hi(ones_tt, la_b[:, ls] * tri_u2)
            tot2 = cum2[TB - 1:TB, :]
            hb = hm2_s[b, p]
            bt = dup_t(b2)
            s_raw = _dot(c2, jnp.where(pairmask, bt, 0.0))
            oi = _dot(c2, hb)
            kv = _dot(bt[:, 0:HD], xdt2 * jnp.exp(tot2 - cum2))
            decay = jnp.where(causal2, jnp.exp(jnp.minimum(cum2 - r2, 0.0)), 0.0)
            yield
            o = _dot(s_raw * decay, blockdiag2(xdt2))
            yield
            hm2_s[b, p] = jnp.where(pairmask, jnp.exp(tot2) * hb + kv, 0.0)
            ys[p] = o + oi * jnp.exp(cum2) + md_ref[:, ls] * x2

        fr = pj(C_GF, W)
        lb = lb_ref[...]
        logf = _log_sigmoid(fr) + jnp.log(1.0 + lb * jnp.exp(jnp.minimum(-fr, EXP_CLIP)))
        hq = _silu(pj(C_GQ, W))
        hk = (1.0 - lb) * jax.nn.sigmoid(-fr)
        cum = _dot_hi(tri_l, logf)
        hq_s[b] = hq
        hk_s[b] = hk
        cum_s[b] = cum
        t8 = lax.broadcasted_iota(jnp.int32, (8, LANE), 0)

        def diag_products(p):
            ls = slice(p * LANE, (p + 1) * LANE)
            for s_ in range(SUB):
                pieces = []
                for i in range(NSUB):
                    kb = jnp.broadcast_to(hk_s[b, i * SUB + s_:i * SUB + s_ + 1, ls], (8, LANE))
                    cb = jnp.broadcast_to(cum_s[b, i * SUB + s_:i * SUB + s_ + 1, ls], (8, LANE))
                    for half in range(SUB // 8):
                        rows = slice(i * SUB + half * 8, i * SUB + half * 8 + 8)
                        if half * 8 + 7 < s_:
                            pieces.append(jnp.zeros((8, LANE), F32))
                        elif half * 8 >= s_:
                            pieces.append(hq[rows, ls] * kb * jnp.exp(cum[rows, ls] - cb))
                        else:
                            e = jnp.exp(jnp.minimum(cum[rows, ls] - cb, 0.0))
                            pieces.append(jnp.where(t8 + half * 8 >= s_, hq[rows, ls] * kb * e, 0.0))
                pv = jnp.concatenate(pieces, axis=0)
                p_s[p, pl.ds(r0, TB), s_ * LANE:(s_ + 1) * LANE] = pv.astype(BF16)
                if s_ % 2 == 1:
                    yield

        def finish():
            my = jnp.concatenate(ys, axis=1) * _silu(pj(C_MZ, W))
            om = my * lax.rsqrt(jnp.mean(my * my, axis=-1, keepdims=True) + EPS) * mnw_ref[...]
            mixed_s[pl.ds(r0, TB), 3 * W:4 * W] = om.astype(BF16)

        gens = ([ret_pair(p) for p in range(W // LANE)] + [m2_pair(p) for p in range(W // LANE)]
                + [diag_products(p) for p in range(W // LANE)])
        return gens, finish

    _for_sequences(nb, phase1_parts, 1)

    for p in range(W // LANE):
        dg_s[p] = jnp.dot(p_s[p], sel_ref[...], preferred_element_type=F32)

    u_tb = jnp.concatenate([u_s[c_] for c_ in range(W // LANE)], axis=1)
    bu_s[...] = _dot(u_tb, sB_ref[...])
    ar = jnp.broadcast_to(sA_ref[0:1, :], (nb, S5N))
    ai = jnp.broadcast_to(sA_ref[1:2, :], (nb, S5N))

    def scan_body(t, carry):
        hr, hi = carry
        row = pl.multiple_of(t * nb, nb)
        nr = ar * hr - ai * hi + bu_s[pl.ds(row, nb), 0:S5N]
        ni = ar * hi + ai * hr + bu_s[pl.ds(row, nb), S5N:2 * S5N]
        bu_s[pl.ds(row, nb), 0:S5N] = nr
        bu_s[pl.ds(row, nb), S5N:2 * S5N] = ni
        return nr, ni
    hr, hi = lax.fori_loop(0, TB, scan_body, (s5_ref[:, 0:S5N], s5_ref[:, S5N:2 * S5N]))
    s5_ref[:, 0:S5N] = hr
    s5_ref[:, S5N:2 * S5N] = hi
    ch_tb = _dot(bu_s[...], sC_ref[...])
    for c_ in range(W // LANE):
        u_s[c_] = ch_tb[:, c_ * LANE:(c_ + 1) * LANE]

    def phase2_parts(b):
        r0 = pl.multiple_of(b * TB, TB)

        def pj(c0, w):
            return proj_s[pl.ds(r0, TB), c0:c0 + w]

        def s5_out():
            chs = jnp.concatenate([u_s[c_, pl.ds(b, TB, stride=nb), :] for c_ in range(W // LANE)],
                                  axis=1)
            gy = _gelu_tanh(chs + sD_ref[...] * pj(C_SU, W))
            glu = _dot(gy, gluw_ref[...])
            yield
            os5 = gy * jax.nn.sigmoid(glu + glub_ref[...]) * _silu(pj(C_SG, W))
            mixed_s[pl.ds(r0, TB), 2 * W:3 * W] = os5.astype(BF16)

        cum = cum_s[b]
        rr = jnp.concatenate(
            [jnp.zeros((SUB, W), F32)]
            + [jnp.broadcast_to(cum_s[b, i * SUB - 1:i * SUB, :], (SUB, W))
               for i in range(1, NSUB)], axis=0)
        ee = jnp.concatenate(
            [jnp.broadcast_to(cum_s[b, i * SUB + SUB - 1:i * SUB + SUB, :], (SUB, W))
             for i in range(NSUB)], axis=0)
        lastrow = lax.broadcasted_iota(jnp.int32, (TB, W), 0) == TB - 1
        totc = _dot_tn_hi(jnp.where(lastrow, cum, 0.0), ones_tv)
        hq = hq_s[b]
        hk = hk_s[b]
        hv = pj(C_GI, W)
        gg = pj(C_GG, W)
        qt = hq * jnp.exp(cum - rr)
        kh_ = hk * jnp.exp(ee - cum)
        qe = hq * jnp.exp(cum)
        tot = cum[TB - 1:TB, :]
        kend = kh_ * jnp.exp(tot - ee)
        trow = lax.broadcasted_iota(jnp.int32, (TB, W), 0) // SUB
        qx, kx = [], []
        for jb in range(NSUB - 1):
            eb = jnp.broadcast_to(ee[jb * SUB:jb * SUB + 1, :], (TB, W))
            qx.append(jnp.where(trow > jb, qt * jnp.exp(jnp.minimum(rr - eb, 0.0)), 0.0))
            kx.append(jnp.where(trow == jb, kh_, 0.0))

        def hg_head(h):
            sl = slice(h * HD, (h + 1) * HD)
            p, h2 = divmod(h, 2)
            hst = hg_ref[b, h]
            off = _dot_nt(qx[0][:, sl], kx[0][:, sl])
            for jb in range(1, NSUB - 1):
                off = off + _dot_nt(qx[jb][:, sl], kx[jb][:, sl])
            oi = _dot(qe[:, sl], hst)
            kv = _dot_tn(kend[:, sl], hv[:, sl])
            yield
            dgt = dg_s[p, pl.ds(r0, TB), h2 * HD:(h2 + 1) * HD]
            o = _dot(jnp.where(blockdiag, dgt, 0.0) + off, hv[:, sl])
            yield
            hg_ref[b, h] = jnp.exp(totc[sl, :]) * hst + kv
            o = _head_rms(o + oi) * hnw_ref[:, sl] * _silu(gg[:, sl])
            mixed_s[pl.ds(r0, TB), 1 * W + h * HD:1 * W + (h + 1) * HD] = o.astype(BF16)

        return [s5_out()] + [hg_head(h) for h in range(NH)], None

    _for_sequences(nb, phase2_parts, 2)

    @pl.when(j == pl.num_programs(0) - 1)
    def _emit_states():
        for b in range(nb):
            for h in range(NH):
                p, h2 = divmod(h, 2)
                blk = (slice(h2 * HD, (h2 + 1) * HD),) * 2
                ret_ref[b, h] = hret_s[(b, p) + blk]
                m2_ref[b, h] = hm2_s[(b, p) + blk]

    res = jnp.dot(mixed_s[...], wout_ref[...], preferred_element_type=F32)
    for b in range(nb):
        xo = x_ref[b] + res[b * TB:(b + 1) * TB]
        xo_ref[b] = xo
        if last:
            y_ref[b] = xo * lax.rsqrt(jnp.mean(xo * xo, axis=-1, keepdims=True) + EPS) * fnw_ref[...]


def _const_spec(shape):
    nd = len(shape)
    return pl.BlockSpec(shape, lambda j: (0,) * nd)


def _prompt_layer(x, l, last, w):
    nb, L, _ = x.shape
    nblk = L // TB
    rows = nb * TB
    xspec = pl.BlockSpec((nb, TB, D), lambda j: (0, j, 0))
    tspec = pl.BlockSpec((TB, LANE), lambda j: (j, 0))

    def per_layer(a):
        nd = a.ndim
        return pl.BlockSpec((None,) + a.shape[1:], lambda j: (l,) + (0,) * (nd - 1))

    stacked = [w[k] for k in ("norm_w", "w_in", "w_out")]
    consts = [w[k] for k in ("rdec", "recum", "rkdec", "retot")]
    ret_nw = [w["ret_norm_w"], w["lb"], w["hgrn_norm_w"]]
    s5 = [w[k] for k in ("s5_A", "s5_Bblk", "s5_Cblk", "s5_D", "s5_glu_w", "s5_glu_b")]
    m2 = [w[k] for k in ("m2_conv_w", "m2_conv_b", "m2_dt_bias", "m2_A_log", "m2_D", "m2_norm_w")]
    pair_consts = [w["bones"], w["hexp"]]
    args = [x, w["cos_p"], w["sin_p"]] + stacked + consts + ret_nw + [w["sel"]] + s5 + m2 + pair_consts
    in_specs = ([xspec, tspec, tspec] + [per_layer(a) for a in stacked] + [_const_spec(a.shape) for a in consts]
                + [per_layer(a) for a in ret_nw] + [_const_spec(w["sel"].shape)]
                + [per_layer(a) for a in s5 + m2] + [_const_spec(a.shape) for a in pair_consts])
    if last:
        args.append(w["final_norm_w"])
        in_specs.append(_const_spec(w["final_norm_w"].shape))
    state_shapes = [jax.ShapeDtypeStruct((nb, NH, HD, HD), F32),
                    jax.ShapeDtypeStruct((nb, NH, HD, HD), F32),
                    jax.ShapeDtypeStruct((nb, 2 * S5N), F32),
                    jax.ShapeDtypeStruct((nb, NH, HD, HD), F32),
                    jax.ShapeDtypeStruct((nb, CONV_K - 1, CONV_CH), F32)]
    n_act = 2 if last else 1
    out_shape = [jax.ShapeDtypeStruct((nb, L, D), F32)] * n_act + state_shapes
    out_specs = [xspec] * n_act + [_const_spec(s.shape) for s in state_shapes]
    scratch = [pltpu.VMEM((rows, D), BF16),
               pltpu.VMEM((rows, P_PAD), F32),
               pltpu.VMEM((rows, D), BF16),
               pltpu.VMEM((rows, 2 * S5N), F32),
               pltpu.VMEM((W // LANE, rows, LANE), F32),
               pltpu.VMEM((nb, TB + 8, CONV_CH), F32),
               pltpu.VMEM((W // LANE, rows, SUB * LANE), BF16),
               pltpu.VMEM((W // LANE, rows, LANE), F32),
               pltpu.VMEM((nb, TB, W), F32),
               pltpu.VMEM((nb, TB, W), F32),
               pltpu.VMEM((nb, TB, W), F32),
               pltpu.VMEM((nb, W // LANE, LANE, LANE), F32),
               pltpu.VMEM((nb, W // LANE, LANE, LANE), F32)]
    return pl.pallas_call(
        functools.partial(_prompt_layer_body, last),
        grid=(nblk,),
        in_specs=in_specs,
        out_specs=out_specs,
        out_shape=out_shape,
        scratch_shapes=scratch,
        compiler_params=pltpu.CompilerParams(dimension_semantics=("arbitrary",),
                                             vmem_limit_bytes=VMEM_LIMIT),
        name="prompt_layer",
    )(*args)


def _rope_tables(pos):
    half = HD // 2
    inv = 1.0 / (ROPE_BASE ** (jnp.arange(half, dtype=F32) / half))
    ang = pos[:, None] * inv[None, :]
    cos = jnp.cos(ang)
    sin = jnp.sin(ang)
    cos_t = jnp.tile(cos, (1, LANE // half))
    sin_t = jnp.tile(jnp.concatenate([-sin, sin], axis=1), (1, LANE // HD))
    return cos_t, sin_t


def _retention_tables():
    log_gamma = jnp.log1p(-(2.0 ** (-5.0 - jnp.arange(NH, dtype=F32))))
    cum = jnp.cumsum(jnp.broadcast_to(log_gamma, (TB, NH)), axis=0)
    total = cum[-1]
    causal = jnp.tril(jnp.ones((TB, TB), dtype=bool))
    diff = cum[:, None, :] - cum[None, :, :]
    dec = jnp.where(causal[:, :, None], jnp.exp(jnp.where(causal[:, :, None], diff, 0.0)), 0.0)
    rdec = jnp.moveaxis(dec, 2, 0)
    recum = jnp.broadcast_to(jnp.exp(cum).T[:, :, None], (NH, TB, HD))
    rkdec = jnp.broadcast_to(jnp.exp(total[None, :] - cum).T[:, :, None], (NH, TB, HD))
    retot = jnp.broadcast_to(jnp.exp(total)[:, None, None], (NH, 1, HD))
    pair = lambda t: jnp.concatenate([t[0::2], t[1::2]], axis=-1)
    return pair(rdec), pair(recum), pair(rkdec), pair(retot), log_gamma


def _sel_matrix():
    sel = np.zeros((SUB, 2, HD, 2, TB), np.float32)
    for s_ in range(SUB):
        for h2 in range(2):
            sel[s_, h2, :, h2, s_::SUB] = 1.0
    return jnp.asarray(sel.reshape(SUB * LANE, LANE), dtype=BF16)


def _rows(v, width=None):
    v = v.astype(F32)
    if width is not None and v.shape[-1] < width:
        v = jnp.pad(v, ((0, 0), (0, width - v.shape[-1])))
    return v[:, None, :]


def _block_diag(blocks):
    g, r, c = blocks.shape
    eye = jnp.eye(g, dtype=blocks.dtype)
    return jnp.einsum('grc,gh->grhc', blocks, eye).reshape(g * r, g * c)


def _s5_discretise(A_re, A_im, log_dt, B_re, B_im):
    A_re, A_im = A_re.astype(F32), A_im.astype(F32)
    dt = jnp.exp(log_dt.astype(F32))[:, None]
    mag = jnp.exp(A_re * dt)
    ab_re, ab_im = mag * jnp.cos(A_im * dt), mag * jnp.sin(A_im * dt)
    nr, ni = ab_re - 1.0, ab_im
    den = A_re * A_re + A_im * A_im
    f_re = (nr * A_re + ni * A_im) / den
    f_im = (ni * A_re - nr * A_im) / den
    B_re, B_im = B_re.astype(F32), B_im.astype(F32)
    bb_re = f_re[..., None] * B_re - f_im[..., None] * B_im
    bb_im = f_re[..., None] * B_im + f_im[..., None] * B_re
    return ab_re, ab_im, bb_re, bb_im


def _s5_matrices(A_re, A_im, log_dt, B_re, B_im, C_re, C_im):
    ab_re, ab_im, bb_re, bb_im = _s5_discretise(A_re, A_im, log_dt, B_re, B_im)
    bblk = jnp.concatenate([_block_diag(jnp.swapaxes(bb_re, 1, 2)),
                            _block_diag(jnp.swapaxes(bb_im, 1, 2))], axis=1)
    cblk = jnp.concatenate([_block_diag(jnp.swapaxes(C_re.astype(F32), 1, 2)),
                            _block_diag(jnp.swapaxes(-C_im.astype(F32), 1, 2))], axis=0)
    return jnp.stack([ab_re.reshape(-1), ab_im.reshape(-1)], axis=0), bblk, cblk


R_COS, R_SIN, R_RETNW, R_LB, R_HNW, R_S5D, R_GLUB, R_MNW = [i * W for i in range(8)]
R_AR = 8 * W
R_AI = R_AR + S5N
R_DTB = R_AI + S5N
R_ALOG = R_DTB + 8
R_MD = R_ALOG + 8
N_COLP = R_MD + 8


def _prepare(p, lb_all, prompt_len):
    depth = p["norm_w"].shape[0]
    s5_A, bblk, cblk = jax.vmap(_s5_matrices)(p["s5_A_re"], p["s5_A_im"], p["s5_log_dt"], p["s5_B_re"],
                                              p["s5_B_im"], p["s5_C_re"], p["s5_C_im"])
    cos_p, sin_p = _rope_tables(jnp.arange(prompt_len, dtype=F32))
    rdec, recum, rkdec, retot, log_gamma = _retention_tables()
    cos_s, sin_s = _rope_tables(PAST_LEN + jnp.arange(1, dtype=F32))
    w = dict(
        norm_w=_rows(p["norm_w"]),
        w_in=jnp.pad(p["w_in"], ((0, 0), (0, 0), (0, P_PAD - P_TOTAL))).astype(BF16),
        w_out=p["w_out"].astype(BF16),
        ret_norm_w=_rows(p["ret_norm_w"]),
        lb=_rows(lb_all),
        hgrn_norm_w=_rows(p["hgrn_norm_w"]),
        s5_A=s5_A,
        s5_Bblk=bblk.astype(BF16),
        s5_Cblk=cblk.astype(BF16),
        s5_D=_rows(p["s5_D"]),
        s5_glu_w=p["s5_glu_w"].astype(BF16),
        s5_glu_b=_rows(p["s5_glu_b"]),
        m2_conv_w=p["m2_conv_w"].astype(F32),
        m2_conv_b=_rows(p["m2_conv_b"]),
        m2_dt_bias=_rows(jnp.repeat(p["m2_dt_bias"], HD, axis=1)),
        m2_A_log=_rows(jnp.repeat(p["m2_A_log"], HD, axis=1)),
        m2_D=_rows(jnp.repeat(p["m2_D"], HD, axis=1)),
        m2_norm_w=_rows(p["m2_norm_w"]),
        final_norm_w=p["final_norm_w"].astype(F32).reshape(1, D),
    )
    tile2 = lambda t: jnp.broadcast_to(jnp.tile(t[0], W // LANE), (depth, W))
    colp = jnp.concatenate(
        [tile2(cos_s), tile2(sin_s), w["ret_norm_w"][:, 0], w["lb"][:, 0], w["hgrn_norm_w"][:, 0],
         w["s5_D"][:, 0], w["s5_glu_b"][:, 0], w["m2_norm_w"][:, 0], s5_A[:, 0], s5_A[:, 1],
         _rows(p["m2_dt_bias"], 8)[:, 0], _rows(p["m2_A_log"], 8)[:, 0], _rows(p["m2_D"], 8)[:, 0]], axis=1)
    w.update(
        colp=jnp.broadcast_to(colp[:, :, None], (depth, N_COLP, LANE)),
        s5_BblkT=jnp.swapaxes(w["s5_Bblk"], 1, 2),
        s5_CblkT=jnp.swapaxes(w["s5_Cblk"], 1, 2),
        s5_glu_wT=jnp.swapaxes(w["s5_glu_w"], 1, 2),
        rgam=jnp.broadcast_to(jnp.exp(log_gamma)[:, None, None], (NH, 8, LANE)),
        cos_p=cos_p, sin_p=sin_p, rdec=rdec, recum=recum, rkdec=rkdec, retot=retot, sel=_sel_matrix(),
        bones=jnp.asarray(np.kron(np.eye(LANE // HD), np.full((HD, HD), 1.0 / HD)), dtype=BF16),
        hexp=jnp.asarray(np.kron(np.eye(LANE, NH), np.ones((1, HD))), dtype=F32),
    )
    return w


KC = 16
NKC = HD // KC
STEPS = NH * NKC


def _rotary_cols(x, cos, sin_signed):
    half = HD // 2
    parts = []
    for h in range(NH):
        parts += [x[h * HD + half:(h + 1) * HD], x[h * HD:h * HD + half]]
    return x * cos + jnp.concatenate(parts, axis=0) * sin_signed


def _expand_rows(dst, x):
    for c in range(x.shape[0]):
        dst[c] = jnp.broadcast_to(x[c:c + 1, :], (8, LANE))


def _sample_body(n_steps, x_ref, normw_ref, win_ref, wout_ref, fnw_ref, colp_ref, sbt_ref, sct_ref,
                 gluwt_ref, cw_ref, cb_ref, rgam_ref,
                 ret_in, hg_in, m2_in, s5re_in, s5im_in, buf_in,
                 y_ref, ret_out, hg_out, m2_out, s5re_out, s5im_out, buf_out,
                 xs_s, pt_s, vt_s, ot_s, mixt_s, o_s, hp_s,
                 kret_s, qret_s, khg_s, qhg_s, ahg_s, km2_s, qm2_s):
    i = pl.program_id(0)
    r = i % STEPS
    h = r // NKC
    kc = r % NKC

    def cp(r0, n=W):
        return colp_ref[r0:r0 + n, :]

    def pc(c0, w):
        return pt_s[c0:c0 + w, :]

    @pl.when(i == 0)
    def _load():
        xs_s[...] = x_ref[...]

    @pl.when(r == 0)
    def _prep():
        x = xs_s[...]
        hn = x * lax.rsqrt(jnp.mean(x * x, axis=-1, keepdims=True) + EPS) * normw_ref[...]
        proj = jnp.dot(hn.astype(BF16), win_ref[...], preferred_element_type=F32)

        xnew = proj[:, C_XBC:C_XBC + CONV_CH]
        acc = cb_ref[...] + xnew * cw_ref[CONV_K - 1:CONV_K, :]
        for t in range(CONV_K - 1):
            acc = acc + buf_in[t] * cw_ref[t:t + 1, :]
        for t in range(CONV_K - 2):
            buf_out[t] = buf_in[t + 1]
        buf_out[CONV_K - 2] = xnew
        xbc = _silu(acc)

        for t in range(P_PAD // LANE):
            c0 = t * LANE
            if C_XBC <= c0 < C_XBC + CONV_CH:
                tile = xbc[:, c0 - C_XBC:c0 - C_XBC + LANE]
            else:
                tile = proj[:, c0:c0 + LANE]
            pt_s[c0:c0 + LANE, :] = tile.T

        cos, sin = cp(R_COS), cp(R_SIN)
        _expand_rows(kret_s, _rotary_cols(pc(C_RK, W), cos, sin) * (HD ** -0.5))
        _expand_rows(qret_s, _rotary_cols(pc(C_RQ, W), cos, sin))
        vt_s[0] = pc(C_RV, W)

        fr = pc(C_GF, W)
        lb = cp(R_LB)
        logf = _log_sigmoid(fr) + jnp.log(1.0 + lb * jnp.exp(jnp.minimum(-fr, EXP_CLIP)))
        _expand_rows(ahg_s, jnp.exp(logf))
        _expand_rows(khg_s, (1.0 - lb) * jax.nn.sigmoid(-fr))
        _expand_rows(qhg_s, _silu(pc(C_GQ, W)))
        vt_s[1] = pc(C_GI, W)

        dt8 = _softplus(pc(C_DT, 8) + cp(R_DTB, 8))
        adec8 = jnp.exp(dt8 * (-jnp.exp(cp(R_ALOG, 8))))
        for hh in range(NH):
            hp_s[hh] = jnp.broadcast_to(adec8[hh:hh + 1, :], (8, LANE))
            vt_s[2, hh * HD:(hh + 1) * HD, :] = pc(C_XBC + hh * HD, HD) * dt8[hh:hh + 1, :]
        _expand_rows(km2_s, pc(C_XBC + W, 2 * HD))
        _expand_rows(qm2_s, pc(C_XBC + W + 2 * HD, 2 * HD))

        u = pc(C_SU, W)
        bu = jnp.dot(sbt_ref[...], u.astype(BF16), preferred_element_type=F32)
        hr, hi = s5re_in[...], s5im_in[...]
        ar, ai = cp(R_AR, S5N), cp(R_AI, S5N)
        nr = ar * hr - ai * hi + bu[0:S5N]
        ni = ar * hi + ai * hr + bu[S5N:2 * S5N]
        s5re_out[...] = nr
        s5im_out[...] = ni
        hcat = jnp.concatenate([nr, ni], axis=0).astype(BF16)
        sy = jnp.dot(sct_ref[...], hcat, preferred_element_type=F32) + cp(R_S5D) * u
        gy = _gelu_tanh(sy)
        glu = jnp.dot(gluwt_ref[...], gy.astype(BF16), preferred_element_type=F32) + cp(R_GLUB)
        mixt_s[2 * W:3 * W, :] = gy * jax.nn.sigmoid(glu) * _silu(pc(C_SG, W))

    @pl.when(kc == 0)
    def _zero():
        o_s[...] = jnp.zeros(o_s.shape, F32)

    hrow = pl.multiple_of(h * HD, HD)
    cbase = h * HD + kc * KC
    gbase = (h // 2) * HD + kc * KC

    def update(m, st_in, st_out, kx, qx, base, decay):
        v3 = vt_s[m, pl.ds(hrow, HD), :].reshape(HD // 8, 8, LANE)

        def body(kk, o):
            s_new = decay(kk) * st_in[kk].reshape(HD // 8, 8, LANE) + kx[base + kk] * v3
            st_out[kk] = s_new.reshape(HD, LANE)
            return o + qx[base + kk] * s_new
        o_s[m] = lax.fori_loop(0, KC, body, o_s[m], unroll=2)

    gam = rgam_ref[h]
    update(0, ret_in, ret_out, kret_s, qret_s, cbase, lambda kk: gam)
    update(1, hg_in, hg_out, khg_s, qhg_s, cbase, lambda kk: ahg_s[cbase + kk])
    adec = hp_s[h]
    update(2, m2_in, m2_out, km2_s, qm2_s, gbase, lambda kk: adec)

    @pl.when(kc == NKC - 1)
    def _head_done():
        for m in range(3):
            ot_s[m, pl.ds(hrow, HD), :] = o_s[m].reshape(HD, LANE)

    @pl.when(r == STEPS - 1)
    def _finish():
        def head_rms_cols(o):
            parts = []
            for hh in range(NH):
                seg = o[hh * HD:(hh + 1) * HD]
                parts.append(seg * lax.rsqrt(jnp.mean(seg * seg, axis=0, keepdims=True) + EPS))
            return jnp.concatenate(parts, axis=0)

        mixt_s[0:W, :] = head_rms_cols(ot_s[0]) * cp(R_RETNW) * _silu(pc(C_RG, W))
        mixt_s[W:2 * W, :] = head_rms_cols(ot_s[1]) * cp(R_HNW) * _silu(pc(C_GG, W))
        md8 = cp(R_MD, 8)
        ym = jnp.concatenate([ot_s[2, hh * HD:(hh + 1) * HD, :] + md8[hh:hh + 1, :] * pc(C_XBC + hh * HD, HD)
                              for hh in range(NH)], axis=0)
        my = ym * _silu(pc(C_MZ, W))
        mixt_s[3 * W:4 * W, :] = my * lax.rsqrt(jnp.mean(my * my, axis=0, keepdims=True) + EPS) * cp(R_MNW)
        mixed = jnp.concatenate([mixt_s[t * LANE:(t + 1) * LANE, :].T for t in range(D // LANE)], axis=1)
        xo = xs_s[...] + jnp.dot(mixed.astype(BF16), wout_ref[...], preferred_element_type=F32)
        xs_s[...] = xo

        @pl.when(i == n_steps - 1)
        def _final_norm():
            y_ref[...] = xo * lax.rsqrt(jnp.mean(xo * xo, axis=-1, keepdims=True) + EPS) * fnw_ref[...]


def _sample_step(x, w, ret, hg, m2, s5re, s5im, buf):
    depth = ret.shape[0]
    n = x.shape[0]
    n_steps = depth * STEPS
    lay = lambda i: i // STEPS

    def per_layer(a):
        nd = a.ndim
        return pl.BlockSpec((None,) + a.shape[1:], lambda i: (lay(i),) + (0,) * (nd - 1))

    st_spec = pl.BlockSpec((None, None, KC, HD, LANE),
                           lambda i: (lay(i), (i % STEPS) // NKC, i % NKC, 0, 0))
    weights = [w["norm_w"], w["w_in"], w["w_out"]]
    tables = [w["colp"], w["s5_BblkT"], w["s5_CblkT"], w["s5_glu_wT"], w["m2_conv_w"], w["m2_conv_b"]]
    in_specs = ([_const_spec(x.shape)] + [per_layer(a) for a in weights] + [_const_spec(w["final_norm_w"].shape)]
                + [per_layer(a) for a in tables] + [_const_spec(w["rgam"].shape)]
                + [st_spec, st_spec, st_spec, per_layer(s5re), per_layer(s5im), per_layer(buf)])
    out_shape = [jax.ShapeDtypeStruct((n, D), F32)] + [jax.ShapeDtypeStruct(a.shape, F32)
                                                       for a in (ret, hg, m2, s5re, s5im, buf)]
    out_specs = [_const_spec((n, D)), st_spec, st_spec, st_spec, per_layer(s5re), per_layer(s5im),
                 per_layer(buf)]
    expand = lambda c: pltpu.VMEM((c, 8, LANE), F32)
    scratch = [pltpu.VMEM((n, D), F32),
               pltpu.VMEM((P_PAD, LANE), F32),
               pltpu.VMEM((3, W, LANE), F32),
               pltpu.VMEM((3, W, LANE), F32),
               pltpu.VMEM((D, LANE), F32),
               pltpu.VMEM((3, HD // 8, 8, LANE), F32),
               pltpu.VMEM((NH, 8, LANE), F32),
               expand(W), expand(W), expand(W), expand(W), expand(W), expand(2 * HD), expand(2 * HD)]
    return pl.pallas_call(
        functools.partial(_sample_body, n_steps),
        grid=(n_steps,),
        in_specs=in_specs,
        out_specs=out_specs,
        out_shape=out_shape,
        scratch_shapes=scratch,
        compiler_params=pltpu.CompilerParams(dimension_semantics=("arbitrary",),
                                             vmem_limit_bytes=VMEM_LIMIT),
        name="sample_step",
    )(x, *weights, w["final_norm_w"], *tables, w["rgam"], ret, hg, m2, s5re, s5im, buf)


def kernel(x_prompt, x_sample, state_ret, state_hgrn, state_s5_re, state_s5_im, state_m2_ssm,
           state_m2_conv, norm_w, w_in, ret_norm_w, hgrn_lb_logits, hgrn_norm_w, s5_A_re, s5_A_im,
           s5_log_dt, s5_B_re, s5_B_im, s5_C_re, s5_C_im, s5_D, s5_glu_w, s5_glu_b, m2_conv_w,
           m2_conv_b, m2_dt_bias, m2_A_log, m2_D, m2_norm_w, w_out, final_norm_w):
    p = dict(norm_w=norm_w, w_in=w_in, ret_norm_w=ret_norm_w, hgrn_norm_w=hgrn_norm_w,
             s5_A_re=s5_A_re, s5_A_im=s5_A_im, s5_log_dt=s5_log_dt, s5_B_re=s5_B_re, s5_B_im=s5_B_im,
             s5_C_re=s5_C_re, s5_C_im=s5_C_im, s5_D=s5_D, s5_glu_w=s5_glu_w, s5_glu_b=s5_glu_b,
             m2_conv_w=m2_conv_w, m2_conv_b=m2_conv_b, m2_dt_bias=m2_dt_bias, m2_A_log=m2_A_log,
             m2_D=m2_D, m2_norm_w=m2_norm_w, w_out=w_out, final_norm_w=final_norm_w)
    depth = norm_w.shape[0]
    nbp, lp, _ = x_prompt.shape
    nbs = x_sample.shape[0]

    lb_sm = jax.nn.softmax(hgrn_lb_logits.astype(F32), axis=0)
    lb_all = jnp.clip(jnp.cumsum(lb_sm, axis=0) - lb_sm[0], 0.0, 1.0)

    w = _prepare(p, lb_all, lp)

    xp = x_prompt
    pst = []
    for l in range(depth):
        outs = _prompt_layer(xp, l, l == depth - 1, w)
        xp = outs[0]
        ret, hg, s5, m2, buf = outs[-5:]
        pst.append((ret, hg, s5[:, :S5N].reshape(nbp, S5G, S5P), s5[:, S5N:].reshape(nbp, S5G, S5P),
                    m2, buf))
    yp = outs[1]

    seq_last = lambda a: jnp.moveaxis(a.astype(F32), 1, -1)
    ys, ret, hg, m2, s5re, s5im, buf = _sample_step(
        x_sample.reshape(nbs, D), w,
        seq_last(state_ret), seq_last(state_hgrn), seq_last(state_m2_ssm),
        seq_last(state_s5_re).reshape(depth, S5N, nbs), seq_last(state_s5_im).reshape(depth, S5N, nbs),
        jnp.swapaxes(state_m2_conv.astype(F32), 1, 2))
    seq_second = lambda a: jnp.moveaxis(a, -1, 1)

    stk = lambda i: jnp.stack([s[i] for s in pst], axis=0)
    return (yp, ys.reshape(nbs, 1, D),
            stk(0), stk(1), stk(2), stk(3), stk(4), stk(5),
            seq_second(ret), seq_second(hg), seq_second(s5re.reshape(depth, S5G, S5P, nbs)),
            seq_second(s5im.reshape(depth, S5G, S5P, nbs)), seq_second(m2), jnp.swapaxes(buf, 1, 2))
```

```python
import functools
import math

import numpy as np
import jax
import jax.numpy as jnp
from jax import lax
from jax.experimental import pallas as pl
from jax.experimental.pallas import tpu as pltpu

F32 = jnp.float32
BF16 = jnp.bfloat16
HI = lax.Precision.HIGHEST

D = 1024
W = 256
NH = 4
HD = 64
S5G = 16
S5C = 16
S5P = 64
S5N = S5G * S5P
CONV_CH = 512
CONV_K = 4
TB = 64
SUB = 16
NSUB = TB // SUB
EPS = 1e-6
EXP_CLIP = 60.0
ROPE_BASE = 10000.0
PAST_LEN = 16384

C_RQ, C_RK, C_RV, C_RG = 0, 256, 512, 768
C_GQ, C_GF, C_GI, C_GG = 1024, 1280, 1536, 1792
C_SU, C_SG = 2048, 2304
C_XBC, C_MZ, C_DT = 2560, 3072, 3328
P_TOTAL = 3332
PCH = 1152
NPC = 3
P_PAD = NPC * PCH
LANE = 128
VMEM_LIMIT = 56 * 1024 * 1024


def _silu(x):
    return x * jax.nn.sigmoid(x)


def _softplus(x):
    return jnp.maximum(x, 0.0) + jnp.log(1.0 + jnp.exp(-jnp.abs(x)))


def _log_sigmoid(x):
    return jnp.minimum(x, 0.0) - jnp.log(1.0 + jnp.exp(-jnp.abs(x)))


def _round_robin(gens):
    gens = list(gens)
    while gens:
        alive = []
        for g in gens:
            try:
                next(g)
                alive.append(g)
            except StopIteration:
                pass
        gens = alive


def _for_sequences(nb, parts, group):
    def body(i, c):
        built = [parts(i * group + k) for k in range(group)]
        _round_robin([g for gens, _ in built for g in gens])
        for _, finish in built:
            if finish is not None:
                finish()
        return c
    lax.fori_loop(0, nb // group, body, 0)


def _gelu_tanh(x):
    c = math.sqrt(2.0 / math.pi)
    return 0.5 * x * (1.0 + jnp.tanh(c * (x + 0.044715 * (x * x * x))))


def _dot(a, b):
    return jnp.dot(a.astype(BF16), b.astype(BF16), preferred_element_type=F32)


def _dot_hi(a, b):
    return jnp.dot(a, b, precision=HI, preferred_element_type=F32)


def _dot_tn_hi(a, b):
    return lax.dot_general(a, b, (((0,), (0,)), ((), ())), precision=HI,
                           preferred_element_type=F32)


def _rot_half_partner(x):
    lane = lax.broadcasted_iota(jnp.int32, x.shape, 1)
    first = (lane % HD) < (HD // 2)
    return jnp.where(first, pltpu.roll(x, LANE - HD // 2, 1), pltpu.roll(x, HD // 2, 1))


def _rotary(x, cos, sin_signed):
    parts = []
    for i in range(W // LANE):
        xi = x[:, i * LANE:(i + 1) * LANE]
        parts.append(xi * cos + _rot_half_partner(xi) * sin_signed)
    return jnp.concatenate(parts, axis=1)


def _prompt_layer_body(last, *refs):
    (x_ref, cos_ref, sin_ref, normw_ref, win_ref, wout_ref,
     rdec_ref, recum_ref, rkdec_ref, retot_ref, retnw_ref,
     lb_ref, hnw_ref, sel_ref,
     sA_ref, sB_ref, sC_ref, sD_ref, gluw_ref, glub_ref,
     cw_ref, cb_ref, dtb_ref, alog_ref, md_ref, mnw_ref, bones_ref, hexp_ref) = refs[:28]
    refs = refs[28:]
    if last:
        fnw_ref = refs[0]
        refs = refs[1:]
    act_ref = refs[0]
    refs = refs[1:]
    (hret_s, hhg_s, s5_ref, hm2_s, m2buf_ref,
     hn_s, proj_s, mixed_s, bu_s, u_s, cv_s, p_s, dg_s, hq_s, hk_s, cum_s) = refs
    j = pl.program_id(0)
    nb = x_ref.shape[0]

    @pl.when(j == 0)
    def _init():
        hret_s[...] = jnp.zeros(hret_s.shape, F32)
        hhg_s[...] = jnp.zeros(hhg_s.shape, F32)
        s5_ref[...] = jnp.zeros(s5_ref.shape, F32)
        hm2_s[...] = jnp.zeros(hm2_s.shape, F32)
        cv_s[...] = jnp.zeros(cv_s.shape, F32)

    ti = lax.broadcasted_iota(jnp.int32, (TB, TB), 0)
    si = lax.broadcasted_iota(jnp.int32, (TB, TB), 1)
    causal = si <= ti
    tri_l = causal.astype(F32)
    ones_tt = jnp.ones((TB, TB), F32)
    pr = lax.broadcasted_iota(jnp.int32, (LANE, LANE), 0)
    pc_ = lax.broadcasted_iota(jnp.int32, (LANE, LANE), 1)
    pairmask = (pr // HD) == (pc_ // HD)
    t2 = lax.broadcasted_iota(jnp.int32, (TB, LANE), 0)
    l2 = lax.broadcasted_iota(jnp.int32, (TB, LANE), 1)
    causal2 = (l2 % HD) <= t2
    tri_u2 = (t2 <= (l2 % HD)).astype(F32)
    first_head = l2 < HD
    subdiag2 = (t2 // SUB) == ((l2 % HD) // SUB)
    srcblock = [pairmask & (((pc_ % HD) // SUB) == jb) for jb in range(NSUB - 1)]
    ones_tl = jnp.ones((TB, LANE), F32)

    def dup_t(x2):
        return jnp.concatenate([x2, x2], axis=0).T

    def blockdiag2(x2):
        return jnp.where(pairmask, jnp.concatenate([x2, x2], axis=0), 0.0)

    def norm_body(b, c):
        xb = x_ref[b]
        hn = xb * lax.rsqrt(jnp.mean(xb * xb, axis=-1, keepdims=True) + EPS) * normw_ref[...]
        hn_s[pl.ds(pl.multiple_of(b * TB, TB), TB), :] = hn.astype(BF16)
        return c
    lax.fori_loop(0, nb, norm_body, 0)
    for c in range(NPC):
        cs = slice(c * PCH, (c + 1) * PCH)
        proj_s[:, cs] = jnp.dot(hn_s[...], win_ref[:, cs], preferred_element_type=F32)

    cos = cos_ref[...]
    sin = sin_ref[...]

    def phase1_parts(b):
        r0 = pl.multiple_of(b * TB, TB)

        def pj(c0, w):
            return proj_s[pl.ds(r0, TB), c0:c0 + w]

        u = pj(C_SU, W)
        for c_ in range(W // LANE):
            u_s[c_, pl.ds(b, TB, stride=nb), :] = u[:, c_ * LANE:(c_ + 1) * LANE]

        rq = _rotary(pj(C_RQ, W), cos, sin)
        rk = _rotary(pj(C_RK, W), cos, sin) * (HD ** -0.5)
        rv = pj(C_RV, W)
        rg = pj(C_RG, W)

        def ret_pair(p):
            ls = slice(p * LANE, (p + 1) * LANE)
            q2, k2, v2 = rq[:, ls], rk[:, ls], rv[:, ls]
            hb = hret_s[b, p]
            kt = dup_t(k2)
            s_raw = _dot(q2, jnp.where(pairmask, kt, 0.0))
            oi = _dot(q2, hb)
            kv = _dot(kt[:, 0:HD], v2 * rkdec_ref[p])
            yield
            o = _dot(s_raw * rdec_ref[p], blockdiag2(v2))
            yield
            o = o + oi * recum_ref[p]
            hret_s[b, p] = jnp.where(pairmask, retot_ref[p] * hb + kv, 0.0)
            ms = _dot(o * o, bones_ref[...])
            yield
            o = o * lax.rsqrt(ms + EPS) * retnw_ref[:, ls] * _silu(rg[:, ls])
            mixed_s[pl.ds(r0, TB), 0 * W + p * LANE:0 * W + (p + 1) * LANE] = o.astype(BF16)

        cv_s[b, 8:8 + TB, :] = pj(C_XBC, CONV_CH)
        acc = cb_ref[...] + cv_s[b, 5:5 + TB, :] * cw_ref[0:1, :]
        for i in range(1, CONV_K):
            acc = acc + cv_s[b, 5 + i:5 + i + TB, :] * cw_ref[i:i + 1, :]
        tail = cv_s[b, TB + 5:TB + 8, :]
        cv_s[b, 5:8, :] = tail
        m2buf_ref[b] = tail
        xbc = _silu(acc)
        xm = xbc[:, 0:W]
        bm = xbc[:, W:W + 2 * HD]
        cm = xbc[:, W + 2 * HD:W + 4 * HD]
        bm_sw = pltpu.roll(bm, HD, 1)
        cm_sw = pltpu.roll(cm, HD, 1)
        dt_b = _softplus(_dot_hi(pj(C_DT, LANE), hexp_ref[...]) + dtb_ref[...])
        la_b = dt_b * (-jnp.exp(alog_ref[...]))
        cum_b = _dot_hi(tri_l, la_b)
        ys = [None] * (W // LANE)

        def m2_pair(p):
            ls = slice(p * LANE, (p + 1) * LANE)
            b2 = jnp.where(first_head, bm, bm_sw) if p == 0 else jnp.where(first_head, bm_sw, bm)
            c2 = jnp.where(first_head, cm, cm_sw) if p == 0 else jnp.where(first_head, cm_sw, cm)
            x2, dt2, cum2 = xm[:, ls], dt_b[:, ls], cum_b[:, ls]
            xdt2 = x2 * dt2
            r2 = _dot_hi(ones_tt, la_b[:, ls] * tri_u2)
            tot2 = cum2[TB - 1:TB, :]
            hb = hm2_s[b, p]
            bt = dup_t(b2)
            s_raw = _dot(c2, jnp.where(pairmask, bt, 0.0))
            oi = _dot(c2, hb)
            kv = _dot(bt[:, 0:HD], xdt2 * jnp.exp(tot2 - cum2))
            decay = jnp.where(causal2, jnp.exp(jnp.minimum(cum2 - r2, 0.0)), 0.0)
            yield
            o = _dot(s_raw * decay, blockdiag2(xdt2))
            yield
            hm2_s[b, p] = jnp.where(pairmask, jnp.exp(tot2) * hb + kv, 0.0)
            ys[p] = o + oi * jnp.exp(cum2) + md_ref[:, ls] * x2

        fr = pj(C_GF, W)
        lb = lb_ref[...]
        logf = _log_sigmoid(fr) + jnp.log(1.0 + lb * jnp.exp(jnp.minimum(-fr, EXP_CLIP)))
        hq = _silu(pj(C_GQ, W))
        hk = (1.0 - lb) * jax.nn.sigmoid(-fr)
        cum = _dot_hi(tri_l, logf)
        hq_s[b] = hq
        hk_s[b] = hk
        cum_s[b] = cum
        t8 = lax.broadcasted_iota(jnp.int32, (8, LANE), 0)

        def diag_products(p):
            ls = slice(p * LANE, (p + 1) * LANE)
            for s_ in range(SUB):
                pieces = []
                for i in range(NSUB):
                    kb = jnp.broadcast_to(hk_s[b, i * SUB + s_:i * SUB + s_ + 1, ls], (8, LANE))
                    cb = jnp.broadcast_to(cum_s[b, i * SUB + s_:i * SUB + s_ + 1, ls], (8, LANE))
                    for half in range(SUB // 8):
                        rows = slice(i * SUB + half * 8, i * SUB + half * 8 + 8)
                        if half * 8 + 7 < s_:
                            pieces.append(jnp.zeros((8, LANE), F32))
                        elif half * 8 >= s_:
                            pieces.append(hq[rows, ls] * kb * jnp.exp(cum[rows, ls] - cb))
                        else:
                            e = jnp.exp(jnp.minimum(cum[rows, ls] - cb, 0.0))
                            pieces.append(jnp.where(t8 + half * 8 >= s_, hq[rows, ls] * kb * e, 0.0))
                pv = jnp.concatenate(pieces, axis=0)
                p_s[p, pl.ds(r0, TB), s_ * LANE:(s_ + 1) * LANE] = pv.astype(BF16)
                if s_ % 2 == 1:
                    yield

        def finish():
            my = jnp.concatenate(ys, axis=1) * _silu(pj(C_MZ, W))
            om = my * lax.rsqrt(jnp.mean(my * my, axis=-1, keepdims=True) + EPS) * mnw_ref[...]
            mixed_s[pl.ds(r0, TB), 3 * W:4 * W] = om.astype(BF16)

        gens = ([ret_pair(p) for p in range(W // LANE)] + [m2_pair(p) for p in range(W // LANE)]
                + [diag_products(p) for p in range(W // LANE)])
        return gens, finish

    _for_sequences(nb, phase1_parts, 1)

    for p in range(W // LANE):
        dg_s[p] = jnp.dot(p_s[p], sel_ref[...], preferred_element_type=F32)

    u_tb = jnp.concatenate([u_s[c_] for c_ in range(W // LANE)], axis=1)
    bu_s[...] = _dot(u_tb, sB_ref[...])
    ar = jnp.broadcast_to(sA_ref[0:1, :], (nb, S5N))
    ai = jnp.broadcast_to(sA_ref[1:2, :], (nb, S5N))

    def scan_body(t, carry):
        hr, hi = carry
        row = pl.multiple_of(t * nb, nb)
        nr = ar * hr - ai * hi + bu_s[pl.ds(row, nb), 0:S5N]
        ni = ar * hi + ai * hr + bu_s[pl.ds(row, nb), S5N:2 * S5N]
        bu_s[pl.ds(row, nb), 0:S5N] = nr
        bu_s[pl.ds(row, nb), S5N:2 * S5N] = ni
        return nr, ni
    hr, hi = lax.fori_loop(0, TB, scan_body, (s5_ref[:, 0:S5N], s5_ref[:, S5N:2 * S5N]))
    s5_ref[:, 0:S5N] = hr
    s5_ref[:, S5N:2 * S5N] = hi
    ch_tb = _dot(bu_s[...], sC_ref[...])
    for c_ in range(W // LANE):
        u_s[c_] = ch_tb[:, c_ * LANE:(c_ + 1) * LANE]

    def phase2_parts(b):
        r0 = pl.multiple_of(b * TB, TB)

        def pj(c0, w):
            return proj_s[pl.ds(r0, TB), c0:c0 + w]

        def s5_out():
            chs = jnp.concatenate([u_s[c_, pl.ds(b, TB, stride=nb), :] for c_ in range(W // LANE)],
                                  axis=1)
            gy = _gelu_tanh(chs + sD_ref[...] * pj(C_SU, W))
            glu = _dot(gy, gluw_ref[...])
            yield
            os5 = gy * jax.nn.sigmoid(glu + glub_ref[...]) * _silu(pj(C_SG, W))
            mixed_s[pl.ds(r0, TB), 2 * W:3 * W] = os5.astype(BF16)

        cum = cum_s[b]
        rr = jnp.concatenate(
            [jnp.zeros((SUB, W), F32)]
            + [jnp.broadcast_to(cum_s[b, i * SUB - 1:i * SUB, :], (SUB, W))
               for i in range(1, NSUB)], axis=0)
        ee = jnp.concatenate(
            [jnp.broadcast_to(cum_s[b, i * SUB + SUB - 1:i * SUB + SUB, :], (SUB, W))
             for i in range(NSUB)], axis=0)
        lastrow = lax.broadcasted_iota(jnp.int32, (TB, W), 0) == TB - 1
        totc = _dot_tn_hi(jnp.where(lastrow, cum, 0.0), ones_tl)
        hq = hq_s[b]
        hk = hk_s[b]
        hv = pj(C_GI, W)
        gg = pj(C_GG, W)
        qt = hq * jnp.exp(cum - rr)
        kh_ = hk * jnp.exp(ee - cum)
        qe = hq * jnp.exp(cum)
        tot = cum[TB - 1:TB, :]
        kend = kh_ * jnp.exp(tot - ee)
        trow = lax.broadcasted_iota(jnp.int32, (TB, W), 0) // SUB
        qx = []
        for jb in range(NSUB - 1):
            eb = jnp.broadcast_to(ee[jb * SUB:jb * SUB + 1, :], (TB, W))
            qx.append(jnp.where(trow > jb, qt * jnp.exp(jnp.minimum(rr - eb, 0.0)), 0.0))

        def hg_pair(p):
            ls = slice(p * LANE, (p + 1) * LANE)
            hb = hhg_s[b, p]
            hv2 = hv[:, ls]
            kt = dup_t(kh_[:, ls])
            off = _dot(qx[0][:, ls], jnp.where(srcblock[0], kt, 0.0))
            for jb in range(1, NSUB - 1):
                off = off + _dot(qx[jb][:, ls], jnp.where(srcblock[jb], kt, 0.0))
            oi = _dot(qe[:, ls], hb)
            kv = _dot(dup_t(kend[:, ls])[:, 0:HD], hv2)
            yield
            o = _dot(jnp.where(subdiag2, dg_s[p, pl.ds(r0, TB), :], 0.0) + off, blockdiag2(hv2))
            yield
            hhg_s[b, p] = jnp.where(pairmask, jnp.exp(totc[ls, :]) * hb + kv, 0.0)
            o = o + oi
            ms = _dot(o * o, bones_ref[...])
            yield
            o = o * lax.rsqrt(ms + EPS) * hnw_ref[:, ls] * _silu(gg[:, ls])
            mixed_s[pl.ds(r0, TB), 1 * W + p * LANE:1 * W + (p + 1) * LANE] = o.astype(BF16)

        return [s5_out()] + [hg_pair(p) for p in range(W // LANE)], None

    _for_sequences(nb, phase2_parts, 2)

    res = jnp.dot(mixed_s[...], wout_ref[...], preferred_element_type=F32)
    for b in range(nb):
        xo = x_ref[b] + res[b * TB:(b + 1) * TB]
        if last:
            xo = xo * lax.rsqrt(jnp.mean(xo * xo, axis=-1, keepdims=True) + EPS) * fnw_ref[...]
        act_ref[b] = xo


def _const_spec(shape):
    nd = len(shape)
    return pl.BlockSpec(shape, lambda j: (0,) * nd)


def _prompt_layer(x, l, last, w):
    nb, L, _ = x.shape
    nblk = L // TB
    rows = nb * TB
    xspec = pl.BlockSpec((nb, TB, D), lambda j: (0, j, 0))
    tspec = pl.BlockSpec((TB, LANE), lambda j: (j, 0))

    def per_layer(a):
        nd = a.ndim
        return pl.BlockSpec((None,) + a.shape[1:], lambda j: (l,) + (0,) * (nd - 1))

    stacked = [w[k] for k in ("norm_w", "w_in", "w_out")]
    consts = [w[k] for k in ("rdec", "recum", "rkdec", "retot")]
    ret_nw = [w["ret_norm_w"], w["lb"], w["hgrn_norm_w"]]
    s5 = [w[k] for k in ("s5_A", "s5_Bblk", "s5_Cblk", "s5_D", "s5_glu_w", "s5_glu_b")]
    m2 = [w[k] for k in ("m2_conv_w", "m2_conv_b", "m2_dt_bias", "m2_A_log", "m2_D", "m2_norm_w")]
    pair_consts = [w["bones"], w["hexp"]]
    args = [x, w["cos_p"], w["sin_p"]] + stacked + consts + ret_nw + [w["sel"]] + s5 + m2 + pair_consts
    in_specs = ([xspec, tspec, tspec] + [per_layer(a) for a in stacked] + [_const_spec(a.shape) for a in consts]
                + [per_layer(a) for a in ret_nw] + [_const_spec(w["sel"].shape)]
                + [per_layer(a) for a in s5 + m2] + [_const_spec(a.shape) for a in pair_consts])
    if last:
        args.append(w["final_norm_w"])
        in_specs.append(_const_spec(w["final_norm_w"].shape))
    pair_state = jax.ShapeDtypeStruct((nb, W // LANE, LANE, LANE), F32)
    state_shapes = [pair_state,
                    pair_state,
                    jax.ShapeDtypeStruct((nb, 2 * S5N), F32),
                    pair_state,
                    jax.ShapeDtypeStruct((nb, CONV_K - 1, CONV_CH), F32)]
    out_shape = [jax.ShapeDtypeStruct((nb, L, D), F32)] + state_shapes
    out_specs = [xspec] + [_const_spec(s.shape) for s in state_shapes]
    scratch = [pltpu.VMEM((rows, D), BF16),
               pltpu.VMEM((rows, P_PAD), F32),
               pltpu.VMEM((rows, D), BF16),
               pltpu.VMEM((rows, 2 * S5N), F32),
               pltpu.VMEM((W // LANE, rows, LANE), F32),
               pltpu.VMEM((nb, TB + 8, CONV_CH), F32),
               pltpu.VMEM((W // LANE, rows, SUB * LANE), BF16),
               pltpu.VMEM((W // LANE, rows, LANE), F32),
               pltpu.VMEM((nb, TB, W), F32),
               pltpu.VMEM((nb, TB, W), F32),
               pltpu.VMEM((nb, TB, W), F32)]
    return pl.pallas_call(
        functools.partial(_prompt_layer_body, last),
        grid=(nblk,),
        in_specs=in_specs,
        out_specs=out_specs,
        out_shape=out_shape,
        scratch_shapes=scratch,
        compiler_params=pltpu.CompilerParams(dimension_semantics=("arbitrary",),
                                             vmem_limit_bytes=VMEM_LIMIT),
        name="prompt_layer",
    )(*args)


def _rope_tables(pos):
    half = HD // 2
    inv = 1.0 / (ROPE_BASE ** (jnp.arange(half, dtype=F32) / half))
    ang = pos[:, None] * inv[None, :]
    cos = jnp.cos(ang)
    sin = jnp.sin(ang)
    cos_t = jnp.tile(cos, (1, LANE // half))
    sin_t = jnp.tile(jnp.concatenate([-sin, sin], axis=1), (1, LANE // HD))
    return cos_t, sin_t


def _retention_tables():
    log_gamma = jnp.log1p(-(2.0 ** (-5.0 - jnp.arange(NH, dtype=F32))))
    cum = jnp.cumsum(jnp.broadcast_to(log_gamma, (TB, NH)), axis=0)
    total = cum[-1]
    causal = jnp.tril(jnp.ones((TB, TB), dtype=bool))
    diff = cum[:, None, :] - cum[None, :, :]
    dec = jnp.where(causal[:, :, None], jnp.exp(jnp.where(causal[:, :, None], diff, 0.0)), 0.0)
    rdec = jnp.moveaxis(dec, 2, 0)
    recum = jnp.broadcast_to(jnp.exp(cum).T[:, :, None], (NH, TB, HD))
    rkdec = jnp.broadcast_to(jnp.exp(total[None, :] - cum).T[:, :, None], (NH, TB, HD))
    retot = jnp.broadcast_to(jnp.exp(total)[:, None, None], (NH, 1, HD))
    pair = lambda t: jnp.concatenate([t[0::2], t[1::2]], axis=-1)
    return pair(rdec), pair(recum), pair(rkdec), pair(retot), log_gamma


def _sel_matrix():
    sel = np.zeros((SUB, 2, HD, 2, TB), np.float32)
    for s_ in range(SUB):
        for h2 in range(2):
            sel[s_, h2, :, h2, s_::SUB] = 1.0
    return jnp.asarray(sel.reshape(SUB * LANE, LANE), dtype=BF16)


def _rows(v, width=None):
    v = v.astype(F32)
    if width is not None and v.shape[-1] < width:
        v = jnp.pad(v, ((0, 0), (0, width - v.shape[-1])))
    return v[:, None, :]


def _block_diag(blocks):
    g, r, c = blocks.shape
    eye = jnp.eye(g, dtype=blocks.dtype)
    return jnp.einsum('grc,gh->grhc', blocks, eye).reshape(g * r, g * c)


def _s5_discretise(A_re, A_im, log_dt, B_re, B_im):
    A_re, A_im = A_re.astype(F32), A_im.astype(F32)
    dt = jnp.exp(log_dt.astype(F32))[:, None]
    mag = jnp.exp(A_re * dt)
    ab_re, ab_im = mag * jnp.cos(A_im * dt), mag * jnp.sin(A_im * dt)
    nr, ni = ab_re - 1.0, ab_im
    den = A_re * A_re + A_im * A_im
    f_re = (nr * A_re + ni * A_im) / den
    f_im = (ni * A_re - nr * A_im) / den
    B_re, B_im = B_re.astype(F32), B_im.astype(F32)
    bb_re = f_re[..., None] * B_re - f_im[..., None] * B_im
    bb_im = f_re[..., None] * B_im + f_im[..., None] * B_re
    return ab_re, ab_im, bb_re, bb_im


def _s5_matrices(A_re, A_im, log_dt, B_re, B_im, C_re, C_im):
    ab_re, ab_im, bb_re, bb_im = _s5_discretise(A_re, A_im, log_dt, B_re, B_im)
    bblk = jnp.concatenate([_block_diag(jnp.swapaxes(bb_re, 1, 2)),
                            _block_diag(jnp.swapaxes(bb_im, 1, 2))], axis=1)
    cblk = jnp.concatenate([_block_diag(jnp.swapaxes(C_re.astype(F32), 1, 2)),
                            _block_diag(jnp.swapaxes(-C_im.astype(F32), 1, 2))], axis=0)
    return jnp.stack([ab_re.reshape(-1), ab_im.reshape(-1)], axis=0), bblk, cblk


R_COS, R_SIN, R_RETNW, R_LB, R_HNW, R_S5D, R_GLUB, R_MNW = [i * W for i in range(8)]
R_AR = 8 * W
R_AI = R_AR + S5N
R_DTB = R_AI + S5N
R_ALOG = R_DTB + 8
R_MD = R_ALOG + 8
N_COLP = R_MD + 8


def _pack_w_in(w_in):
    ref_mz, ref_xbc, ref_dt = C_XBC, C_XBC + W, C_XBC + W + CONV_CH
    parts = [w_in[..., :ref_mz], w_in[..., ref_xbc:ref_dt], w_in[..., ref_mz:ref_xbc], w_in[..., ref_dt:]]
    packed = jnp.concatenate(parts, axis=-1)
    return jnp.pad(packed, ((0, 0), (0, 0), (0, P_PAD - P_TOTAL))).astype(BF16)


def _pair_diagonal(st):
    n = st.shape[0]
    blocks = st.reshape(n, W // LANE, LANE // HD, HD, LANE // HD, HD)
    diag = jnp.stack([blocks[:, :, i, :, i, :] for i in range(LANE // HD)], axis=2)
    return diag.reshape(n, NH, HD, HD)


def _prepare(p, lb_all, prompt_len):
    depth = p["norm_w"].shape[0]
    s5_A, bblk, cblk = jax.vmap(_s5_matrices)(p["s5_A_re"], p["s5_A_im"], p["s5_log_dt"], p["s5_B_re"],
                                              p["s5_B_im"], p["s5_C_re"], p["s5_C_im"])
    cos_p, sin_p = _rope_tables(jnp.arange(prompt_len, dtype=F32))
    rdec, recum, rkdec, retot, log_gamma = _retention_tables()
    cos_s, sin_s = _rope_tables(PAST_LEN + jnp.arange(1, dtype=F32))
    w = dict(
        norm_w=_rows(p["norm_w"]),
        w_in=_pack_w_in(p["w_in"]),
        w_out=p["w_out"].astype(BF16),
        ret_norm_w=_rows(p["ret_norm_w"]),
        lb=_rows(lb_all),
        hgrn_norm_w=_rows(p["hgrn_norm_w"]),
        s5_A=s5_A,
        s5_Bblk=bblk.astype(BF16),
        s5_Cblk=cblk.astype(BF16),
        s5_D=_rows(p["s5_D"]),
        s5_glu_w=p["s5_glu_w"].astype(BF16),
        s5_glu_b=_rows(p["s5_glu_b"]),
        m2_conv_w=p["m2_conv_w"].astype(F32),
        m2_conv_b=_rows(p["m2_conv_b"]),
        m2_dt_bias=_rows(jnp.repeat(p["m2_dt_bias"], HD, axis=1)),
        m2_A_log=_rows(jnp.repeat(p["m2_A_log"], HD, axis=1)),
        m2_D=_rows(jnp.repeat(p["m2_D"], HD, axis=1)),
        m2_norm_w=_rows(p["m2_norm_w"]),
        final_norm_w=p["final_norm_w"].astype(F32).reshape(1, D),
    )
    tile2 = lambda t: jnp.broadcast_to(jnp.tile(t[0], W // LANE), (depth, W))
    colp = jnp.concatenate(
        [tile2(cos_s), tile2(sin_s), w["ret_norm_w"][:, 0], w["lb"][:, 0], w["hgrn_norm_w"][:, 0],
         w["s5_D"][:, 0], w["s5_glu_b"][:, 0], w["m2_norm_w"][:, 0], s5_A[:, 0], s5_A[:, 1],
         _rows(p["m2_dt_bias"], 8)[:, 0], _rows(p["m2_A_log"], 8)[:, 0], _rows(p["m2_D"], 8)[:, 0]], axis=1)
    w.update(
        colp=jnp.broadcast_to(colp[:, :, None], (depth, N_COLP, LANE)),
        s5_BblkT=jnp.swapaxes(w["s5_Bblk"], 1, 2),
        s5_CblkT=jnp.swapaxes(w["s5_Cblk"], 1, 2),
        s5_glu_wT=jnp.swapaxes(w["s5_glu_w"], 1, 2),
        rgam=jnp.broadcast_to(jnp.exp(log_gamma)[:, None, None], (NH, 8, LANE)),
        cos_p=cos_p, sin_p=sin_p, rdec=rdec, recum=recum, rkdec=rkdec, retot=retot, sel=_sel_matrix(),
        bones=jnp.asarray(np.kron(np.eye(LANE // HD), np.full((HD, HD), 1.0 / HD)), dtype=BF16),
        hexp=jnp.asarray(np.kron(np.eye(LANE, NH), np.ones((1, HD))), dtype=F32),
    )
    return w


KC = 16
NKC = HD // KC
STEPS = NH * NKC


def _rotary_cols(x, cos, sin_signed):
    half = HD // 2
    parts = []
    for h in range(NH):
        parts += [x[h * HD + half:(h + 1) * HD], x[h * HD:h * HD + half]]
    return x * cos + jnp.concatenate(parts, axis=0) * sin_signed


def _expand_rows(dst, x):
    for c in range(x.shape[0]):
        dst[c] = jnp.broadcast_to(x[c:c + 1, :], (8, LANE))


def _sample_body(n_steps, x_ref, normw_ref, win_ref, wout_ref, fnw_ref, colp_ref, sbt_ref, sct_ref,
                 gluwt_ref, cw_ref, cb_ref, rgam_ref,
                 ret_in, hg_in, m2_in, s5re_in, s5im_in, buf_in,
                 y_ref, ret_out, hg_out, m2_out, s5re_out, s5im_out, buf_out,
                 xs_s, pt_s, vt_s, ot_s, mixt_s, o_s, hp_s,
                 kret_s, qret_s, khg_s, qhg_s, ahg_s, km2_s, qm2_s):
    i = pl.program_id(0)
    r = i % STEPS
    h = r // NKC
    kc = r % NKC

    def cp(r0, n=W):
        return colp_ref[r0:r0 + n, :]

    def pc(c0, w):
        return pt_s[c0:c0 + w, :]

    @pl.when(i == 0)
    def _load():
        xs_s[...] = x_ref[...]

    @pl.when(r == 0)
    def _prep():
        x = xs_s[...]
        hn = x * lax.rsqrt(jnp.mean(x * x, axis=-1, keepdims=True) + EPS) * normw_ref[...]
        proj = jnp.dot(hn.astype(BF16), win_ref[...], preferred_element_type=F32)

        xnew = proj[:, C_XBC:C_XBC + CONV_CH]
        acc = cb_ref[...] + xnew * cw_ref[CONV_K - 1:CONV_K, :]
        for t in range(CONV_K - 1):
            acc = acc + buf_in[t] * cw_ref[t:t + 1, :]
        for t in range(CONV_K - 2):
            buf_out[t] = buf_in[t + 1]
        buf_out[CONV_K - 2] = xnew
        xbc = _silu(acc)

        for t in range(P_PAD // LANE):
            c0 = t * LANE
            if C_XBC <= c0 < C_XBC + CONV_CH:
                tile = xbc[:, c0 - C_XBC:c0 - C_XBC + LANE]
            else:
                tile = proj[:, c0:c0 + LANE]
            pt_s[c0:c0 + LANE, :] = tile.T

        cos, sin = cp(R_COS), cp(R_SIN)
        _expand_rows(kret_s, _rotary_cols(pc(C_RK, W), cos, sin) * (HD ** -0.5))
        _expand_rows(qret_s, _rotary_cols(pc(C_RQ, W), cos, sin))
        vt_s[0] = pc(C_RV, W)

        fr = pc(C_GF, W)
        lb = cp(R_LB)
        logf = _log_sigmoid(fr) + jnp.log(1.0 + lb * jnp.exp(jnp.minimum(-fr, EXP_CLIP)))
        _expand_rows(ahg_s, jnp.exp(logf))
        _expand_rows(khg_s, (1.0 - lb) * jax.nn.sigmoid(-fr))
        _expand_rows(qhg_s, _silu(pc(C_GQ, W)))
        vt_s[1] = pc(C_GI, W)

        dt8 = _softplus(pc(C_DT, 8) + cp(R_DTB, 8))
        adec8 = jnp.exp(dt8 * (-jnp.exp(cp(R_ALOG, 8))))
        for hh in range(NH):
            hp_s[hh] = jnp.broadcast_to(adec8[hh:hh + 1, :], (8, LANE))
            vt_s[2, hh * HD:(hh + 1) * HD, :] = pc(C_XBC + hh * HD, HD) * dt8[hh:hh + 1, :]
        _expand_rows(km2_s, pc(C_XBC + W, 2 * HD))
        _expand_rows(qm2_s, pc(C_XBC + W + 2 * HD, 2 * HD))

        u = pc(C_SU, W)
        bu = jnp.dot(sbt_ref[...], u.astype(BF16), preferred_element_type=F32)
        hr, hi = s5re_in[...], s5im_in[...]
        ar, ai = cp(R_AR, S5N), cp(R_AI, S5N)
        nr = ar * hr - ai * hi + bu[0:S5N]
        ni = ar * hi + ai * hr + bu[S5N:2 * S5N]
        s5re_out[...] = nr
        s5im_out[...] = ni
        hcat = jnp.concatenate([nr, ni], axis=0).astype(BF16)
        sy = jnp.dot(sct_ref[...], hcat, preferred_element_type=F32) + cp(R_S5D) * u
        gy = _gelu_tanh(sy)
        glu = jnp.dot(gluwt_ref[...], gy.astype(BF16), preferred_element_type=F32) + cp(R_GLUB)
        mixt_s[2 * W:3 * W, :] = gy * jax.nn.sigmoid(glu) * _silu(pc(C_SG, W))

    @pl.when(kc == 0)
    def _zero():
        o_s[...] = jnp.zeros(o_s.shape, F32)

    hrow = pl.multiple_of(h * HD, HD)
    cbase = h * HD + kc * KC
    gbase = (h // 2) * HD + kc * KC

    def update(m, st_in, st_out, kx, qx, base, decay):
        v3 = vt_s[m, pl.ds(hrow, HD), :].reshape(HD // 8, 8, LANE)

        def body(kk, o):
            s_new = decay(kk) * st_in[kk].reshape(HD // 8, 8, LANE) + kx[base + kk] * v3
            st_out[kk] = s_new.reshape(HD, LANE)
            return o + qx[base + kk] * s_new
        o_s[m] = lax.fori_loop(0, KC, body, o_s[m], unroll=2)

    gam = rgam_ref[h]
    update(0, ret_in, ret_out, kret_s, qret_s, cbase, lambda kk: gam)
    update(1, hg_in, hg_out, khg_s, qhg_s, cbase, lambda kk: ahg_s[cbase + kk])
    adec = hp_s[h]
    update(2, m2_in, m2_out, km2_s, qm2_s, gbase, lambda kk: adec)

    @pl.when(kc == NKC - 1)
    def _head_done():
        for m in range(3):
            ot_s[m, pl.ds(hrow, HD), :] = o_s[m].reshape(HD, LANE)

    @pl.when(r == STEPS - 1)
    def _finish():
        def head_rms_cols(o):
            parts = []
            for hh in range(NH):
                seg = o[hh * HD:(hh + 1) * HD]
                parts.append(seg * lax.rsqrt(jnp.mean(seg * seg, axis=0, keepdims=True) + EPS))
            return jnp.concatenate(parts, axis=0)

        mixt_s[0:W, :] = head_rms_cols(ot_s[0]) * cp(R_RETNW) * _silu(pc(C_RG, W))
        mixt_s[W:2 * W, :] = head_rms_cols(ot_s[1]) * cp(R_HNW) * _silu(pc(C_GG, W))
        md8 = cp(R_MD, 8)
        ym = jnp.concatenate([ot_s[2, hh * HD:(hh + 1) * HD, :] + md8[hh:hh + 1, :] * pc(C_XBC + hh * HD, HD)
                              for hh in range(NH)], axis=0)
        my = ym * _silu(pc(C_MZ, W))
        mixt_s[3 * W:4 * W, :] = my * lax.rsqrt(jnp.mean(my * my, axis=0, keepdims=True) + EPS) * cp(R_MNW)
        mixed = jnp.concatenate([mixt_s[t * LANE:(t + 1) * LANE, :].T for t in range(D // LANE)], axis=1)
        xo = xs_s[...] + jnp.dot(mixed.astype(BF16), wout_ref[...], preferred_element_type=F32)
        xs_s[...] = xo

        @pl.when(i == n_steps - 1)
        def _final_norm():
            y_ref[...] = xo * lax.rsqrt(jnp.mean(xo * xo, axis=-1, keepdims=True) + EPS) * fnw_ref[...]


def _sample_step(x, w, ret, hg, m2, s5re, s5im, buf):
    depth = ret.shape[0]
    n = x.shape[0]
    n_steps = depth * STEPS
    lay = lambda i: i // STEPS

    def per_layer(a):
        nd = a.ndim
        return pl.BlockSpec((None,) + a.shape[1:], lambda i: (lay(i),) + (0,) * (nd - 1))

    st_spec = pl.BlockSpec((None, None, KC, HD, LANE),
                           lambda i: (lay(i), (i % STEPS) // NKC, i % NKC, 0, 0))
    weights = [w["norm_w"], w["w_in"], w["w_out"]]
    tables = [w["colp"], w["s5_BblkT"], w["s5_CblkT"], w["s5_glu_wT"], w["m2_conv_w"], w["m2_conv_b"]]
    in_specs = ([_const_spec(x.shape)] + [per_layer(a) for a in weights] + [_const_spec(w["final_norm_w"].shape)]
                + [per_layer(a) for a in tables] + [_const_spec(w["rgam"].shape)]
                + [st_spec, st_spec, st_spec, per_layer(s5re), per_layer(s5im), per_layer(buf)])
    out_shape = [jax.ShapeDtypeStruct((n, D), F32)] + [jax.ShapeDtypeStruct(a.shape, F32)
                                                       for a in (ret, hg, m2, s5re, s5im, buf)]
    out_specs = [_const_spec((n, D)), st_spec, st_spec, st_spec, per_layer(s5re), per_layer(s5im),
                 per_layer(buf)]
    expand = lambda c: pltpu.VMEM((c, 8, LANE), F32)
    scratch = [pltpu.VMEM((n, D), F32),
               pltpu.VMEM((P_PAD, LANE), F32),
               pltpu.VMEM((3, W, LANE), F32),
               pltpu.VMEM((3, W, LANE), F32),
               pltpu.VMEM((D, LANE), F32),
               pltpu.VMEM((3, HD // 8, 8, LANE), F32),
               pltpu.VMEM((NH, 8, LANE), F32),
               expand(W), expand(W), expand(W), expand(W), expand(W), expand(2 * HD), expand(2 * HD)]
    return pl.pallas_call(
        functools.partial(_sample_body, n_steps),
        grid=(n_steps,),
        in_specs=in_specs,
        out_specs=out_specs,
        out_shape=out_shape,
        scratch_shapes=scratch,
        compiler_params=pltpu.CompilerParams(dimension_semantics=("arbitrary",),
                                             vmem_limit_bytes=VMEM_LIMIT),
        name="sample_step",
    )(x, *weights, w["final_norm_w"], *tables, w["rgam"], ret, hg, m2, s5re, s5im, buf)


def kernel(x_prompt, x_sample, state_ret, state_hgrn, state_s5_re, state_s5_im, state_m2_ssm,
           state_m2_conv, norm_w, w_in, ret_norm_w, hgrn_lb_logits, hgrn_norm_w, s5_A_re, s5_A_im,
           s5_log_dt, s5_B_re, s5_B_im, s5_C_re, s5_C_im, s5_D, s5_glu_w, s5_glu_b, m2_conv_w,
           m2_conv_b, m2_dt_bias, m2_A_log, m2_D, m2_norm_w, w_out, final_norm_w):
    p = dict(norm_w=norm_w, w_in=w_in, ret_norm_w=ret_norm_w, hgrn_norm_w=hgrn_norm_w,
             s5_A_re=s5_A_re, s5_A_im=s5_A_im, s5_log_dt=s5_log_dt, s5_B_re=s5_B_re, s5_B_im=s5_B_im,
             s5_C_re=s5_C_re, s5_C_im=s5_C_im, s5_D=s5_D, s5_glu_w=s5_glu_w, s5_glu_b=s5_glu_b,
             m2_conv_w=m2_conv_w, m2_conv_b=m2_conv_b, m2_dt_bias=m2_dt_bias, m2_A_log=m2_A_log,
             m2_D=m2_D, m2_norm_w=m2_norm_w, w_out=w_out, final_norm_w=final_norm_w)
    depth = norm_w.shape[0]
    nbp, lp, _ = x_prompt.shape
    nbs = x_sample.shape[0]

    lb_sm = jax.nn.softmax(hgrn_lb_logits.astype(F32), axis=0)
    lb_all = jnp.clip(jnp.cumsum(lb_sm, axis=0) - lb_sm[0], 0.0, 1.0)

    w = _prepare(p, lb_all, lp)

    xp = x_prompt
    pst = []
    for l in range(depth):
        outs = _prompt_layer(xp, l, l == depth - 1, w)
        xp = outs[0]
        ret, hg, s5, m2, buf = outs[-5:]
        ret, hg, m2 = _pair_diagonal(ret), _pair_diagonal(hg), _pair_diagonal(m2)
        pst.append((ret, hg, s5[:, :S5N].reshape(nbp, S5G, S5P), s5[:, S5N:].reshape(nbp, S5G, S5P),
                    m2, buf))
    yp = xp

    seq_last = lambda a: jnp.moveaxis(a.astype(F32), 1, -1)
    ys, ret, hg, m2, s5re, s5im, buf = _sample_step(
        x_sample.reshape(nbs, D), w,
        seq_last(state_ret), seq_last(state_hgrn), seq_last(state_m2_ssm),
        seq_last(state_s5_re).reshape(depth, S5N, nbs), seq_last(state_s5_im).reshape(depth, S5N, nbs),
        jnp.swapaxes(state_m2_conv.astype(F32), 1, 2))
    seq_second = lambda a: jnp.moveaxis(a, -1, 1)

    stk = lambda i: jnp.stack([s[i] for s in pst], axis=0)
    return (yp, ys.reshape(nbs, 1, D),
            stk(0), stk(1), stk(2), stk(3), stk(4), stk(5),
            seq_second(ret), seq_second(hg), seq_second(s5re.reshape(depth, S5G, S5P, nbs)),
            seq_second(s5im.reshape(depth, S5G, S5P, nbs)), seq_second(m2), jnp.swapaxes(buf, 1, 2))
```

```python
import functools
import math

import numpy as np
import jax
import jax.numpy as jnp
from jax import lax
from jax.experimental import pallas as pl
from jax.experimental.pallas import tpu as pltpu

F32 = jnp.float32
BF16 = jnp.bfloat16
HI = lax.Precision.HIGHEST

D = 1024
W = 256
NH = 4
HD = 64
S5G = 16
S5C = 16
S5P = 64
S5N = S5G * S5P
CONV_CH = 512
CONV_K = 4
TB = 64
SUB = 16
NSUB = TB // SUB
EPS = 1e-6
EXP_CLIP = 60.0
ROPE_BASE = 10000.0
PAST_LEN = 16384

C_RQ, C_RK, C_RV, C_RG = 0, 256, 512, 768
C_GQ, C_GF, C_GI, C_GG = 1024, 1280, 1536, 1792
C_SU, C_SG = 2048, 2304
C_MZ, C_XBC, C_DT = 2560, 2816, 3328
P_TOTAL = 3332
PCH = 1152
NPC = 3
P_PAD = NPC * PCH
LANE = 128
VMEM_LIMIT = 56 * 1024 * 1024


def _silu(x):
    return x * jax.nn.sigmoid(x)


def _softplus(x):
    return jnp.maximum(x, 0.0) + jnp.log(1.0 + jnp.exp(-jnp.abs(x)))


def _log_sigmoid(x):
    return jnp.minimum(x, 0.0) - jnp.log(1.0 + jnp.exp(-jnp.abs(x)))


def _round_robin(gens):
    gens = list(gens)
    while gens:
        alive = []
        for g in gens:
            try:
                next(g)
                alive.append(g)
            except StopIteration:
                pass
        gens = alive


def _for_sequences(nb, parts, group):
    def body(i, c):
        built = [parts(i * group + k) for k in range(group)]
        _round_robin([g for gens, _ in built for g in gens])
        for _, finish in built:
            if finish is not None:
                finish()
        return c
    lax.fori_loop(0, nb // group, body, 0)


def _gelu_tanh(x):
    c = math.sqrt(2.0 / math.pi)
    return 0.5 * x * (1.0 + jnp.tanh(c * (x + 0.044715 * (x * x * x))))


def _dot(a, b):
    return jnp.dot(a.astype(BF16), b.astype(BF16), preferred_element_type=F32)


def _dot_hi(a, b):
    return jnp.dot(a, b, precision=HI, preferred_element_type=F32)


def _dot_tn_hi(a, b):
    return lax.dot_general(a, b, (((0,), (0,)), ((), ())), precision=HI,
                           preferred_element_type=F32)


def _rot_half_partner(x):
    lane = lax.broadcasted_iota(jnp.int32, x.shape, 1)
    first = (lane % HD) < (HD // 2)
    return jnp.where(first, pltpu.roll(x, LANE - HD // 2, 1), pltpu.roll(x, HD // 2, 1))


def _rotary(x, cos, sin_signed):
    parts = []
    for i in range(W // LANE):
        xi = x[:, i * LANE:(i + 1) * LANE]
        parts.append(xi * cos + _rot_half_partner(xi) * sin_signed)
    return jnp.concatenate(parts, axis=1)


def _prompt_layer_body(last, *refs):
    (x_ref, cos_ref, sin_ref, normw_ref, win_ref, wout_ref,
     rdec_ref, recum_ref, rkdec_ref, retot_ref, retnw_ref,
     lb_ref, hnw_ref, sel_ref,
     sA_ref, sB_ref, sC_ref, sD_ref, gluw_ref, glub_ref,
     cw_ref, cb_ref, dtb_ref, alog_ref, md_ref, mnw_ref, bones_ref, hexp_ref) = refs[:28]
    refs = refs[28:]
    if last:
        fnw_ref = refs[0]
        refs = refs[1:]
    act_ref = refs[0]
    refs = refs[1:]
    (ret_ref, hg_ref, s5_ref, m2_ref, m2buf_ref,
     hn_s, proj_s, mixed_s, bu_s, u_s, cv_s, p_s, dg_s, hq_s, hk_s, cum_s, hret_s, hhg_s, hm2_s) = refs
    j = pl.program_id(0)
    nb = x_ref.shape[0]

    @pl.when(j == 0)
    def _init():
        hret_s[...] = jnp.zeros(hret_s.shape, F32)
        hhg_s[...] = jnp.zeros(hhg_s.shape, F32)
        s5_ref[...] = jnp.zeros(s5_ref.shape, F32)
        hm2_s[...] = jnp.zeros(hm2_s.shape, F32)
        cv_s[...] = jnp.zeros(cv_s.shape, F32)

    ti = lax.broadcasted_iota(jnp.int32, (TB, TB), 0)
    si = lax.broadcasted_iota(jnp.int32, (TB, TB), 1)
    causal = si <= ti
    tri_l = causal.astype(F32)
    ones_tt = jnp.ones((TB, TB), F32)
    pr = lax.broadcasted_iota(jnp.int32, (LANE, LANE), 0)
    pc_ = lax.broadcasted_iota(jnp.int32, (LANE, LANE), 1)
    pairmask = (pr // HD) == (pc_ // HD)
    t2 = lax.broadcasted_iota(jnp.int32, (TB, LANE), 0)
    l2 = lax.broadcasted_iota(jnp.int32, (TB, LANE), 1)
    causal2 = (l2 % HD) <= t2
    tri_u2 = (t2 <= (l2 % HD)).astype(F32)
    first_head = l2 < HD
    subdiag2 = (t2 // SUB) == ((l2 % HD) // SUB)
    srcblock = [pairmask & (((pc_ % HD) // SUB) == jb) for jb in range(NSUB - 1)]
    ones_tl = jnp.ones((TB, LANE), F32)

    def dup_t(x2):
        return jnp.concatenate([x2, x2], axis=0).T

    def blockdiag2(x2):
        return jnp.where(pairmask, jnp.concatenate([x2, x2], axis=0), 0.0)

    def norm_body(b, c):
        xb = x_ref[b]
        hn = xb * lax.rsqrt(jnp.mean(xb * xb, axis=-1, keepdims=True) + EPS) * normw_ref[...]
        hn_s[pl.ds(pl.multiple_of(b * TB, TB), TB), :] = hn.astype(BF16)
        return c
    lax.fori_loop(0, nb, norm_body, 0)
    for c in range(NPC):
        cs = slice(c * PCH, (c + 1) * PCH)
        proj_s[:, cs] = jnp.dot(hn_s[...], win_ref[:, cs], preferred_element_type=F32)

    cos = cos_ref[...]
    sin = sin_ref[...]

    def phase1_parts(b):
        r0 = pl.multiple_of(b * TB, TB)

        def pj(c0, w):
            return proj_s[pl.ds(r0, TB), c0:c0 + w]

        u = pj(C_SU, W)
        for c_ in range(W // LANE):
            u_s[c_, pl.ds(b, TB, stride=nb), :] = u[:, c_ * LANE:(c_ + 1) * LANE]

        rq = _rotary(pj(C_RQ, W), cos, sin)
        rk = _rotary(pj(C_RK, W), cos, sin) * (HD ** -0.5)
        rv = pj(C_RV, W)
        rg = pj(C_RG, W)

        def ret_pair(p):
            ls = slice(p * LANE, (p + 1) * LANE)
            q2, k2, v2 = rq[:, ls], rk[:, ls], rv[:, ls]
            hb = hret_s[b, p]
            kt = dup_t(k2)
            s_raw = _dot(q2, jnp.where(pairmask, kt, 0.0))
            oi = _dot(q2, hb)
            kv = _dot(kt[:, 0:HD], v2 * rkdec_ref[p])
            yield
            o = _dot(s_raw * rdec_ref[p], blockdiag2(v2))
            yield
            o = o + oi * recum_ref[p]
            hret_s[b, p] = jnp.where(pairmask, retot_ref[p] * hb + kv, 0.0)
            ms = _dot(o * o, bones_ref[...])
            yield
            o = o * lax.rsqrt(ms + EPS) * retnw_ref[:, ls] * _silu(rg[:, ls])
            mixed_s[pl.ds(r0, TB), 0 * W + p * LANE:0 * W + (p + 1) * LANE] = o.astype(BF16)

        cv_s[b, 8:8 + TB, :] = pj(C_XBC, CONV_CH)
        acc = cb_ref[...] + cv_s[b, 5:5 + TB, :] * cw_ref[0:1, :]
        for i in range(1, CONV_K):
            acc = acc + cv_s[b, 5 + i:5 + i + TB, :] * cw_ref[i:i + 1, :]
        tail = cv_s[b, TB + 5:TB + 8, :]
        cv_s[b, 5:8, :] = tail
        m2buf_ref[b] = tail
        xbc = _silu(acc)
        xm = xbc[:, 0:W]
        bm = xbc[:, W:W + 2 * HD]
        cm = xbc[:, W + 2 * HD:W + 4 * HD]
        bm_sw = pltpu.roll(bm, HD, 1)
        cm_sw = pltpu.roll(cm, HD, 1)
        dt_b = _softplus(_dot_hi(pj(C_DT, LANE), hexp_ref[...]) + dtb_ref[...])
        la_b = dt_b * (-jnp.exp(alog_ref[...]))
        cum_b = _dot_hi(tri_l, la_b)
        ys = [None] * (W // LANE)

        def m2_pair(p):
            ls = slice(p * LANE, (p + 1) * LANE)
            b2 = jnp.where(first_head, bm, bm_sw) if p == 0 else jnp.where(first_head, bm_sw, bm)
            c2 = jnp.where(first_head, cm, cm_sw) if p == 0 else jnp.where(first_head, cm_sw, cm)
            x2, dt2, cum2 = xm[:, ls], dt_b[:, ls], cum_b[:, ls]
            xdt2 = x2 * dt2
            r2 = _dot_hi(ones_tt, la_b[:, ls] * tri_u2)
            tot2 = cum2[TB - 1:TB, :]
            hb = hm2_s[b, p]
            bt = dup_t(b2)
            s_raw = _dot(c2, jnp.where(pairmask, bt, 0.0))
            oi = _dot(c2, hb)
            kv = _dot(bt[:, 0:HD], xdt2 * jnp.exp(tot2 - cum2))
            decay = jnp.where(causal2, jnp.exp(jnp.minimum(cum2 - r2, 0.0)), 0.0)
            yield
            o = _dot(s_raw * decay, blockdiag2(xdt2))
            yield
            hm2_s[b, p] = jnp.where(pairmask, jnp.exp(tot2) * hb + kv, 0.0)
            ys[p] = o + oi * jnp.exp(cum2) + md_ref[:, ls] * x2

        fr = pj(C_GF, W)
        lb = lb_ref[...]
        logf = _log_sigmoid(fr) + jnp.log(1.0 + lb * jnp.exp(jnp.minimum(-fr, EXP_CLIP)))
        hq = _silu(pj(C_GQ, W))
        hk = (1.0 - lb) * jax.nn.sigmoid(-fr)
        cum = _dot_hi(tri_l, logf)
        hq_s[b] = hq
        hk_s[b] = hk
        cum_s[b] = cum
        t8 = lax.broadcasted_iota(jnp.int32, (8, LANE), 0)

        def diag_products(p):
            ls = slice(p * LANE, (p + 1) * LANE)
            for s_ in range(SUB):
                pieces = []
                for i in range(NSUB):
                    kb = jnp.broadcast_to(hk_s[b, i * SUB + s_:i * SUB + s_ + 1, ls], (8, LANE))
                    cb = jnp.broadcast_to(cum_s[b, i * SUB + s_:i * SUB + s_ + 1, ls], (8, LANE))
                    for half in range(SUB // 8):
                        rows = slice(i * SUB + half * 8, i * SUB + half * 8 + 8)
                        if half * 8 + 7 < s_:
                            pieces.append(jnp.zeros((8, LANE), F32))
                        elif half * 8 >= s_:
                            pieces.append(hq[rows, ls] * kb * jnp.exp(cum[rows, ls] - cb))
                        else:
                            e = jnp.exp(jnp.minimum(cum[rows, ls] - cb, 0.0))
                            pieces.append(jnp.where(t8 + half * 8 >= s_, hq[rows, ls] * kb * e, 0.0))
                pv = jnp.concatenate(pieces, axis=0)
                p_s[p, pl.ds(r0, TB), s_ * LANE:(s_ + 1) * LANE] = pv.astype(BF16)
                if s_ % 2 == 1:
                    yield

        def finish():
            my = jnp.concatenate(ys, axis=1) * _silu(pj(C_MZ, W))
            om = my * lax.rsqrt(jnp.mean(my * my, axis=-1, keepdims=True) + EPS) * mnw_ref[...]
            mixed_s[pl.ds(r0, TB), 3 * W:4 * W] = om.astype(BF16)

        gens = ([ret_pair(p) for p in range(W // LANE)] + [m2_pair(p) for p in range(W // LANE)]
                + [diag_products(p) for p in range(W // LANE)])
        return gens, finish

    _for_sequences(nb, phase1_parts, 1)

    for p in range(W // LANE):
        dg_s[p] = jnp.dot(p_s[p], sel_ref[...], preferred_element_type=F32)

    u_tb = jnp.concatenate([u_s[c_] for c_ in range(W // LANE)], axis=1)
    bu_s[...] = _dot(u_tb, sB_ref[...])
    ar = jnp.broadcast_to(sA_ref[0:1, :], (nb, S5N))
    ai = jnp.broadcast_to(sA_ref[1:2, :], (nb, S5N))

    def scan_body(t, carry):
        hr, hi = carry
        row = pl.multiple_of(t * nb, nb)
        nr = ar * hr - ai * hi + bu_s[pl.ds(row, nb), 0:S5N]
        ni = ar * hi + ai * hr + bu_s[pl.ds(row, nb), S5N:2 * S5N]
        bu_s[pl.ds(row, nb), 0:S5N] = nr
        bu_s[pl.ds(row, nb), S5N:2 * S5N] = ni
        return nr, ni
    hr, hi = lax.fori_loop(0, TB, scan_body, (s5_ref[:, 0:S5N], s5_ref[:, S5N:2 * S5N]))
    s5_ref[:, 0:S5N] = hr
    s5_ref[:, S5N:2 * S5N] = hi
    ch_tb = _dot(bu_s[...], sC_ref[...])
    for c_ in range(W // LANE):
        u_s[c_] = ch_tb[:, c_ * LANE:(c_ + 1) * LANE]

    def phase2_parts(b):
        r0 = pl.multiple_of(b * TB, TB)

        def pj(c0, w):
            return proj_s[pl.ds(r0, TB), c0:c0 + w]

        def s5_out():
            chs = jnp.concatenate([u_s[c_, pl.ds(b, TB, stride=nb), :] for c_ in range(W // LANE)],
                                  axis=1)
            gy = _gelu_tanh(chs + sD_ref[...] * pj(C_SU, W))
            glu = _dot(gy, gluw_ref[...])
            yield
            os5 = gy * jax.nn.sigmoid(glu + glub_ref[...]) * _silu(pj(C_SG, W))
            mixed_s[pl.ds(r0, TB), 2 * W:3 * W] = os5.astype(BF16)

        cum = cum_s[b]
        rr = jnp.concatenate(
            [jnp.zeros((SUB, W), F32)]
            + [jnp.broadcast_to(cum_s[b, i * SUB - 1:i * SUB, :], (SUB, W))
               for i in range(1, NSUB)], axis=0)
        ee = jnp.concatenate(
            [jnp.broadcast_to(cum_s[b, i * SUB + SUB - 1:i * SUB + SUB, :], (SUB, W))
             for i in range(NSUB)], axis=0)
        lastrow = lax.broadcasted_iota(jnp.int32, (TB, W), 0) == TB - 1
        totc = _dot_tn_hi(jnp.where(lastrow, cum, 0.0), ones_tl)
        hq = hq_s[b]
        hk = hk_s[b]
        hv = pj(C_GI, W)
        gg = pj(C_GG, W)
        qt = hq * jnp.exp(cum - rr)
        kh_ = hk * jnp.exp(ee - cum)
        qe = hq * jnp.exp(cum)
        tot = cum[TB - 1:TB, :]
        kend = kh_ * jnp.exp(tot - ee)
        trow = lax.broadcasted_iota(jnp.int32, (TB, W), 0) // SUB
        qx = []
        for jb in range(NSUB - 1):
            eb = jnp.broadcast_to(ee[jb * SUB:jb * SUB + 1, :], (TB, W))
            qx.append(jnp.where(trow > jb, qt * jnp.exp(jnp.minimum(rr - eb, 0.0)), 0.0))

        def hg_pair(p):
            ls = slice(p * LANE, (p + 1) * LANE)
            hb = hhg_s[b, p]
            hv2 = hv[:, ls]
            kt = dup_t(kh_[:, ls])
            off = _dot(qx[0][:, ls], jnp.where(srcblock[0], kt, 0.0))
            for jb in range(1, NSUB - 1):
                off = off + _dot(qx[jb][:, ls], jnp.where(srcblock[jb], kt, 0.0))
            oi = _dot(qe[:, ls], hb)
            kv = _dot(dup_t(kend[:, ls])[:, 0:HD], hv2)
            yield
            o = _dot(jnp.where(subdiag2, dg_s[p, pl.ds(r0, TB), :], 0.0) + off, blockdiag2(hv2))
            yield
            hhg_s[b, p] = jnp.where(pairmask, jnp.exp(totc[ls, :]) * hb + kv, 0.0)
            o = o + oi
            ms = _dot(o * o, bones_ref[...])
            yield
            o = o * lax.rsqrt(ms + EPS) * hnw_ref[:, ls] * _silu(gg[:, ls])
            mixed_s[pl.ds(r0, TB), 1 * W + p * LANE:1 * W + (p + 1) * LANE] = o.astype(BF16)

        return [s5_out()] + [hg_pair(p) for p in range(W // LANE)], None

    _for_sequences(nb, phase2_parts, 2)

    @pl.when(j == pl.num_programs(0) - 1)
    def _emit_states():
        for b in range(nb):
            for h in range(NH):
                p, h2 = divmod(h, 2)
                blk = (slice(h2 * HD, (h2 + 1) * HD),) * 2
                ret_ref[b, h] = hret_s[(b, p) + blk]
                hg_ref[b, h] = hhg_s[(b, p) + blk]
                m2_ref[b, h] = hm2_s[(b, p) + blk]

    res = jnp.dot(mixed_s[...], wout_ref[...], preferred_element_type=F32)
    for b in range(nb):
        xo = x_ref[b] + res[b * TB:(b + 1) * TB]
        if last:
            xo = xo * lax.rsqrt(jnp.mean(xo * xo, axis=-1, keepdims=True) + EPS) * fnw_ref[...]
        act_ref[b] = xo


def _const_spec(shape):
    nd = len(shape)
    return pl.BlockSpec(shape, lambda j: (0,) * nd)


def _prompt_layer(x, l, last, w):
    nb, L, _ = x.shape
    nblk = L // TB
    rows = nb * TB
    xspec = pl.BlockSpec((nb, TB, D), lambda j: (0, j, 0))
    tspec = pl.BlockSpec((TB, LANE), lambda j: (j, 0))

    def per_layer(a):
        nd = a.ndim
        return pl.BlockSpec((None,) + a.shape[1:], lambda j: (l,) + (0,) * (nd - 1))

    stacked = [w[k] for k in ("norm_w", "w_in", "w_out")]
    consts = [w[k] for k in ("rdec", "recum", "rkdec", "retot")]
    ret_nw = [w["ret_norm_w"], w["lb"], w["hgrn_norm_w"]]
    s5 = [w[k] for k in ("s5_A", "s5_Bblk", "s5_Cblk", "s5_D", "s5_glu_w", "s5_glu_b")]
    m2 = [w[k] for k in ("m2_conv_w", "m2_conv_b", "m2_dt_bias", "m2_A_log", "m2_D", "m2_norm_w")]
    pair_consts = [w["bones"], w["hexp"]]
    args = [x, w["cos_p"], w["sin_p"]] + stacked + consts + ret_nw + [w["sel"]] + s5 + m2 + pair_consts
    in_specs = ([xspec, tspec, tspec] + [per_layer(a) for a in stacked] + [_const_spec(a.shape) for a in consts]
                + [per_layer(a) for a in ret_nw] + [_const_spec(w["sel"].shape)]
                + [per_layer(a) for a in s5 + m2] + [_const_spec(a.shape) for a in pair_consts])
    if last:
        args.append(w["final_norm_w"])
        in_specs.append(_const_spec(w["final_norm_w"].shape))
    state_shapes = [jax.ShapeDtypeStruct((nb, NH, HD, HD), F32),
                    jax.ShapeDtypeStruct((nb, NH, HD, HD), F32),
                    jax.ShapeDtypeStruct((nb, 2 * S5N), F32),
                    jax.ShapeDtypeStruct((nb, NH, HD, HD), F32),
                    jax.ShapeDtypeStruct((nb, CONV_K - 1, CONV_CH), F32)]
    out_shape = [jax.ShapeDtypeStruct((nb, L, D), F32)] + state_shapes
    out_specs = [xspec] + [_const_spec(s.shape) for s in state_shapes]
    scratch = [pltpu.VMEM((rows, D), BF16),
               pltpu.VMEM((rows, P_PAD), F32),
               pltpu.VMEM((rows, D), BF16),
               pltpu.VMEM((rows, 2 * S5N), F32),
               pltpu.VMEM((W // LANE, rows, LANE), F32),
               pltpu.VMEM((nb, TB + 8, CONV_CH), F32),
               pltpu.VMEM((W // LANE, rows, SUB * LANE), BF16),
               pltpu.VMEM((W // LANE, rows, LANE), F32),
               pltpu.VMEM((nb, TB, W), F32),
               pltpu.VMEM((nb, TB, W), F32),
               pltpu.VMEM((nb, TB, W), F32)] + [pltpu.VMEM((nb, W // LANE, LANE, LANE), F32)] * 3
    return pl.pallas_call(
        functools.partial(_prompt_layer_body, last),
        grid=(nblk,),
        in_specs=in_specs,
        out_specs=out_specs,
        out_shape=out_shape,
        scratch_shapes=scratch,
        compiler_params=pltpu.CompilerParams(dimension_semantics=("arbitrary",),
                                             vmem_limit_bytes=VMEM_LIMIT),
        name="prompt_layer",
    )(*args)


def _rope_tables(pos):
    half = HD // 2
    inv = 1.0 / (ROPE_BASE ** (jnp.arange(half, dtype=F32) / half))
    ang = pos[:, None] * inv[None, :]
    cos = jnp.cos(ang)
    sin = jnp.sin(ang)
    cos_t = jnp.tile(cos, (1, LANE // half))
    sin_t = jnp.tile(jnp.concatenate([-sin, sin], axis=1), (1, LANE // HD))
    return cos_t, sin_t


def _retention_tables():
    log_gamma = jnp.log1p(-(2.0 ** (-5.0 - jnp.arange(NH, dtype=F32))))
    cum = jnp.cumsum(jnp.broadcast_to(log_gamma, (TB, NH)), axis=0)
    total = cum[-1]
    causal = jnp.tril(jnp.ones((TB, TB), dtype=bool))
    diff = cum[:, None, :] - cum[None, :, :]
    dec = jnp.where(causal[:, :, None], jnp.exp(jnp.where(causal[:, :, None], diff, 0.0)), 0.0)
    rdec = jnp.moveaxis(dec, 2, 0)
    recum = jnp.broadcast_to(jnp.exp(cum).T[:, :, None], (NH, TB, HD))
    rkdec = jnp.broadcast_to(jnp.exp(total[None, :] - cum).T[:, :, None], (NH, TB, HD))
    retot = jnp.broadcast_to(jnp.exp(total)[:, None, None], (NH, 1, HD))
    pair = lambda t: jnp.concatenate([t[0::2], t[1::2]], axis=-1)
    return pair(rdec), pair(recum), pair(rkdec), pair(retot), log_gamma


def _sel_matrix():
    sel = np.zeros((SUB, 2, HD, 2, TB), np.float32)
    for s_ in range(SUB):
        for h2 in range(2):
            sel[s_, h2, :, h2, s_::SUB] = 1.0
    return jnp.asarray(sel.reshape(SUB * LANE, LANE), dtype=BF16)


def _rows(v, width=None):
    v = v.astype(F32)
    if width is not None and v.shape[-1] < width:
        v = jnp.pad(v, ((0, 0), (0, width - v.shape[-1])))
    return v[:, None, :]


def _block_diag(blocks):
    g, r, c = blocks.shape
    eye = jnp.eye(g, dtype=blocks.dtype)
    return jnp.einsum('grc,gh->grhc', blocks, eye).reshape(g * r, g * c)


def _s5_discretise(A_re, A_im, log_dt, B_re, B_im):
    A_re, A_im = A_re.astype(F32), A_im.astype(F32)
    dt = jnp.exp(log_dt.astype(F32))[:, None]
    mag = jnp.exp(A_re * dt)
    ab_re, ab_im = mag * jnp.cos(A_im * dt), mag * jnp.sin(A_im * dt)
    nr, ni = ab_re - 1.0, ab_im
    den = A_re * A_re + A_im * A_im
    f_re = (nr * A_re + ni * A_im) / den
    f_im = (ni * A_re - nr * A_im) / den
    B_re, B_im = B_re.astype(F32), B_im.astype(F32)
    bb_re = f_re[..., None] * B_re - f_im[..., None] * B_im
    bb_im = f_re[..., None] * B_im + f_im[..., None] * B_re
    return ab_re, ab_im, bb_re, bb_im


def _s5_matrices(A_re, A_im, log_dt, B_re, B_im, C_re, C_im):
    ab_re, ab_im, bb_re, bb_im = _s5_discretise(A_re, A_im, log_dt, B_re, B_im)
    bblk = jnp.concatenate([_block_diag(jnp.swapaxes(bb_re, 1, 2)),
                            _block_diag(jnp.swapaxes(bb_im, 1, 2))], axis=1)
    cblk = jnp.concatenate([_block_diag(jnp.swapaxes(C_re.astype(F32), 1, 2)),
                            _block_diag(jnp.swapaxes(-C_im.astype(F32), 1, 2))], axis=0)
    return jnp.stack([ab_re.reshape(-1), ab_im.reshape(-1)], axis=0), bblk, cblk


R_COS, R_SIN, R_RETNW, R_LB, R_HNW, R_S5D, R_GLUB, R_MNW = [i * W for i in range(8)]
R_AR = 8 * W
R_AI = R_AR + S5N
R_DTB = R_AI + S5N
R_ALOG = R_DTB + 8
R_MD = R_ALOG + 8
N_COLP = R_MD + 8


def _prepare(p, lb_all, prompt_len):
    depth = p["norm_w"].shape[0]
    s5_A, bblk, cblk = jax.vmap(_s5_matrices)(p["s5_A_re"], p["s5_A_im"], p["s5_log_dt"], p["s5_B_re"],
                                              p["s5_B_im"], p["s5_C_re"], p["s5_C_im"])
    cos_p, sin_p = _rope_tables(jnp.arange(prompt_len, dtype=F32))
    rdec, recum, rkdec, retot, log_gamma = _retention_tables()
    cos_s, sin_s = _rope_tables(PAST_LEN + jnp.arange(1, dtype=F32))
    w = dict(
        norm_w=_rows(p["norm_w"]),
        w_in=jnp.pad(p["w_in"].astype(BF16), ((0, 0), (0, 0), (0, P_PAD - P_TOTAL))),
        w_out=p["w_out"].astype(BF16),
        ret_norm_w=_rows(p["ret_norm_w"]),
        lb=_rows(lb_all),
        hgrn_norm_w=_rows(p["hgrn_norm_w"]),
        s5_A=s5_A,
        s5_Bblk=bblk.astype(BF16),
        s5_Cblk=cblk.astype(BF16),
        s5_D=_rows(p["s5_D"]),
        s5_glu_w=p["s5_glu_w"].astype(BF16),
        s5_glu_b=_rows(p["s5_glu_b"]),
        m2_conv_w=p["m2_conv_w"].astype(F32),
        m2_conv_b=_rows(p["m2_conv_b"]),
        m2_dt_bias=_rows(jnp.repeat(p["m2_dt_bias"], HD, axis=1)),
        m2_A_log=_rows(jnp.repeat(p["m2_A_log"], HD, axis=1)),
        m2_D=_rows(jnp.repeat(p["m2_D"], HD, axis=1)),
        m2_norm_w=_rows(p["m2_norm_w"]),
        final_norm_w=p["final_norm_w"].astype(F32).reshape(1, D),
    )
    tile2 = lambda t: jnp.broadcast_to(jnp.tile(t[0], W // LANE), (depth, W))
    colp = jnp.concatenate(
        [tile2(cos_s), tile2(sin_s), w["ret_norm_w"][:, 0], w["lb"][:, 0], w["hgrn_norm_w"][:, 0],
         w["s5_D"][:, 0], w["s5_glu_b"][:, 0], w["m2_norm_w"][:, 0], s5_A[:, 0], s5_A[:, 1],
         _rows(p["m2_dt_bias"], 8)[:, 0], _rows(p["m2_A_log"], 8)[:, 0], _rows(p["m2_D"], 8)[:, 0]], axis=1)
    w.update(
        colp=jnp.broadcast_to(colp[:, :, None], (depth, N_COLP, LANE)),
        s5_BblkT=jnp.swapaxes(w["s5_Bblk"], 1, 2),
        s5_CblkT=jnp.swapaxes(w["s5_Cblk"], 1, 2),
        s5_glu_wT=jnp.swapaxes(w["s5_glu_w"], 1, 2),
        rgam=jnp.broadcast_to(jnp.exp(log_gamma)[:, None, None], (NH, 8, LANE)),
        cos_p=cos_p, sin_p=sin_p, rdec=rdec, recum=recum, rkdec=rkdec, retot=retot, sel=_sel_matrix(),
        bones=jnp.asarray(np.kron(np.eye(LANE // HD), np.full((HD, HD), 1.0 / HD)), dtype=BF16),
        hexp=jnp.asarray(np.kron(np.eye(LANE, NH), np.ones((1, HD))), dtype=F32),
    )
    return w


KC = 16
NKC = HD // KC
STEPS = NH * NKC


def _rotary_cols(x, cos, sin_signed):
    half = HD // 2
    parts = []
    for h in range(NH):
        parts += [x[h * HD + half:(h + 1) * HD], x[h * HD:h * HD + half]]
    return x * cos + jnp.concatenate(parts, axis=0) * sin_signed


def _expand_rows(dst, x):
    for c in range(x.shape[0]):
        dst[c] = jnp.broadcast_to(x[c:c + 1, :], (8, LANE))


def _sample_body(n_steps, x_ref, normw_ref, win_ref, wout_ref, fnw_ref, colp_ref, sbt_ref, sct_ref,
                 gluwt_ref, cw_ref, cb_ref, rgam_ref,
                 ret_in, hg_in, m2_in, s5re_in, s5im_in, buf_in,
                 y_ref, ret_out, hg_out, m2_out, s5re_out, s5im_out, buf_out,
                 xs_s, pt_s, vt_s, ot_s, mixt_s, o_s, hp_s,
                 kret_s, qret_s, khg_s, qhg_s, ahg_s, km2_s, qm2_s):
    i = pl.program_id(0)
    r = i % STEPS
    h = r // NKC
    kc = r % NKC

    def cp(r0, n=W):
        return colp_ref[r0:r0 + n, :]

    def pc(c0, w):
        return pt_s[c0:c0 + w, :]

    @pl.when(i == 0)
    def _load():
        xs_s[...] = x_ref[...]

    @pl.when(r == 0)
    def _prep():
        x = xs_s[...]
        hn = x * lax.rsqrt(jnp.mean(x * x, axis=-1, keepdims=True) + EPS) * normw_ref[...]
        proj = jnp.dot(hn.astype(BF16), win_ref[...], preferred_element_type=F32)

        xnew = proj[:, C_XBC:C_XBC + CONV_CH]
        acc = cb_ref[...] + xnew * cw_ref[CONV_K - 1:CONV_K, :]
        for t in range(CONV_K - 1):
            acc = acc + buf_in[t] * cw_ref[t:t + 1, :]
        for t in range(CONV_K - 2):
            buf_out[t] = buf_in[t + 1]
        buf_out[CONV_K - 2] = xnew
        xbc = _silu(acc)

        for t in range(P_PAD // LANE):
            c0 = t * LANE
            if C_XBC <= c0 < C_XBC + CONV_CH:
                tile = xbc[:, c0 - C_XBC:c0 - C_XBC + LANE]
            else:
                tile = proj[:, c0:c0 + LANE]
            pt_s[c0:c0 + LANE, :] = tile.T

        cos, sin = cp(R_COS), cp(R_SIN)
        _expand_rows(kret_s, _rotary_cols(pc(C_RK, W), cos, sin) * (HD ** -0.5))
        _expand_rows(qret_s, _rotary_cols(pc(C_RQ, W), cos, sin))
        vt_s[0] = pc(C_RV, W)

        fr = pc(C_GF, W)
        lb = cp(R_LB)
        logf = _log_sigmoid(fr) + jnp.log(1.0 + lb * jnp.exp(jnp.minimum(-fr, EXP_CLIP)))
        _expand_rows(ahg_s, jnp.exp(logf))
        _expand_rows(khg_s, (1.0 - lb) * jax.nn.sigmoid(-fr))
        _expand_rows(qhg_s, _silu(pc(C_GQ, W)))
        vt_s[1] = pc(C_GI, W)

        dt8 = _softplus(pc(C_DT, 8) + cp(R_DTB, 8))
        adec8 = jnp.exp(dt8 * (-jnp.exp(cp(R_ALOG, 8))))
        for hh in range(NH):
            hp_s[hh] = jnp.broadcast_to(adec8[hh:hh + 1, :], (8, LANE))
            vt_s[2, hh * HD:(hh + 1) * HD, :] = pc(C_XBC + hh * HD, HD) * dt8[hh:hh + 1, :]
        _expand_rows(km2_s, pc(C_XBC + W, 2 * HD))
        _expand_rows(qm2_s, pc(C_XBC + W + 2 * HD, 2 * HD))

        u = pc(C_SU, W)
        bu = jnp.dot(sbt_ref[...], u.astype(BF16), preferred_element_type=F32)
        hr, hi = s5re_in[...], s5im_in[...]
        ar, ai = cp(R_AR, S5N), cp(R_AI, S5N)
        nr = ar * hr - ai * hi + bu[0:S5N]
        ni = ar * hi + ai * hr + bu[S5N:2 * S5N]
        s5re_out[...] = nr
        s5im_out[...] = ni
        hcat = jnp.concatenate([nr, ni], axis=0).astype(BF16)
        sy = jnp.dot(sct_ref[...], hcat, preferred_element_type=F32) + cp(R_S5D) * u
        gy = _gelu_tanh(sy)
        glu = jnp.dot(gluwt_ref[...], gy.astype(BF16), preferred_element_type=F32) + cp(R_GLUB)
        mixt_s[2 * W:3 * W, :] = gy * jax.nn.sigmoid(glu) * _silu(pc(C_SG, W))

    @pl.when(kc == 0)
    def _zero():
        o_s[...] = jnp.zeros(o_s.shape, F32)

    hrow = pl.multiple_of(h * HD, HD)
    cbase = h * HD + kc * KC
    gbase = (h // 2) * HD + kc * KC

    def update(m, st_in, st_out, kx, qx, base, decay):
        v3 = vt_s[m, pl.ds(hrow, HD), :].reshape(HD // 8, 8, LANE)

        def body(kk, o):
            s_new = decay(kk) * st_in[kk].reshape(HD // 8, 8, LANE) + kx[base + kk] * v3
            st_out[kk] = s_new.reshape(HD, LANE)
            return o + qx[base + kk] * s_new
        o_s[m] = lax.fori_loop(0, KC, body, o_s[m], unroll=2)

    gam = rgam_ref[h]
    update(0, ret_in, ret_out, kret_s, qret_s, cbase, lambda kk: gam)
    update(1, hg_in, hg_out, khg_s, qhg_s, cbase, lambda kk: ahg_s[cbase + kk])
    adec = hp_s[h]
    update(2, m2_in, m2_out, km2_s, qm2_s, gbase, lambda kk: adec)

    @pl.when(kc == NKC - 1)
    def _head_done():
        for m in range(3):
            ot_s[m, pl.ds(hrow, HD), :] = o_s[m].reshape(HD, LANE)

    @pl.when(r == STEPS - 1)
    def _finish():
        def head_rms_cols(o):
            parts = []
            for hh in range(NH):
                seg = o[hh * HD:(hh + 1) * HD]
                parts.append(seg * lax.rsqrt(jnp.mean(seg * seg, axis=0, keepdims=True) + EPS))
            return jnp.concatenate(parts, axis=0)

        mixt_s[0:W, :] = head_rms_cols(ot_s[0]) * cp(R_RETNW) * _silu(pc(C_RG, W))
        mixt_s[W:2 * W, :] = head_rms_cols(ot_s[1]) * cp(R_HNW) * _silu(pc(C_GG, W))
        md8 = cp(R_MD, 8)
        ym = jnp.concatenate([ot_s[2, hh * HD:(hh + 1) * HD, :] + md8[hh:hh + 1, :] * pc(C_XBC + hh * HD, HD)
                              for hh in range(NH)], axis=0)
        my = ym * _silu(pc(C_MZ, W))
        mixt_s[3 * W:4 * W, :] = my * lax.rsqrt(jnp.mean(my * my, axis=0, keepdims=True) + EPS) * cp(R_MNW)
        mixed = jnp.concatenate([mixt_s[t * LANE:(t + 1) * LANE, :].T for t in range(D // LANE)], axis=1)
        xo = xs_s[...] + jnp.dot(mixed.astype(BF16), wout_ref[...], preferred_element_type=F32)
        xs_s[...] = xo

        @pl.when(i == n_steps - 1)
        def _final_norm():
            y_ref[...] = xo * lax.rsqrt(jnp.mean(xo * xo, axis=-1, keepdims=True) + EPS) * fnw_ref[...]


def _sample_step(x, w, ret, hg, m2, s5re, s5im, buf):
    depth = ret.shape[0]
    n = x.shape[0]
    n_steps = depth * STEPS
    lay = lambda i: i // STEPS

    def per_layer(a):
        nd = a.ndim
        return pl.BlockSpec((None,) + a.shape[1:], lambda i: (lay(i),) + (0,) * (nd - 1))

    st_spec = pl.BlockSpec((None, None, KC, HD, LANE),
                           lambda i: (lay(i), (i % STEPS) // NKC, i % NKC, 0, 0))
    weights = [w["norm_w"], w["w_in"], w["w_out"]]
    tables = [w["colp"], w["s5_BblkT"], w["s5_CblkT"], w["s5_glu_wT"], w["m2_conv_w"], w["m2_conv_b"]]
    in_specs = ([_const_spec(x.shape)] + [per_layer(a) for a in weights] + [_const_spec(w["final_norm_w"].shape)]
                + [per_layer(a) for a in tables] + [_const_spec(w["rgam"].shape)]
                + [st_spec, st_spec, st_spec, per_layer(s5re), per_layer(s5im), per_layer(buf)])
    out_shape = [jax.ShapeDtypeStruct((n, D), F32)] + [jax.ShapeDtypeStruct(a.shape, F32)
                                                       for a in (ret, hg, m2, s5re, s5im, buf)]
    out_specs = [_const_spec((n, D)), st_spec, st_spec, st_spec, per_layer(s5re), per_layer(s5im),
                 per_layer(buf)]
    expand = lambda c: pltpu.VMEM((c, 8, LANE), F32)
    scratch = [pltpu.VMEM((n, D), F32),
               pltpu.VMEM((P_PAD, LANE), F32),
               pltpu.VMEM((3, W, LANE), F32),
               pltpu.VMEM((3, W, LANE), F32),
               pltpu.VMEM((D, LANE), F32),
               pltpu.VMEM((3, HD // 8, 8, LANE), F32),
               pltpu.VMEM((NH, 8, LANE), F32),
               expand(W), expand(W), expand(W), expand(W), expand(W), expand(2 * HD), expand(2 * HD)]
    return pl.pallas_call(
        functools.partial(_sample_body, n_steps),
        grid=(n_steps,),
        in_specs=in_specs,
        out_specs=out_specs,
        out_shape=out_shape,
        scratch_shapes=scratch,
        compiler_params=pltpu.CompilerParams(dimension_semantics=("arbitrary",),
                                             vmem_limit_bytes=VMEM_LIMIT),
        name="sample_step",
    )(x, *weights, w["final_norm_w"], *tables, w["rgam"], ret, hg, m2, s5re, s5im, buf)


def kernel(x_prompt, x_sample, state_ret, state_hgrn, state_s5_re, state_s5_im, state_m2_ssm,
           state_m2_conv, norm_w, w_in, ret_norm_w, hgrn_lb_logits, hgrn_norm_w, s5_A_re, s5_A_im,
           s5_log_dt, s5_B_re, s5_B_im, s5_C_re, s5_C_im, s5_D, s5_glu_w, s5_glu_b, m2_conv_w,
           m2_conv_b, m2_dt_bias, m2_A_log, m2_D, m2_norm_w, w_out, final_norm_w):
    p = dict(norm_w=norm_w, w_in=w_in, ret_norm_w=ret_norm_w, hgrn_norm_w=hgrn_norm_w,
             s5_A_re=s5_A_re, s5_A_im=s5_A_im, s5_log_dt=s5_log_dt, s5_B_re=s5_B_re, s5_B_im=s5_B_im,
             s5_C_re=s5_C_re, s5_C_im=s5_C_im, s5_D=s5_D, s5_glu_w=s5_glu_w, s5_glu_b=s5_glu_b,
             m2_conv_w=m2_conv_w, m2_conv_b=m2_conv_b, m2_dt_bias=m2_dt_bias, m2_A_log=m2_A_log,
             m2_D=m2_D, m2_norm_w=m2_norm_w, w_out=w_out, final_norm_w=final_norm_w)
    depth = norm_w.shape[0]
    nbp, lp, _ = x_prompt.shape
    nbs = x_sample.shape[0]

    lb_sm = jax.nn.softmax(hgrn_lb_logits.astype(F32), axis=0)
    lb_all = jnp.clip(jnp.cumsum(lb_sm, axis=0) - lb_sm[0], 0.0, 1.0)

    w = _prepare(p, lb_all, lp)

    xp = x_prompt
    pst = []
    for l in range(depth):
        outs = _prompt_layer(xp, l, l == depth - 1, w)
        xp = outs[0]
        ret, hg, s5, m2, buf = outs[-5:]
        pst.append((ret, hg, s5[:, :S5N].reshape(nbp, S5G, S5P), s5[:, S5N:].reshape(nbp, S5G, S5P),
                    m2, buf))
    yp = xp

    seq_last = lambda a: jnp.moveaxis(a.astype(F32), 1, -1)
    ys, ret, hg, m2, s5re, s5im, buf = _sample_step(
        x_sample.reshape(nbs, D), w,
        seq_last(state_ret), seq_last(state_hgrn), seq_last(state_m2_ssm),
        seq_last(state_s5_re).reshape(depth, S5N, nbs), seq_last(state_s5_im).reshape(depth, S5N, nbs),
        jnp.swapaxes(state_m2_conv.astype(F32), 1, 2))
    seq_second = lambda a: jnp.moveaxis(a, -1, 1)

    stk = lambda i: jnp.stack([s[i] for s in pst], axis=0)
    return (yp, ys.reshape(nbs, 1, D),
            stk(0), stk(1), stk(2), stk(3), stk(4), stk(5),
            seq_second(ret), seq_second(hg), seq_second(s5re.reshape(depth, S5G, S5P, nbs)),
            seq_second(s5im.reshape(depth, S5G, S5P, nbs)), seq_second(m2), jnp.swapaxes(buf, 1, 2))
```

```python
import functools
import math

import numpy as np
import jax
import jax.numpy as jnp
from jax import lax
from jax.experimental import pallas as pl
from jax.experimental.pallas import tpu as pltpu

F32 = jnp.float32
BF16 = jnp.bfloat16
HI = lax.Precision.HIGHEST

D = 1024
W = 256
NH = 4
HD = 64
S5G = 16
S5C = 16
S5P = 64
S5N = S5G * S5P
CONV_CH = 512
CONV_K = 4
TB = 64
SUB = 16
NSUB = TB // SUB
EPS = 1e-6
EXP_CLIP = 60.0
ROPE_BASE = 10000.0
PAST_LEN = 16384

C_RQ, C_RK, C_RV, C_RG = 0, 256, 512, 768
C_GQ, C_GF, C_GI, C_GG = 1024, 1280, 1536, 1792
C_SU, C_SG = 2048, 2304
C_MZ, C_XBC, C_DT = 2560, 2816, 3328
P_TOTAL = 3332
PCH = 1152
NPC = 3
P_PAD = NPC * PCH
LANE = 128
VMEM_LIMIT = 56 * 1024 * 1024


def _silu(x):
    return x * jax.nn.sigmoid(x)


def _softplus(x):
    return jnp.maximum(x, 0.0) + jnp.log(1.0 + jnp.exp(-jnp.abs(x)))


def _log_sigmoid(x):
    return jnp.minimum(x, 0.0) - jnp.log(1.0 + jnp.exp(-jnp.abs(x)))


def _round_robin(gens):
    gens = list(gens)
    while gens:
        alive = []
        for g in gens:
            try:
                next(g)
                alive.append(g)
            except StopIteration:
                pass
        gens = alive


def _for_sequences(nb, parts, group):
    def body(i, c):
        built = [parts(i * group + k) for k in range(group)]
        _round_robin([g for gens, _ in built for g in gens])
        for _, finish in built:
            if finish is not None:
                finish()
        return c
    lax.fori_loop(0, nb // group, body, 0)


def _gelu_tanh(x):
    c = math.sqrt(2.0 / math.pi)
    return 0.5 * x * (1.0 + jnp.tanh(c * (x + 0.044715 * (x * x * x))))


def _dot(a, b):
    return jnp.dot(a.astype(BF16), b.astype(BF16), preferred_element_type=F32)


def _dot_hi(a, b):
    return jnp.dot(a, b, precision=HI, preferred_element_type=F32)


def _dot_tn_hi(a, b):
    return lax.dot_general(a, b, (((0,), (0,)), ((), ())), precision=HI,
                           preferred_element_type=F32)


def _rot_half_partner(x):
    lane = lax.broadcasted_iota(jnp.int32, x.shape, 1)
    first = (lane % HD) < (HD // 2)
    return jnp.where(first, pltpu.roll(x, LANE - HD // 2, 1), pltpu.roll(x, HD // 2, 1))


def _rotary(x, cos, sin_signed):
    parts = []
    for i in range(W // LANE):
        xi = x[:, i * LANE:(i + 1) * LANE]
        parts.append(xi * cos + _rot_half_partner(xi) * sin_signed)
    return jnp.concatenate(parts, axis=1)


def _prompt_layer_body(last, *refs):
    (x_ref, cos_ref, sin_ref, normw_ref, win_ref, wout_ref,
     rdec_ref, recum_ref, rkdec_ref, retot_ref, retnw_ref,
     lb_ref, hnw_ref, sel_ref,
     sA_ref, sB_ref, sC_ref, sD_ref, gluw_ref, glub_ref,
     cw_ref, cb_ref, dtb_ref, alog_ref, md_ref, mnw_ref, bones_ref, hexp_ref) = refs[:28]
    refs = refs[28:]
    if last:
        fnw_ref = refs[0]
        refs = refs[1:]
    act_ref = refs[0]
    refs = refs[1:]
    (ret_ref, hg_ref, s5_ref, m2_ref, m2buf_ref,
     hn_s, proj_s, mixed_s, bu_s, u_s, cv_s, p_s, dg_s, hq_s, hk_s, cum_s, hret_s, hhg_s, hm2_s) = refs
    j = pl.program_id(0)
    nb = x_ref.shape[0]

    @pl.when(j == 0)
    def _init():
        hret_s[...] = jnp.zeros(hret_s.shape, F32)
        hhg_s[...] = jnp.zeros(hhg_s.shape, F32)
        s5_ref[...] = jnp.zeros(s5_ref.shape, F32)
        hm2_s[...] = jnp.zeros(hm2_s.shape, F32)
        cv_s[...] = jnp.zeros(cv_s.shape, F32)

    ti = lax.broadcasted_iota(jnp.int32, (TB, TB), 0)
    si = lax.broadcasted_iota(jnp.int32, (TB, TB), 1)
    causal = si <= ti
    tri_l = causal.astype(F32)
    ones_tt = jnp.ones((TB, TB), F32)
    pr = lax.broadcasted_iota(jnp.int32, (LANE, LANE), 0)
    pc_ = lax.broadcasted_iota(jnp.int32, (LANE, LANE), 1)
    pairmask = (pr // HD) == (pc_ // HD)
    t2 = lax.broadcasted_iota(jnp.int32, (TB, LANE), 0)
    l2 = lax.broadcasted_iota(jnp.int32, (TB, LANE), 1)
    causal2 = (l2 % HD) <= t2
    tri_u2 = (t2 <= (l2 % HD)).astype(F32)
    first_head = l2 < HD
    subdiag2 = (t2 // SUB) == ((l2 % HD) // SUB)
    srcblock = [pairmask & (((pc_ % HD) // SUB) == jb) for jb in range(NSUB - 1)]
    ones_tl = jnp.ones((TB, LANE), F32)

    def dup_t(x2):
        return jnp.concatenate([x2, x2], axis=0).T

    def blockdiag2(x2):
        return jnp.where(pairmask, jnp.concatenate([x2, x2], axis=0), 0.0)

    def norm_body(b, c):
        xb = x_ref[b]
        hn = xb * lax.rsqrt(jnp.mean(xb * xb, axis=-1, keepdims=True) + EPS) * normw_ref[...]
        hn_s[pl.ds(pl.multiple_of(b * TB, TB), TB), :] = hn.astype(BF16)
        return c
    lax.fori_loop(0, nb, norm_body, 0)
    for c in range(NPC):
        cs = slice(c * PCH, (c + 1) * PCH)
        proj_s[:, cs] = jnp.dot(hn_s[...], win_ref[:, cs], preferred_element_type=F32)

    cos = cos_ref[...]
    sin = sin_ref[...]

    def phase1_parts(b):
        r0 = pl.multiple_of(b * TB, TB)

        def pj(c0, w):
            return proj_s[pl.ds(r0, TB), c0:c0 + w]

        u = pj(C_SU, W)
        for c_ in range(W // LANE):
            u_s[c_, pl.ds(b, TB, stride=nb), :] = u[:, c_ * LANE:(c_ + 1) * LANE]

        rq = _rotary(pj(C_RQ, W), cos, sin)
        rk = _rotary(pj(C_RK, W), cos, sin) * (HD ** -0.5)
        rv = pj(C_RV, W)
        rg = pj(C_RG, W)

        def ret_pair(p):
            ls = slice(p * LANE, (p + 1) * LANE)
            q2, k2, v2 = rq[:, ls], rk[:, ls], rv[:, ls]
            hb = hret_s[b, p]
            kt = dup_t(k2)
            s_raw = _dot(q2, jnp.where(pairmask, kt, 0.0))
            oi = _dot(q2, hb)
            kv = _dot(kt[:, 0:HD], v2 * rkdec_ref[p])
            yield
            o = _dot(s_raw * rdec_ref[p], blockdiag2(v2))
            yield
            o = o + oi * recum_ref[p]
            hret_s[b, p] = jnp.where(pairmask, retot_ref[p] * hb + kv, 0.0)
            ms = _dot(o * o, bones_ref[...])
            yield
            o = o * lax.rsqrt(ms + EPS) * retnw_ref[:, ls] * _silu(rg[:, ls])
            mixed_s[pl.ds(r0, TB), 0 * W + p * LANE:0 * W + (p + 1) * LANE] = o.astype(BF16)

        cv_s[b, 8:8 + TB, :] = pj(C_XBC, CONV_CH)
        acc = cb_ref[...] + cv_s[b, 5:5 + TB, :] * cw_ref[0:1, :]
        for i in range(1, CONV_K):
            acc = acc + cv_s[b, 5 + i:5 + i + TB, :] * cw_ref[i:i + 1, :]
        tail = cv_s[b, TB + 5:TB + 8, :]
        cv_s[b, 5:8, :] = tail
        m2buf_ref[b] = tail
        xbc = _silu(acc)
        xm = xbc[:, 0:W]
        bm = xbc[:, W:W + 2 * HD]
        cm = xbc[:, W + 2 * HD:W + 4 * HD]
        bm_sw = pltpu.roll(bm, HD, 1)
        cm_sw = pltpu.roll(cm, HD, 1)
        dt_b = _softplus(_dot_hi(pj(C_DT, LANE), hexp_ref[...]) + dtb_ref[...])
        la_b = dt_b * (-jnp.exp(alog_ref[...]))
        cum_b = _dot_hi(tri_l, la_b)
        ys = [None] * (W // LANE)

        def m2_pair(p):
            ls = slice(p * LANE, (p + 1) * LANE)
            b2 = jnp.where(first_head, bm, bm_sw) if p == 0 else jnp.where(first_head, bm_sw, bm)
            c2 = jnp.where(first_head, cm, cm_sw) if p == 0 else jnp.where(first_head, cm_sw, cm)
            x2, dt2, cum2 = xm[:, ls], dt_b[:, ls], cum_b[:, ls]
            xdt2 = x2 * dt2
            r2 = _dot_hi(ones_tt, la_b[:, ls] * tri_u2)
            tot2 = cum2[TB - 1:TB, :]
            hb = hm2_s[b, p]
            bt = dup_t(b2)
            s_raw = _dot(c2, jnp.where(pairmask, bt, 0.0))
            oi = _dot(c2, hb)
            kv = _dot(bt[:, 0:HD], xdt2 * jnp.exp(tot2 - cum2))
            decay = jnp.where(causal2, jnp.exp(jnp.minimum(cum2 - r2, 0.0)), 0.0)
            yield
            o = _dot(s_raw * decay, blockdiag2(xdt2))
            yield
            hm2_s[b, p] = jnp.where(pairmask, jnp.exp(tot2) * hb + kv, 0.0)
            ys[p] = o + oi * jnp.exp(cum2) + md_ref[:, ls] * x2

        fr = pj(C_GF, W)
        lb = lb_ref[...]
        logf = _log_sigmoid(fr) + jnp.log(1.0 + lb * jnp.exp(jnp.minimum(-fr, EXP_CLIP)))
        hq = _silu(pj(C_GQ, W))
        hk = (1.0 - lb) * jax.nn.sigmoid(-fr)
        cum = _dot_hi(tri_l, logf)
        hq_s[b] = hq
        hk_s[b] = hk
        cum_s[b] = cum
        t8 = lax.broadcasted_iota(jnp.int32, (8, LANE), 0)

        def diag_products(p):
            ls = slice(p * LANE, (p + 1) * LANE)
            for s_ in range(SUB):
                pieces = []
                for i in range(NSUB):
                    kb = jnp.broadcast_to(hk_s[b, i * SUB + s_:i * SUB + s_ + 1, ls], (8, LANE))
                    cb = jnp.broadcast_to(cum_s[b, i * SUB + s_:i * SUB + s_ + 1, ls], (8, LANE))
                    for half in range(SUB // 8):
                        rows = slice(i * SUB + half * 8, i * SUB + half * 8 + 8)
                        if half * 8 + 7 < s_:
                            pieces.append(jnp.zeros((8, LANE), F32))
                        elif half * 8 >= s_:
                            pieces.append(hq[rows, ls] * kb * jnp.exp(cum[rows, ls] - cb))
                        else:
                            e = jnp.exp(jnp.minimum(cum[rows, ls] - cb, 0.0))
                            pieces.append(jnp.where(t8 + half * 8 >= s_, hq[rows, ls] * kb * e, 0.0))
                pv = jnp.concatenate(pieces, axis=0)
                p_s[p, pl.ds(r0, TB), s_ * LANE:(s_ + 1) * LANE] = pv.astype(BF16)
                if s_ % 2 == 1:
                    yield

        def finish():
            my = jnp.concatenate(ys, axis=1) * _silu(pj(C_MZ, W))
            om = my * lax.rsqrt(jnp.mean(my * my, axis=-1, keepdims=True) + EPS) * mnw_ref[...]
            mixed_s[pl.ds(r0, TB), 3 * W:4 * W] = om.astype(BF16)

        gens = ([ret_pair(p) for p in range(W // LANE)] + [m2_pair(p) for p in range(W // LANE)]
                + [diag_products(p) for p in range(W // LANE)])
        return gens, finish

    _for_sequences(nb, phase1_parts, 2)

    for p in range(W // LANE):
        dg_s[p] = jnp.dot(p_s[p], sel_ref[...], preferred_element_type=F32)

    u_tb = jnp.concatenate([u_s[c_] for c_ in range(W // LANE)], axis=1)
    bu_s[...] = _dot(u_tb, sB_ref[...])
    ar = jnp.broadcast_to(sA_ref[0:1, :], (nb, S5N))
    ai = jnp.broadcast_to(sA_ref[1:2, :], (nb, S5N))

    def scan_body(t, carry):
        hr, hi = carry
        row = pl.multiple_of(t * nb, nb)
        nr = ar * hr - ai * hi + bu_s[pl.ds(row, nb), 0:S5N]
        ni = ar * hi + ai * hr + bu_s[pl.ds(row, nb), S5N:2 * S5N]
        bu_s[pl.ds(row, nb), 0:S5N] = nr
        bu_s[pl.ds(row, nb), S5N:2 * S5N] = ni
        return nr, ni
    hr, hi = lax.fori_loop(0, TB, scan_body, (s5_ref[:, 0:S5N], s5_ref[:, S5N:2 * S5N]))
    s5_ref[:, 0:S5N] = hr
    s5_ref[:, S5N:2 * S5N] = hi
    ch_tb = _dot(bu_s[...], sC_ref[...])
    for c_ in range(W // LANE):
        u_s[c_] = ch_tb[:, c_ * LANE:(c_ + 1) * LANE]

    def phase2_parts(b):
        r0 = pl.multiple_of(b * TB, TB)

        def pj(c0, w):
            return proj_s[pl.ds(r0, TB), c0:c0 + w]

        def s5_out():
            chs = jnp.concatenate([u_s[c_, pl.ds(b, TB, stride=nb), :] for c_ in range(W // LANE)],
                                  axis=1)
            gy = _gelu_tanh(chs + sD_ref[...] * pj(C_SU, W))
            glu = _dot(gy, gluw_ref[...])
            yield
            os5 = gy * jax.nn.sigmoid(glu + glub_ref[...]) * _silu(pj(C_SG, W))
            mixed_s[pl.ds(r0, TB), 2 * W:3 * W] = os5.astype(BF16)

        cum = cum_s[b]
        rr = jnp.concatenate(
            [jnp.zeros((SUB, W), F32)]
            + [jnp.broadcast_to(cum_s[b, i * SUB - 1:i * SUB, :], (SUB, W))
               for i in range(1, NSUB)], axis=0)
        ee = jnp.concatenate(
            [jnp.broadcast_to(cum_s[b, i * SUB + SUB - 1:i * SUB + SUB, :], (SUB, W))
             for i in range(NSUB)], axis=0)
        lastrow = lax.broadcasted_iota(jnp.int32, (TB, W), 0) == TB - 1
        totc = _dot_tn_hi(jnp.where(lastrow, cum, 0.0), ones_tl)
        hq = hq_s[b]
        hk = hk_s[b]
        hv = pj(C_GI, W)
        gg = pj(C_GG, W)
        qt = hq * jnp.exp(cum - rr)
        kh_ = hk * jnp.exp(ee - cum)
        qe = hq * jnp.exp(cum)
        tot = cum[TB - 1:TB, :]
        kend = kh_ * jnp.exp(tot - ee)
        trow = lax.broadcasted_iota(jnp.int32, (TB, W), 0) // SUB
        qx = []
        for jb in range(NSUB - 1):
            eb = jnp.broadcast_to(ee[jb * SUB:jb * SUB + 1, :], (TB, W))
            qx.append(jnp.where(trow > jb, qt * jnp.exp(jnp.minimum(rr - eb, 0.0)), 0.0))

        def hg_pair(p):
            ls = slice(p * LANE, (p + 1) * LANE)
            hb = hhg_s[b, p]
            hv2 = hv[:, ls]
            kt = dup_t(kh_[:, ls])
            off = _dot(qx[0][:, ls], jnp.where(srcblock[0], kt, 0.0))
            for jb in range(1, NSUB - 1):
                off = off + _dot(qx[jb][:, ls], jnp.where(srcblock[jb], kt, 0.0))
            oi = _dot(qe[:, ls], hb)
            kv = _dot(dup_t(kend[:, ls])[:, 0:HD], hv2)
            yield
            o = _dot(jnp.where(subdiag2, dg_s[p, pl.ds(r0, TB), :], 0.0) + off, blockdiag2(hv2))
            yield
            hhg_s[b, p] = jnp.where(pairmask, jnp.exp(totc[ls, :]) * hb + kv, 0.0)
            o = o + oi
            ms = _dot(o * o, bones_ref[...])
            yield
            o = o * lax.rsqrt(ms + EPS) * hnw_ref[:, ls] * _silu(gg[:, ls])
            mixed_s[pl.ds(r0, TB), 1 * W + p * LANE:1 * W + (p + 1) * LANE] = o.astype(BF16)

        return [s5_out()] + [hg_pair(p) for p in range(W // LANE)], None

    _for_sequences(nb, phase2_parts, 4)

    @pl.when(j == pl.num_programs(0) - 1)
    def _emit_states():
        for b in range(nb):
            for h in range(NH):
                p, h2 = divmod(h, 2)
                blk = (slice(h2 * HD, (h2 + 1) * HD),) * 2
                ret_ref[b, h] = hret_s[(b, p) + blk]
                hg_ref[b, h] = hhg_s[(b, p) + blk]
                m2_ref[b, h] = hm2_s[(b, p) + blk]

    res = jnp.dot(mixed_s[...], wout_ref[...], preferred_element_type=F32)
    for b in range(nb):
        xo = x_ref[b] + res[b * TB:(b + 1) * TB]
        if last:
            xo = xo * lax.rsqrt(jnp.mean(xo * xo, axis=-1, keepdims=True) + EPS) * fnw_ref[...]
        act_ref[b] = xo


def _const_spec(shape):
    nd = len(shape)
    return pl.BlockSpec(shape, lambda j: (0,) * nd)


def _prompt_layer(x, l, last, w):
    nb, L, _ = x.shape
    nblk = L // TB
    rows = nb * TB
    xspec = pl.BlockSpec((nb, TB, D), lambda j: (0, j, 0))
    tspec = pl.BlockSpec((TB, LANE), lambda j: (j, 0))

    def per_layer(a):
        nd = a.ndim
        return pl.BlockSpec((None,) + a.shape[1:], lambda j: (l,) + (0,) * (nd - 1))

    stacked = [w[k] for k in ("norm_w", "w_in", "w_out")]
    consts = [w[k] for k in ("rdec", "recum", "rkdec", "retot")]
    ret_nw = [w["ret_norm_w"], w["lb"], w["hgrn_norm_w"]]
    s5 = [w[k] for k in ("s5_A", "s5_Bblk", "s5_Cblk", "s5_D", "s5_glu_w", "s5_glu_b")]
    m2 = [w[k] for k in ("m2_conv_w", "m2_conv_b", "m2_dt_bias", "m2_A_log", "m2_D", "m2_norm_w")]
    pair_consts = [w["bones"], w["hexp"]]
    args = [x, w["cos_p"], w["sin_p"]] + stacked + consts + ret_nw + [w["sel"]] + s5 + m2 + pair_consts
    in_specs = ([xspec, tspec, tspec] + [per_layer(a) for a in stacked] + [_const_spec(a.shape) for a in consts]
                + [per_layer(a) for a in ret_nw] + [_const_spec(w["sel"].shape)]
                + [per_layer(a) for a in s5 + m2] + [_const_spec(a.shape) for a in pair_consts])
    if last:
        args.append(w["final_norm_w"])
        in_specs.append(_const_spec(w["final_norm_w"].shape))
    state_shapes = [jax.ShapeDtypeStruct((nb, NH, HD, HD), F32),
                    jax.ShapeDtypeStruct((nb, NH, HD, HD), F32),
                    jax.ShapeDtypeStruct((nb, 2 * S5N), F32),
                    jax.ShapeDtypeStruct((nb, NH, HD, HD), F32),
                    jax.ShapeDtypeStruct((nb, CONV_K - 1, CONV_CH), F32)]
    out_shape = [jax.ShapeDtypeStruct((nb, L, D), F32)] + state_shapes
    out_specs = [xspec] + [_const_spec(s.shape) for s in state_shapes]
    scratch = [pltpu.VMEM((rows, D), BF16),
               pltpu.VMEM((rows, P_PAD), F32),
               pltpu.VMEM((rows, D), BF16),
               pltpu.VMEM((rows, 2 * S5N), F32),
               pltpu.VMEM((W // LANE, rows, LANE), F32),
               pltpu.VMEM((nb, TB + 8, CONV_CH), F32),
               pltpu.VMEM((W // LANE, rows, SUB * LANE), BF16),
               pltpu.VMEM((W // LANE, rows, LANE), F32),
               pltpu.VMEM((nb, TB, W), F32),
               pltpu.VMEM((nb, TB, W), F32),
               pltpu.VMEM((nb, TB, W), F32)] + [pltpu.VMEM((nb, W // LANE, LANE, LANE), F32)] * 3
    return pl.pallas_call(
        functools.partial(_prompt_layer_body, last),
        grid=(nblk,),
        in_specs=in_specs,
        out_specs=out_specs,
        out_shape=out_shape,
        scratch_shapes=scratch,
        compiler_params=pltpu.CompilerParams(dimension_semantics=("arbitrary",),
                                             vmem_limit_bytes=VMEM_LIMIT),
        name="prompt_layer",
    )(*args)


def _rope_tables(pos):
    half = HD // 2
    inv = 1.0 / (ROPE_BASE ** (jnp.arange(half, dtype=F32) / half))
    ang = pos[:, None] * inv[None, :]
    cos = jnp.cos(ang)
    sin = jnp.sin(ang)
    cos_t = jnp.tile(cos, (1, LANE // half))
    sin_t = jnp.tile(jnp.concatenate([-sin, sin], axis=1), (1, LANE // HD))
    return cos_t, sin_t


def _retention_tables():
    log_gamma = jnp.log1p(-(2.0 ** (-5.0 - jnp.arange(NH, dtype=F32))))
    cum = jnp.cumsum(jnp.broadcast_to(log_gamma, (TB, NH)), axis=0)
    total = cum[-1]
    causal = jnp.tril(jnp.ones((TB, TB), dtype=bool))
    diff = cum[:, None, :] - cum[None, :, :]
    dec = jnp.where(causal[:, :, None], jnp.exp(jnp.where(causal[:, :, None], diff, 0.0)), 0.0)
    rdec = jnp.moveaxis(dec, 2, 0)
    recum = jnp.broadcast_to(jnp.exp(cum).T[:, :, None], (NH, TB, HD))
    rkdec = jnp.broadcast_to(jnp.exp(total[None, :] - cum).T[:, :, None], (NH, TB, HD))
    retot = jnp.broadcast_to(jnp.exp(total)[:, None, None], (NH, 1, HD))
    pair = lambda t: jnp.concatenate([t[0::2], t[1::2]], axis=-1)
    return pair(rdec), pair(recum), pair(rkdec), pair(retot), log_gamma


def _sel_matrix():
    sel = np.zeros((SUB, 2, HD, 2, TB), np.float32)
    for s_ in range(SUB):
        for h2 in range(2):
            sel[s_, h2, :, h2, s_::SUB] = 1.0
    return jnp.asarray(sel.reshape(SUB * LANE, LANE), dtype=BF16)


def _rows(v, width=None):
    v = v.astype(F32)
    if width is not None and v.shape[-1] < width:
        v = jnp.pad(v, ((0, 0), (0, width - v.shape[-1])))
    return v[:, None, :]


def _block_diag(blocks):
    g, r, c = blocks.shape
    eye = jnp.eye(g, dtype=blocks.dtype)
    return jnp.einsum('grc,gh->grhc', blocks, eye).reshape(g * r, g * c)


def _s5_discretise(A_re, A_im, log_dt, B_re, B_im):
    A_re, A_im = A_re.astype(F32), A_im.astype(F32)
    dt = jnp.exp(log_dt.astype(F32))[:, None]
    mag = jnp.exp(A_re * dt)
    ab_re, ab_im = mag * jnp.cos(A_im * dt), mag * jnp.sin(A_im * dt)
    nr, ni = ab_re - 1.0, ab_im
    den = A_re * A_re + A_im * A_im
    f_re = (nr * A_re + ni * A_im) / den
    f_im = (ni * A_re - nr * A_im) / den
    B_re, B_im = B_re.astype(F32), B_im.astype(F32)
    bb_re = f_re[..., None] * B_re - f_im[..., None] * B_im
    bb_im = f_re[..., None] * B_im + f_im[..., None] * B_re
    return ab_re, ab_im, bb_re, bb_im


def _s5_matrices(A_re, A_im, log_dt, B_re, B_im, C_re, C_im):
    ab_re, ab_im, bb_re, bb_im = _s5_discretise(A_re, A_im, log_dt, B_re, B_im)
    bblk = jnp.concatenate([_block_diag(jnp.swapaxes(bb_re, 1, 2)),
                            _block_diag(jnp.swapaxes(bb_im, 1, 2))], axis=1)
    cblk = jnp.concatenate([_block_diag(jnp.swapaxes(C_re.astype(F32), 1, 2)),
                            _block_diag(jnp.swapaxes(-C_im.astype(F32), 1, 2))], axis=0)
    return jnp.stack([ab_re.reshape(-1), ab_im.reshape(-1)], axis=0), bblk, cblk


R_COS, R_SIN, R_RETNW, R_LB, R_HNW, R_S5D, R_GLUB, R_MNW = [i * W for i in range(8)]
R_AR = 8 * W
R_AI = R_AR + S5N
R_DTB = R_AI + S5N
R_ALOG = R_DTB + 8
R_MD = R_ALOG + 8
N_COLP = R_MD + 8


def _prepare(p, lb_all, prompt_len):
    depth = p["norm_w"].shape[0]
    s5_A, bblk, cblk = jax.vmap(_s5_matrices)(p["s5_A_re"], p["s5_A_im"], p["s5_log_dt"], p["s5_B_re"],
                                              p["s5_B_im"], p["s5_C_re"], p["s5_C_im"])
    cos_p, sin_p = _rope_tables(jnp.arange(prompt_len, dtype=F32))
    rdec, recum, rkdec, retot, log_gamma = _retention_tables()
    cos_s, sin_s = _rope_tables(PAST_LEN + jnp.arange(1, dtype=F32))
    w = dict(
        norm_w=_rows(p["norm_w"]),
        w_in=jnp.pad(p["w_in"].astype(BF16), ((0, 0), (0, 0), (0, P_PAD - P_TOTAL))),
        w_out=p["w_out"].astype(BF16),
        ret_norm_w=_rows(p["ret_norm_w"]),
        lb=_rows(lb_all),
        hgrn_norm_w=_rows(p["hgrn_norm_w"]),
        s5_A=s5_A,
        s5_Bblk=bblk.astype(BF16),
        s5_Cblk=cblk.astype(BF16),
        s5_D=_rows(p["s5_D"]),
        s5_glu_w=p["s5_glu_w"].astype(BF16),
        s5_glu_b=_rows(p["s5_glu_b"]),
        m2_conv_w=p["m2_conv_w"].astype(F32),
        m2_conv_b=_rows(p["m2_conv_b"]),
        m2_dt_bias=_rows(jnp.repeat(p["m2_dt_bias"], HD, axis=1)),
        m2_A_log=_rows(jnp.repeat(p["m2_A_log"], HD, axis=1)),
        m2_D=_rows(jnp.repeat(p["m2_D"], HD, axis=1)),
        m2_norm_w=_rows(p["m2_norm_w"]),
        final_norm_w=p["final_norm_w"].astype(F32).reshape(1, D),
    )
    tile2 = lambda t: jnp.broadcast_to(jnp.tile(t[0], W // LANE), (depth, W))
    colp = jnp.concatenate(
        [tile2(cos_s), tile2(sin_s), w["ret_norm_w"][:, 0], w["lb"][:, 0], w["hgrn_norm_w"][:, 0],
         w["s5_D"][:, 0], w["s5_glu_b"][:, 0], w["m2_norm_w"][:, 0], s5_A[:, 0], s5_A[:, 1],
         _rows(p["m2_dt_bias"], 8)[:, 0], _rows(p["m2_A_log"], 8)[:, 0], _rows(p["m2_D"], 8)[:, 0]], axis=1)
    w.update(
        colp=jnp.broadcast_to(colp[:, :, None], (depth, N_COLP, LANE)),
        s5_BblkT=jnp.swapaxes(w["s5_Bblk"], 1, 2),
        s5_CblkT=jnp.swapaxes(w["s5_Cblk"], 1, 2),
        s5_glu_wT=jnp.swapaxes(w["s5_glu_w"], 1, 2),
        rgam=jnp.broadcast_to(jnp.exp(log_gamma)[:, None, None], (NH, 8, LANE)),
        cos_p=cos_p, sin_p=sin_p, rdec=rdec, recum=recum, rkdec=rkdec, retot=retot, sel=_sel_matrix(),
        bones=jnp.asarray(np.kron(np.eye(LANE // HD), np.full((HD, HD), 1.0 / HD)), dtype=BF16),
        hexp=jnp.asarray(np.kron(np.eye(LANE, NH), np.ones((1, HD))), dtype=F32),
    )
    return w


KC = 16
NKC = HD // KC
STEPS = NH * NKC


def _rotary_cols(x, cos, sin_signed):
    half = HD // 2
    parts = []
    for h in range(NH):
        parts += [x[h * HD + half:(h + 1) * HD], x[h * HD:h * HD + half]]
    return x * cos + jnp.concatenate(parts, axis=0) * sin_signed


def _expand_rows(dst, x):
    for c in range(x.shape[0]):
        dst[c] = jnp.broadcast_to(x[c:c + 1, :], (8, LANE))


def _sample_body(n_steps, x_ref, normw_ref, win_ref, wout_ref, fnw_ref, colp_ref, sbt_ref, sct_ref,
                 gluwt_ref, cw_ref, cb_ref, rgam_ref,
                 ret_in, hg_in, m2_in, s5re_in, s5im_in, buf_in,
                 y_ref, ret_out, hg_out, m2_out, s5re_out, s5im_out, buf_out,
                 xs_s, pt_s, vt_s, ot_s, mixt_s, o_s, hp_s,
                 kret_s, qret_s, khg_s, qhg_s, ahg_s, km2_s, qm2_s):
    i = pl.program_id(0)
    r = i % STEPS
    h = r // NKC
    kc = r % NKC

    def cp(r0, n=W):
        return colp_ref[r0:r0 + n, :]

    def pc(c0, w):
        return pt_s[c0:c0 + w, :]

    @pl.when(i == 0)
    def _load():
        xs_s[...] = x_ref[...]

    @pl.when(r == 0)
    def _prep():
        x = xs_s[...]
        hn = x * lax.rsqrt(jnp.mean(x * x, axis=-1, keepdims=True) + EPS) * normw_ref[...]
        proj = jnp.dot(hn.astype(BF16), win_ref[...], preferred_element_type=F32)

        xnew = proj[:, C_XBC:C_XBC + CONV_CH]
        acc = cb_ref[...] + xnew * cw_ref[CONV_K - 1:CONV_K, :]
        for t in range(CONV_K - 1):
            acc = acc + buf_in[t] * cw_ref[t:t + 1, :]
        for t in range(CONV_K - 2):
            buf_out[t] = buf_in[t + 1]
        buf_out[CONV_K - 2] = xnew
        xbc = _silu(acc)

        for t in range(P_PAD // LANE):
            c0 = t * LANE
            if C_XBC <= c0 < C_XBC + CONV_CH:
                tile = xbc[:, c0 - C_XBC:c0 - C_XBC + LANE]
            else:
                tile = proj[:, c0:c0 + LANE]
            pt_s[c0:c0 + LANE, :] = tile.T

        cos, sin = cp(R_COS), cp(R_SIN)
        _expand_rows(kret_s, _rotary_cols(pc(C_RK, W), cos, sin) * (HD ** -0.5))
        _expand_rows(qret_s, _rotary_cols(pc(C_RQ, W), cos, sin))
        vt_s[0] = pc(C_RV, W)

        fr = pc(C_GF, W)
        lb = cp(R_LB)
        logf = _log_sigmoid(fr) + jnp.log(1.0 + lb * jnp.exp(jnp.minimum(-fr, EXP_CLIP)))
        _expand_rows(ahg_s, jnp.exp(logf))
        _expand_rows(khg_s, (1.0 - lb) * jax.nn.sigmoid(-fr))
        _expand_rows(qhg_s, _silu(pc(C_GQ, W)))
        vt_s[1] = pc(C_GI, W)

        dt8 = _softplus(pc(C_DT, 8) + cp(R_DTB, 8))
        adec8 = jnp.exp(dt8 * (-jnp.exp(cp(R_ALOG, 8))))
        for hh in range(NH):
            hp_s[hh] = jnp.broadcast_to(adec8[hh:hh + 1, :], (8, LANE))
            vt_s[2, hh * HD:(hh + 1) * HD, :] = pc(C_XBC + hh * HD, HD) * dt8[hh:hh + 1, :]
        _expand_rows(km2_s, pc(C_XBC + W, 2 * HD))
        _expand_rows(qm2_s, pc(C_XBC + W + 2 * HD, 2 * HD))

        u = pc(C_SU, W)
        bu = jnp.dot(sbt_ref[...], u.astype(BF16), preferred_element_type=F32)
        hr, hi = s5re_in[...], s5im_in[...]
        ar, ai = cp(R_AR, S5N), cp(R_AI, S5N)
        nr = ar * hr - ai * hi + bu[0:S5N]
        ni = ar * hi + ai * hr + bu[S5N:2 * S5N]
        s5re_out[...] = nr
        s5im_out[...] = ni
        hcat = jnp.concatenate([nr, ni], axis=0).astype(BF16)
        sy = jnp.dot(sct_ref[...], hcat, preferred_element_type=F32) + cp(R_S5D) * u
        gy = _gelu_tanh(sy)
        glu = jnp.dot(gluwt_ref[...], gy.astype(BF16), preferred_element_type=F32) + cp(R_GLUB)
        mixt_s[2 * W:3 * W, :] = gy * jax.nn.sigmoid(glu) * _silu(pc(C_SG, W))

    @pl.when(kc == 0)
    def _zero():
        o_s[...] = jnp.zeros(o_s.shape, F32)

    hrow = pl.multiple_of(h * HD, HD)
    cbase = h * HD + kc * KC
    gbase = (h // 2) * HD + kc * KC

    def update(m, st_in, st_out, kx, qx, base, decay):
        v3 = vt_s[m, pl.ds(hrow, HD), :].reshape(HD // 8, 8, LANE)

        def body(kk, o):
            s_new = decay(kk) * st_in[kk].reshape(HD // 8, 8, LANE) + kx[base + kk] * v3
            st_out[kk] = s_new.reshape(HD, LANE)
            return o + qx[base + kk] * s_new
        o_s[m] = lax.fori_loop(0, KC, body, o_s[m], unroll=2)

    gam = rgam_ref[h]
    update(0, ret_in, ret_out, kret_s, qret_s, cbase, lambda kk: gam)
    update(1, hg_in, hg_out, khg_s, qhg_s, cbase, lambda kk: ahg_s[cbase + kk])
    adec = hp_s[h]
    update(2, m2_in, m2_out, km2_s, qm2_s, gbase, lambda kk: adec)

    @pl.when(kc == NKC - 1)
    def _head_done():
        for m in range(3):
            ot_s[m, pl.ds(hrow, HD), :] = o_s[m].reshape(HD, LANE)

    @pl.when(r == STEPS - 1)
    def _finish():
        def head_rms_cols(o):
            parts = []
            for hh in range(NH):
                seg = o[hh * HD:(hh + 1) * HD]
                parts.append(seg * lax.rsqrt(jnp.mean(seg * seg, axis=0, keepdims=True) + EPS))
            return jnp.concatenate(parts, axis=0)

        mixt_s[0:W, :] = head_rms_cols(ot_s[0]) * cp(R_RETNW) * _silu(pc(C_RG, W))
        mixt_s[W:2 * W, :] = head_rms_cols(ot_s[1]) * cp(R_HNW) * _silu(pc(C_GG, W))
        md8 = cp(R_MD, 8)
        ym = jnp.concatenate([ot_s[2, hh * HD:(hh + 1) * HD, :] + md8[hh:hh + 1, :] * pc(C_XBC + hh * HD, HD)
                              for hh in range(NH)], axis=0)
        my = ym * _silu(pc(C_MZ, W))
        mixt_s[3 * W:4 * W, :] = my * lax.rsqrt(jnp.mean(my * my, axis=0, keepdims=True) + EPS) * cp(R_MNW)
        mixed = jnp.concatenate([mixt_s[t * LANE:(t + 1) * LANE, :].T for t in range(D // LANE)], axis=1)
        xo = xs_s[...] + jnp.dot(mixed.astype(BF16), wout_ref[...], preferred_element_type=F32)
        xs_s[...] = xo

        @pl.when(i == n_steps - 1)
        def _final_norm():
            y_ref[...] = xo * lax.rsqrt(jnp.mean(xo * xo, axis=-1, keepdims=True) + EPS) * fnw_ref[...]


def _sample_step(x, w, ret, hg, m2, s5re, s5im, buf):
    depth = ret.shape[0]
    n = x.shape[0]
    n_steps = depth * STEPS
    lay = lambda i: i // STEPS

    def per_layer(a):
        nd = a.ndim
        return pl.BlockSpec((None,) + a.shape[1:], lambda i: (lay(i),) + (0,) * (nd - 1))

    st_spec = pl.BlockSpec((None, None, KC, HD, LANE),
                           lambda i: (lay(i), (i % STEPS) // NKC, i % NKC, 0, 0))
    weights = [w["norm_w"], w["w_in"], w["w_out"]]
    tables = [w["colp"], w["s5_BblkT"], w["s5_CblkT"], w["s5_glu_wT"], w["m2_conv_w"], w["m2_conv_b"]]
    in_specs = ([_const_spec(x.shape)] + [per_layer(a) for a in weights] + [_const_spec(w["final_norm_w"].shape)]
                + [per_layer(a) for a in tables] + [_const_spec(w["rgam"].shape)]
                + [st_spec, st_spec, st_spec, per_layer(s5re), per_layer(s5im), per_layer(buf)])
    out_shape = [jax.ShapeDtypeStruct((n, D), F32)] + [jax.ShapeDtypeStruct(a.shape, F32)
                                                       for a in (ret, hg, m2, s5re, s5im, buf)]
    out_specs = [_const_spec((n, D)), st_spec, st_spec, st_spec, per_layer(s5re), per_layer(s5im),
                 per_layer(buf)]
    expand = lambda c: pltpu.VMEM((c, 8, LANE), F32)
    scratch = [pltpu.VMEM((n, D), F32),
               pltpu.VMEM((P_PAD, LANE), F32),
               pltpu.VMEM((3, W, LANE), F32),
               pltpu.VMEM((3, W, LANE), F32),
               pltpu.VMEM((D, LANE), F32),
               pltpu.VMEM((3, HD // 8, 8, LANE), F32),
               pltpu.VMEM((NH, 8, LANE), F32),
               expand(W), expand(W), expand(W), expand(W), expand(W), expand(2 * HD), expand(2 * HD)]
    return pl.pallas_call(
        functools.partial(_sample_body, n_steps),
        grid=(n_steps,),
        in_specs=in_specs,
        out_specs=out_specs,
        out_shape=out_shape,
        scratch_shapes=scratch,
        compiler_params=pltpu.CompilerParams(dimension_semantics=("arbitrary",),
                                             vmem_limit_bytes=VMEM_LIMIT),
        name="sample_step",
    )(x, *weights, w["final_norm_w"], *tables, w["rgam"], ret, hg, m2, s5re, s5im, buf)


def kernel(x_prompt, x_sample, state_ret, state_hgrn, state_s5_re, state_s5_im, state_m2_ssm,
           state_m2_conv, norm_w, w_in, ret_norm_w, hgrn_lb_logits, hgrn_norm_w, s5_A_re, s5_A_im,
           s5_log_dt, s5_B_re, s5_B_im, s5_C_re, s5_C_im, s5_D, s5_glu_w, s5_glu_b, m2_conv_w,
           m2_conv_b, m2_dt_bias, m2_A_log, m2_D, m2_norm_w, w_out, final_norm_w):
    p = dict(norm_w=norm_w, w_in=w_in, ret_norm_w=ret_norm_w, hgrn_norm_w=hgrn_norm_w,
             s5_A_re=s5_A_re, s5_A_im=s5_A_im, s5_log_dt=s5_log_dt, s5_B_re=s5_B_re, s5_B_im=s5_B_im,
             s5_C_re=s5_C_re, s5_C_im=s5_C_im, s5_D=s5_D, s5_glu_w=s5_glu_w, s5_glu_b=s5_glu_b,
             m2_conv_w=m2_conv_w, m2_conv_b=m2_conv_b, m2_dt_bias=m2_dt_bias, m2_A_log=m2_A_log,
             m2_D=m2_D, m2_norm_w=m2_norm_w, w_out=w_out, final_norm_w=final_norm_w)
    depth = norm_w.shape[0]
    nbp, lp, _ = x_prompt.shape
    nbs = x_sample.shape[0]

    lb_sm = jax.nn.softmax(hgrn_lb_logits.astype(F32), axis=0)
    lb_all = jnp.clip(jnp.cumsum(lb_sm, axis=0) - lb_sm[0], 0.0, 1.0)

    w = _prepare(p, lb_all, lp)

    xp = x_prompt
    pst = []
    for l in range(depth):
        outs = _prompt_layer(xp, l, l == depth - 1, w)
        xp = outs[0]
        ret, hg, s5, m2, buf = outs[-5:]
        pst.append((ret, hg, s5[:, :S5N].reshape(nbp, S5G, S5P), s5[:, S5N:].reshape(nbp, S5G, S5P),
                    m2, buf))
    yp = xp

    seq_last = lambda a: jnp.moveaxis(a.astype(F32), 1, -1)
    ys, ret, hg, m2, s5re, s5im, buf = _sample_step(
        x_sample.reshape(nbs, D), w,
        seq_last(state_ret), seq_last(state_hgrn), seq_last(state_m2_ssm),
        seq_last(state_s5_re).reshape(depth, S5N, nbs), seq_last(state_s5_im).reshape(depth, S5N, nbs),
        jnp.swapaxes(state_m2_conv.astype(F32), 1, 2))
    seq_second = lambda a: jnp.moveaxis(a, -1, 1)

    stk = lambda i: jnp.stack([s[i] for s in pst], axis=0)
    return (yp, ys.reshape(nbs, 1, D),
            stk(0), stk(1), stk(2), stk(3), stk(4), stk(5),
            seq_second(ret), seq_second(hg), seq_second(s5re.reshape(depth, S5G, S5P, nbs)),
            seq_second(s5im.reshape(depth, S5G, S5P, nbs)), seq_second(m2), jnp.swapaxes(buf, 1, 2))
```

```python
import functools
import math

import numpy as np
import jax
import jax.numpy as jnp
from jax import lax
from jax.experimental import pallas as pl
from jax.experimental.pallas import tpu as pltpu

F32 = jnp.float32
BF16 = jnp.bfloat16
HI = lax.Precision.HIGHEST

D = 1024
W = 256
NH = 4
HD = 64
S5G = 16
S5C = 16
S5P = 64
S5N = S5G * S5P
NBC = 8
BCW = 2 * S5N // NBC
CONV_CH = 512
CONV_K = 4
TB = 64
SUB = 16
NSUB = TB // SUB
EPS = 1e-6
EXP_CLIP = 60.0
ROPE_BASE = 10000.0
PAST_LEN = 16384

C_RQ, C_RK, C_RV, C_RG = 0, 256, 512, 768
C_GQ, C_GF, C_GI, C_GG = 1024, 1280, 1536, 1792
C_SU, C_SG = 2048, 2304
C_MZ, C_XBC, C_DT = 2560, 2816, 3328
P_TOTAL = 3332
PCH = 1152
NPC = 3
P_PAD = NPC * PCH
LANE = 128
VMEM_LIMIT = 56 * 1024 * 1024


def _silu(x):
    return x * jax.nn.sigmoid(x)


def _softplus(x):
    return jnp.maximum(x, 0.0) + jnp.log(1.0 + jnp.exp(-jnp.abs(x)))


def _log_sigmoid(x):
    return jnp.minimum(x, 0.0) - jnp.log(1.0 + jnp.exp(-jnp.abs(x)))


def _round_robin(gens):
    gens = list(gens)
    while gens:
        alive = []
        for g in gens:
            try:
                next(g)
                alive.append(g)
            except StopIteration:
                pass
        gens = alive


def _for_sequences(nb, parts, group):
    def body(i, c):
        built = [parts(i * group + k) for k in range(group)]
        _round_robin([g for gens, _ in built for g in gens])
        for _, finish in built:
            if finish is not None:
                finish()
        return c
    lax.fori_loop(0, nb // group, body, 0)


def _gelu_tanh(x):
    c = math.sqrt(2.0 / math.pi)
    return 0.5 * x * (1.0 + jnp.tanh(c * (x + 0.044715 * (x * x * x))))


def _dot(a, b):
    return jnp.dot(a.astype(BF16), b.astype(BF16), preferred_element_type=F32)


def _dot_hi(a, b):
    return jnp.dot(a, b, precision=HI, preferred_element_type=F32)


def _dot_tn_hi(a, b):
    return lax.dot_general(a, b, (((0,), (0,)), ((), ())), precision=HI,
                           preferred_element_type=F32)


def _rot_half_partner(x):
    lane = lax.broadcasted_iota(jnp.int32, x.shape, 1)
    first = (lane % HD) < (HD // 2)
    return jnp.where(first, pltpu.roll(x, LANE - HD // 2, 1), pltpu.roll(x, HD // 2, 1))


def _rotary(x, cos, sin_signed):
    parts = []
    for i in range(W // LANE):
        xi = x[:, i * LANE:(i + 1) * LANE]
        parts.append(xi * cos + _rot_half_partner(xi) * sin_signed)
    return jnp.concatenate(parts, axis=1)


def _prompt_layer_body(last, *refs):
    (x_ref, cos_ref, sin_ref, normw_ref, win_ref, wout_ref,
     rdec_ref, recum_ref, rkdec_ref, retot_ref, retnw_ref,
     lb_ref, hnw_ref, sel_ref,
     sA_ref, sB_ref, sC_ref, sD_ref, gluw_ref, glub_ref,
     cw_ref, cb_ref, dtb_ref, alog_ref, md_ref, mnw_ref, bones_ref, hexp_ref) = refs[:28]
    refs = refs[28:]
    if last:
        fnw_ref = refs[0]
        refs = refs[1:]
    act_ref = refs[0]
    refs = refs[1:]
    (ret_ref, hg_ref, s5_ref, m2_ref, m2buf_ref,
     hn_s, proj_s, mixed_s, bu_s, u_s, ub_s, cv_s, p_s, dg_s, hq_s, hk_s, cum_s, hret_s, hhg_s, hm2_s) = refs
    j = pl.program_id(0)
    nb = x_ref.shape[0]

    @pl.when(j == 0)
    def _init():
        hret_s[...] = jnp.zeros(hret_s.shape, F32)
        hhg_s[...] = jnp.zeros(hhg_s.shape, F32)
        s5_ref[...] = jnp.zeros(s5_ref.shape, F32)
        hm2_s[...] = jnp.zeros(hm2_s.shape, F32)
        cv_s[...] = jnp.zeros(cv_s.shape, F32)

    ti = lax.broadcasted_iota(jnp.int32, (TB, TB), 0)
    si = lax.broadcasted_iota(jnp.int32, (TB, TB), 1)
    causal = si <= ti
    tri_l = causal.astype(F32)
    ones_tt = jnp.ones((TB, TB), F32)
    pr = lax.broadcasted_iota(jnp.int32, (LANE, LANE), 0)
    pc_ = lax.broadcasted_iota(jnp.int32, (LANE, LANE), 1)
    pairmask = (pr // HD) == (pc_ // HD)
    t2 = lax.broadcasted_iota(jnp.int32, (TB, LANE), 0)
    l2 = lax.broadcasted_iota(jnp.int32, (TB, LANE), 1)
    causal2 = (l2 % HD) <= t2
    tri_u2 = (t2 <= (l2 % HD)).astype(F32)
    first_head = l2 < HD
    subdiag2 = (t2 // SUB) == ((l2 % HD) // SUB)
    srcblock = [pairmask & (((pc_ % HD) // SUB) == jb) for jb in range(NSUB - 1)]
    ones_tl = jnp.ones((TB, LANE), F32)

    def dup_t(x2):
        return jnp.concatenate([x2, x2], axis=0).T

    def blockdiag2(x2):
        return jnp.where(pairmask, jnp.concatenate([x2, x2], axis=0), 0.0)

    def norm_body(b, c):
        xb = x_ref[b]
        hn = xb * lax.rsqrt(jnp.mean(xb * xb, axis=-1, keepdims=True) + EPS) * normw_ref[...]
        hn_s[pl.ds(pl.multiple_of(b * TB, TB), TB), :] = hn.astype(BF16)
        return c
    lax.fori_loop(0, nb, norm_body, 0)
    for c in range(NPC):
        cs = slice(c * PCH, (c + 1) * PCH)
        proj_s[:, cs] = jnp.dot(hn_s[...], win_ref[:, cs], preferred_element_type=F32)

    cos = cos_ref[...]
    sin = sin_ref[...]

    def reorder_u(b, c):
        u = proj_s[pl.ds(pl.multiple_of(b * TB, TB), TB), C_SU:C_SU + W]
        for c_ in range(W // LANE):
            u_s[c_, pl.ds(b, TB, stride=nb), :] = u[:, c_ * LANE:(c_ + 1) * LANE]
        return c
    lax.fori_loop(0, nb, reorder_u, 0)
    ub_s[...] = jnp.concatenate([u_s[c_] for c_ in range(W // LANE)], axis=1).astype(BF16)

    def s5_input_piece(cb):
        bu_s[cb] = jnp.dot(ub_s[...], sB_ref[cb], preferred_element_type=F32)
        yield

    def phase1_parts(b):
        r0 = pl.multiple_of(b * TB, TB)

        def pj(c0, w):
            return proj_s[pl.ds(r0, TB), c0:c0 + w]

        rq = _rotary(pj(C_RQ, W), cos, sin)
        rk = _rotary(pj(C_RK, W), cos, sin) * (HD ** -0.5)
        rv = pj(C_RV, W)
        rg = pj(C_RG, W)

        def ret_pair(p):
            ls = slice(p * LANE, (p + 1) * LANE)
            q2, k2, v2 = rq[:, ls], rk[:, ls], rv[:, ls]
            hb = hret_s[b, p]
            kt = dup_t(k2)
            s_raw = _dot(q2, jnp.where(pairmask, kt, 0.0))
            oi = _dot(q2, hb)
            kv = _dot(kt[:, 0:HD], v2 * rkdec_ref[p])
            yield
            o = _dot(s_raw * rdec_ref[p], blockdiag2(v2))
            yield
            o = o + oi * recum_ref[p]
            hret_s[b, p] = jnp.where(pairmask, retot_ref[p] * hb + kv, 0.0)
            ms = _dot(o * o, bones_ref[...])
            yield
            o = o * lax.rsqrt(ms + EPS) * retnw_ref[:, ls] * _silu(rg[:, ls])
            mixed_s[pl.ds(r0, TB), 0 * W + p * LANE:0 * W + (p + 1) * LANE] = o.astype(BF16)

        cv_s[b, 8:8 + TB, :] = pj(C_XBC, CONV_CH)
        acc = cb_ref[...] + cv_s[b, 5:5 + TB, :] * cw_ref[0:1, :]
        for i in range(1, CONV_K):
            acc = acc + cv_s[b, 5 + i:5 + i + TB, :] * cw_ref[i:i + 1, :]
        tail = cv_s[b, TB + 5:TB + 8, :]
        cv_s[b, 5:8, :] = tail
        m2buf_ref[b] = tail
        xbc = _silu(acc)
        xm = xbc[:, 0:W]
        bm = xbc[:, W:W + 2 * HD]
        cm = xbc[:, W + 2 * HD:W + 4 * HD]
        bm_sw = pltpu.roll(bm, HD, 1)
        cm_sw = pltpu.roll(cm, HD, 1)
        dt_b = _softplus(_dot_hi(pj(C_DT, LANE), hexp_ref[...]) + dtb_ref[...])
        la_b = dt_b * (-jnp.exp(alog_ref[...]))
        cum_b = _dot_hi(tri_l, la_b)
        ys = [None] * (W // LANE)

        def m2_pair(p):
            ls = slice(p * LANE, (p + 1) * LANE)
            b2 = jnp.where(first_head, bm, bm_sw) if p == 0 else jnp.where(first_head, bm_sw, bm)
            c2 = jnp.where(first_head, cm, cm_sw) if p == 0 else jnp.where(first_head, cm_sw, cm)
            x2, dt2, cum2 = xm[:, ls], dt_b[:, ls], cum_b[:, ls]
            xdt2 = x2 * dt2
            r2 = _dot_hi(ones_tt, la_b[:, ls] * tri_u2)
            tot2 = cum2[TB - 1:TB, :]
            hb = hm2_s[b, p]
            bt = dup_t(b2)
            s_raw = _dot(c2, jnp.where(pairmask, bt, 0.0))
            oi = _dot(c2, hb)
            kv = _dot(bt[:, 0:HD], xdt2 * jnp.exp(tot2 - cum2))
            decay = jnp.where(causal2, jnp.exp(jnp.minimum(cum2 - r2, 0.0)), 0.0)
            yield
            o = _dot(s_raw * decay, blockdiag2(xdt2))
            yield
            hm2_s[b, p] = jnp.where(pairmask, jnp.exp(tot2) * hb + kv, 0.0)
            ys[p] = o + oi * jnp.exp(cum2) + md_ref[:, ls] * x2

        fr = pj(C_GF, W)
        lb = lb_ref[...]
        logf = _log_sigmoid(fr) + jnp.log(1.0 + lb * jnp.exp(jnp.minimum(-fr, EXP_CLIP)))
        hq = _silu(pj(C_GQ, W))
        hk = (1.0 - lb) * jax.nn.sigmoid(-fr)
        cum = _dot_hi(tri_l, logf)
        hq_s[b] = hq
        hk_s[b] = hk
        cum_s[b] = cum
        t8 = lax.broadcasted_iota(jnp.int32, (8, LANE), 0)

        def diag_products(p):
            ls = slice(p * LANE, (p + 1) * LANE)
            for s_ in range(SUB):
                pieces = []
                for i in range(NSUB):
                    kb = jnp.broadcast_to(hk_s[b, i * SUB + s_:i * SUB + s_ + 1, ls], (8, LANE))
                    cb = jnp.broadcast_to(cum_s[b, i * SUB + s_:i * SUB + s_ + 1, ls], (8, LANE))
                    for half in range(SUB // 8):
                        rows = slice(i * SUB + half * 8, i * SUB + half * 8 + 8)
                        if half * 8 + 7 < s_:
                            pieces.append(jnp.zeros((8, LANE), F32))
                        elif half * 8 >= s_:
                            pieces.append(hq[rows, ls] * kb * jnp.exp(cum[rows, ls] - cb))
                        else:
                            e = jnp.exp(jnp.minimum(cum[rows, ls] - cb, 0.0))
                            pieces.append(jnp.where(t8 + half * 8 >= s_, hq[rows, ls] * kb * e, 0.0))
                pv = jnp.concatenate(pieces, axis=0)
                p_s[p, pl.ds(r0, TB), s_ * LANE:(s_ + 1) * LANE] = pv.astype(BF16)
                if s_ % 2 == 1:
                    yield

        def finish():
            my = jnp.concatenate(ys, axis=1) * _silu(pj(C_MZ, W))
            om = my * lax.rsqrt(jnp.mean(my * my, axis=-1, keepdims=True) + EPS) * mnw_ref[...]
            mixed_s[pl.ds(r0, TB), 3 * W:4 * W] = om.astype(BF16)

        gens = ([s5_input_piece(b)] + [ret_pair(p) for p in range(W // LANE)]
                + [m2_pair(p) for p in range(W // LANE)] + [diag_products(p) for p in range(W // LANE)])
        return gens, finish

    _for_sequences(nb, phase1_parts, 2)

    for p in range(W // LANE):
        dg_s[p] = jnp.dot(p_s[p], sel_ref[...], preferred_element_type=F32)

    ar = jnp.broadcast_to(sA_ref[0:1, :], (nb, S5N))
    ai = jnp.broadcast_to(sA_ref[1:2, :], (nb, S5N))

    def scan_body(t, carry):
        hr, hi = carry
        row = pl.multiple_of(t * nb, nb)
        half = NBC // 2
        nr = ar * hr - ai * hi + jnp.concatenate([bu_s[c_, pl.ds(row, nb), :] for c_ in range(half)], axis=1)
        ni = ar * hi + ai * hr + jnp.concatenate([bu_s[half + c_, pl.ds(row, nb), :] for c_ in range(half)],
                                                 axis=1)
        for c_ in range(half):
            bu_s[c_, pl.ds(row, nb), :] = nr[:, c_ * BCW:(c_ + 1) * BCW]
            bu_s[half + c_, pl.ds(row, nb), :] = ni[:, c_ * BCW:(c_ + 1) * BCW]
        return nr, ni
    hr, hi = lax.fori_loop(0, TB, scan_body, (s5_ref[:, 0:S5N], s5_ref[:, S5N:2 * S5N]))
    s5_ref[:, 0:S5N] = hr
    s5_ref[:, S5N:2 * S5N] = hi
    ch_tb = _dot(bu_s[0], sC_ref[0:BCW, :])
    for c_ in range(1, NBC):
        ch_tb = ch_tb + _dot(bu_s[c_], sC_ref[c_ * BCW:(c_ + 1) * BCW, :])
    for c_ in range(W // LANE):
        u_s[c_] = ch_tb[:, c_ * LANE:(c_ + 1) * LANE]

    def phase2_parts(b):
        r0 = pl.multiple_of(b * TB, TB)

        def pj(c0, w):
            return proj_s[pl.ds(r0, TB), c0:c0 + w]

        def s5_out():
            chs = jnp.concatenate([u_s[c_, pl.ds(b, TB, stride=nb), :] for c_ in range(W // LANE)],
                                  axis=1)
            gy = _gelu_tanh(chs + sD_ref[...] * pj(C_SU, W))
            glu = _dot(gy, gluw_ref[...])
            yield
            os5 = gy * jax.nn.sigmoid(glu + glub_ref[...]) * _silu(pj(C_SG, W))
            mixed_s[pl.ds(r0, TB), 2 * W:3 * W] = os5.astype(BF16)

        cum = cum_s[b]
        rr = jnp.concatenate(
            [jnp.zeros((SUB, W), F32)]
            + [jnp.broadcast_to(cum_s[b, i * SUB - 1:i * SUB, :], (SUB, W))
               for i in range(1, NSUB)], axis=0)
        ee = jnp.concatenate(
            [jnp.broadcast_to(cum_s[b, i * SUB + SUB - 1:i * SUB + SUB, :], (SUB, W))
             for i in range(NSUB)], axis=0)
        lastrow = lax.broadcasted_iota(jnp.int32, (TB, W), 0) == TB - 1
        totc = _dot_tn_hi(jnp.where(lastrow, cum, 0.0), ones_tl)
        hq = hq_s[b]
        hk = hk_s[b]
        hv = pj(C_GI, W)
        gg = pj(C_GG, W)
        qt = hq * jnp.exp(cum - rr)
        kh_ = hk * jnp.exp(ee - cum)
        qe = hq * jnp.exp(cum)
        tot = cum[TB - 1:TB, :]
        kend = kh_ * jnp.exp(tot - ee)
        trow = lax.broadcasted_iota(jnp.int32, (TB, W), 0) // SUB
        qx = []
        for jb in range(NSUB - 1):
            eb = jnp.broadcast_to(ee[jb * SUB:jb * SUB + 1, :], (TB, W))
            qx.append(jnp.where(trow > jb, qt * jnp.exp(jnp.minimum(rr - eb, 0.0)), 0.0))

        def hg_pair(p):
            ls = slice(p * LANE, (p + 1) * LANE)
            hb = hhg_s[b, p]
            hv2 = hv[:, ls]
            kt = dup_t(kh_[:, ls])
            off = _dot(qx[0][:, ls], jnp.where(srcblock[0], kt, 0.0))
            for jb in range(1, NSUB - 1):
                off = off + _dot(qx[jb][:, ls], jnp.where(srcblock[jb], kt, 0.0))
            oi = _dot(qe[:, ls], hb)
            kv = _dot(dup_t(kend[:, ls])[:, 0:HD], hv2)
            yield
            o = _dot(jnp.where(subdiag2, dg_s[p, pl.ds(r0, TB), :], 0.0) + off, blockdiag2(hv2))
            yield
            hhg_s[b, p] = jnp.where(pairmask, jnp.exp(totc[ls, :]) * hb + kv, 0.0)
            o = o + oi
            ms = _dot(o * o, bones_ref[...])
            yield
            o = o * lax.rsqrt(ms + EPS) * hnw_ref[:, ls] * _silu(gg[:, ls])
            mixed_s[pl.ds(r0, TB), 1 * W + p * LANE:1 * W + (p + 1) * LANE] = o.astype(BF16)

        return [s5_out()] + [hg_pair(p) for p in range(W // LANE)], None

    _for_sequences(nb, phase2_parts, 4)

    @pl.when(j == pl.num_programs(0) - 1)
    def _emit_states():
        for b in range(nb):
            for h in range(NH):
                p, h2 = divmod(h, 2)
                blk = (slice(h2 * HD, (h2 + 1) * HD),) * 2
                ret_ref[b, h] = hret_s[(b, p) + blk]
                hg_ref[b, h] = hhg_s[(b, p) + blk]
                m2_ref[b, h] = hm2_s[(b, p) + blk]

    res = jnp.dot(mixed_s[...], wout_ref[...], preferred_element_type=F32)
    for b in range(nb):
        xo = x_ref[b] + res[b * TB:(b + 1) * TB]
        if last:
            xo = xo * lax.rsqrt(jnp.mean(xo * xo, axis=-1, keepdims=True) + EPS) * fnw_ref[...]
        act_ref[b] = xo


def _const_spec(shape):
    nd = len(shape)
    return pl.BlockSpec(shape, lambda j: (0,) * nd)


def _prompt_layer(x, l, last, w):
    nb, L, _ = x.shape
    nblk = L // TB
    rows = nb * TB
    xspec = pl.BlockSpec((nb, TB, D), lambda j: (0, j, 0))
    tspec = pl.BlockSpec((TB, LANE), lambda j: (j, 0))

    def per_layer(a):
        nd = a.ndim
        return pl.BlockSpec((None,) + a.shape[1:], lambda j: (l,) + (0,) * (nd - 1))

    stacked = [w[k] for k in ("norm_w", "w_in", "w_out")]
    consts = [w[k] for k in ("rdec", "recum", "rkdec", "retot")]
    ret_nw = [w["ret_norm_w"], w["lb"], w["hgrn_norm_w"]]
    s5 = [w[k] for k in ("s5_A", "s5_Bblk_cols", "s5_Cblk", "s5_D", "s5_glu_w", "s5_glu_b")]
    m2 = [w[k] for k in ("m2_conv_w", "m2_conv_b", "m2_dt_bias", "m2_A_log", "m2_D", "m2_norm_w")]
    pair_consts = [w["bones"], w["hexp"]]
    args = [x, w["cos_p"], w["sin_p"]] + stacked + consts + ret_nw + [w["sel"]] + s5 + m2 + pair_consts
    in_specs = ([xspec, tspec, tspec] + [per_layer(a) for a in stacked] + [_const_spec(a.shape) for a in consts]
                + [per_layer(a) for a in ret_nw] + [_const_spec(w["sel"].shape)]
                + [per_layer(a) for a in s5 + m2] + [_const_spec(a.shape) for a in pair_consts])
    if last:
        args.append(w["final_norm_w"])
        in_specs.append(_const_spec(w["final_norm_w"].shape))
    state_shapes = [jax.ShapeDtypeStruct((nb, NH, HD, HD), F32),
                    jax.ShapeDtypeStruct((nb, NH, HD, HD), F32),
                    jax.ShapeDtypeStruct((nb, 2 * S5N), F32),
                    jax.ShapeDtypeStruct((nb, NH, HD, HD), F32),
                    jax.ShapeDtypeStruct((nb, CONV_K - 1, CONV_CH), F32)]
    out_shape = [jax.ShapeDtypeStruct((nb, L, D), F32)] + state_shapes
    out_specs = [xspec] + [_const_spec(s.shape) for s in state_shapes]
    scratch = [pltpu.VMEM((rows, D), BF16),
               pltpu.VMEM((rows, P_PAD), F32),
               pltpu.VMEM((rows, D), BF16),
               pltpu.VMEM((NBC, rows, BCW), F32),
               pltpu.VMEM((W // LANE, rows, LANE), F32),
               pltpu.VMEM((rows, W), BF16),
               pltpu.VMEM((nb, TB + 8, CONV_CH), F32),
               pltpu.VMEM((W // LANE, rows, SUB * LANE), BF16),
               pltpu.VMEM((W // LANE, rows, LANE), F32),
               pltpu.VMEM((nb, TB, W), F32),
               pltpu.VMEM((nb, TB, W), F32),
               pltpu.VMEM((nb, TB, W), F32)] + [pltpu.VMEM((nb, W // LANE, LANE, LANE), F32)] * 3
    return pl.pallas_call(
        functools.partial(_prompt_layer_body, last),
        grid=(nblk,),
        in_specs=in_specs,
        out_specs=out_specs,
        out_shape=out_shape,
        scratch_shapes=scratch,
        compiler_params=pltpu.CompilerParams(dimension_semantics=("arbitrary",),
                                             vmem_limit_bytes=VMEM_LIMIT),
        name="prompt_layer",
    )(*args)


def _rope_tables(pos):
    half = HD // 2
    inv = 1.0 / (ROPE_BASE ** (jnp.arange(half, dtype=F32) / half))
    ang = pos[:, None] * inv[None, :]
    cos = jnp.cos(ang)
    sin = jnp.sin(ang)
    cos_t = jnp.tile(cos, (1, LANE // half))
    sin_t = jnp.tile(jnp.concatenate([-sin, sin], axis=1), (1, LANE // HD))
    return cos_t, sin_t


def _retention_tables():
    log_gamma = jnp.log1p(-(2.0 ** (-5.0 - jnp.arange(NH, dtype=F32))))
    cum = jnp.cumsum(jnp.broadcast_to(log_gamma, (TB, NH)), axis=0)
    total = cum[-1]
    causal = jnp.tril(jnp.ones((TB, TB), dtype=bool))
    diff = cum[:, None, :] - cum[None, :, :]
    dec = jnp.where(causal[:, :, None], jnp.exp(jnp.where(causal[:, :, None], diff, 0.0)), 0.0)
    rdec = jnp.moveaxis(dec, 2, 0)
    recum = jnp.broadcast_to(jnp.exp(cum).T[:, :, None], (NH, TB, HD))
    rkdec = jnp.broadcast_to(jnp.exp(total[None, :] - cum).T[:, :, None], (NH, TB, HD))
    retot = jnp.broadcast_to(jnp.exp(total)[:, None, None], (NH, 1, HD))
    pair = lambda t: jnp.concatenate([t[0::2], t[1::2]], axis=-1)
    return pair(rdec), pair(recum), pair(rkdec), pair(retot), log_gamma


def _sel_matrix():
    sel = np.zeros((SUB, 2, HD, 2, TB), np.float32)
    for s_ in range(SUB):
        for h2 in range(2):
            sel[s_, h2, :, h2, s_::SUB] = 1.0
    return jnp.asarray(sel.reshape(SUB * LANE, LANE), dtype=BF16)


def _rows(v, width=None):
    v = v.astype(F32)
    if width is not None and v.shape[-1] < width:
        v = jnp.pad(v, ((0, 0), (0, width - v.shape[-1])))
    return v[:, None, :]


def _block_diag(blocks):
    g, r, c = blocks.shape
    eye = jnp.eye(g, dtype=blocks.dtype)
    return jnp.einsum('grc,gh->grhc', blocks, eye).reshape(g * r, g * c)


def _s5_discretise(A_re, A_im, log_dt, B_re, B_im):
    A_re, A_im = A_re.astype(F32), A_im.astype(F32)
    dt = jnp.exp(log_dt.astype(F32))[:, None]
    mag = jnp.exp(A_re * dt)
    ab_re, ab_im = mag * jnp.cos(A_im * dt), mag * jnp.sin(A_im * dt)
    nr, ni = ab_re - 1.0, ab_im
    den = A_re * A_re + A_im * A_im
    f_re = (nr * A_re + ni * A_im) / den
    f_im = (ni * A_re - nr * A_im) / den
    B_re, B_im = B_re.astype(F32), B_im.astype(F32)
    bb_re = f_re[..., None] * B_re - f_im[..., None] * B_im
    bb_im = f_re[..., None] * B_im + f_im[..., None] * B_re
    return ab_re, ab_im, bb_re, bb_im


def _s5_matrices(A_re, A_im, log_dt, B_re, B_im, C_re, C_im):
    ab_re, ab_im, bb_re, bb_im = _s5_discretise(A_re, A_im, log_dt, B_re, B_im)
    bblk = jnp.concatenate([_block_diag(jnp.swapaxes(bb_re, 1, 2)),
                            _block_diag(jnp.swapaxes(bb_im, 1, 2))], axis=1)
    cblk = jnp.concatenate([_block_diag(jnp.swapaxes(C_re.astype(F32), 1, 2)),
                            _block_diag(jnp.swapaxes(-C_im.astype(F32), 1, 2))], axis=0)
    return jnp.stack([ab_re.reshape(-1), ab_im.reshape(-1)], axis=0), bblk, cblk


R_COS, R_SIN, R_RETNW, R_LB, R_HNW, R_S5D, R_GLUB, R_MNW = [i * W for i in range(8)]
R_AR = 8 * W
R_AI = R_AR + S5N
R_DTB = R_AI + S5N
R_ALOG = R_DTB + 8
R_MD = R_ALOG + 8
N_COLP = R_MD + 8


def _prepare(p, lb_all, prompt_len):
    depth = p["norm_w"].shape[0]
    s5_A, bblk, cblk = jax.vmap(_s5_matrices)(p["s5_A_re"], p["s5_A_im"], p["s5_log_dt"], p["s5_B_re"],
                                              p["s5_B_im"], p["s5_C_re"], p["s5_C_im"])
    cos_p, sin_p = _rope_tables(jnp.arange(prompt_len, dtype=F32))
    rdec, recum, rkdec, retot, log_gamma = _retention_tables()
    cos_s, sin_s = _rope_tables(PAST_LEN + jnp.arange(1, dtype=F32))
    w = dict(
        norm_w=_rows(p["norm_w"]),
        w_in=jnp.pad(p["w_in"].astype(BF16), ((0, 0), (0, 0), (0, P_PAD - P_TOTAL))),
        w_out=p["w_out"].astype(BF16),
        ret_norm_w=_rows(p["ret_norm_w"]),
        lb=_rows(lb_all),
        hgrn_norm_w=_rows(p["hgrn_norm_w"]),
        s5_A=s5_A,
        s5_Bblk=bblk.astype(BF16),
        s5_Cblk=cblk.astype(BF16),
        s5_D=_rows(p["s5_D"]),
        s5_glu_w=p["s5_glu_w"].astype(BF16),
        s5_glu_b=_rows(p["s5_glu_b"]),
        m2_conv_w=p["m2_conv_w"].astype(F32),
        m2_conv_b=_rows(p["m2_conv_b"]),
        m2_dt_bias=_rows(jnp.repeat(p["m2_dt_bias"], HD, axis=1)),
        m2_A_log=_rows(jnp.repeat(p["m2_A_log"], HD, axis=1)),
        m2_D=_rows(jnp.repeat(p["m2_D"], HD, axis=1)),
        m2_norm_w=_rows(p["m2_norm_w"]),
        final_norm_w=p["final_norm_w"].astype(F32).reshape(1, D),
    )
    tile2 = lambda t: jnp.broadcast_to(jnp.tile(t[0], W // LANE), (depth, W))
    colp = jnp.concatenate(
        [tile2(cos_s), tile2(sin_s), w["ret_norm_w"][:, 0], w["lb"][:, 0], w["hgrn_norm_w"][:, 0],
         w["s5_D"][:, 0], w["s5_glu_b"][:, 0], w["m2_norm_w"][:, 0], s5_A[:, 0], s5_A[:, 1],
         _rows(p["m2_dt_bias"], 8)[:, 0], _rows(p["m2_A_log"], 8)[:, 0], _rows(p["m2_D"], 8)[:, 0]], axis=1)
    w.update(
        colp=jnp.broadcast_to(colp[:, :, None], (depth, N_COLP, LANE)),
        s5_Bblk_cols=jnp.moveaxis(w["s5_Bblk"].reshape(depth, W, NBC, BCW), 2, 1),
        s5_BblkT=jnp.swapaxes(w["s5_Bblk"], 1, 2),
        s5_CblkT=jnp.swapaxes(w["s5_Cblk"], 1, 2),
        s5_glu_wT=jnp.swapaxes(w["s5_glu_w"], 1, 2),
        rgam=jnp.broadcast_to(jnp.exp(log_gamma)[:, None, None], (NH, 8, LANE)),
        cos_p=cos_p, sin_p=sin_p, rdec=rdec, recum=recum, rkdec=rkdec, retot=retot, sel=_sel_matrix(),
        bones=jnp.asarray(np.kron(np.eye(LANE // HD), np.full((HD, HD), 1.0 / HD)), dtype=BF16),
        hexp=jnp.asarray(np.kron(np.eye(LANE, NH), np.ones((1, HD))), dtype=F32),
    )
    return w


KC = 16
NKC = HD // KC
STEPS = NH * NKC


def _rotary_cols(x, cos, sin_signed):
    half = HD // 2
    parts = []
    for h in range(NH):
        parts += [x[h * HD + half:(h + 1) * HD], x[h * HD:h * HD + half]]
    return x * cos + jnp.concatenate(parts, axis=0) * sin_signed


def _expand_rows(dst, x):
    for c in range(x.shape[0]):
        dst[c] = jnp.broadcast_to(x[c:c + 1, :], (8, LANE))


def _sample_body(n_steps, x_ref, normw_ref, win_ref, wout_ref, fnw_ref, colp_ref, sbt_ref, sct_ref,
                 gluwt_ref, cw_ref, cb_ref, rgam_ref,
                 ret_in, hg_in, m2_in, s5re_in, s5im_in, buf_in,
                 y_ref, ret_out, hg_out, m2_out, s5re_out, s5im_out, buf_out,
                 xs_s, pt_s, vt_s, ot_s, mixt_s, o_s, hp_s,
                 kret_s, qret_s, khg_s, qhg_s, ahg_s, km2_s, qm2_s):
    i = pl.program_id(0)
    r = i % STEPS
    h = r // NKC
    kc = r % NKC

    def cp(r0, n=W):
        return colp_ref[r0:r0 + n, :]

    def pc(c0, w):
        return pt_s[c0:c0 + w, :]

    @pl.when(i == 0)
    def _load():
        xs_s[...] = x_ref[...]

    @pl.when(r == 0)
    def _prep():
        x = xs_s[...]
        hn = x * lax.rsqrt(jnp.mean(x * x, axis=-1, keepdims=True) + EPS) * normw_ref[...]
        proj = jnp.dot(hn.astype(BF16), win_ref[...], preferred_element_type=F32)

        xnew = proj[:, C_XBC:C_XBC + CONV_CH]
        acc = cb_ref[...] + xnew * cw_ref[CONV_K - 1:CONV_K, :]
        for t in range(CONV_K - 1):
            acc = acc + buf_in[t] * cw_ref[t:t + 1, :]
        for t in range(CONV_K - 2):
            buf_out[t] = buf_in[t + 1]
        buf_out[CONV_K - 2] = xnew
        xbc = _silu(acc)

        for t in range(P_PAD // LANE):
            c0 = t * LANE
            if C_XBC <= c0 < C_XBC + CONV_CH:
                tile = xbc[:, c0 - C_XBC:c0 - C_XBC + LANE]
            else:
                tile = proj[:, c0:c0 + LANE]
            pt_s[c0:c0 + LANE, :] = tile.T

        cos, sin = cp(R_COS), cp(R_SIN)
        _expand_rows(kret_s, _rotary_cols(pc(C_RK, W), cos, sin) * (HD ** -0.5))
        _expand_rows(qret_s, _rotary_cols(pc(C_RQ, W), cos, sin))
        vt_s[0] = pc(C_RV, W)

        fr = pc(C_GF, W)
        lb = cp(R_LB)
        logf = _log_sigmoid(fr) + jnp.log(1.0 + lb * jnp.exp(jnp.minimum(-fr, EXP_CLIP)))
        _expand_rows(ahg_s, jnp.exp(logf))
        _expand_rows(khg_s, (1.0 - lb) * jax.nn.sigmoid(-fr))
        _expand_rows(qhg_s, _silu(pc(C_GQ, W)))
        vt_s[1] = pc(C_GI, W)

        dt8 = _softplus(pc(C_DT, 8) + cp(R_DTB, 8))
        adec8 = jnp.exp(dt8 * (-jnp.exp(cp(R_ALOG, 8))))
        for hh in range(NH):
            hp_s[hh] = jnp.broadcast_to(adec8[hh:hh + 1, :], (8, LANE))
            vt_s[2, hh * HD:(hh + 1) * HD, :] = pc(C_XBC + hh * HD, HD) * dt8[hh:hh + 1, :]
        _expand_rows(km2_s, pc(C_XBC + W, 2 * HD))
        _expand_rows(qm2_s, pc(C_XBC + W + 2 * HD, 2 * HD))

        u = pc(C_SU, W)
        bu = jnp.dot(sbt_ref[...], u.astype(BF16), preferred_element_type=F32)
        hr, hi = s5re_in[...], s5im_in[...]
        ar, ai = cp(R_AR, S5N), cp(R_AI, S5N)
        nr = ar * hr - ai * hi + bu[0:S5N]
        ni = ar * hi + ai * hr + bu[S5N:2 * S5N]
        s5re_out[...] = nr
        s5im_out[...] = ni
        hcat = jnp.concatenate([nr, ni], axis=0).astype(BF16)
        sy = jnp.dot(sct_ref[...], hcat, preferred_element_type=F32) + cp(R_S5D) * u
        gy = _gelu_tanh(sy)
        glu = jnp.dot(gluwt_ref[...], gy.astype(BF16), preferred_element_type=F32) + cp(R_GLUB)
        mixt_s[2 * W:3 * W, :] = gy * jax.nn.sigmoid(glu) * _silu(pc(C_SG, W))

    @pl.when(kc == 0)
    def _zero():
        o_s[...] = jnp.zeros(o_s.shape, F32)

    hrow = pl.multiple_of(h * HD, HD)
    cbase = h * HD + kc * KC
    gbase = (h // 2) * HD + kc * KC

    def update(m, st_in, st_out, kx, qx, base, decay):
        v3 = vt_s[m, pl.ds(hrow, HD), :].reshape(HD // 8, 8, LANE)

        def body(kk, o):
            s_new = decay(kk) * st_in[kk].reshape(HD // 8, 8, LANE) + kx[base + kk] * v3
            st_out[kk] = s_new.reshape(HD, LANE)
            return o + qx[base + kk] * s_new
        o_s[m] = lax.fori_loop(0, KC, body, o_s[m], unroll=2)

    gam = rgam_ref[h]
    update(0, ret_in, ret_out, kret_s, qret_s, cbase, lambda kk: gam)
    update(1, hg_in, hg_out, khg_s, qhg_s, cbase, lambda kk: ahg_s[cbase + kk])
    adec = hp_s[h]
    update(2, m2_in, m2_out, km2_s, qm2_s, gbase, lambda kk: adec)

    @pl.when(kc == NKC - 1)
    def _head_done():
        for m in range(3):
            ot_s[m, pl.ds(hrow, HD), :] = o_s[m].reshape(HD, LANE)

    @pl.when(r == STEPS - 1)
    def _finish():
        def head_rms_cols(o):
            parts = []
            for hh in range(NH):
                seg = o[hh * HD:(hh + 1) * HD]
                parts.append(seg * lax.rsqrt(jnp.mean(seg * seg, axis=0, keepdims=True) + EPS))
            return jnp.concatenate(parts, axis=0)

        mixt_s[0:W, :] = head_rms_cols(ot_s[0]) * cp(R_RETNW) * _silu(pc(C_RG, W))
        mixt_s[W:2 * W, :] = head_rms_cols(ot_s[1]) * cp(R_HNW) * _silu(pc(C_GG, W))
        md8 = cp(R_MD, 8)
        ym = jnp.concatenate([ot_s[2, hh * HD:(hh + 1) * HD, :] + md8[hh:hh + 1, :] * pc(C_XBC + hh * HD, HD)
                              for hh in range(NH)], axis=0)
        my = ym * _silu(pc(C_MZ, W))
        mixt_s[3 * W:4 * W, :] = my * lax.rsqrt(jnp.mean(my * my, axis=0, keepdims=True) + EPS) * cp(R_MNW)
        mixed = jnp.concatenate([mixt_s[t * LANE:(t + 1) * LANE, :].T for t in range(D // LANE)], axis=1)
        xo = xs_s[...] + jnp.dot(mixed.astype(BF16), wout_ref[...], preferred_element_type=F32)
        xs_s[...] = xo

        @pl.when(i == n_steps - 1)
        def _final_norm():
            y_ref[...] = xo * lax.rsqrt(jnp.mean(xo * xo, axis=-1, keepdims=True) + EPS) * fnw_ref[...]


def _sample_step(x, w, ret, hg, m2, s5re, s5im, buf):
    depth = ret.shape[0]
    n = x.shape[0]
    n_steps = depth * STEPS
    lay = lambda i: i // STEPS

    def per_layer(a):
        nd = a.ndim
        return pl.BlockSpec((None,) + a.shape[1:], lambda i: (lay(i),) + (0,) * (nd - 1))

    st_spec = pl.BlockSpec((None, None, KC, HD, LANE),
                           lambda i: (lay(i), (i % STEPS) // NKC, i % NKC, 0, 0))
    weights = [w["norm_w"], w["w_in"], w["w_out"]]
    tables = [w["colp"], w["s5_BblkT"], w["s5_CblkT"], w["s5_glu_wT"], w["m2_conv_w"], w["m2_conv_b"]]
    in_specs = ([_const_spec(x.shape)] + [per_layer(a) for a in weights] + [_const_spec(w["final_norm_w"].shape)]
                + [per_layer(a) for a in tables] + [_const_spec(w["rgam"].shape)]
                + [st_spec, st_spec, st_spec, per_layer(s5re), per_layer(s5im), per_layer(buf)])
    out_shape = [jax.ShapeDtypeStruct((n, D), F32)] + [jax.ShapeDtypeStruct(a.shape, F32)
                                                       for a in (ret, hg, m2, s5re, s5im, buf)]
    out_specs = [_const_spec((n, D)), st_spec, st_spec, st_spec, per_layer(s5re), per_layer(s5im),
                 per_layer(buf)]
    expand = lambda c: pltpu.VMEM((c, 8, LANE), F32)
    scratch = [pltpu.VMEM((n, D), F32),
               pltpu.VMEM((P_PAD, LANE), F32),
               pltpu.VMEM((3, W, LANE), F32),
               pltpu.VMEM((3, W, LANE), F32),
               pltpu.VMEM((D, LANE), F32),
               pltpu.VMEM((3, HD // 8, 8, LANE), F32),
               pltpu.VMEM((NH, 8, LANE), F32),
               expand(W), expand(W), expand(W), expand(W), expand(W), expand(2 * HD), expand(2 * HD)]
    return pl.pallas_call(
        functools.partial(_sample_body, n_steps),
        grid=(n_steps,),
        in_specs=in_specs,
        out_specs=out_specs,
        out_shape=out_shape,
        scratch_shapes=scratch,
        compiler_params=pltpu.CompilerParams(dimension_semantics=("arbitrary",),
                                             vmem_limit_bytes=VMEM_LIMIT),
        name="sample_step",
    )(x, *weights, w["final_norm_w"], *tables, w["rgam"], ret, hg, m2, s5re, s5im, buf)


def kernel(x_prompt, x_sample, state_ret, state_hgrn, state_s5_re, state_s5_im, state_m2_ssm,
           state_m2_conv, norm_w, w_in, ret_norm_w, hgrn_lb_logits, hgrn_norm_w, s5_A_re, s5_A_im,
           s5_log_dt, s5_B_re, s5_B_im, s5_C_re, s5_C_im, s5_D, s5_glu_w, s5_glu_b, m2_conv_w,
           m2_conv_b, m2_dt_bias, m2_A_log, m2_D, m2_norm_w, w_out, final_norm_w):
    p = dict(norm_w=norm_w, w_in=w_in, ret_norm_w=ret_norm_w, hgrn_norm_w=hgrn_norm_w,
             s5_A_re=s5_A_re, s5_A_im=s5_A_im, s5_log_dt=s5_log_dt, s5_B_re=s5_B_re, s5_B_im=s5_B_im,
             s5_C_re=s5_C_re, s5_C_im=s5_C_im, s5_D=s5_D, s5_glu_w=s5_glu_w, s5_glu_b=s5_glu_b,
             m2_conv_w=m2_conv_w, m2_conv_b=m2_conv_b, m2_dt_bias=m2_dt_bias, m2_A_log=m2_A_log,
             m2_D=m2_D, m2_norm_w=m2_norm_w, w_out=w_out, final_norm_w=final_norm_w)
    depth = norm_w.shape[0]
    nbp, lp, _ = x_prompt.shape
    nbs = x_sample.shape[0]

    lb_sm = jax.nn.softmax(hgrn_lb_logits.astype(F32), axis=0)
    lb_all = jnp.clip(jnp.cumsum(lb_sm, axis=0) - lb_sm[0], 0.0, 1.0)

    w = _prepare(p, lb_all, lp)

    xp = x_prompt
    pst = []
    for l in range(depth):
        outs = _prompt_layer(xp, l, l == depth - 1, w)
        xp = outs[0]
        ret, hg, s5, m2, buf = outs[-5:]
        pst.append((ret, hg, s5[:, :S5N].reshape(nbp, S5G, S5P), s5[:, S5N:].reshape(nbp, S5G, S5P),
                    m2, buf))
    yp = xp

    seq_last = lambda a: jnp.moveaxis(a.astype(F32), 1, -1)
    ys, ret, hg, m2, s5re, s5im, buf = _sample_step(
        x_sample.reshape(nbs, D), w,
        seq_last(state_ret), seq_last(state_hgrn), seq_last(state_m2_ssm),
        seq_last(state_s5_re).reshape(depth, S5N, nbs), seq_last(state_s5_im).reshape(depth, S5N, nbs),
        jnp.swapaxes(state_m2_conv.astype(F32), 1, 2))
    seq_second = lambda a: jnp.moveaxis(a, -1, 1)

    stk = lambda i: jnp.stack([s[i] for s in pst], axis=0)
    return (yp, ys.reshape(nbs, 1, D),
            stk(0), stk(1), stk(2), stk(3), stk(4), stk(5),
            seq_second(ret), seq_second(hg), seq_second(s5re.reshape(depth, S5G, S5P, nbs)),
            seq_second(s5im.reshape(depth, S5G, S5P, nbs)), seq_second(m2), jnp.swapaxes(buf, 1, 2))
```

```python
import functools
import math

import numpy as np
import jax
import jax.numpy as jnp
from jax import lax
from jax.experimental import pallas as pl
from jax.experimental.pallas import tpu as pltpu

F32 = jnp.float32
BF16 = jnp.bfloat16
HI = lax.Precision.HIGHEST

D = 1024
W = 256
NH = 4
HD = 64
S5G = 16
S5C = 16
S5P = 64
S5N = S5G * S5P
NBC = 8
BCW = 2 * S5N // NBC
CONV_CH = 512
CONV_K = 4
TB = 64
SUB = 16
NSUB = TB // SUB
EPS = 1e-6
EXP_CLIP = 60.0
ROPE_BASE = 10000.0
PAST_LEN = 16384

C_RQ, C_RK, C_RV, C_RG = 0, 256, 512, 768
C_GQ, C_GF, C_GI, C_GG = 1024, 1280, 1536, 1792
C_SU, C_SG = 2048, 2304
C_MZ, C_XBC, C_DT = 2560, 2816, 3328
P_TOTAL = 3332
PCH = 1152
NPC = 3
P_PAD = NPC * PCH
LANE = 128
VMEM_LIMIT = 56 * 1024 * 1024


def _silu(x):
    return x * jax.nn.sigmoid(x)


def _softplus(x):
    return jnp.maximum(x, 0.0) + jnp.log(1.0 + jnp.exp(-jnp.abs(x)))


def _log_sigmoid(x):
    return jnp.minimum(x, 0.0) - jnp.log(1.0 + jnp.exp(-jnp.abs(x)))


def _round_robin(gens):
    gens = list(gens)
    while gens:
        alive = []
        for g in gens:
            try:
                next(g)
                alive.append(g)
            except StopIteration:
                pass
        gens = alive


def _for_sequences(nb, parts, group):
    def body(i, c):
        built = [parts(i * group + k) for k in range(group)]
        _round_robin([g for gens, _ in built for g in gens])
        for _, finish in built:
            if finish is not None:
                finish()
        return c
    lax.fori_loop(0, nb // group, body, 0)


def _gelu_tanh(x):
    c = math.sqrt(2.0 / math.pi)
    return 0.5 * x * (1.0 + jnp.tanh(c * (x + 0.044715 * (x * x * x))))


def _dot(a, b):
    return jnp.dot(a.astype(BF16), b.astype(BF16), preferred_element_type=F32)


def _dot_hi(a, b):
    return jnp.dot(a, b, precision=HI, preferred_element_type=F32)


def _dot_tn_hi(a, b):
    return lax.dot_general(a, b, (((0,), (0,)), ((), ())), precision=HI,
                           preferred_element_type=F32)


def _rot_half_partner(x):
    lane = lax.broadcasted_iota(jnp.int32, x.shape, 1)
    first = (lane % HD) < (HD // 2)
    return jnp.where(first, pltpu.roll(x, LANE - HD // 2, 1), pltpu.roll(x, HD // 2, 1))


def _rotary(x, cos, sin_signed):
    parts = []
    for i in range(W // LANE):
        xi = x[:, i * LANE:(i + 1) * LANE]
        parts.append(xi * cos + _rot_half_partner(xi) * sin_signed)
    return jnp.concatenate(parts, axis=1)


def _prompt_layer_body(last, *refs):
    (x_ref, cos_ref, sin_ref, normw_ref, win_ref, wout_ref,
     rdec_ref, recum_ref, rkdec_ref, retot_ref, retnw_ref,
     lb_ref, hnw_ref, sel_ref,
     sA_ref, sB_ref, sC_ref, sD_ref, gluw_ref, glub_ref,
     cw_ref, cb_ref, dtb_ref, alog_ref, md_ref, mnw_ref, bones_ref, hexp_ref) = refs[:28]
    refs = refs[28:]
    if last:
        fnw_ref = refs[0]
        refs = refs[1:]
    act_ref = refs[0]
    refs = refs[1:]
    (ret_ref, hg_ref, s5_ref, m2_ref, m2buf_ref,
     hn_s, proj_s, mixed_s, bu_s, u_s, ub_s, cv_s, p_s, dg_s, hq_s, hk_s, cum_s, hret_s, hhg_s, hm2_s) = refs
    j = pl.program_id(0)
    nb = x_ref.shape[0]

    @pl.when(j == 0)
    def _init():
        hret_s[...] = jnp.zeros(hret_s.shape, F32)
        hhg_s[...] = jnp.zeros(hhg_s.shape, F32)
        s5_ref[...] = jnp.zeros(s5_ref.shape, F32)
        hm2_s[...] = jnp.zeros(hm2_s.shape, F32)
        cv_s[...] = jnp.zeros(cv_s.shape, F32)

    ti = lax.broadcasted_iota(jnp.int32, (TB, TB), 0)
    si = lax.broadcasted_iota(jnp.int32, (TB, TB), 1)
    causal = si <= ti
    tri_l = causal.astype(F32)
    ones_tt = jnp.ones((TB, TB), F32)
    pr = lax.broadcasted_iota(jnp.int32, (LANE, LANE), 0)
    pc_ = lax.broadcasted_iota(jnp.int32, (LANE, LANE), 1)
    pairmask = (pr // HD) == (pc_ // HD)
    t2 = lax.broadcasted_iota(jnp.int32, (TB, LANE), 0)
    l2 = lax.broadcasted_iota(jnp.int32, (TB, LANE), 1)
    causal2 = (l2 % HD) <= t2
    tri_u2 = (t2 <= (l2 % HD)).astype(F32)
    first_head = l2 < HD
    subdiag2 = (t2 // SUB) == ((l2 % HD) // SUB)
    srcblock = [pairmask & (((pc_ % HD) // SUB) == jb) for jb in range(NSUB - 1)]
    ones_tl = jnp.ones((TB, LANE), F32)

    def dup_t(x2):
        return jnp.concatenate([x2, x2], axis=0).T

    def blockdiag2(x2):
        return jnp.where(pairmask, jnp.concatenate([x2, x2], axis=0), 0.0)

    def norm_body(b, c):
        xb = x_ref[b]
        hn = xb * lax.rsqrt(jnp.mean(xb * xb, axis=-1, keepdims=True) + EPS) * normw_ref[...]
        hn_s[pl.ds(pl.multiple_of(b * TB, TB), TB), :] = hn.astype(BF16)
        return c
    lax.fori_loop(0, nb, norm_body, 0)
    for c in range(NPC):
        cs = slice(c * PCH, (c + 1) * PCH)
        proj_s[:, cs] = jnp.dot(hn_s[...], win_ref[:, cs], preferred_element_type=F32)

    cos = cos_ref[...]
    sin = sin_ref[...]

    def reorder_u(b, c):
        u = proj_s[pl.ds(pl.multiple_of(b * TB, TB), TB), C_SU:C_SU + W]
        for c_ in range(W // LANE):
            u_s[c_, pl.ds(b, TB, stride=nb), :] = u[:, c_ * LANE:(c_ + 1) * LANE]
        return c
    lax.fori_loop(0, nb, reorder_u, 0)
    ub_s[...] = jnp.concatenate([u_s[c_] for c_ in range(W // LANE)], axis=1).astype(BF16)

    def s5_input_piece(cb):
        bu_s[cb] = jnp.dot(ub_s[...], sB_ref[cb], preferred_element_type=F32)
        yield

    def phase1_parts(b):
        r0 = pl.multiple_of(b * TB, TB)

        def pj(c0, w):
            return proj_s[pl.ds(r0, TB), c0:c0 + w]

        rq = _rotary(pj(C_RQ, W), cos, sin)
        rk = _rotary(pj(C_RK, W), cos, sin) * (HD ** -0.5)
        rv = pj(C_RV, W)
        rg = pj(C_RG, W)

        def ret_pair(p):
            ls = slice(p * LANE, (p + 1) * LANE)
            q2, k2, v2 = rq[:, ls], rk[:, ls], rv[:, ls]
            hb = hret_s[b, p]
            kt = dup_t(k2)
            s_raw = _dot(q2, jnp.where(pairmask, kt, 0.0))
            oi = _dot(q2, hb)
            kv = _dot(kt[:, 0:HD], v2 * rkdec_ref[p])
            yield
            o = _dot(s_raw * rdec_ref[p], blockdiag2(v2))
            yield
            o = o + oi * recum_ref[p]
            hret_s[b, p] = jnp.where(pairmask, retot_ref[p] * hb + kv, 0.0)
            ms = _dot(o * o, bones_ref[...])
            yield
            o = o * lax.rsqrt(ms + EPS) * retnw_ref[:, ls] * _silu(rg[:, ls])
            mixed_s[pl.ds(r0, TB), 0 * W + p * LANE:0 * W + (p + 1) * LANE] = o.astype(BF16)

        cv_s[b, 8:8 + TB, :] = pj(C_XBC, CONV_CH)
        acc = cb_ref[...] + cv_s[b, 5:5 + TB, :] * cw_ref[0:1, :]
        for i in range(1, CONV_K):
            acc = acc + cv_s[b, 5 + i:5 + i + TB, :] * cw_ref[i:i + 1, :]
        tail = cv_s[b, TB + 5:TB + 8, :]
        cv_s[b, 5:8, :] = tail
        m2buf_ref[b] = tail
        xbc = _silu(acc)
        xm = xbc[:, 0:W]
        bm = xbc[:, W:W + 2 * HD]
        cm = xbc[:, W + 2 * HD:W + 4 * HD]
        bm_sw = pltpu.roll(bm, HD, 1)
        cm_sw = pltpu.roll(cm, HD, 1)
        dt_b = _softplus(_dot_hi(pj(C_DT, LANE), hexp_ref[...]) + dtb_ref[...])
        la_b = dt_b * (-jnp.exp(alog_ref[...]))
        cum_b = _dot_hi(tri_l, la_b)
        ys = [None] * (W // LANE)

        def m2_pair(p):
            ls = slice(p * LANE, (p + 1) * LANE)
            b2 = jnp.where(first_head, bm, bm_sw) if p == 0 else jnp.where(first_head, bm_sw, bm)
            c2 = jnp.where(first_head, cm, cm_sw) if p == 0 else jnp.where(first_head, cm_sw, cm)
            x2, dt2, cum2 = xm[:, ls], dt_b[:, ls], cum_b[:, ls]
            xdt2 = x2 * dt2
            r2 = _dot_hi(ones_tt, la_b[:, ls] * tri_u2)
            tot2 = cum2[TB - 1:TB, :]
            hb = hm2_s[b, p]
            bt = dup_t(b2)
            s_raw = _dot(c2, jnp.where(pairmask, bt, 0.0))
            oi = _dot(c2, hb)
            kv = _dot(bt[:, 0:HD], xdt2 * jnp.exp(tot2 - cum2))
            decay = jnp.where(causal2, jnp.exp(jnp.minimum(cum2 - r2, 0.0)), 0.0)
            yield
            o = _dot(s_raw * decay, blockdiag2(xdt2))
            yield
            hm2_s[b, p] = jnp.where(pairmask, jnp.exp(tot2) * hb + kv, 0.0)
            ys[p] = o + oi * jnp.exp(cum2) + md_ref[:, ls] * x2

        fr = pj(C_GF, W)
        lb = lb_ref[...]
        logf = _log_sigmoid(fr) + jnp.log(1.0 + lb * jnp.exp(jnp.minimum(-fr, EXP_CLIP)))
        hq = _silu(pj(C_GQ, W))
        hk = (1.0 - lb) * jax.nn.sigmoid(-fr)
        cum = _dot_hi(tri_l, logf)
        hq_s[b] = hq
        hk_s[b] = hk
        cum_s[b] = cum
        t8 = lax.broadcasted_iota(jnp.int32, (8, LANE), 0)

        def diag_products(p):
            ls = slice(p * LANE, (p + 1) * LANE)
            for s_ in range(SUB):
                pieces = []
                for i in range(NSUB):
                    kb = jnp.broadcast_to(hk_s[b, i * SUB + s_:i * SUB + s_ + 1, ls], (8, LANE))
                    cb = jnp.broadcast_to(cum_s[b, i * SUB + s_:i * SUB + s_ + 1, ls], (8, LANE))
                    for half in range(SUB // 8):
                        rows = slice(i * SUB + half * 8, i * SUB + half * 8 + 8)
                        if half * 8 + 7 < s_:
                            pieces.append(jnp.zeros((8, LANE), F32))
                        elif half * 8 >= s_:
                            pieces.append(hq[rows, ls] * kb * jnp.exp(cum[rows, ls] - cb))
                        else:
                            e = jnp.exp(jnp.minimum(cum[rows, ls] - cb, 0.0))
                            pieces.append(jnp.where(t8 + half * 8 >= s_, hq[rows, ls] * kb * e, 0.0))
                pv = jnp.concatenate(pieces, axis=0)
                p_s[p, pl.ds(r0, TB), s_ * LANE:(s_ + 1) * LANE] = pv.astype(BF16)
                if s_ % 2 == 1:
                    yield

        def finish():
            my = jnp.concatenate(ys, axis=1) * _silu(pj(C_MZ, W))
            om = my * lax.rsqrt(jnp.mean(my * my, axis=-1, keepdims=True) + EPS) * mnw_ref[...]
            mixed_s[pl.ds(r0, TB), 3 * W:4 * W] = om.astype(BF16)

        gens = ([s5_input_piece(b)] + [ret_pair(p) for p in range(W // LANE)]
                + [m2_pair(p) for p in range(W // LANE)] + [diag_products(p) for p in range(W // LANE)])
        return gens, finish

    _for_sequences(nb, phase1_parts, 2)

    for p in range(W // LANE):
        dg_s[p] = jnp.dot(p_s[p], sel_ref[...], preferred_element_type=F32)

    ar = jnp.broadcast_to(sA_ref[0:1, :], (nb, S5N))
    ai = jnp.broadcast_to(sA_ref[1:2, :], (nb, S5N))

    def scan_body(t, carry):
        hr, hi = carry
        row = pl.multiple_of(t * nb, nb)
        half = NBC // 2
        nr = ar * hr - ai * hi + jnp.concatenate([bu_s[c_, pl.ds(row, nb), :] for c_ in range(half)], axis=1)
        ni = ar * hi + ai * hr + jnp.concatenate([bu_s[half + c_, pl.ds(row, nb), :] for c_ in range(half)],
                                                 axis=1)
        for c_ in range(half):
            bu_s[c_, pl.ds(row, nb), :] = nr[:, c_ * BCW:(c_ + 1) * BCW]
            bu_s[half + c_, pl.ds(row, nb), :] = ni[:, c_ * BCW:(c_ + 1) * BCW]
        return nr, ni
    hr, hi = lax.fori_loop(0, TB, scan_body, (s5_ref[:, 0:S5N], s5_ref[:, S5N:2 * S5N]))
    s5_ref[:, 0:S5N] = hr
    s5_ref[:, S5N:2 * S5N] = hi
    ch_tb = _dot(bu_s[0], sC_ref[0:BCW, :])
    for c_ in range(1, NBC):
        ch_tb = ch_tb + _dot(bu_s[c_], sC_ref[c_ * BCW:(c_ + 1) * BCW, :])
    for c_ in range(W // LANE):
        u_s[c_] = ch_tb[:, c_ * LANE:(c_ + 1) * LANE]

    def phase2_parts(b):
        r0 = pl.multiple_of(b * TB, TB)

        def pj(c0, w):
            return proj_s[pl.ds(r0, TB), c0:c0 + w]

        def s5_out():
            chs = jnp.concatenate([u_s[c_, pl.ds(b, TB, stride=nb), :] for c_ in range(W // LANE)],
                                  axis=1)
            gy = _gelu_tanh(chs + sD_ref[...] * pj(C_SU, W))
            glu = _dot(gy, gluw_ref[...])
            yield
            os5 = gy * jax.nn.sigmoid(glu + glub_ref[...]) * _silu(pj(C_SG, W))
            mixed_s[pl.ds(r0, TB), 2 * W:3 * W] = os5.astype(BF16)

        cum = cum_s[b]
        rr = jnp.concatenate(
            [jnp.zeros((SUB, W), F32)]
            + [jnp.broadcast_to(cum_s[b, i * SUB - 1:i * SUB, :], (SUB, W))
               for i in range(1, NSUB)], axis=0)
        ee = jnp.concatenate(
            [jnp.broadcast_to(cum_s[b, i * SUB + SUB - 1:i * SUB + SUB, :], (SUB, W))
             for i in range(NSUB)], axis=0)
        lastrow = lax.broadcasted_iota(jnp.int32, (TB, W), 0) == TB - 1
        totc = _dot_tn_hi(jnp.where(lastrow, cum, 0.0), ones_tl)
        hq = hq_s[b]
        hk = hk_s[b]
        hv = pj(C_GI, W)
        gg = pj(C_GG, W)
        qt = hq * jnp.exp(cum - rr)
        kh_ = hk * jnp.exp(ee - cum)
        qe = hq * jnp.exp(cum)
        tot = cum[TB - 1:TB, :]
        kend = kh_ * jnp.exp(tot - ee)
        trow = lax.broadcasted_iota(jnp.int32, (TB, W), 0) // SUB
        qx = []
        for jb in range(NSUB - 1):
            eb = jnp.broadcast_to(ee[jb * SUB:jb * SUB + 1, :], (TB, W))
            qx.append(jnp.where(trow > jb, qt * jnp.exp(jnp.minimum(rr - eb, 0.0)), 0.0))

        def hg_pair(p):
            ls = slice(p * LANE, (p + 1) * LANE)
            hb = hhg_s[b, p]
            hv2 = hv[:, ls]
            kt = dup_t(kh_[:, ls])
            off = _dot(qx[0][:, ls], jnp.where(srcblock[0], kt, 0.0))
            for jb in range(1, NSUB - 1):
                off = off + _dot(qx[jb][:, ls], jnp.where(srcblock[jb], kt, 0.0))
            oi = _dot(qe[:, ls], hb)
            kv = _dot(dup_t(kend[:, ls])[:, 0:HD], hv2)
            yield
            o = _dot(jnp.where(subdiag2, dg_s[p, pl.ds(r0, TB), :], 0.0) + off, blockdiag2(hv2))
            yield
            hhg_s[b, p] = jnp.where(pairmask, jnp.exp(totc[ls, :]) * hb + kv, 0.0)
            o = o + oi
            ms = _dot(o * o, bones_ref[...])
            yield
            o = o * lax.rsqrt(ms + EPS) * hnw_ref[:, ls] * _silu(gg[:, ls])
            mixed_s[pl.ds(r0, TB), 1 * W + p * LANE:1 * W + (p + 1) * LANE] = o.astype(BF16)

        return [s5_out()] + [hg_pair(p) for p in range(W // LANE)], None

    _for_sequences(nb, phase2_parts, 4)

    @pl.when(j == pl.num_programs(0) - 1)
    def _emit_states():
        for b in range(nb):
            for h in range(NH):
                p, h2 = divmod(h, 2)
                blk = (slice(h2 * HD, (h2 + 1) * HD),) * 2
                ret_ref[b, h] = hret_s[(b, p) + blk]
                hg_ref[b, h] = hhg_s[(b, p) + blk]
                m2_ref[b, h] = hm2_s[(b, p) + blk]

    res = jnp.dot(mixed_s[...], wout_ref[...], preferred_element_type=F32)
    for b in range(nb):
        xo = x_ref[b] + res[b * TB:(b + 1) * TB]
        if last:
            xo = xo * lax.rsqrt(jnp.mean(xo * xo, axis=-1, keepdims=True) + EPS) * fnw_ref[...]
        act_ref[b] = xo


def _const_spec(shape):
    nd = len(shape)
    return pl.BlockSpec(shape, lambda j: (0,) * nd)


def _prompt_layer(x, l, last, w):
    nb, L, _ = x.shape
    nblk = L // TB
    rows = nb * TB
    xspec = pl.BlockSpec((nb, TB, D), lambda j: (0, j, 0))
    tspec = pl.BlockSpec((TB, LANE), lambda j: (j, 0))

    def per_layer(a):
        nd = a.ndim
        return pl.BlockSpec((None,) + a.shape[1:], lambda j: (l,) + (0,) * (nd - 1))

    stacked = [w[k] for k in ("norm_w", "w_in", "w_out")]
    consts = [w[k] for k in ("rdec", "recum", "rkdec", "retot")]
    ret_nw = [w["ret_norm_w"], w["lb"], w["hgrn_norm_w"]]
    s5 = [w[k] for k in ("s5_A", "s5_Bblk_cols", "s5_Cblk", "s5_D", "s5_glu_w", "s5_glu_b")]
    m2 = [w[k] for k in ("m2_conv_w", "m2_conv_b", "m2_dt_bias", "m2_A_log", "m2_D", "m2_norm_w")]
    pair_consts = [w["bones"], w["hexp"]]
    args = [x, w["cos_p"], w["sin_p"]] + stacked + consts + ret_nw + [w["sel"]] + s5 + m2 + pair_consts
    in_specs = ([xspec, tspec, tspec] + [per_layer(a) for a in stacked] + [_const_spec(a.shape) for a in consts]
                + [per_layer(a) for a in ret_nw] + [_const_spec(w["sel"].shape)]
                + [per_layer(a) for a in s5 + m2] + [_const_spec(a.shape) for a in pair_consts])
    if last:
        args.append(w["final_norm_w"])
        in_specs.append(_const_spec(w["final_norm_w"].shape))
    state_shapes = [jax.ShapeDtypeStruct((nb, NH, HD, HD), F32),
                    jax.ShapeDtypeStruct((nb, NH, HD, HD), F32),
                    jax.ShapeDtypeStruct((nb, 2 * S5N), F32),
                    jax.ShapeDtypeStruct((nb, NH, HD, HD), F32),
                    jax.ShapeDtypeStruct((nb, CONV_K - 1, CONV_CH), F32)]
    out_shape = [jax.ShapeDtypeStruct((nb, L, D), F32)] + state_shapes
    out_specs = [xspec] + [_const_spec(s.shape) for s in state_shapes]
    scratch = [pltpu.VMEM((rows, D), BF16),
               pltpu.VMEM((rows, P_PAD), F32),
               pltpu.VMEM((rows, D), BF16),
               pltpu.VMEM((NBC, rows, BCW), F32),
               pltpu.VMEM((W // LANE, rows, LANE), F32),
               pltpu.VMEM((rows, W), BF16),
               pltpu.VMEM((nb, TB + 8, CONV_CH), F32),
               pltpu.VMEM((W // LANE, rows, SUB * LANE), BF16),
               pltpu.VMEM((W // LANE, rows, LANE), F32),
               pltpu.VMEM((nb, TB, W), F32),
               pltpu.VMEM((nb, TB, W), F32),
               pltpu.VMEM((nb, TB, W), F32)] + [pltpu.VMEM((nb, W // LANE, LANE, LANE), F32)] * 3
    return pl.pallas_call(
        functools.partial(_prompt_layer_body, last),
        grid=(nblk,),
        in_specs=in_specs,
        out_specs=out_specs,
        out_shape=out_shape,
        scratch_shapes=scratch,
        compiler_params=pltpu.CompilerParams(dimension_semantics=("arbitrary",),
                                             vmem_limit_bytes=VMEM_LIMIT),
        name="prompt_layer",
    )(*args)


def _rope_tables(pos):
    half = HD // 2
    inv = 1.0 / (ROPE_BASE ** (np.arange(half, dtype=np.float64) / half))
    ang = pos.astype(np.float64)[:, None] * inv[None, :]
    cos, sin = np.cos(ang), np.sin(ang)
    cos_t = np.tile(cos, (1, LANE // half))
    sin_t = np.tile(np.concatenate([-sin, sin], axis=1), (1, LANE // HD))
    return cos_t.astype(np.float32), sin_t.astype(np.float32)


def _retention_tables():
    log_gamma = np.log1p(-(2.0 ** (-5.0 - np.arange(NH, dtype=np.float64))))
    cum = np.cumsum(np.broadcast_to(log_gamma, (TB, NH)), axis=0)
    total = cum[-1]
    causal = np.tril(np.ones((TB, TB), dtype=bool))
    diff = cum[:, None, :] - cum[None, :, :]
    dec = np.where(causal[:, :, None], np.exp(np.where(causal[:, :, None], diff, 0.0)), 0.0)
    rdec = np.moveaxis(dec, 2, 0)
    recum = np.broadcast_to(np.exp(cum).T[:, :, None], (NH, TB, HD))
    rkdec = np.broadcast_to(np.exp(total[None, :] - cum).T[:, :, None], (NH, TB, HD))
    retot = np.broadcast_to(np.exp(total)[:, None, None], (NH, 1, HD))
    pair = lambda t: np.concatenate([t[0::2], t[1::2]], axis=-1).astype(np.float32)
    return pair(rdec), pair(recum), pair(rkdec), pair(retot), log_gamma


def _sel_matrix():
    sel = np.zeros((SUB, 2, HD, 2, TB), np.float32)
    for s_ in range(SUB):
        for h2 in range(2):
            sel[s_, h2, :, h2, s_::SUB] = 1.0
    return jnp.asarray(sel.reshape(SUB * LANE, LANE), dtype=BF16)


CAST_ROWS = 256


def _cast_body(n, x_ref, o_ref):
    o_ref[:, 0:n] = x_ref[...].astype(BF16)
    if o_ref.shape[1] > n:
        o_ref[:, n:] = jnp.zeros((o_ref.shape[0], o_ref.shape[1] - n), BF16)


def _to_bf16_padded(w, n_pad):
    depth, k, n = w.shape
    return pl.pallas_call(
        functools.partial(_cast_body, n),
        grid=(depth, k // CAST_ROWS),
        in_specs=[pl.BlockSpec((None, CAST_ROWS, n), lambda l, i: (l, i, 0))],
        out_specs=pl.BlockSpec((None, CAST_ROWS, n_pad), lambda l, i: (l, i, 0)),
        out_shape=jax.ShapeDtypeStruct((depth, k, n_pad), BF16),
        compiler_params=pltpu.CompilerParams(dimension_semantics=("arbitrary", "arbitrary")),
        name="weights_to_bf16",
    )(w.astype(F32))


def _rows(v, width=None):
    v = v.astype(F32)
    if width is not None and v.shape[-1] < width:
        v = jnp.pad(v, ((0, 0), (0, width - v.shape[-1])))
    return v[:, None, :]


def _block_diag(blocks):
    g, r, c = blocks.shape
    eye = jnp.eye(g, dtype=blocks.dtype)
    return jnp.einsum('grc,gh->grhc', blocks, eye).reshape(g * r, g * c)


def _s5_discretise(A_re, A_im, log_dt, B_re, B_im):
    A_re, A_im = A_re.astype(F32), A_im.astype(F32)
    dt = jnp.exp(log_dt.astype(F32))[:, None]
    mag = jnp.exp(A_re * dt)
    ab_re, ab_im = mag * jnp.cos(A_im * dt), mag * jnp.sin(A_im * dt)
    nr, ni = ab_re - 1.0, ab_im
    den = A_re * A_re + A_im * A_im
    f_re = (nr * A_re + ni * A_im) / den
    f_im = (ni * A_re - nr * A_im) / den
    B_re, B_im = B_re.astype(F32), B_im.astype(F32)
    bb_re = f_re[..., None] * B_re - f_im[..., None] * B_im
    bb_im = f_re[..., None] * B_im + f_im[..., None] * B_re
    return ab_re, ab_im, bb_re, bb_im


def _s5_matrices(A_re, A_im, log_dt, B_re, B_im, C_re, C_im):
    ab_re, ab_im, bb_re, bb_im = _s5_discretise(A_re, A_im, log_dt, B_re, B_im)
    bblk = jnp.concatenate([_block_diag(jnp.swapaxes(bb_re, 1, 2)),
                            _block_diag(jnp.swapaxes(bb_im, 1, 2))], axis=1)
    cblk = jnp.concatenate([_block_diag(jnp.swapaxes(C_re.astype(F32), 1, 2)),
                            _block_diag(jnp.swapaxes(-C_im.astype(F32), 1, 2))], axis=0)
    return jnp.stack([ab_re.reshape(-1), ab_im.reshape(-1)], axis=0), bblk, cblk


R_COS, R_SIN, R_RETNW, R_LB, R_HNW, R_S5D, R_GLUB, R_MNW = [i * W for i in range(8)]
R_AR = 8 * W
R_AI = R_AR + S5N
R_DTB = R_AI + S5N
R_ALOG = R_DTB + 8
R_MD = R_ALOG + 8
N_COLP = R_MD + 8


def _prepare(p, lb_all, prompt_len):
    depth = p["norm_w"].shape[0]
    s5_A, bblk, cblk = jax.vmap(_s5_matrices)(p["s5_A_re"], p["s5_A_im"], p["s5_log_dt"], p["s5_B_re"],
                                              p["s5_B_im"], p["s5_C_re"], p["s5_C_im"])
    cos_p, sin_p = _rope_tables(np.arange(prompt_len, dtype=np.float32))
    rdec, recum, rkdec, retot, log_gamma = _retention_tables()
    cos_s, sin_s = _rope_tables(np.float32(PAST_LEN) + np.arange(1, dtype=np.float32))
    w = dict(
        norm_w=_rows(p["norm_w"]),
        w_in=_to_bf16_padded(p["w_in"], P_PAD),
        w_out=_to_bf16_padded(p["w_out"], D),
        ret_norm_w=_rows(p["ret_norm_w"]),
        lb=_rows(lb_all),
        hgrn_norm_w=_rows(p["hgrn_norm_w"]),
        s5_A=s5_A,
        s5_Bblk=bblk.astype(BF16),
        s5_Cblk=cblk.astype(BF16),
        s5_D=_rows(p["s5_D"]),
        s5_glu_w=p["s5_glu_w"].astype(BF16),
        s5_glu_b=_rows(p["s5_glu_b"]),
        m2_conv_w=p["m2_conv_w"].astype(F32),
        m2_conv_b=_rows(p["m2_conv_b"]),
        m2_dt_bias=_rows(jnp.repeat(p["m2_dt_bias"], HD, axis=1)),
        m2_A_log=_rows(jnp.repeat(p["m2_A_log"], HD, axis=1)),
        m2_D=_rows(jnp.repeat(p["m2_D"], HD, axis=1)),
        m2_norm_w=_rows(p["m2_norm_w"]),
        final_norm_w=p["final_norm_w"].astype(F32).reshape(1, D),
    )
    tile2 = lambda t: jnp.asarray(np.broadcast_to(np.tile(t[0], W // LANE), (depth, W)))
    colp = jnp.concatenate(
        [tile2(cos_s), tile2(sin_s), w["ret_norm_w"][:, 0], w["lb"][:, 0], w["hgrn_norm_w"][:, 0],
         w["s5_D"][:, 0], w["s5_glu_b"][:, 0], w["m2_norm_w"][:, 0], s5_A[:, 0], s5_A[:, 1],
         _rows(p["m2_dt_bias"], 8)[:, 0], _rows(p["m2_A_log"], 8)[:, 0], _rows(p["m2_D"], 8)[:, 0]], axis=1)
    w.update(
        colp=jnp.broadcast_to(colp[:, :, None], (depth, N_COLP, LANE)),
        s5_Bblk_cols=jnp.moveaxis(w["s5_Bblk"].reshape(depth, W, NBC, BCW), 2, 1),
        s5_BblkT=jnp.swapaxes(w["s5_Bblk"], 1, 2),
        s5_CblkT=jnp.swapaxes(w["s5_Cblk"], 1, 2),
        s5_glu_wT=jnp.swapaxes(w["s5_glu_w"], 1, 2),
        rgam=jnp.asarray(np.broadcast_to(np.exp(log_gamma)[:, None, None], (NH, 8, LANE)), dtype=F32),
        cos_p=jnp.asarray(cos_p), sin_p=jnp.asarray(sin_p), rdec=jnp.asarray(rdec), recum=jnp.asarray(recum),
        rkdec=jnp.asarray(rkdec), retot=jnp.asarray(retot), sel=_sel_matrix(),
        bones=jnp.asarray(np.kron(np.eye(LANE // HD), np.full((HD, HD), 1.0 / HD)), dtype=BF16),
        hexp=jnp.asarray(np.kron(np.eye(LANE, NH), np.ones((1, HD))), dtype=F32),
    )
    return w


KC = 16
NKC = HD // KC
STEPS = NH * NKC


def _rotary_cols(x, cos, sin_signed):
    half = HD // 2
    parts = []
    for h in range(NH):
        parts += [x[h * HD + half:(h + 1) * HD], x[h * HD:h * HD + half]]
    return x * cos + jnp.concatenate(parts, axis=0) * sin_signed


def _expand_rows(dst, x):
    for c in range(x.shape[0]):
        dst[c] = jnp.broadcast_to(x[c:c + 1, :], (8, LANE))


def _sample_body(n_steps, x_ref, normw_ref, win_ref, wout_ref, fnw_ref, colp_ref, sbt_ref, sct_ref,
                 gluwt_ref, cw_ref, cb_ref, rgam_ref,
                 ret_in, hg_in, m2_in, s5re_in, s5im_in, buf_in,
                 y_ref, ret_out, hg_out, m2_out, s5re_out, s5im_out, buf_out,
                 xs_s, pt_s, vt_s, ot_s, mixt_s, o_s, hp_s,
                 kret_s, qret_s, khg_s, qhg_s, ahg_s, km2_s, qm2_s):
    i = pl.program_id(0)
    r = i % STEPS
    h = r // NKC
    kc = r % NKC

    def cp(r0, n=W):
        return colp_ref[r0:r0 + n, :]

    def pc(c0, w):
        return pt_s[c0:c0 + w, :]

    @pl.when(i == 0)
    def _load():
        xs_s[...] = x_ref[...]

    @pl.when(r == 0)
    def _prep():
        x = xs_s[...]
        hn = x * lax.rsqrt(jnp.mean(x * x, axis=-1, keepdims=True) + EPS) * normw_ref[...]
        proj = jnp.dot(hn.astype(BF16), win_ref[...], preferred_element_type=F32)

        xnew = proj[:, C_XBC:C_XBC + CONV_CH]
        acc = cb_ref[...] + xnew * cw_ref[CONV_K - 1:CONV_K, :]
        for t in range(CONV_K - 1):
            acc = acc + buf_in[t] * cw_ref[t:t + 1, :]
        for t in range(CONV_K - 2):
            buf_out[t] = buf_in[t + 1]
        buf_out[CONV_K - 2] = xnew
        xbc = _silu(acc)

        for t in range(P_PAD // LANE):
            c0 = t * LANE
            if C_XBC <= c0 < C_XBC + CONV_CH:
                tile = xbc[:, c0 - C_XBC:c0 - C_XBC + LANE]
            else:
                tile = proj[:, c0:c0 + LANE]
            pt_s[c0:c0 + LANE, :] = tile.T

        cos, sin = cp(R_COS), cp(R_SIN)
        _expand_rows(kret_s, _rotary_cols(pc(C_RK, W), cos, sin) * (HD ** -0.5))
        _expand_rows(qret_s, _rotary_cols(pc(C_RQ, W), cos, sin))
        vt_s[0] = pc(C_RV, W)

        fr = pc(C_GF, W)
        lb = cp(R_LB)
        logf = _log_sigmoid(fr) + jnp.log(1.0 + lb * jnp.exp(jnp.minimum(-fr, EXP_CLIP)))
        _expand_rows(ahg_s, jnp.exp(logf))
        _expand_rows(khg_s, (1.0 - lb) * jax.nn.sigmoid(-fr))
        _expand_rows(qhg_s, _silu(pc(C_GQ, W)))
        vt_s[1] = pc(C_GI, W)

        dt8 = _softplus(pc(C_DT, 8) + cp(R_DTB, 8))
        adec8 = jnp.exp(dt8 * (-jnp.exp(cp(R_ALOG, 8))))
        for hh in range(NH):
            hp_s[hh] = jnp.broadcast_to(adec8[hh:hh + 1, :], (8, LANE))
            vt_s[2, hh * HD:(hh + 1) * HD, :] = pc(C_XBC + hh * HD, HD) * dt8[hh:hh + 1, :]
        _expand_rows(km2_s, pc(C_XBC + W, 2 * HD))
        _expand_rows(qm2_s, pc(C_XBC + W + 2 * HD, 2 * HD))

        u = pc(C_SU, W)
        bu = jnp.dot(sbt_ref[...], u.astype(BF16), preferred_element_type=F32)
        hr, hi = s5re_in[...], s5im_in[...]
        ar, ai = cp(R_AR, S5N), cp(R_AI, S5N)
        nr = ar * hr - ai * hi + bu[0:S5N]
        ni = ar * hi + ai * hr + bu[S5N:2 * S5N]
        s5re_out[...] = nr
        s5im_out[...] = ni
        hcat = jnp.concatenate([nr, ni], axis=0).astype(BF16)
        sy = jnp.dot(sct_ref[...], hcat, preferred_element_type=F32) + cp(R_S5D) * u
        gy = _gelu_tanh(sy)
        glu = jnp.dot(gluwt_ref[...], gy.astype(BF16), preferred_element_type=F32) + cp(R_GLUB)
        mixt_s[2 * W:3 * W, :] = gy * jax.nn.sigmoid(glu) * _silu(pc(C_SG, W))

    @pl.when(kc == 0)
    def _zero():
        o_s[...] = jnp.zeros(o_s.shape, F32)

    hrow = pl.multiple_of(h * HD, HD)
    cbase = h * HD + kc * KC
    gbase = (h // 2) * HD + kc * KC

    def update(m, st_in, st_out, kx, qx, base, decay):
        v3 = vt_s[m, pl.ds(hrow, HD), :].reshape(HD // 8, 8, LANE)

        def body(kk, o):
            s_new = decay(kk) * st_in[kk].reshape(HD // 8, 8, LANE) + kx[base + kk] * v3
            st_out[kk] = s_new.reshape(HD, LANE)
            return o + qx[base + kk] * s_new
        o_s[m] = lax.fori_loop(0, KC, body, o_s[m], unroll=2)

    gam = rgam_ref[h]
    update(0, ret_in, ret_out, kret_s, qret_s, cbase, lambda kk: gam)
    update(1, hg_in, hg_out, khg_s, qhg_s, cbase, lambda kk: ahg_s[cbase + kk])
    adec = hp_s[h]
    update(2, m2_in, m2_out, km2_s, qm2_s, gbase, lambda kk: adec)

    @pl.when(kc == NKC - 1)
    def _head_done():
        for m in range(3):
            ot_s[m, pl.ds(hrow, HD), :] = o_s[m].reshape(HD, LANE)

    @pl.when(r == STEPS - 1)
    def _finish():
        def head_rms_cols(o):
            parts = []
            for hh in range(NH):
                seg = o[hh * HD:(hh + 1) * HD]
                parts.append(seg * lax.rsqrt(jnp.mean(seg * seg, axis=0, keepdims=True) + EPS))
            return jnp.concatenate(parts, axis=0)

        mixt_s[0:W, :] = head_rms_cols(ot_s[0]) * cp(R_RETNW) * _silu(pc(C_RG, W))
        mixt_s[W:2 * W, :] = head_rms_cols(ot_s[1]) * cp(R_HNW) * _silu(pc(C_GG, W))
        md8 = cp(R_MD, 8)
        ym = jnp.concatenate([ot_s[2, hh * HD:(hh + 1) * HD, :] + md8[hh:hh + 1, :] * pc(C_XBC + hh * HD, HD)
                              for hh in range(NH)], axis=0)
        my = ym * _silu(pc(C_MZ, W))
        mixt_s[3 * W:4 * W, :] = my * lax.rsqrt(jnp.mean(my * my, axis=0, keepdims=True) + EPS) * cp(R_MNW)
        mixed = jnp.concatenate([mixt_s[t * LANE:(t + 1) * LANE, :].T for t in range(D // LANE)], axis=1)
        xo = xs_s[...] + jnp.dot(mixed.astype(BF16), wout_ref[...], preferred_element_type=F32)
        xs_s[...] = xo

        @pl.when(i == n_steps - 1)
        def _final_norm():
            y_ref[...] = xo * lax.rsqrt(jnp.mean(xo * xo, axis=-1, keepdims=True) + EPS) * fnw_ref[...]


def _sample_step(x, w, ret, hg, m2, s5re, s5im, buf):
    depth = ret.shape[0]
    n = x.shape[0]
    n_steps = depth * STEPS
    lay = lambda i: i // STEPS

    def per_layer(a):
        nd = a.ndim
        return pl.BlockSpec((None,) + a.shape[1:], lambda i: (lay(i),) + (0,) * (nd - 1))

    st_spec = pl.BlockSpec((None, None, KC, HD, LANE),
                           lambda i: (lay(i), (i % STEPS) // NKC, i % NKC, 0, 0))
    weights = [w["norm_w"], w["w_in"], w["w_out"]]
    tables = [w["colp"], w["s5_BblkT"], w["s5_CblkT"], w["s5_glu_wT"], w["m2_conv_w"], w["m2_conv_b"]]
    in_specs = ([_const_spec(x.shape)] + [per_layer(a) for a in weights] + [_const_spec(w["final_norm_w"].shape)]
                + [per_layer(a) for a in tables] + [_const_spec(w["rgam"].shape)]
                + [st_spec, st_spec, st_spec, per_layer(s5re), per_layer(s5im), per_layer(buf)])
    out_shape = [jax.ShapeDtypeStruct((n, D), F32)] + [jax.ShapeDtypeStruct(a.shape, F32)
                                                       for a in (ret, hg, m2, s5re, s5im, buf)]
    out_specs = [_const_spec((n, D)), st_spec, st_spec, st_spec, per_layer(s5re), per_layer(s5im),
                 per_layer(buf)]
    expand = lambda c: pltpu.VMEM((c, 8, LANE), F32)
    scratch = [pltpu.VMEM((n, D), F32),
               pltpu.VMEM((P_PAD, LANE), F32),
               pltpu.VMEM((3, W, LANE), F32),
               pltpu.VMEM((3, W, LANE), F32),
               pltpu.VMEM((D, LANE), F32),
               pltpu.VMEM((3, HD // 8, 8, LANE), F32),
               pltpu.VMEM((NH, 8, LANE), F32),
               expand(W), expand(W), expand(W), expand(W), expand(W), expand(2 * HD), expand(2 * HD)]
    return pl.pallas_call(
        functools.partial(_sample_body, n_steps),
        grid=(n_steps,),
        in_specs=in_specs,
        out_specs=out_specs,
        out_shape=out_shape,
        scratch_shapes=scratch,
        compiler_params=pltpu.CompilerParams(dimension_semantics=("arbitrary",),
                                             vmem_limit_bytes=VMEM_LIMIT),
        name="sample_step",
    )(x, *weights, w["final_norm_w"], *tables, w["rgam"], ret, hg, m2, s5re, s5im, buf)


def kernel(x_prompt, x_sample, state_ret, state_hgrn, state_s5_re, state_s5_im, state_m2_ssm,
           state_m2_conv, norm_w, w_in, ret_norm_w, hgrn_lb_logits, hgrn_norm_w, s5_A_re, s5_A_im,
           s5_log_dt, s5_B_re, s5_B_im, s5_C_re, s5_C_im, s5_D, s5_glu_w, s5_glu_b, m2_conv_w,
           m2_conv_b, m2_dt_bias, m2_A_log, m2_D, m2_norm_w, w_out, final_norm_w):
    p = dict(norm_w=norm_w, w_in=w_in, ret_norm_w=ret_norm_w, hgrn_norm_w=hgrn_norm_w,
             s5_A_re=s5_A_re, s5_A_im=s5_A_im, s5_log_dt=s5_log_dt, s5_B_re=s5_B_re, s5_B_im=s5_B_im,
             s5_C_re=s5_C_re, s5_C_im=s5_C_im, s5_D=s5_D, s5_glu_w=s5_glu_w, s5_glu_b=s5_glu_b,
             m2_conv_w=m2_conv_w, m2_conv_b=m2_conv_b, m2_dt_bias=m2_dt_bias, m2_A_log=m2_A_log,
             m2_D=m2_D, m2_norm_w=m2_norm_w, w_out=w_out, final_norm_w=final_norm_w)
    depth = norm_w.shape[0]
    nbp, lp, _ = x_prompt.shape
    nbs = x_sample.shape[0]

    lb_sm = jax.nn.softmax(hgrn_lb_logits.astype(F32), axis=0)
    lb_all = jnp.clip(jnp.cumsum(lb_sm, axis=0) - lb_sm[0], 0.0, 1.0)

    w = _prepare(p, lb_all, lp)

    xp = x_prompt
    pst = []
    for l in range(depth):
        outs = _prompt_layer(xp, l, l == depth - 1, w)
        xp = outs[0]
        ret, hg, s5, m2, buf = outs[-5:]
        pst.append((ret, hg, s5[:, :S5N].reshape(nbp, S5G, S5P), s5[:, S5N:].reshape(nbp, S5G, S5P),
                    m2, buf))
    yp = xp

    seq_last = lambda a: jnp.moveaxis(a.astype(F32), 1, -1)
    ys, ret, hg, m2, s5re, s5im, buf = _sample_step(
        x_sample.reshape(nbs, D), w,
        seq_last(state_ret), seq_last(state_hgrn), seq_last(state_m2_ssm),
        seq_last(state_s5_re).reshape(depth, S5N, nbs), seq_last(state_s5_im).reshape(depth, S5N, nbs),
        jnp.swapaxes(state_m2_conv.astype(F32), 1, 2))
    seq_second = lambda a: jnp.moveaxis(a, -1, 1)

    stk = lambda i: jnp.stack([s[i] for s in pst], axis=0)
    return (yp, ys.reshape(nbs, 1, D),
            stk(0), stk(1), stk(2), stk(3), stk(4), stk(5),
            seq_second(ret), seq_second(hg), seq_second(s5re.reshape(depth, S5G, S5P, nbs)),
            seq_second(s5im.reshape(depth, S5G, S5P, nbs)), seq_second(m2), jnp.swapaxes(buf, 1, 2))
```

```python
import functools
import math

import numpy as np
import jax
import jax.numpy as jnp
from jax import lax
from jax.experimental import pallas as pl
from jax.experimental.pallas import tpu as pltpu

F32 = jnp.float32
BF16 = jnp.bfloat16
HI = lax.Precision.HIGHEST

D = 1024
W = 256
NH = 4
HD = 64
S5G = 16
S5C = 16
S5P = 64
S5N = S5G * S5P
NBC = 8
BCW = 2 * S5N // NBC
CONV_CH = 512
CONV_K = 4
TB = 64
SUB = 16
NSUB = TB // SUB
EPS = 1e-6
EXP_CLIP = 60.0
ROPE_BASE = 10000.0
PAST_LEN = 16384

C_RQ, C_RK, C_RV, C_RG = 0, 256, 512, 768
C_GQ, C_GF, C_GI, C_GG = 1024, 1280, 1536, 1792
C_SU, C_SG = 2048, 2304
C_MZ, C_XBC, C_DT = 2560, 2816, 3328
P_TOTAL = 3332
PCH = 1152
NPC = 3
P_PAD = NPC * PCH
LANE = 128
VMEM_LIMIT = 56 * 1024 * 1024


def _silu(x):
    return x * jax.nn.sigmoid(x)


def _softplus(x):
    return jnp.maximum(x, 0.0) + jnp.log(1.0 + jnp.exp(-jnp.abs(x)))


def _log_sigmoid(x):
    return jnp.minimum(x, 0.0) - jnp.log(1.0 + jnp.exp(-jnp.abs(x)))


def _round_robin(gens):
    gens = list(gens)
    while gens:
        alive = []
        for g in gens:
            try:
                next(g)
                alive.append(g)
            except StopIteration:
                pass
        gens = alive


def _for_sequences(nb, parts, group):
    def body(i, c):
        built = [parts(i * group + k) for k in range(group)]
        _round_robin([g for gens, _ in built for g in gens])
        for _, finish in built:
            if finish is not None:
                finish()
        return c
    lax.fori_loop(0, nb // group, body, 0)


def _gelu_tanh(x):
    c = math.sqrt(2.0 / math.pi)
    return 0.5 * x * (1.0 + jnp.tanh(c * (x + 0.044715 * (x * x * x))))


def _dot(a, b):
    return jnp.dot(a.astype(BF16), b.astype(BF16), preferred_element_type=F32)


def _dot_hi(a, b):
    return jnp.dot(a, b, precision=HI, preferred_element_type=F32)


def _dot_tn_hi(a, b):
    return lax.dot_general(a, b, (((0,), (0,)), ((), ())), precision=HI,
                           preferred_element_type=F32)


def _rot_half_partner(x):
    lane = lax.broadcasted_iota(jnp.int32, x.shape, 1)
    first = (lane % HD) < (HD // 2)
    return jnp.where(first, pltpu.roll(x, LANE - HD // 2, 1), pltpu.roll(x, HD // 2, 1))


def _rotary(x, cos, sin_signed):
    parts = []
    for i in range(W // LANE):
        xi = x[:, i * LANE:(i + 1) * LANE]
        parts.append(xi * cos + _rot_half_partner(xi) * sin_signed)
    return jnp.concatenate(parts, axis=1)


def _prompt_layer_body(last, *refs):
    (x_ref, cos_ref, sin_ref, normw_ref, win_ref, wout_ref,
     rdec_ref, recum_ref, rkdec_ref, retot_ref, retnw_ref,
     lb_ref, hnw_ref, sel_ref,
     sA_ref, sB_ref, sC_ref, sD_ref, gluw_ref, glub_ref,
     cw_ref, cb_ref, dtb_ref, alog_ref, md_ref, mnw_ref, bones_ref, hexp_ref) = refs[:28]
    refs = refs[28:]
    if last:
        fnw_ref = refs[0]
        refs = refs[1:]
    act_ref = refs[0]
    refs = refs[1:]
    (ret_ref, hg_ref, s5_ref, m2_ref, m2buf_ref,
     hn_s, proj_s, mixed_s, bu_s, u_s, ub_s, cv_s, p_s, dg_s, hq_s, hk_s, cum_s, hret_s, hhg_s, hm2_s) = refs
    j = pl.program_id(0)
    nb = x_ref.shape[0]

    @pl.when(j == 0)
    def _init():
        hret_s[...] = jnp.zeros(hret_s.shape, F32)
        hhg_s[...] = jnp.zeros(hhg_s.shape, F32)
        s5_ref[...] = jnp.zeros(s5_ref.shape, F32)
        hm2_s[...] = jnp.zeros(hm2_s.shape, F32)
        cv_s[...] = jnp.zeros(cv_s.shape, F32)

    ti = lax.broadcasted_iota(jnp.int32, (TB, TB), 0)
    si = lax.broadcasted_iota(jnp.int32, (TB, TB), 1)
    causal = si <= ti
    tri_l = causal.astype(F32)
    ones_tt = jnp.ones((TB, TB), F32)
    pr = lax.broadcasted_iota(jnp.int32, (LANE, LANE), 0)
    pc_ = lax.broadcasted_iota(jnp.int32, (LANE, LANE), 1)
    pairmask = (pr // HD) == (pc_ // HD)
    t2 = lax.broadcasted_iota(jnp.int32, (TB, LANE), 0)
    l2 = lax.broadcasted_iota(jnp.int32, (TB, LANE), 1)
    causal2 = (l2 % HD) <= t2
    tri_u2 = (t2 <= (l2 % HD)).astype(F32)
    first_head = l2 < HD
    subdiag2 = (t2 // SUB) == ((l2 % HD) // SUB)
    srcblock = [pairmask & (((pc_ % HD) // SUB) == jb) for jb in range(NSUB - 1)]
    ones_tl = jnp.ones((TB, LANE), F32)

    def dup_t(x2):
        return jnp.concatenate([x2, x2], axis=0).T

    def blockdiag2(x2):
        return jnp.where(pairmask, jnp.concatenate([x2, x2], axis=0), 0.0)

    def norm_body(b, c):
        xb = x_ref[b]
        hn = xb * lax.rsqrt(jnp.mean(xb * xb, axis=-1, keepdims=True) + EPS) * normw_ref[...]
        hn_s[pl.ds(pl.multiple_of(b * TB, TB), TB), :] = hn.astype(BF16)
        return c
    lax.fori_loop(0, nb, norm_body, 0)
    for c in range(NPC):
        cs = slice(c * PCH, (c + 1) * PCH)
        proj_s[:, cs] = jnp.dot(hn_s[...], win_ref[:, cs], preferred_element_type=F32)

    cos = cos_ref[...]
    sin = sin_ref[...]

    def reorder_u(b, c):
        u = proj_s[pl.ds(pl.multiple_of(b * TB, TB), TB), C_SU:C_SU + W]
        for c_ in range(W // LANE):
            u_s[c_, pl.ds(b, TB, stride=nb), :] = u[:, c_ * LANE:(c_ + 1) * LANE]
        return c
    lax.fori_loop(0, nb, reorder_u, 0)
    ub_s[...] = jnp.concatenate([u_s[c_] for c_ in range(W // LANE)], axis=1).astype(BF16)

    def s5_input_piece(cb):
        bu_s[cb] = jnp.dot(ub_s[...], sB_ref[cb], preferred_element_type=F32)
        yield

    def phase1_parts(b):
        r0 = pl.multiple_of(b * TB, TB)

        def pj(c0, w):
            return proj_s[pl.ds(r0, TB), c0:c0 + w]

        rq = _rotary(pj(C_RQ, W), cos, sin)
        rk = _rotary(pj(C_RK, W), cos, sin) * (HD ** -0.5)
        rv = pj(C_RV, W)
        rg = pj(C_RG, W)

        def ret_pair(p):
            ls = slice(p * LANE, (p + 1) * LANE)
            q2, k2, v2 = rq[:, ls], rk[:, ls], rv[:, ls]
            hb = hret_s[b, p]
            kt = dup_t(k2)
            s_raw = _dot(q2, jnp.where(pairmask, kt, 0.0))
            oi = _dot(q2, hb)
            kv = _dot(kt[:, 0:HD], v2 * rkdec_ref[p])
            yield
            o = _dot(s_raw * rdec_ref[p], blockdiag2(v2))
            yield
            o = o + oi * recum_ref[p]
            hret_s[b, p] = jnp.where(pairmask, retot_ref[p] * hb + kv, 0.0)
            ms = _dot(o * o, bones_ref[...])
            yield
            o = o * lax.rsqrt(ms + EPS) * retnw_ref[:, ls] * _silu(rg[:, ls])
            mixed_s[pl.ds(r0, TB), 0 * W + p * LANE:0 * W + (p + 1) * LANE] = o.astype(BF16)

        cv_s[b, 8:8 + TB, :] = pj(C_XBC, CONV_CH)
        acc = cb_ref[...] + cv_s[b, 5:5 + TB, :] * cw_ref[0:1, :]
        for i in range(1, CONV_K):
            acc = acc + cv_s[b, 5 + i:5 + i + TB, :] * cw_ref[i:i + 1, :]
        tail = cv_s[b, TB + 5:TB + 8, :]
        cv_s[b, 5:8, :] = tail
        m2buf_ref[b] = tail
        xbc = _silu(acc)
        xm = xbc[:, 0:W]
        bm = xbc[:, W:W + 2 * HD]
        cm = xbc[:, W + 2 * HD:W + 4 * HD]
        bm_sw = pltpu.roll(bm, HD, 1)
        cm_sw = pltpu.roll(cm, HD, 1)
        dt_b = _softplus(_dot_hi(pj(C_DT, LANE), hexp_ref[...]) + dtb_ref[...])
        la_b = dt_b * (-jnp.exp(alog_ref[...]))
        cum_b = _dot_hi(tri_l, la_b)
        ys = [None] * (W // LANE)

        def m2_pair(p):
            ls = slice(p * LANE, (p + 1) * LANE)
            b2 = jnp.where(first_head, bm, bm_sw) if p == 0 else jnp.where(first_head, bm_sw, bm)
            c2 = jnp.where(first_head, cm, cm_sw) if p == 0 else jnp.where(first_head, cm_sw, cm)
            x2, dt2, cum2 = xm[:, ls], dt_b[:, ls], cum_b[:, ls]
            xdt2 = x2 * dt2
            r2 = _dot_hi(ones_tt, la_b[:, ls] * tri_u2)
            tot2 = cum2[TB - 1:TB, :]
            hb = hm2_s[b, p]
            bt = dup_t(b2)
            s_raw = _dot(c2, jnp.where(pairmask, bt, 0.0))
            oi = _dot(c2, hb)
            kv = _dot(bt[:, 0:HD], xdt2 * jnp.exp(tot2 - cum2))
            decay = jnp.where(causal2, jnp.exp(jnp.minimum(cum2 - r2, 0.0)), 0.0)
            yield
            o = _dot(s_raw * decay, blockdiag2(xdt2))
            yield
            hm2_s[b, p] = jnp.where(pairmask, jnp.exp(tot2) * hb + kv, 0.0)
            ys[p] = o + oi * jnp.exp(cum2) + md_ref[:, ls] * x2

        fr = pj(C_GF, W)
        lb = lb_ref[...]
        logf = _log_sigmoid(fr) + jnp.log(1.0 + lb * jnp.exp(jnp.minimum(-fr, EXP_CLIP)))
        hq = _silu(pj(C_GQ, W))
        hk = (1.0 - lb) * jax.nn.sigmoid(-fr)
        cum = _dot_hi(tri_l, logf)
        hq_s[b] = hq
        hk_s[b] = hk
        cum_s[b] = cum
        t8 = lax.broadcasted_iota(jnp.int32, (8, LANE), 0)

        def diag_products(p):
            ls = slice(p * LANE, (p + 1) * LANE)
            for s_ in range(SUB):
                pieces = []
                for i in range(NSUB):
                    kb = jnp.broadcast_to(hk_s[b, i * SUB + s_:i * SUB + s_ + 1, ls], (8, LANE))
                    cb = jnp.broadcast_to(cum_s[b, i * SUB + s_:i * SUB + s_ + 1, ls], (8, LANE))
                    for half in range(SUB // 8):
                        rows = slice(i * SUB + half * 8, i * SUB + half * 8 + 8)
                        if half * 8 + 7 < s_:
                            pieces.append(jnp.zeros((8, LANE), F32))
                        elif half * 8 >= s_:
                            pieces.append(hq[rows, ls] * kb * jnp.exp(cum[rows, ls] - cb))
                        else:
                            e = jnp.exp(jnp.minimum(cum[rows, ls] - cb, 0.0))
                            pieces.append(jnp.where(t8 + half * 8 >= s_, hq[rows, ls] * kb * e, 0.0))
                pv = jnp.concatenate(pieces, axis=0)
                p_s[p, pl.ds(r0, TB), s_ * LANE:(s_ + 1) * LANE] = pv.astype(BF16)
                if s_ % 2 == 1:
                    yield

        def finish():
            my = jnp.concatenate(ys, axis=1) * _silu(pj(C_MZ, W))
            om = my * lax.rsqrt(jnp.mean(my * my, axis=-1, keepdims=True) + EPS) * mnw_ref[...]
            mixed_s[pl.ds(r0, TB), 3 * W:4 * W] = om.astype(BF16)

        gens = ([s5_input_piece(b)] + [ret_pair(p) for p in range(W // LANE)]
                + [m2_pair(p) for p in range(W // LANE)] + [diag_products(p) for p in range(W // LANE)])
        return gens, finish

    _for_sequences(nb, phase1_parts, 2)

    for p in range(W // LANE):
        dg_s[p] = jnp.dot(p_s[p], sel_ref[...], preferred_element_type=F32)

    ar = jnp.broadcast_to(sA_ref[0:1, :], (nb, S5N))
    ai = jnp.broadcast_to(sA_ref[1:2, :], (nb, S5N))

    def scan_body(t, carry):
        hr, hi = carry
        row = pl.multiple_of(t * nb, nb)
        half = NBC // 2
        nr = ar * hr - ai * hi + jnp.concatenate([bu_s[c_, pl.ds(row, nb), :] for c_ in range(half)], axis=1)
        ni = ar * hi + ai * hr + jnp.concatenate([bu_s[half + c_, pl.ds(row, nb), :] for c_ in range(half)],
                                                 axis=1)
        for c_ in range(half):
            bu_s[c_, pl.ds(row, nb), :] = nr[:, c_ * BCW:(c_ + 1) * BCW]
            bu_s[half + c_, pl.ds(row, nb), :] = ni[:, c_ * BCW:(c_ + 1) * BCW]
        return nr, ni
    hr, hi = lax.fori_loop(0, TB, scan_body, (s5_ref[:, 0:S5N], s5_ref[:, S5N:2 * S5N]))
    s5_ref[:, 0:S5N] = hr
    s5_ref[:, S5N:2 * S5N] = hi
    ch_tb = _dot(bu_s[0], sC_ref[0:BCW, :])
    for c_ in range(1, NBC):
        ch_tb = ch_tb + _dot(bu_s[c_], sC_ref[c_ * BCW:(c_ + 1) * BCW, :])
    for c_ in range(W // LANE):
        u_s[c_] = ch_tb[:, c_ * LANE:(c_ + 1) * LANE]

    def phase2_parts(b):
        r0 = pl.multiple_of(b * TB, TB)

        def pj(c0, w):
            return proj_s[pl.ds(r0, TB), c0:c0 + w]

        def s5_out():
            chs = jnp.concatenate([u_s[c_, pl.ds(b, TB, stride=nb), :] for c_ in range(W // LANE)],
                                  axis=1)
            gy = _gelu_tanh(chs + sD_ref[...] * pj(C_SU, W))
            glu = _dot(gy, gluw_ref[...])
            yield
            os5 = gy * jax.nn.sigmoid(glu + glub_ref[...]) * _silu(pj(C_SG, W))
            mixed_s[pl.ds(r0, TB), 2 * W:3 * W] = os5.astype(BF16)

        cum = cum_s[b]
        rr = jnp.concatenate(
            [jnp.zeros((SUB, W), F32)]
            + [jnp.broadcast_to(cum_s[b, i * SUB - 1:i * SUB, :], (SUB, W))
               for i in range(1, NSUB)], axis=0)
        ee = jnp.concatenate(
            [jnp.broadcast_to(cum_s[b, i * SUB + SUB - 1:i * SUB + SUB, :], (SUB, W))
             for i in range(NSUB)], axis=0)
        lastrow = lax.broadcasted_iota(jnp.int32, (TB, W), 0) == TB - 1
        totc = _dot_tn_hi(jnp.where(lastrow, cum, 0.0), ones_tl)
        hq = hq_s[b]
        hk = hk_s[b]
        hv = pj(C_GI, W)
        gg = pj(C_GG, W)
        qt = hq * jnp.exp(cum - rr)
        kh_ = hk * jnp.exp(ee - cum)
        qe = hq * jnp.exp(cum)
        tot = cum[TB - 1:TB, :]
        kend = kh_ * jnp.exp(tot - ee)
        trow = lax.broadcasted_iota(jnp.int32, (TB, W), 0) // SUB
        qx = []
        for jb in range(NSUB - 1):
            eb = jnp.broadcast_to(ee[jb * SUB:jb * SUB + 1, :], (TB, W))
            qx.append(jnp.where(trow > jb, qt * jnp.exp(jnp.minimum(rr - eb, 0.0)), 0.0))

        def hg_pair(p):
            ls = slice(p * LANE, (p + 1) * LANE)
            hb = hhg_s[b, p]
            hv2 = hv[:, ls]
            kt = dup_t(kh_[:, ls])
            off = _dot(qx[0][:, ls], jnp.where(srcblock[0], kt, 0.0))
            for jb in range(1, NSUB - 1):
                off = off + _dot(qx[jb][:, ls], jnp.where(srcblock[jb], kt, 0.0))
            oi = _dot(qe[:, ls], hb)
            kv = _dot(dup_t(kend[:, ls])[:, 0:HD], hv2)
            yield
            o = _dot(jnp.where(subdiag2, dg_s[p, pl.ds(r0, TB), :], 0.0) + off, blockdiag2(hv2))
            yield
            hhg_s[b, p] = jnp.where(pairmask, jnp.exp(totc[ls, :]) * hb + kv, 0.0)
            o = o + oi
            ms = _dot(o * o, bones_ref[...])
            yield
            o = o * lax.rsqrt(ms + EPS) * hnw_ref[:, ls] * _silu(gg[:, ls])
            mixed_s[pl.ds(r0, TB), 1 * W + p * LANE:1 * W + (p + 1) * LANE] = o.astype(BF16)

        return [s5_out()] + [hg_pair(p) for p in range(W // LANE)], None

    _for_sequences(nb, phase2_parts, 4)

    @pl.when(j == pl.num_programs(0) - 1)
    def _emit_states():
        for b in range(nb):
            for h in range(NH):
                p, h2 = divmod(h, 2)
                blk = (slice(h2 * HD, (h2 + 1) * HD),) * 2
                ret_ref[b, h] = hret_s[(b, p) + blk]
                hg_ref[b, h] = hhg_s[(b, p) + blk]
                m2_ref[b, h] = hm2_s[(b, p) + blk]

    res = jnp.dot(mixed_s[...], wout_ref[...], preferred_element_type=F32)
    for b in range(nb):
        xo = x_ref[b] + res[b * TB:(b + 1) * TB]
        if last:
            xo = xo * lax.rsqrt(jnp.mean(xo * xo, axis=-1, keepdims=True) + EPS) * fnw_ref[...]
        act_ref[b] = xo


def _const_spec(shape):
    nd = len(shape)
    return pl.BlockSpec(shape, lambda j: (0,) * nd)


def _prompt_layer(x, l, last, w):
    nb, L, _ = x.shape
    nblk = L // TB
    rows = nb * TB
    xspec = pl.BlockSpec((nb, TB, D), lambda j: (0, j, 0))
    tspec = pl.BlockSpec((TB, LANE), lambda j: (j, 0))

    def per_layer(a):
        nd = a.ndim
        return pl.BlockSpec((None,) + a.shape[1:], lambda j: (l,) + (0,) * (nd - 1))

    stacked = [w[k] for k in ("norm_w", "w_in", "w_out")]
    consts = [w[k] for k in ("rdec", "recum", "rkdec", "retot")]
    ret_nw = [w["ret_norm_w"], w["lb"], w["hgrn_norm_w"]]
    s5 = [w[k] for k in ("s5_A", "s5_Bblk_cols", "s5_Cblk", "s5_D", "s5_glu_w", "s5_glu_b")]
    m2 = [w[k] for k in ("m2_conv_w", "m2_conv_b", "m2_dt_bias", "m2_A_log", "m2_D", "m2_norm_w")]
    pair_consts = [w["bones"], w["hexp"]]
    args = [x, w["cos_p"], w["sin_p"]] + stacked + consts + ret_nw + [w["sel"]] + s5 + m2 + pair_consts
    in_specs = ([xspec, tspec, tspec] + [per_layer(a) for a in stacked] + [_const_spec(a.shape) for a in consts]
                + [per_layer(a) for a in ret_nw] + [_const_spec(w["sel"].shape)]
                + [per_layer(a) for a in s5 + m2] + [_const_spec(a.shape) for a in pair_consts])
    if last:
        args.append(w["final_norm_w"])
        in_specs.append(_const_spec(w["final_norm_w"].shape))
    state_shapes = [jax.ShapeDtypeStruct((nb, NH, HD, HD), F32),
                    jax.ShapeDtypeStruct((nb, NH, HD, HD), F32),
                    jax.ShapeDtypeStruct((nb, 2 * S5N), F32),
                    jax.ShapeDtypeStruct((nb, NH, HD, HD), F32),
                    jax.ShapeDtypeStruct((nb, CONV_K - 1, CONV_CH), F32)]
    out_shape = [jax.ShapeDtypeStruct((nb, L, D), F32)] + state_shapes
    out_specs = [xspec] + [_const_spec(s.shape) for s in state_shapes]
    scratch = [pltpu.VMEM((rows, D), BF16),
               pltpu.VMEM((rows, P_PAD), F32),
               pltpu.VMEM((rows, D), BF16),
               pltpu.VMEM((NBC, rows, BCW), F32),
               pltpu.VMEM((W // LANE, rows, LANE), F32),
               pltpu.VMEM((rows, W), BF16),
               pltpu.VMEM((nb, TB + 8, CONV_CH), F32),
               pltpu.VMEM((W // LANE, rows, SUB * LANE), BF16),
               pltpu.VMEM((W // LANE, rows, LANE), F32),
               pltpu.VMEM((nb, TB, W), F32),
               pltpu.VMEM((nb, TB, W), F32),
               pltpu.VMEM((nb, TB, W), F32)] + [pltpu.VMEM((nb, W // LANE, LANE, LANE), F32)] * 3
    return pl.pallas_call(
        functools.partial(_prompt_layer_body, last),
        grid=(nblk,),
        in_specs=in_specs,
        out_specs=out_specs,
        out_shape=out_shape,
        scratch_shapes=scratch,
        compiler_params=pltpu.CompilerParams(dimension_semantics=("arbitrary",),
                                             vmem_limit_bytes=VMEM_LIMIT),
        name="prompt_layer",
    )(*args)


def _rope_tables(pos):
    half = HD // 2
    inv = 1.0 / (ROPE_BASE ** (np.arange(half, dtype=np.float64) / half))
    ang = pos.astype(np.float64)[:, None] * inv[None, :]
    cos, sin = np.cos(ang), np.sin(ang)
    cos_t = np.tile(cos, (1, LANE // half))
    sin_t = np.tile(np.concatenate([-sin, sin], axis=1), (1, LANE // HD))
    return cos_t.astype(np.float32), sin_t.astype(np.float32)


def _retention_tables():
    log_gamma = np.log1p(-(2.0 ** (-5.0 - np.arange(NH, dtype=np.float64))))
    cum = np.cumsum(np.broadcast_to(log_gamma, (TB, NH)), axis=0)
    total = cum[-1]
    causal = np.tril(np.ones((TB, TB), dtype=bool))
    diff = cum[:, None, :] - cum[None, :, :]
    dec = np.where(causal[:, :, None], np.exp(np.where(causal[:, :, None], diff, 0.0)), 0.0)
    rdec = np.moveaxis(dec, 2, 0)
    recum = np.broadcast_to(np.exp(cum).T[:, :, None], (NH, TB, HD))
    rkdec = np.broadcast_to(np.exp(total[None, :] - cum).T[:, :, None], (NH, TB, HD))
    retot = np.broadcast_to(np.exp(total)[:, None, None], (NH, 1, HD))
    pair = lambda t: np.concatenate([t[0::2], t[1::2]], axis=-1).astype(np.float32)
    return pair(rdec), pair(recum), pair(rkdec), pair(retot), log_gamma


def _sel_matrix():
    sel = np.zeros((SUB, 2, HD, 2, TB), np.float32)
    for s_ in range(SUB):
        for h2 in range(2):
            sel[s_, h2, :, h2, s_::SUB] = 1.0
    return jnp.asarray(sel.reshape(SUB * LANE, LANE), dtype=BF16)


def _rows(v, width=None):
    v = v.astype(F32)
    if width is not None and v.shape[-1] < width:
        v = jnp.pad(v, ((0, 0), (0, width - v.shape[-1])))
    return v[:, None, :]


def _block_diag(blocks):
    g, r, c = blocks.shape
    eye = jnp.eye(g, dtype=blocks.dtype)
    return jnp.einsum('grc,gh->grhc', blocks, eye).reshape(g * r, g * c)


def _s5_discretise(A_re, A_im, log_dt, B_re, B_im):
    A_re, A_im = A_re.astype(F32), A_im.astype(F32)
    dt = jnp.exp(log_dt.astype(F32))[:, None]
    mag = jnp.exp(A_re * dt)
    ab_re, ab_im = mag * jnp.cos(A_im * dt), mag * jnp.sin(A_im * dt)
    nr, ni = ab_re - 1.0, ab_im
    den = A_re * A_re + A_im * A_im
    f_re = (nr * A_re + ni * A_im) / den
    f_im = (ni * A_re - nr * A_im) / den
    B_re, B_im = B_re.astype(F32), B_im.astype(F32)
    bb_re = f_re[..., None] * B_re - f_im[..., None] * B_im
    bb_im = f_re[..., None] * B_im + f_im[..., None] * B_re
    return ab_re, ab_im, bb_re, bb_im


def _s5_matrices(A_re, A_im, log_dt, B_re, B_im, C_re, C_im):
    ab_re, ab_im, bb_re, bb_im = _s5_discretise(A_re, A_im, log_dt, B_re, B_im)
    bblk = jnp.concatenate([_block_diag(jnp.swapaxes(bb_re, 1, 2)),
                            _block_diag(jnp.swapaxes(bb_im, 1, 2))], axis=1)
    cblk = jnp.concatenate([_block_diag(jnp.swapaxes(C_re.astype(F32), 1, 2)),
                            _block_diag(jnp.swapaxes(-C_im.astype(F32), 1, 2))], axis=0)
    return jnp.stack([ab_re.reshape(-1), ab_im.reshape(-1)], axis=0), bblk, cblk


R_COS, R_SIN, R_RETNW, R_LB, R_HNW, R_S5D, R_GLUB, R_MNW = [i * W for i in range(8)]
R_AR = 8 * W
R_AI = R_AR + S5N
R_DTB = R_AI + S5N
R_ALOG = R_DTB + 8
R_MD = R_ALOG + 8
N_COLP = R_MD + 8


def _prepare(p, lb_all, prompt_len):
    depth = p["norm_w"].shape[0]
    s5_A, bblk, cblk = jax.vmap(_s5_matrices)(p["s5_A_re"], p["s5_A_im"], p["s5_log_dt"], p["s5_B_re"],
                                              p["s5_B_im"], p["s5_C_re"], p["s5_C_im"])
    cos_p, sin_p = _rope_tables(np.arange(prompt_len, dtype=np.float32))
    rdec, recum, rkdec, retot, log_gamma = _retention_tables()
    cos_s, sin_s = _rope_tables(np.float32(PAST_LEN) + np.arange(1, dtype=np.float32))
    w = dict(
        norm_w=_rows(p["norm_w"]),
        w_in=jnp.pad(p["w_in"].astype(BF16), ((0, 0), (0, 0), (0, P_PAD - P_TOTAL))),
        w_out=p["w_out"].astype(BF16),
        ret_norm_w=_rows(p["ret_norm_w"]),
        lb=_rows(lb_all),
        hgrn_norm_w=_rows(p["hgrn_norm_w"]),
        s5_A=s5_A,
        s5_Bblk=bblk.astype(BF16),
        s5_Cblk=cblk.astype(BF16),
        s5_D=_rows(p["s5_D"]),
        s5_glu_w=p["s5_glu_w"].astype(BF16),
        s5_glu_b=_rows(p["s5_glu_b"]),
        m2_conv_w=p["m2_conv_w"].astype(F32),
        m2_conv_b=_rows(p["m2_conv_b"]),
        m2_dt_bias=_rows(jnp.repeat(p["m2_dt_bias"], HD, axis=1)),
        m2_A_log=_rows(jnp.repeat(p["m2_A_log"], HD, axis=1)),
        m2_D=_rows(jnp.repeat(p["m2_D"], HD, axis=1)),
        m2_norm_w=_rows(p["m2_norm_w"]),
        final_norm_w=p["final_norm_w"].astype(F32).reshape(1, D),
    )
    tile2 = lambda t: jnp.asarray(np.broadcast_to(np.tile(t[0], W // LANE), (depth, W)))
    colp = jnp.concatenate(
        [tile2(cos_s), tile2(sin_s), w["ret_norm_w"][:, 0], w["lb"][:, 0], w["hgrn_norm_w"][:, 0],
         w["s5_D"][:, 0], w["s5_glu_b"][:, 0], w["m2_norm_w"][:, 0], s5_A[:, 0], s5_A[:, 1],
         _rows(p["m2_dt_bias"], 8)[:, 0], _rows(p["m2_A_log"], 8)[:, 0], _rows(p["m2_D"], 8)[:, 0]], axis=1)
    w.update(
        colp=jnp.broadcast_to(colp[:, :, None], (depth, N_COLP, LANE)),
        s5_Bblk_cols=jnp.moveaxis(w["s5_Bblk"].reshape(depth, W, NBC, BCW), 2, 1),
        s5_BblkT=jnp.swapaxes(w["s5_Bblk"], 1, 2),
        s5_CblkT=jnp.swapaxes(w["s5_Cblk"], 1, 2),
        s5_glu_wT=jnp.swapaxes(w["s5_glu_w"], 1, 2),
        rgam=jnp.asarray(np.broadcast_to(np.exp(log_gamma)[:, None, None], (NH, 8, LANE)), dtype=F32),
        cos_p=jnp.asarray(cos_p), sin_p=jnp.asarray(sin_p), rdec=jnp.asarray(rdec), recum=jnp.asarray(recum),
        rkdec=jnp.asarray(rkdec), retot=jnp.asarray(retot), sel=_sel_matrix(),
        bones=jnp.asarray(np.kron(np.eye(LANE // HD), np.full((HD, HD), 1.0 / HD)), dtype=BF16),
        hexp=jnp.asarray(np.kron(np.eye(LANE, NH), np.ones((1, HD))), dtype=F32),
    )
    return w


KC = 16
NKC = HD // KC
STEPS = NH * NKC


def _rotary_cols(x, cos, sin_signed):
    half = HD // 2
    parts = []
    for h in range(NH):
        parts += [x[h * HD + half:(h + 1) * HD], x[h * HD:h * HD + half]]
    return x * cos + jnp.concatenate(parts, axis=0) * sin_signed


def _expand_rows(dst, x):
    for c in range(x.shape[0]):
        dst[c] = jnp.broadcast_to(x[c:c + 1, :], (8, LANE))


def _sample_body(n_steps, x_ref, normw_ref, win_ref, wout_ref, fnw_ref, colp_ref, sbt_ref, sct_ref,
                 gluwt_ref, cw_ref, cb_ref, rgam_ref,
                 ret_in, hg_in, m2_in, s5re_in, s5im_in, buf_in,
                 y_ref, ret_out, hg_out, m2_out, s5re_out, s5im_out, buf_out,
                 xs_s, pt_s, vt_s, ot_s, mixt_s, o_s, hp_s,
                 kret_s, qret_s, khg_s, qhg_s, ahg_s, km2_s, qm2_s):
    i = pl.program_id(0)
    r = i % STEPS
    h = r // NKC
    kc = r % NKC

    def cp(r0, n=W):
        return colp_ref[r0:r0 + n, :]

    def pc(c0, w):
        return pt_s[c0:c0 + w, :]

    @pl.when(i == 0)
    def _load():
        xs_s[...] = x_ref[...]

    @pl.when(r == 0)
    def _prep():
        x = xs_s[...]
        hn = x * lax.rsqrt(jnp.mean(x * x, axis=-1, keepdims=True) + EPS) * normw_ref[...]
        proj = jnp.dot(hn.astype(BF16), win_ref[...], preferred_element_type=F32)

        xnew = proj[:, C_XBC:C_XBC + CONV_CH]
        acc = cb_ref[...] + xnew * cw_ref[CONV_K - 1:CONV_K, :]
        for t in range(CONV_K - 1):
            acc = acc + buf_in[t] * cw_ref[t:t + 1, :]
        for t in range(CONV_K - 2):
            buf_out[t] = buf_in[t + 1]
        buf_out[CONV_K - 2] = xnew
        xbc = _silu(acc)

        for t in range(P_PAD // LANE):
            c0 = t * LANE
            if C_XBC <= c0 < C_XBC + CONV_CH:
                tile = xbc[:, c0 - C_XBC:c0 - C_XBC + LANE]
            else:
                tile = proj[:, c0:c0 + LANE]
            pt_s[c0:c0 + LANE, :] = tile.T

        cos, sin = cp(R_COS), cp(R_SIN)
        _expand_rows(kret_s, _rotary_cols(pc(C_RK, W), cos, sin) * (HD ** -0.5))
        _expand_rows(qret_s, _rotary_cols(pc(C_RQ, W), cos, sin))
        vt_s[0] = pc(C_RV, W)

        fr = pc(C_GF, W)
        lb = cp(R_LB)
        logf = _log_sigmoid(fr) + jnp.log(1.0 + lb * jnp.exp(jnp.minimum(-fr, EXP_CLIP)))
        _expand_rows(ahg_s, jnp.exp(logf))
        _expand_rows(khg_s, (1.0 - lb) * jax.nn.sigmoid(-fr))
        _expand_rows(qhg_s, _silu(pc(C_GQ, W)))
        vt_s[1] = pc(C_GI, W)

        dt8 = _softplus(pc(C_DT, 8) + cp(R_DTB, 8))
        adec8 = jnp.exp(dt8 * (-jnp.exp(cp(R_ALOG, 8))))
        for hh in range(NH):
            hp_s[hh] = jnp.broadcast_to(adec8[hh:hh + 1, :], (8, LANE))
            vt_s[2, hh * HD:(hh + 1) * HD, :] = pc(C_XBC + hh * HD, HD) * dt8[hh:hh + 1, :]
        _expand_rows(km2_s, pc(C_XBC + W, 2 * HD))
        _expand_rows(qm2_s, pc(C_XBC + W + 2 * HD, 2 * HD))

        u = pc(C_SU, W)
        bu = jnp.dot(sbt_ref[...], u.astype(BF16), preferred_element_type=F32)
        hr, hi = s5re_in[...], s5im_in[...]
        ar, ai = cp(R_AR, S5N), cp(R_AI, S5N)
        nr = ar * hr - ai * hi + bu[0:S5N]
        ni = ar * hi + ai * hr + bu[S5N:2 * S5N]
        s5re_out[...] = nr
        s5im_out[...] = ni
        hcat = jnp.concatenate([nr, ni], axis=0).astype(BF16)
        sy = jnp.dot(sct_ref[...], hcat, preferred_element_type=F32) + cp(R_S5D) * u
        gy = _gelu_tanh(sy)
        glu = jnp.dot(gluwt_ref[...], gy.astype(BF16), preferred_element_type=F32) + cp(R_GLUB)
        mixt_s[2 * W:3 * W, :] = gy * jax.nn.sigmoid(glu) * _silu(pc(C_SG, W))

    @pl.when(kc == 0)
    def _zero():
        o_s[...] = jnp.zeros(o_s.shape, F32)

    hrow = pl.multiple_of(h * HD, HD)
    cbase = h * HD + kc * KC
    gbase = (h // 2) * HD + kc * KC

    def update(m, st_in, st_out, kx, qx, base, decay):
        v3 = vt_s[m, pl.ds(hrow, HD), :].reshape(HD // 8, 8, LANE)

        def body(kk, o):
            s_new = decay(kk) * st_in[kk].reshape(HD // 8, 8, LANE) + kx[base + kk] * v3
            st_out[kk] = s_new.reshape(HD, LANE)
            return o + qx[base + kk] * s_new
        o_s[m] = lax.fori_loop(0, KC, body, o_s[m], unroll=2)

    gam = rgam_ref[h]
    update(0, ret_in, ret_out, kret_s, qret_s, cbase, lambda kk: gam)
    update(1, hg_in, hg_out, khg_s, qhg_s, cbase, lambda kk: ahg_s[cbase + kk])
    adec = hp_s[h]
    update(2, m2_in, m2_out, km2_s, qm2_s, gbase, lambda kk: adec)

    @pl.when(kc == NKC - 1)
    def _head_done():
        for m in range(3):
            ot_s[m, pl.ds(hrow, HD), :] = o_s[m].reshape(HD, LANE)

    @pl.when(r == STEPS - 1)
    def _finish():
        def head_rms_cols(o):
            parts = []
            for hh in range(NH):
                seg = o[hh * HD:(hh + 1) * HD]
                parts.append(seg * lax.rsqrt(jnp.mean(seg * seg, axis=0, keepdims=True) + EPS))
            return jnp.concatenate(parts, axis=0)

        mixt_s[0:W, :] = head_rms_cols(ot_s[0]) * cp(R_RETNW) * _silu(pc(C_RG, W))
        mixt_s[W:2 * W, :] = head_rms_cols(ot_s[1]) * cp(R_HNW) * _silu(pc(C_GG, W))
        md8 = cp(R_MD, 8)
        ym = jnp.concatenate([ot_s[2, hh * HD:(hh + 1) * HD, :] + md8[hh:hh + 1, :] * pc(C_XBC + hh * HD, HD)
                              for hh in range(NH)], axis=0)
        my = ym * _silu(pc(C_MZ, W))
        mixt_s[3 * W:4 * W, :] = my * lax.rsqrt(jnp.mean(my * my, axis=0, keepdims=True) + EPS) * cp(R_MNW)
        mixed = jnp.concatenate([mixt_s[t * LANE:(t + 1) * LANE, :].T for t in range(D // LANE)], axis=1)
        xo = xs_s[...] + jnp.dot(mixed.astype(BF16), wout_ref[...], preferred_element_type=F32)
        xs_s[...] = xo

        @pl.when(i == n_steps - 1)
        def _final_norm():
            y_ref[...] = xo * lax.rsqrt(jnp.mean(xo * xo, axis=-1, keepdims=True) + EPS) * fnw_ref[...]


def _sample_step(x, w, ret, hg, m2, s5re, s5im, buf):
    depth = ret.shape[0]
    n = x.shape[0]
    n_steps = depth * STEPS
    lay = lambda i: i // STEPS

    def per_layer(a):
        nd = a.ndim
        return pl.BlockSpec((None,) + a.shape[1:], lambda i: (lay(i),) + (0,) * (nd - 1))

    st_spec = pl.BlockSpec((None, None, KC, HD, LANE),
                           lambda i: (lay(i), (i % STEPS) // NKC, i % NKC, 0, 0))
    weights = [w["norm_w"], w["w_in"], w["w_out"]]
    tables = [w["colp"], w["s5_BblkT"], w["s5_CblkT"], w["s5_glu_wT"], w["m2_conv_w"], w["m2_conv_b"]]
    in_specs = ([_const_spec(x.shape)] + [per_layer(a) for a in weights] + [_const_spec(w["final_norm_w"].shape)]
                + [per_layer(a) for a in tables] + [_const_spec(w["rgam"].shape)]
                + [st_spec, st_spec, st_spec, per_layer(s5re), per_layer(s5im), per_layer(buf)])
    out_shape = [jax.ShapeDtypeStruct((n, D), F32)] + [jax.ShapeDtypeStruct(a.shape, F32)
                                                       for a in (ret, hg, m2, s5re, s5im, buf)]
    out_specs = [_const_spec((n, D)), st_spec, st_spec, st_spec, per_layer(s5re), per_layer(s5im),
                 per_layer(buf)]
    expand = lambda c: pltpu.VMEM((c, 8, LANE), F32)
    scratch = [pltpu.VMEM((n, D), F32),
               pltpu.VMEM((P_PAD, LANE), F32),
               pltpu.VMEM((3, W, LANE), F32),
               pltpu.VMEM((3, W, LANE), F32),
               pltpu.VMEM((D, LANE), F32),
               pltpu.VMEM((3, HD // 8, 8, LANE), F32),
               pltpu.VMEM((NH, 8, LANE), F32),
               expand(W), expand(W), expand(W), expand(W), expand(W), expand(2 * HD), expand(2 * HD)]
    return pl.pallas_call(
        functools.partial(_sample_body, n_steps),
        grid=(n_steps,),
        in_specs=in_specs,
        out_specs=out_specs,
        out_shape=out_shape,
        scratch_shapes=scratch,
        compiler_params=pltpu.CompilerParams(dimension_semantics=("arbitrary",),
                                             vmem_limit_bytes=VMEM_LIMIT),
        name="sample_step",
    )(x, *weights, w["final_norm_w"], *tables, w["rgam"], ret, hg, m2, s5re, s5im, buf)


def kernel(x_prompt, x_sample, state_ret, state_hgrn, state_s5_re, state_s5_im, state_m2_ssm,
           state_m2_conv, norm_w, w_in, ret_norm_w, hgrn_lb_logits, hgrn_norm_w, s5_A_re, s5_A_im,
           s5_log_dt, s5_B_re, s5_B_im, s5_C_re, s5_C_im, s5_D, s5_glu_w, s5_glu_b, m2_conv_w,
           m2_conv_b, m2_dt_bias, m2_A_log, m2_D, m2_norm_w, w_out, final_norm_w):
    p = dict(norm_w=norm_w, w_in=w_in, ret_norm_w=ret_norm_w, hgrn_norm_w=hgrn_norm_w,
             s5_A_re=s5_A_re, s5_A_im=s5_A_im, s5_log_dt=s5_log_dt, s5_B_re=s5_B_re, s5_B_im=s5_B_im,
             s5_C_re=s5_C_re, s5_C_im=s5_C_im, s5_D=s5_D, s5_glu_w=s5_glu_w, s5_glu_b=s5_glu_b,
             m2_conv_w=m2_conv_w, m2_conv_b=m2_conv_b, m2_dt_bias=m2_dt_bias, m2_A_log=m2_A_log,
             m2_D=m2_D, m2_norm_w=m2_norm_w, w_out=w_out, final_norm_w=final_norm_w)
    depth = norm_w.shape[0]
    nbp, lp, _ = x_prompt.shape
    nbs = x_sample.shape[0]

    lb_sm = jax.nn.softmax(hgrn_lb_logits.astype(F32), axis=0)
    lb_all = jnp.clip(jnp.cumsum(lb_sm, axis=0) - lb_sm[0], 0.0, 1.0)

    w = _prepare(p, lb_all, lp)

    xp = x_prompt
    pst = []
    for l in range(depth):
        outs = _prompt_layer(xp, l, l == depth - 1, w)
        xp = outs[0]
        ret, hg, s5, m2, buf = outs[-5:]
        pst.append((ret, hg, s5[:, :S5N].reshape(nbp, S5G, S5P), s5[:, S5N:].reshape(nbp, S5G, S5P),
                    m2, buf))
    yp = xp

    seq_last = lambda a: jnp.moveaxis(a.astype(F32), 1, -1)
    ys, ret, hg, m2, s5re, s5im, buf = _sample_step(
        x_sample.reshape(nbs, D), w,
        seq_last(state_ret), seq_last(state_hgrn), seq_last(state_m2_ssm),
        seq_last(state_s5_re).reshape(depth, S5N, nbs), seq_last(state_s5_im).reshape(depth, S5N, nbs),
        jnp.swapaxes(state_m2_conv.astype(F32), 1, 2))
    seq_second = lambda a: jnp.moveaxis(a, -1, 1)

    stk = lambda i: jnp.stack([s[i] for s in pst], axis=0)
    return (yp, ys.reshape(nbs, 1, D),
            stk(0), stk(1), stk(2), stk(3), stk(4), stk(5),
            seq_second(ret), seq_second(hg), seq_second(s5re.reshape(depth, S5G, S5P, nbs)),
            seq_second(s5im.reshape(depth, S5G, S5P, nbs)), seq_second(m2), jnp.swapaxes(buf, 1, 2))
```

```python
import functools
import math

import numpy as np
import jax
import jax.numpy as jnp
from jax import lax
from jax.experimental import pallas as pl
from jax.experimental.pallas import tpu as pltpu

F32 = jnp.float32
BF16 = jnp.bfloat16
HI = lax.Precision.HIGHEST

D = 1024
W = 256
NH = 4
HD = 64
S5G = 16
S5C = 16
S5P = 64
S5N = S5G * S5P
NBC = 8
BCW = 2 * S5N // NBC
CONV_CH = 512
CONV_K = 4
TB = 64
SUB = 16
NSUB = TB // SUB
EPS = 1e-6
EXP_CLIP = 60.0
ROPE_BASE = 10000.0
PAST_LEN = 16384

C_RQ, C_RK, C_RV, C_RG = 0, 256, 512, 768
C_GQ, C_GF, C_GI, C_GG = 1024, 1280, 1536, 1792
C_SU, C_SG = 2048, 2304
C_MZ, C_XBC, C_DT = 2560, 2816, 3328
P_TOTAL = 3332
PCH = 1152
NPC = 3
P_PAD = NPC * PCH
LANE = 128
VMEM_LIMIT = 56 * 1024 * 1024


def _silu(x):
    return x * jax.nn.sigmoid(x)


def _softplus(x):
    return jnp.maximum(x, 0.0) + jnp.log(1.0 + jnp.exp(-jnp.abs(x)))


def _log_sigmoid(x):
    return jnp.minimum(x, 0.0) - jnp.log(1.0 + jnp.exp(-jnp.abs(x)))


def _round_robin(gens):
    gens = list(gens)
    while gens:
        alive = []
        for g in gens:
            try:
                next(g)
                alive.append(g)
            except StopIteration:
                pass
        gens = alive


def _for_sequences(nb, parts, group):
    def body(i, c):
        built = [parts(i * group + k) for k in range(group)]
        _round_robin([g for gens, _ in built for g in gens])
        for _, finish in built:
            if finish is not None:
                finish()
        return c
    lax.fori_loop(0, nb // group, body, 0)


def _gelu_tanh(x):
    c = math.sqrt(2.0 / math.pi)
    return 0.5 * x * (1.0 + jnp.tanh(c * (x + 0.044715 * (x * x * x))))


def _dot(a, b):
    return jnp.dot(a.astype(BF16), b.astype(BF16), preferred_element_type=F32)


def _dot_hi(a, b):
    return jnp.dot(a, b, precision=HI, preferred_element_type=F32)


def _rot_half_partner(x):
    lane = lax.broadcasted_iota(jnp.int32, x.shape, 1)
    first = (lane % HD) < (HD // 2)
    return jnp.where(first, pltpu.roll(x, LANE - HD // 2, 1), pltpu.roll(x, HD // 2, 1))


def _rotary(x, cos, sin_signed):
    parts = []
    for i in range(W // LANE):
        xi = x[:, i * LANE:(i + 1) * LANE]
        parts.append(xi * cos + _rot_half_partner(xi) * sin_signed)
    return jnp.concatenate(parts, axis=1)


def _prompt_layer_body(last, *refs):
    (x_ref, cos_ref, sin_ref, normw_ref, win_ref, wout_ref,
     rdec_ref, recum_ref, rkdec_ref, retot_ref, retnw_ref,
     lb_ref, hnw_ref, sel_ref,
     sA_ref, sB_ref, sC_ref, sD_ref, gluw_ref, glub_ref,
     cw_ref, cb_ref, dtb_ref, alog_ref, md_ref, mnw_ref, bones_ref, hexp_ref) = refs[:28]
    refs = refs[28:]
    if last:
        fnw_ref = refs[0]
        refs = refs[1:]
    act_ref = refs[0]
    refs = refs[1:]
    (ret_ref, hg_ref, s5_ref, m2_ref, m2buf_ref,
     hn_s, proj_s, mixed_s, bu_s, u_s, ub_s, cv_s, p_s, dg_s, hq_s, hk_s, cum_s, hret_s, hhg_s, hm2_s) = refs
    j = pl.program_id(0)
    nb = x_ref.shape[0]

    @pl.when(j == 0)
    def _init():
        hret_s[...] = jnp.zeros(hret_s.shape, F32)
        hhg_s[...] = jnp.zeros(hhg_s.shape, F32)
        s5_ref[...] = jnp.zeros(s5_ref.shape, F32)
        hm2_s[...] = jnp.zeros(hm2_s.shape, F32)
        cv_s[...] = jnp.zeros(cv_s.shape, F32)

    ti = lax.broadcasted_iota(jnp.int32, (TB, TB), 0)
    si = lax.broadcasted_iota(jnp.int32, (TB, TB), 1)
    causal = si <= ti
    tri_l = causal.astype(F32)
    pr = lax.broadcasted_iota(jnp.int32, (LANE, LANE), 0)
    pc_ = lax.broadcasted_iota(jnp.int32, (LANE, LANE), 1)
    pairmask = (pr // HD) == (pc_ // HD)
    t2 = lax.broadcasted_iota(jnp.int32, (TB, LANE), 0)
    l2 = lax.broadcasted_iota(jnp.int32, (TB, LANE), 1)
    causal2 = (l2 % HD) <= t2
    first_head = l2 < HD
    subdiag2 = (t2 // SUB) == ((l2 % HD) // SUB)
    srcblock = [pairmask & (((pc_ % HD) // SUB) == jb) for jb in range(NSUB - 1)]

    def dup_t(x2):
        return jnp.concatenate([x2, x2], axis=0).T

    def blockdiag2(x2):
        return jnp.where(pairmask, jnp.concatenate([x2, x2], axis=0), 0.0)

    def norm_body(b, c):
        xb = x_ref[b]
        hn = xb * lax.rsqrt(jnp.mean(xb * xb, axis=-1, keepdims=True) + EPS) * normw_ref[...]
        hn_s[pl.ds(pl.multiple_of(b * TB, TB), TB), :] = hn.astype(BF16)
        return c
    lax.fori_loop(0, nb, norm_body, 0)
    for c in range(NPC):
        cs = slice(c * PCH, (c + 1) * PCH)
        proj_s[:, cs] = jnp.dot(hn_s[...], win_ref[:, cs], preferred_element_type=F32)

    cos = cos_ref[...]
    sin = sin_ref[...]

    def reorder_u(b, c):
        u = proj_s[pl.ds(pl.multiple_of(b * TB, TB), TB), C_SU:C_SU + W]
        for c_ in range(W // LANE):
            u_s[c_, pl.ds(b, TB, stride=nb), :] = u[:, c_ * LANE:(c_ + 1) * LANE]
        return c
    lax.fori_loop(0, nb, reorder_u, 0)
    ub_s[...] = jnp.concatenate([u_s[c_] for c_ in range(W // LANE)], axis=1).astype(BF16)

    def s5_input_piece(cb):
        bu_s[cb] = jnp.dot(ub_s[...], sB_ref[cb], preferred_element_type=F32)
        yield

    def phase1_parts(b):
        r0 = pl.multiple_of(b * TB, TB)

        def pj(c0, w):
            return proj_s[pl.ds(r0, TB), c0:c0 + w]

        rq = _rotary(pj(C_RQ, W), cos, sin)
        rk = _rotary(pj(C_RK, W), cos, sin) * (HD ** -0.5)
        rv = pj(C_RV, W)
        rg = pj(C_RG, W)

        def ret_pair(p):
            ls = slice(p * LANE, (p + 1) * LANE)
            q2, k2, v2 = rq[:, ls], rk[:, ls], rv[:, ls]
            hb = hret_s[b, p]
            kt = dup_t(k2)
            s_raw = _dot(q2, jnp.where(pairmask, kt, 0.0))
            oi = _dot(q2, hb)
            kv = _dot(kt[:, 0:HD], v2 * rkdec_ref[p])
            yield
            o = _dot(s_raw * rdec_ref[p], blockdiag2(v2))
            yield
            o = o + oi * recum_ref[p]
            hret_s[b, p] = jnp.where(pairmask, retot_ref[p] * hb + kv, 0.0)
            ms = _dot(o * o, bones_ref[...])
            yield
            o = o * lax.rsqrt(ms + EPS) * retnw_ref[:, ls] * _silu(rg[:, ls])
            mixed_s[pl.ds(r0, TB), 0 * W + p * LANE:0 * W + (p + 1) * LANE] = o.astype(BF16)

        cv_s[b, 8:8 + TB, :] = pj(C_XBC, CONV_CH)
        acc = cb_ref[...] + cv_s[b, 5:5 + TB, :] * cw_ref[0:1, :]
        for i in range(1, CONV_K):
            acc = acc + cv_s[b, 5 + i:5 + i + TB, :] * cw_ref[i:i + 1, :]
        tail = cv_s[b, TB + 5:TB + 8, :]
        cv_s[b, 5:8, :] = tail
        m2buf_ref[b] = tail
        xbc = _silu(acc)
        xm = xbc[:, 0:W]
        bm = xbc[:, W:W + 2 * HD]
        cm = xbc[:, W + 2 * HD:W + 4 * HD]
        bm_sw = pltpu.roll(bm, HD, 1)
        cm_sw = pltpu.roll(cm, HD, 1)
        dt_b = _softplus(_dot_hi(pj(C_DT, LANE), hexp_ref[...]) + dtb_ref[...])
        la_b = dt_b * (-jnp.exp(alog_ref[...]))
        cum_b = _dot_hi(tri_l, la_b)
        ys = [None] * (W // LANE)

        def m2_pair(p):
            ls = slice(p * LANE, (p + 1) * LANE)
            b2 = jnp.where(first_head, bm, bm_sw) if p == 0 else jnp.where(first_head, bm_sw, bm)
            c2 = jnp.where(first_head, cm, cm_sw) if p == 0 else jnp.where(first_head, cm_sw, cm)
            x2, dt2, cum2 = xm[:, ls], dt_b[:, ls], cum_b[:, ls]
            xdt2 = x2 * dt2
            ct = dup_t(cum2)
            r2 = jnp.where(first_head, ct[0:TB], ct[HD:HD + TB])
            tot2 = cum2[TB - 1:TB, :]
            hb = hm2_s[b, p]
            bt = dup_t(b2)
            s_raw = _dot(c2, jnp.where(pairmask, bt, 0.0))
            oi = _dot(c2, hb)
            kv = _dot(bt[:, 0:HD], xdt2 * jnp.exp(tot2 - cum2))
            decay = jnp.where(causal2, jnp.exp(jnp.minimum(cum2 - r2, 0.0)), 0.0)
            yield
            o = _dot(s_raw * decay, blockdiag2(xdt2))
            yield
            hm2_s[b, p] = jnp.where(pairmask, jnp.exp(tot2) * hb + kv, 0.0)
            ys[p] = o + oi * jnp.exp(cum2) + md_ref[:, ls] * x2

        fr = pj(C_GF, W)
        lb = lb_ref[...]
        logf = _log_sigmoid(fr) + jnp.log(1.0 + lb * jnp.exp(jnp.minimum(-fr, EXP_CLIP)))
        hq = _silu(pj(C_GQ, W))
        hk = (1.0 - lb) * jax.nn.sigmoid(-fr)
        cum = _dot_hi(tri_l, logf)
        hq_s[b] = hq
        hk_s[b] = hk
        cum_s[b] = cum
        t8 = lax.broadcasted_iota(jnp.int32, (8, LANE), 0)

        def diag_products(p):
            ls = slice(p * LANE, (p + 1) * LANE)
            for s_ in range(SUB):
                pieces = []
                for i in range(NSUB):
                    kb = jnp.broadcast_to(hk_s[b, i * SUB + s_:i * SUB + s_ + 1, ls], (8, LANE))
                    cb = jnp.broadcast_to(cum_s[b, i * SUB + s_:i * SUB + s_ + 1, ls], (8, LANE))
                    for half in range(SUB // 8):
                        rows = slice(i * SUB + half * 8, i * SUB + half * 8 + 8)
                        if half * 8 + 7 < s_:
                            pieces.append(jnp.zeros((8, LANE), F32))
                        elif half * 8 >= s_:
                            pieces.append(hq[rows, ls] * kb * jnp.exp(cum[rows, ls] - cb))
                        else:
                            e = jnp.exp(jnp.minimum(cum[rows, ls] - cb, 0.0))
                            pieces.append(jnp.where(t8 + half * 8 >= s_, hq[rows, ls] * kb * e, 0.0))
                pv = jnp.concatenate(pieces, axis=0)
                p_s[p, pl.ds(r0, TB), s_ * LANE:(s_ + 1) * LANE] = pv.astype(BF16)
                if s_ % 2 == 1:
                    yield

        def finish():
            my = jnp.concatenate(ys, axis=1) * _silu(pj(C_MZ, W))
            om = my * lax.rsqrt(jnp.mean(my * my, axis=-1, keepdims=True) + EPS) * mnw_ref[...]
            mixed_s[pl.ds(r0, TB), 3 * W:4 * W] = om.astype(BF16)

        gens = ([s5_input_piece(b)] + [ret_pair(p) for p in range(W // LANE)]
                + [m2_pair(p) for p in range(W // LANE)] + [diag_products(p) for p in range(W // LANE)])
        return gens, finish

    _for_sequences(nb, phase1_parts, 2)

    for p in range(W // LANE):
        dg_s[p] = jnp.dot(p_s[p], sel_ref[...], preferred_element_type=F32)

    ar = jnp.broadcast_to(sA_ref[0:1, :], (nb, S5N))
    ai = jnp.broadcast_to(sA_ref[1:2, :], (nb, S5N))

    def scan_body(t, carry):
        hr, hi = carry
        row = pl.multiple_of(t * nb, nb)
        half = NBC // 2
        nr = ar * hr - ai * hi + jnp.concatenate([bu_s[c_, pl.ds(row, nb), :] for c_ in range(half)], axis=1)
        ni = ar * hi + ai * hr + jnp.concatenate([bu_s[half + c_, pl.ds(row, nb), :] for c_ in range(half)],
                                                 axis=1)
        for c_ in range(half):
            bu_s[c_, pl.ds(row, nb), :] = nr[:, c_ * BCW:(c_ + 1) * BCW]
            bu_s[half + c_, pl.ds(row, nb), :] = ni[:, c_ * BCW:(c_ + 1) * BCW]
        return nr, ni
    hr, hi = lax.fori_loop(0, TB, scan_body, (s5_ref[:, 0:S5N], s5_ref[:, S5N:2 * S5N]))
    s5_ref[:, 0:S5N] = hr
    s5_ref[:, S5N:2 * S5N] = hi
    ch_tb = _dot(bu_s[0], sC_ref[0:BCW, :])
    for c_ in range(1, NBC):
        ch_tb = ch_tb + _dot(bu_s[c_], sC_ref[c_ * BCW:(c_ + 1) * BCW, :])
    for c_ in range(W // LANE):
        u_s[c_] = ch_tb[:, c_ * LANE:(c_ + 1) * LANE]

    def phase2_parts(b):
        r0 = pl.multiple_of(b * TB, TB)

        def pj(c0, w):
            return proj_s[pl.ds(r0, TB), c0:c0 + w]

        def s5_out():
            chs = jnp.concatenate([u_s[c_, pl.ds(b, TB, stride=nb), :] for c_ in range(W // LANE)],
                                  axis=1)
            gy = _gelu_tanh(chs + sD_ref[...] * pj(C_SU, W))
            glu = _dot(gy, gluw_ref[...])
            yield
            os5 = gy * jax.nn.sigmoid(glu + glub_ref[...]) * _silu(pj(C_SG, W))
            mixed_s[pl.ds(r0, TB), 2 * W:3 * W] = os5.astype(BF16)

        cum = cum_s[b]
        rr = jnp.concatenate(
            [jnp.zeros((SUB, W), F32)]
            + [jnp.broadcast_to(cum_s[b, i * SUB - 1:i * SUB, :], (SUB, W))
               for i in range(1, NSUB)], axis=0)
        ee = jnp.concatenate(
            [jnp.broadcast_to(cum_s[b, i * SUB + SUB - 1:i * SUB + SUB, :], (SUB, W))
             for i in range(NSUB)], axis=0)
        totc = jnp.broadcast_to(cum[TB - 1:TB, :], (LANE, W)).T
        hq = hq_s[b]
        hk = hk_s[b]
        hv = pj(C_GI, W)
        gg = pj(C_GG, W)
        qt = hq * jnp.exp(cum - rr)
        kh_ = hk * jnp.exp(ee - cum)
        qe = hq * jnp.exp(cum)
        tot = cum[TB - 1:TB, :]
        kend = kh_ * jnp.exp(tot - ee)
        trow = lax.broadcasted_iota(jnp.int32, (TB, W), 0) // SUB
        qx = []
        for jb in range(NSUB - 1):
            eb = jnp.broadcast_to(ee[jb * SUB:jb * SUB + 1, :], (TB, W))
            qx.append(jnp.where(trow > jb, qt * jnp.exp(jnp.minimum(rr - eb, 0.0)), 0.0))

        def hg_pair(p):
            ls = slice(p * LANE, (p + 1) * LANE)
            hb = hhg_s[b, p]
            hv2 = hv[:, ls]
            kt = dup_t(kh_[:, ls])
            off = _dot(qx[0][:, ls], jnp.where(srcblock[0], kt, 0.0))
            for jb in range(1, NSUB - 1):
                off = off + _dot(qx[jb][:, ls], jnp.where(srcblock[jb], kt, 0.0))
            oi = _dot(qe[:, ls], hb)
            kv = _dot(dup_t(kend[:, ls])[:, 0:HD], hv2)
            yield
            o = _dot(jnp.where(subdiag2, dg_s[p, pl.ds(r0, TB), :], 0.0) + off, blockdiag2(hv2))
            yield
            hhg_s[b, p] = jnp.where(pairmask, jnp.exp(totc[ls, :]) * hb + kv, 0.0)
            o = o + oi
            ms = _dot(o * o, bones_ref[...])
            yield
            o = o * lax.rsqrt(ms + EPS) * hnw_ref[:, ls] * _silu(gg[:, ls])
            mixed_s[pl.ds(r0, TB), 1 * W + p * LANE:1 * W + (p + 1) * LANE] = o.astype(BF16)

        return [s5_out()] + [hg_pair(p) for p in range(W // LANE)], None

    _for_sequences(nb, phase2_parts, 4)

    @pl.when(j == pl.num_programs(0) - 1)
    def _emit_states():
        for b in range(nb):
            for h in range(NH):
                p, h2 = divmod(h, 2)
                blk = (slice(h2 * HD, (h2 + 1) * HD),) * 2
                ret_ref[b, h] = hret_s[(b, p) + blk]
                hg_ref[b, h] = hhg_s[(b, p) + blk]
                m2_ref[b, h] = hm2_s[(b, p) + blk]

    res = jnp.dot(mixed_s[...], wout_ref[...], preferred_element_type=F32)
    for b in range(nb):
        xo = x_ref[b] + res[b * TB:(b + 1) * TB]
        if last:
            xo = xo * lax.rsqrt(jnp.mean(xo * xo, axis=-1, keepdims=True) + EPS) * fnw_ref[...]
        act_ref[b] = xo


def _const_spec(shape):
    nd = len(shape)
    return pl.BlockSpec(shape, lambda j: (0,) * nd)


def _prompt_layer(x, l, last, w):
    nb, L, _ = x.shape
    nblk = L // TB
    rows = nb * TB
    xspec = pl.BlockSpec((nb, TB, D), lambda j: (0, j, 0))
    tspec = pl.BlockSpec((TB, LANE), lambda j: (j, 0))

    def per_layer(a):
        nd = a.ndim
        return pl.BlockSpec((None,) + a.shape[1:], lambda j: (l,) + (0,) * (nd - 1))

    stacked = [w[k] for k in ("norm_w", "w_in", "w_out")]
    consts = [w[k] for k in ("rdec", "recum", "rkdec", "retot")]
    ret_nw = [w["ret_norm_w"], w["lb"], w["hgrn_norm_w"]]
    s5 = [w[k] for k in ("s5_A", "s5_Bblk_cols", "s5_Cblk", "s5_D", "s5_glu_w", "s5_glu_b")]
    m2 = [w[k] for k in ("m2_conv_w", "m2_conv_b", "m2_dt_bias", "m2_A_log", "m2_D", "m2_norm_w")]
    pair_consts = [w["bones"], w["hexp"]]
    args = [x, w["cos_p"], w["sin_p"]] + stacked + consts + ret_nw + [w["sel"]] + s5 + m2 + pair_consts
    in_specs = ([xspec, tspec, tspec] + [per_layer(a) for a in stacked] + [_const_spec(a.shape) for a in consts]
                + [per_layer(a) for a in ret_nw] + [_const_spec(w["sel"].shape)]
                + [per_layer(a) for a in s5 + m2] + [_const_spec(a.shape) for a in pair_consts])
    if last:
        args.append(w["final_norm_w"])
        in_specs.append(_const_spec(w["final_norm_w"].shape))
    state_shapes = [jax.ShapeDtypeStruct((nb, NH, HD, HD), F32),
                    jax.ShapeDtypeStruct((nb, NH, HD, HD), F32),
                    jax.ShapeDtypeStruct((nb, 2 * S5N), F32),
                    jax.ShapeDtypeStruct((nb, NH, HD, HD), F32),
                    jax.ShapeDtypeStruct((nb, CONV_K - 1, CONV_CH), F32)]
    out_shape = [jax.ShapeDtypeStruct((nb, L, D), F32)] + state_shapes
    out_specs = [xspec] + [_const_spec(s.shape) for s in state_shapes]
    scratch = [pltpu.VMEM((rows, D), BF16),
               pltpu.VMEM((rows, P_PAD), F32),
               pltpu.VMEM((rows, D), BF16),
               pltpu.VMEM((NBC, rows, BCW), F32),
               pltpu.VMEM((W // LANE, rows, LANE), F32),
               pltpu.VMEM((rows, W), BF16),
               pltpu.VMEM((nb, TB + 8, CONV_CH), F32),
               pltpu.VMEM((W // LANE, rows, SUB * LANE), BF16),
               pltpu.VMEM((W // LANE, rows, LANE), F32),
               pltpu.VMEM((nb, TB, W), F32),
               pltpu.VMEM((nb, TB, W), F32),
               pltpu.VMEM((nb, TB, W), F32)] + [pltpu.VMEM((nb, W // LANE, LANE, LANE), F32)] * 3
    return pl.pallas_call(
        functools.partial(_prompt_layer_body, last),
        grid=(nblk,),
        in_specs=in_specs,
        out_specs=out_specs,
        out_shape=out_shape,
        scratch_shapes=scratch,
        compiler_params=pltpu.CompilerParams(dimension_semantics=("arbitrary",),
                                             vmem_limit_bytes=VMEM_LIMIT),
        name="prompt_layer",
    )(*args)


def _rope_tables(pos):
    half = HD // 2
    inv = 1.0 / (ROPE_BASE ** (np.arange(half, dtype=np.float64) / half))
    ang = pos.astype(np.float64)[:, None] * inv[None, :]
    cos, sin = np.cos(ang), np.sin(ang)
    cos_t = np.tile(cos, (1, LANE // half))
    sin_t = np.tile(np.concatenate([-sin, sin], axis=1), (1, LANE // HD))
    return cos_t.astype(np.float32), sin_t.astype(np.float32)


def _retention_tables():
    log_gamma = np.log1p(-(2.0 ** (-5.0 - np.arange(NH, dtype=np.float64))))
    cum = np.cumsum(np.broadcast_to(log_gamma, (TB, NH)), axis=0)
    total = cum[-1]
    causal = np.tril(np.ones((TB, TB), dtype=bool))
    diff = cum[:, None, :] - cum[None, :, :]
    dec = np.where(causal[:, :, None], np.exp(np.where(causal[:, :, None], diff, 0.0)), 0.0)
    rdec = np.moveaxis(dec, 2, 0)
    recum = np.broadcast_to(np.exp(cum).T[:, :, None], (NH, TB, HD))
    rkdec = np.broadcast_to(np.exp(total[None, :] - cum).T[:, :, None], (NH, TB, HD))
    retot = np.broadcast_to(np.exp(total)[:, None, None], (NH, 1, HD))
    pair = lambda t: np.concatenate([t[0::2], t[1::2]], axis=-1).astype(np.float32)
    return pair(rdec), pair(recum), pair(rkdec), pair(retot), log_gamma


def _sel_matrix():
    sel = np.zeros((SUB, 2, HD, 2, TB), np.float32)
    for s_ in range(SUB):
        for h2 in range(2):
            sel[s_, h2, :, h2, s_::SUB] = 1.0
    return jnp.asarray(sel.reshape(SUB * LANE, LANE), dtype=BF16)


def _rows(v, width=None):
    v = v.astype(F32)
    if width is not None and v.shape[-1] < width:
        v = jnp.pad(v, ((0, 0), (0, width - v.shape[-1])))
    return v[:, None, :]


def _block_diag(blocks):
    g, r, c = blocks.shape
    eye = jnp.eye(g, dtype=blocks.dtype)
    return jnp.einsum('grc,gh->grhc', blocks, eye).reshape(g * r, g * c)


def _s5_discretise(A_re, A_im, log_dt, B_re, B_im):
    A_re, A_im = A_re.astype(F32), A_im.astype(F32)
    dt = jnp.exp(log_dt.astype(F32))[:, None]
    mag = jnp.exp(A_re * dt)
    ab_re, ab_im = mag * jnp.cos(A_im * dt), mag * jnp.sin(A_im * dt)
    nr, ni = ab_re - 1.0, ab_im
    den = A_re * A_re + A_im * A_im
    f_re = (nr * A_re + ni * A_im) / den
    f_im = (ni * A_re - nr * A_im) / den
    B_re, B_im = B_re.astype(F32), B_im.astype(F32)
    bb_re = f_re[..., None] * B_re - f_im[..., None] * B_im
    bb_im = f_re[..., None] * B_im + f_im[..., None] * B_re
    return ab_re, ab_im, bb_re, bb_im


def _s5_matrices(A_re, A_im, log_dt, B_re, B_im, C_re, C_im):
    ab_re, ab_im, bb_re, bb_im = _s5_discretise(A_re, A_im, log_dt, B_re, B_im)
    bblk = jnp.concatenate([_block_diag(jnp.swapaxes(bb_re, 1, 2)),
                            _block_diag(jnp.swapaxes(bb_im, 1, 2))], axis=1)
    cblk = jnp.concatenate([_block_diag(jnp.swapaxes(C_re.astype(F32), 1, 2)),
                            _block_diag(jnp.swapaxes(-C_im.astype(F32), 1, 2))], axis=0)
    return jnp.stack([ab_re.reshape(-1), ab_im.reshape(-1)], axis=0), bblk, cblk


R_COS, R_SIN, R_RETNW, R_LB, R_HNW, R_S5D, R_GLUB, R_MNW = [i * W for i in range(8)]
R_AR = 8 * W
R_AI = R_AR + S5N
R_DTB = R_AI + S5N
R_ALOG = R_DTB + 8
R_MD = R_ALOG + 8
N_COLP = R_MD + 8


def _prepare(p, lb_all, prompt_len):
    depth = p["norm_w"].shape[0]
    s5_A, bblk, cblk = jax.vmap(_s5_matrices)(p["s5_A_re"], p["s5_A_im"], p["s5_log_dt"], p["s5_B_re"],
                                              p["s5_B_im"], p["s5_C_re"], p["s5_C_im"])
    cos_p, sin_p = _rope_tables(np.arange(prompt_len, dtype=np.float32))
    rdec, recum, rkdec, retot, log_gamma = _retention_tables()
    cos_s, sin_s = _rope_tables(np.float32(PAST_LEN) + np.arange(1, dtype=np.float32))
    w = dict(
        norm_w=_rows(p["norm_w"]),
        w_in=jnp.pad(p["w_in"].astype(BF16), ((0, 0), (0, 0), (0, P_PAD - P_TOTAL))),
        w_out=p["w_out"].astype(BF16),
        ret_norm_w=_rows(p["ret_norm_w"]),
        lb=_rows(lb_all),
        hgrn_norm_w=_rows(p["hgrn_norm_w"]),
        s5_A=s5_A,
        s5_Bblk=bblk.astype(BF16),
        s5_Cblk=cblk.astype(BF16),
        s5_D=_rows(p["s5_D"]),
        s5_glu_w=p["s5_glu_w"].astype(BF16),
        s5_glu_b=_rows(p["s5_glu_b"]),
        m2_conv_w=p["m2_conv_w"].astype(F32),
        m2_conv_b=_rows(p["m2_conv_b"]),
        m2_dt_bias=_rows(jnp.repeat(p["m2_dt_bias"], HD, axis=1)),
        m2_A_log=_rows(jnp.repeat(p["m2_A_log"], HD, axis=1)),
        m2_D=_rows(jnp.repeat(p["m2_D"], HD, axis=1)),
        m2_norm_w=_rows(p["m2_norm_w"]),
        final_norm_w=p["final_norm_w"].astype(F32).reshape(1, D),
    )
    tile2 = lambda t: jnp.asarray(np.broadcast_to(np.tile(t[0], W // LANE), (depth, W)))
    colp = jnp.concatenate(
        [tile2(cos_s), tile2(sin_s), w["ret_norm_w"][:, 0], w["lb"][:, 0], w["hgrn_norm_w"][:, 0],
         w["s5_D"][:, 0], w["s5_glu_b"][:, 0], w["m2_norm_w"][:, 0], s5_A[:, 0], s5_A[:, 1],
         _rows(p["m2_dt_bias"], 8)[:, 0], _rows(p["m2_A_log"], 8)[:, 0], _rows(p["m2_D"], 8)[:, 0]], axis=1)
    w.update(
        colp=jnp.broadcast_to(colp[:, :, None], (depth, N_COLP, LANE)),
        s5_Bblk_cols=jnp.moveaxis(w["s5_Bblk"].reshape(depth, W, NBC, BCW), 2, 1),
        s5_BblkT=jnp.swapaxes(w["s5_Bblk"], 1, 2),
        s5_CblkT=jnp.swapaxes(w["s5_Cblk"], 1, 2),
        s5_glu_wT=jnp.swapaxes(w["s5_glu_w"], 1, 2),
        rgam=jnp.asarray(np.broadcast_to(np.exp(log_gamma)[:, None, None], (NH, 8, LANE)), dtype=F32),
        cos_p=jnp.asarray(cos_p), sin_p=jnp.asarray(sin_p), rdec=jnp.asarray(rdec), recum=jnp.asarray(recum),
        rkdec=jnp.asarray(rkdec), retot=jnp.asarray(retot), sel=_sel_matrix(),
        bones=jnp.asarray(np.kron(np.eye(LANE // HD), np.full((HD, HD), 1.0 / HD)), dtype=BF16),
        hexp=jnp.asarray(np.kron(np.eye(LANE, NH), np.ones((1, HD))), dtype=F32),
    )
    return w


KC = 16
NKC = HD // KC
STEPS = NH * NKC


def _rotary_cols(x, cos, sin_signed):
    half = HD // 2
    parts = []
    for h in range(NH):
        parts += [x[h * HD + half:(h + 1) * HD], x[h * HD:h * HD + half]]
    return x * cos + jnp.concatenate(parts, axis=0) * sin_signed


def _expand_rows(dst, x):
    for c in range(x.shape[0]):
        dst[c] = jnp.broadcast_to(x[c:c + 1, :], (8, LANE))


def _sample_body(n_steps, x_ref, normw_ref, win_ref, wout_ref, fnw_ref, colp_ref, sbt_ref, sct_ref,
                 gluwt_ref, cw_ref, cb_ref, rgam_ref,
                 ret_in, hg_in, m2_in, s5re_in, s5im_in, buf_in,
                 y_ref, ret_out, hg_out, m2_out, s5re_out, s5im_out, buf_out,
                 xs_s, pt_s, vt_s, ot_s, mixt_s, o_s, hp_s,
                 kret_s, qret_s, khg_s, qhg_s, ahg_s, km2_s, qm2_s):
    i = pl.program_id(0)
    r = i % STEPS
    h = r // NKC
    kc = r % NKC

    def cp(r0, n=W):
        return colp_ref[r0:r0 + n, :]

    def pc(c0, w):
        return pt_s[c0:c0 + w, :]

    @pl.when(i == 0)
    def _load():
        xs_s[...] = x_ref[...]

    @pl.when(r == 0)
    def _prep():
        x = xs_s[...]
        hn = x * lax.rsqrt(jnp.mean(x * x, axis=-1, keepdims=True) + EPS) * normw_ref[...]
        proj = jnp.dot(hn.astype(BF16), win_ref[...], preferred_element_type=F32)

        xnew = proj[:, C_XBC:C_XBC + CONV_CH]
        acc = cb_ref[...] + xnew * cw_ref[CONV_K - 1:CONV_K, :]
        for t in range(CONV_K - 1):
            acc = acc + buf_in[t] * cw_ref[t:t + 1, :]
        for t in range(CONV_K - 2):
            buf_out[t] = buf_in[t + 1]
        buf_out[CONV_K - 2] = xnew
        xbc = _silu(acc)

        for t in range(P_PAD // LANE):
            c0 = t * LANE
            if C_XBC <= c0 < C_XBC + CONV_CH:
                tile = xbc[:, c0 - C_XBC:c0 - C_XBC + LANE]
            else:
                tile = proj[:, c0:c0 + LANE]
            pt_s[c0:c0 + LANE, :] = tile.T

        cos, sin = cp(R_COS), cp(R_SIN)
        _expand_rows(kret_s, _rotary_cols(pc(C_RK, W), cos, sin) * (HD ** -0.5))
        _expand_rows(qret_s, _rotary_cols(pc(C_RQ, W), cos, sin))
        vt_s[0] = pc(C_RV, W)

        fr = pc(C_GF, W)
        lb = cp(R_LB)
        logf = _log_sigmoid(fr) + jnp.log(1.0 + lb * jnp.exp(jnp.minimum(-fr, EXP_CLIP)))
        _expand_rows(ahg_s, jnp.exp(logf))
        _expand_rows(khg_s, (1.0 - lb) * jax.nn.sigmoid(-fr))
        _expand_rows(qhg_s, _silu(pc(C_GQ, W)))
        vt_s[1] = pc(C_GI, W)

        dt8 = _softplus(pc(C_DT, 8) + cp(R_DTB, 8))
        adec8 = jnp.exp(dt8 * (-jnp.exp(cp(R_ALOG, 8))))
        for hh in range(NH):
            hp_s[hh] = jnp.broadcast_to(adec8[hh:hh + 1, :], (8, LANE))
            vt_s[2, hh * HD:(hh + 1) * HD, :] = pc(C_XBC + hh * HD, HD) * dt8[hh:hh + 1, :]
        _expand_rows(km2_s, pc(C_XBC + W, 2 * HD))
        _expand_rows(qm2_s, pc(C_XBC + W + 2 * HD, 2 * HD))

        u = pc(C_SU, W)
        bu = jnp.dot(sbt_ref[...], u.astype(BF16), preferred_element_type=F32)
        hr, hi = s5re_in[...], s5im_in[...]
        ar, ai = cp(R_AR, S5N), cp(R_AI, S5N)
        nr = ar * hr - ai * hi + bu[0:S5N]
        ni = ar * hi + ai * hr + bu[S5N:2 * S5N]
        s5re_out[...] = nr
        s5im_out[...] = ni
        hcat = jnp.concatenate([nr, ni], axis=0).astype(BF16)
        sy = jnp.dot(sct_ref[...], hcat, preferred_element_type=F32) + cp(R_S5D) * u
        gy = _gelu_tanh(sy)
        glu = jnp.dot(gluwt_ref[...], gy.astype(BF16), preferred_element_type=F32) + cp(R_GLUB)
        mixt_s[2 * W:3 * W, :] = gy * jax.nn.sigmoid(glu) * _silu(pc(C_SG, W))

    @pl.when(kc == 0)
    def _zero():
        o_s[...] = jnp.zeros(o_s.shape, F32)

    hrow = pl.multiple_of(h * HD, HD)
    cbase = h * HD + kc * KC
    gbase = (h // 2) * HD + kc * KC

    def update(m, st_in, st_out, kx, qx, base, decay):
        v3 = vt_s[m, pl.ds(hrow, HD), :].reshape(HD // 8, 8, LANE)

        def body(kk, o):
            s_new = decay(kk) * st_in[kk].reshape(HD // 8, 8, LANE) + kx[base + kk] * v3
            st_out[kk] = s_new.reshape(HD, LANE)
            return o + qx[base + kk] * s_new
        o_s[m] = lax.fori_loop(0, KC, body, o_s[m], unroll=2)

    gam = rgam_ref[h]
    update(0, ret_in, ret_out, kret_s, qret_s, cbase, lambda kk: gam)
    update(1, hg_in, hg_out, khg_s, qhg_s, cbase, lambda kk: ahg_s[cbase + kk])
    adec = hp_s[h]
    update(2, m2_in, m2_out, km2_s, qm2_s, gbase, lambda kk: adec)

    @pl.when(kc == NKC - 1)
    def _head_done():
        for m in range(3):
            ot_s[m, pl.ds(hrow, HD), :] = o_s[m].reshape(HD, LANE)

    @pl.when(r == STEPS - 1)
    def _finish():
        def head_rms_cols(o):
            parts = []
            for hh in range(NH):
                seg = o[hh * HD:(hh + 1) * HD]
                parts.append(seg * lax.rsqrt(jnp.mean(seg * seg, axis=0, keepdims=True) + EPS))
            return jnp.concatenate(parts, axis=0)

        mixt_s[0:W, :] = head_rms_cols(ot_s[0]) * cp(R_RETNW) * _silu(pc(C_RG, W))
        mixt_s[W:2 * W, :] = head_rms_cols(ot_s[1]) * cp(R_HNW) * _silu(pc(C_GG, W))
        md8 = cp(R_MD, 8)
        ym = jnp.concatenate([ot_s[2, hh * HD:(hh + 1) * HD, :] + md8[hh:hh + 1, :] * pc(C_XBC + hh * HD, HD)
                              for hh in range(NH)], axis=0)
        my = ym * _silu(pc(C_MZ, W))
        mixt_s[3 * W:4 * W, :] = my * lax.rsqrt(jnp.mean(my * my, axis=0, keepdims=True) + EPS) * cp(R_MNW)
        mixed = jnp.concatenate([mixt_s[t * LANE:(t + 1) * LANE, :].T for t in range(D // LANE)], axis=1)
        xo = xs_s[...] + jnp.dot(mixed.astype(BF16), wout_ref[...], preferred_element_type=F32)
        xs_s[...] = xo

        @pl.when(i == n_steps - 1)
        def _final_norm():
            y_ref[...] = xo * lax.rsqrt(jnp.mean(xo * xo, axis=-1, keepdims=True) + EPS) * fnw_ref[...]


def _sample_step(x, w, ret, hg, m2, s5re, s5im, buf):
    depth = ret.shape[0]
    n = x.shape[0]
    n_steps = depth * STEPS
    lay = lambda i: i // STEPS

    def per_layer(a):
        nd = a.ndim
        return pl.BlockSpec((None,) + a.shape[1:], lambda i: (lay(i),) + (0,) * (nd - 1))

    st_spec = pl.BlockSpec((None, None, KC, HD, LANE),
                           lambda i: (lay(i), (i % STEPS) // NKC, i % NKC, 0, 0))
    weights = [w["norm_w"], w["w_in"], w["w_out"]]
    tables = [w["colp"], w["s5_BblkT"], w["s5_CblkT"], w["s5_glu_wT"], w["m2_conv_w"], w["m2_conv_b"]]
    in_specs = ([_const_spec(x.shape)] + [per_layer(a) for a in weights] + [_const_spec(w["final_norm_w"].shape)]
                + [per_layer(a) for a in tables] + [_const_spec(w["rgam"].shape)]
                + [st_spec, st_spec, st_spec, per_layer(s5re), per_layer(s5im), per_layer(buf)])
    out_shape = [jax.ShapeDtypeStruct((n, D), F32)] + [jax.ShapeDtypeStruct(a.shape, F32)
                                                       for a in (ret, hg, m2, s5re, s5im, buf)]
    out_specs = [_const_spec((n, D)), st_spec, st_spec, st_spec, per_layer(s5re), per_layer(s5im),
                 per_layer(buf)]
    expand = lambda c: pltpu.VMEM((c, 8, LANE), F32)
    scratch = [pltpu.VMEM((n, D), F32),
               pltpu.VMEM((P_PAD, LANE), F32),
               pltpu.VMEM((3, W, LANE), F32),
               pltpu.VMEM((3, W, LANE), F32),
               pltpu.VMEM((D, LANE), F32),
               pltpu.VMEM((3, HD // 8, 8, LANE), F32),
               pltpu.VMEM((NH, 8, LANE), F32),
               expand(W), expand(W), expand(W), expand(W), expand(W), expand(2 * HD), expand(2 * HD)]
    return pl.pallas_call(
        functools.partial(_sample_body, n_steps),
        grid=(n_steps,),
        in_specs=in_specs,
        out_specs=out_specs,
        out_shape=out_shape,
        scratch_shapes=scratch,
        compiler_params=pltpu.CompilerParams(dimension_semantics=("arbitrary",),
                                             vmem_limit_bytes=VMEM_LIMIT),
        name="sample_step",
    )(x, *weights, w["final_norm_w"], *tables, w["rgam"], ret, hg, m2, s5re, s5im, buf)


def kernel(x_prompt, x_sample, state_ret, state_hgrn, state_s5_re, state_s5_im, state_m2_ssm,
           state_m2_conv, norm_w, w_in, ret_norm_w, hgrn_lb_logits, hgrn_norm_w, s5_A_re, s5_A_im,
           s5_log_dt, s5_B_re, s5_B_im, s5_C_re, s5_C_im, s5_D, s5_glu_w, s5_glu_b, m2_conv_w,
           m2_conv_b, m2_dt_bias, m2_A_log, m2_D, m2_norm_w, w_out, final_norm_w):
    p = dict(norm_w=norm_w, w_in=w_in, ret_norm_w=ret_norm_w, hgrn_norm_w=hgrn_norm_w,
             s5_A_re=s5_A_re, s5_A_im=s5_A_im, s5_log_dt=s5_log_dt, s5_B_re=s5_B_re, s5_B_im=s5_B_im,
             s5_C_re=s5_C_re, s5_C_im=s5_C_im, s5_D=s5_D, s5_glu_w=s5_glu_w, s5_glu_b=s5_glu_b,
             m2_conv_w=m2_conv_w, m2_conv_b=m2_conv_b, m2_dt_bias=m2_dt_bias, m2_A_log=m2_A_log,
             m2_D=m2_D, m2_norm_w=m2_norm_w, w_out=w_out, final_norm_w=final_norm_w)
    depth = norm_w.shape[0]
    nbp, lp, _ = x_prompt.shape
    nbs = x_sample.shape[0]

    lb_sm = jax.nn.softmax(hgrn_lb_logits.astype(F32), axis=0)
    lb_all = jnp.clip(jnp.cumsum(lb_sm, axis=0) - lb_sm[0], 0.0, 1.0)

    w = _prepare(p, lb_all, lp)

    xp = x_prompt
    pst = []
    for l in range(depth):
        outs = _prompt_layer(xp, l, l == depth - 1, w)
        xp = outs[0]
        ret, hg, s5, m2, buf = outs[-5:]
        pst.append((ret, hg, s5[:, :S5N].reshape(nbp, S5G, S5P), s5[:, S5N:].reshape(nbp, S5G, S5P),
                    m2, buf))
    yp = xp

    seq_last = lambda a: jnp.moveaxis(a.astype(F32), 1, -1)
    ys, ret, hg, m2, s5re, s5im, buf = _sample_step(
        x_sample.reshape(nbs, D), w,
        seq_last(state_ret), seq_last(state_hgrn), seq_last(state_m2_ssm),
        seq_last(state_s5_re).reshape(depth, S5N, nbs), seq_last(state_s5_im).reshape(depth, S5N, nbs),
        jnp.swapaxes(state_m2_conv.astype(F32), 1, 2))
    seq_second = lambda a: jnp.moveaxis(a, -1, 1)

    stk = lambda i: jnp.stack([s[i] for s in pst], axis=0)
    return (yp, ys.reshape(nbs, 1, D),
            stk(0), stk(1), stk(2), stk(3), stk(4), stk(5),
            seq_second(ret), seq_second(hg), seq_second(s5re.reshape(depth, S5G, S5P, nbs)),
            seq_second(s5im.reshape(depth, S5G, S5P, nbs)), seq_second(m2), jnp.swapaxes(buf, 1, 2))
```

```python
import functools
import math

import numpy as np
import jax
import jax.numpy as jnp
from jax import lax
from jax.experimental import pallas as pl
from jax.experimental.pallas import tpu as pltpu

F32 = jnp.float32
BF16 = jnp.bfloat16

D = 1024
W = 256
NH = 4
HD = 64
S5G = 16
S5C = 16
S5P = 64
S5N = S5G * S5P
NBC = 8
BCW = 2 * S5N // NBC
CONV_CH = 512
CONV_K = 4
TB = 64
SUB = 16
NSUB = TB // SUB
EPS = 1e-6
EXP_CLIP = 60.0
ROPE_BASE = 10000.0
PAST_LEN = 16384

C_RQ, C_RK, C_RV, C_RG = 0, 256, 512, 768
C_GQ, C_GF, C_GI, C_GG = 1024, 1280, 1536, 1792
C_SU, C_SG = 2048, 2304
C_MZ, C_XBC, C_DT = 2560, 2816, 3328
P_TOTAL = 3332
PCH = 1152
NPC = 3
P_PAD = NPC * PCH
LANE = 128
VMEM_LIMIT = 56 * 1024 * 1024


def _silu(x):
    return x * jax.nn.sigmoid(x)


def _softplus(x):
    return jnp.maximum(x, 0.0) + jnp.log(1.0 + jnp.exp(-jnp.abs(x)))


def _log_sigmoid(x):
    return jnp.minimum(x, 0.0) - jnp.log(1.0 + jnp.exp(-jnp.abs(x)))


def _round_robin(gens):
    gens = list(gens)
    while gens:
        alive = []
        for g in gens:
            try:
                next(g)
                alive.append(g)
            except StopIteration:
                pass
        gens = alive


def _for_sequences(nb, parts, group):
    def body(i, c):
        built = [parts(i * group + k) for k in range(group)]
        _round_robin([g for gens, _ in built for g in gens])
        for _, finish in built:
            if finish is not None:
                finish()
        return c
    lax.fori_loop(0, nb // group, body, 0)


def _gelu_tanh(x):
    c = math.sqrt(2.0 / math.pi)
    return 0.5 * x * (1.0 + jnp.tanh(c * (x + 0.044715 * (x * x * x))))


def _dot(a, b):
    return jnp.dot(a.astype(BF16), b.astype(BF16), preferred_element_type=F32)


def _split3(x):
    hi = x.astype(BF16)
    rest = x - hi.astype(F32)
    mid = rest.astype(BF16)
    return hi, mid, (rest - mid.astype(F32)).astype(BF16)


def _select_rows(m01, x):
    m = m01.astype(BF16)
    hi, mid, lo = _split3(x)
    return (jnp.dot(m, hi, preferred_element_type=F32) + jnp.dot(m, mid, preferred_element_type=F32)
            + jnp.dot(m, lo, preferred_element_type=F32))


def _select_cols(x, m01):
    m = m01.astype(BF16)
    hi, mid, lo = _split3(x)
    return (jnp.dot(hi, m, preferred_element_type=F32) + jnp.dot(mid, m, preferred_element_type=F32)
            + jnp.dot(lo, m, preferred_element_type=F32))


def _rot_half_partner(x):
    lane = lax.broadcasted_iota(jnp.int32, x.shape, 1)
    first = (lane % HD) < (HD // 2)
    return jnp.where(first, pltpu.roll(x, LANE - HD // 2, 1), pltpu.roll(x, HD // 2, 1))


def _rotary(x, cos, sin_signed):
    parts = []
    for i in range(W // LANE):
        xi = x[:, i * LANE:(i + 1) * LANE]
        parts.append(xi * cos + _rot_half_partner(xi) * sin_signed)
    return jnp.concatenate(parts, axis=1)


def _prompt_layer_body(last, *refs):
    (x_ref, cos_ref, sin_ref, normw_ref, win_ref, wout_ref,
     rdec_ref, recum_ref, rkdec_ref, retot_ref, retnw_ref,
     lb_ref, hnw_ref, sel_ref,
     sA_ref, sB_ref, sC_ref, sD_ref, gluw_ref, glub_ref,
     cw_ref, cb_ref, dtb_ref, alog_ref, md_ref, mnw_ref, bones_ref, hexp_ref) = refs[:28]
    refs = refs[28:]
    if last:
        fnw_ref = refs[0]
        refs = refs[1:]
    act_ref = refs[0]
    refs = refs[1:]
    (ret_ref, hg_ref, s5_ref, m2_ref, m2buf_ref,
     hn_s, proj_s, mixed_s, bu_s, u_s, ub_s, cv_s, p_s, dg_s, hq_s, hk_s, cum_s, hret_s, hhg_s, hm2_s) = refs
    j = pl.program_id(0)
    nb = x_ref.shape[0]

    @pl.when(j == 0)
    def _init():
        hret_s[...] = jnp.zeros(hret_s.shape, F32)
        hhg_s[...] = jnp.zeros(hhg_s.shape, F32)
        s5_ref[...] = jnp.zeros(s5_ref.shape, F32)
        hm2_s[...] = jnp.zeros(hm2_s.shape, F32)
        cv_s[...] = jnp.zeros(cv_s.shape, F32)

    ti = lax.broadcasted_iota(jnp.int32, (TB, TB), 0)
    si = lax.broadcasted_iota(jnp.int32, (TB, TB), 1)
    causal = si <= ti
    tri_l = causal.astype(F32)
    pr = lax.broadcasted_iota(jnp.int32, (LANE, LANE), 0)
    pc_ = lax.broadcasted_iota(jnp.int32, (LANE, LANE), 1)
    pairmask = (pr // HD) == (pc_ // HD)
    t2 = lax.broadcasted_iota(jnp.int32, (TB, LANE), 0)
    l2 = lax.broadcasted_iota(jnp.int32, (TB, LANE), 1)
    causal2 = (l2 % HD) <= t2
    first_head = l2 < HD
    subdiag2 = (t2 // SUB) == ((l2 % HD) // SUB)
    srcblock = [pairmask & (((pc_ % HD) // SUB) == jb) for jb in range(NSUB - 1)]

    def dup_t(x2):
        return jnp.concatenate([x2, x2], axis=0).T

    def blockdiag2(x2):
        return jnp.where(pairmask, jnp.concatenate([x2, x2], axis=0), 0.0)

    def norm_body(b, c):
        xb = x_ref[b]
        hn = xb * lax.rsqrt(jnp.mean(xb * xb, axis=-1, keepdims=True) + EPS) * normw_ref[...]
        hn_s[pl.ds(pl.multiple_of(b * TB, TB), TB), :] = hn.astype(BF16)
        return c
    lax.fori_loop(0, nb, norm_body, 0)
    for c in range(NPC):
        cs = slice(c * PCH, (c + 1) * PCH)
        proj_s[:, cs] = jnp.dot(hn_s[...], win_ref[:, cs], preferred_element_type=F32)

    cos = cos_ref[...]
    sin = sin_ref[...]

    def reorder_u(b, c):
        u = proj_s[pl.ds(pl.multiple_of(b * TB, TB), TB), C_SU:C_SU + W]
        for c_ in range(W // LANE):
            u_s[c_, pl.ds(b, TB, stride=nb), :] = u[:, c_ * LANE:(c_ + 1) * LANE]
        return c
    lax.fori_loop(0, nb, reorder_u, 0)
    ub_s[...] = jnp.concatenate([u_s[c_] for c_ in range(W // LANE)], axis=1).astype(BF16)

    def s5_input_piece(cb):
        bu_s[cb] = jnp.dot(ub_s[...], sB_ref[cb], preferred_element_type=F32)
        yield

    def phase1_parts(b):
        r0 = pl.multiple_of(b * TB, TB)

        def pj(c0, w):
            return proj_s[pl.ds(r0, TB), c0:c0 + w]

        rq = _rotary(pj(C_RQ, W), cos, sin)
        rk = _rotary(pj(C_RK, W), cos, sin) * (HD ** -0.5)
        rv = pj(C_RV, W)
        rg = pj(C_RG, W)

        def ret_pair(p):
            ls = slice(p * LANE, (p + 1) * LANE)
            q2, k2, v2 = rq[:, ls], rk[:, ls], rv[:, ls]
            hb = hret_s[b, p]
            kt = dup_t(k2)
            s_raw = _dot(q2, jnp.where(pairmask, kt, 0.0))
            oi = _dot(q2, hb)
            kv = _dot(kt[:, 0:HD], v2 * rkdec_ref[p])
            yield
            o = _dot(s_raw * rdec_ref[p], blockdiag2(v2))
            yield
            o = o + oi * recum_ref[p]
            hret_s[b, p] = jnp.where(pairmask, retot_ref[p] * hb + kv, 0.0)
            ms = _dot(o * o, bones_ref[...])
            yield
            o = o * lax.rsqrt(ms + EPS) * retnw_ref[:, ls] * _silu(rg[:, ls])
            mixed_s[pl.ds(r0, TB), 0 * W + p * LANE:0 * W + (p + 1) * LANE] = o.astype(BF16)

        cv_s[b, 8:8 + TB, :] = pj(C_XBC, CONV_CH)
        acc = cb_ref[...] + cv_s[b, 5:5 + TB, :] * cw_ref[0:1, :]
        for i in range(1, CONV_K):
            acc = acc + cv_s[b, 5 + i:5 + i + TB, :] * cw_ref[i:i + 1, :]
        tail = cv_s[b, TB + 5:TB + 8, :]
        cv_s[b, 5:8, :] = tail
        m2buf_ref[b] = tail
        xbc = _silu(acc)
        xm = xbc[:, 0:W]
        bm = xbc[:, W:W + 2 * HD]
        cm = xbc[:, W + 2 * HD:W + 4 * HD]
        bm_sw = pltpu.roll(bm, HD, 1)
        cm_sw = pltpu.roll(cm, HD, 1)
        dt_b = _select_cols(_softplus(pj(C_DT, LANE) + dtb_ref[...]), hexp_ref[...])
        la_b = dt_b * (-jnp.exp(alog_ref[...]))
        cum_b = _select_rows(tri_l, la_b)
        ys = [None] * (W // LANE)

        def m2_pair(p):
            ls = slice(p * LANE, (p + 1) * LANE)
            b2 = jnp.where(first_head, bm, bm_sw) if p == 0 else jnp.where(first_head, bm_sw, bm)
            c2 = jnp.where(first_head, cm, cm_sw) if p == 0 else jnp.where(first_head, cm_sw, cm)
            x2, dt2, cum2 = xm[:, ls], dt_b[:, ls], cum_b[:, ls]
            xdt2 = x2 * dt2
            ct = dup_t(cum2)
            r2 = jnp.where(first_head, ct[0:TB], ct[HD:HD + TB])
            tot2 = cum2[TB - 1:TB, :]
            hb = hm2_s[b, p]
            bt = dup_t(b2)
            s_raw = _dot(c2, jnp.where(pairmask, bt, 0.0))
            oi = _dot(c2, hb)
            kv = _dot(bt[:, 0:HD], xdt2 * jnp.exp(tot2 - cum2))
            decay = jnp.where(causal2, jnp.exp(jnp.minimum(cum2 - r2, 0.0)), 0.0)
            yield
            o = _dot(s_raw * decay, blockdiag2(xdt2))
            yield
            hm2_s[b, p] = jnp.where(pairmask, jnp.exp(tot2) * hb + kv, 0.0)
            ys[p] = o + oi * jnp.exp(cum2) + md_ref[:, ls] * x2

        fr = pj(C_GF, W)
        lb = lb_ref[...]
        logf = _log_sigmoid(fr) + jnp.log(1.0 + lb * jnp.exp(jnp.minimum(-fr, EXP_CLIP)))
        hq = _silu(pj(C_GQ, W))
        hk = (1.0 - lb) * jax.nn.sigmoid(-fr)
        cum = _select_rows(tri_l, logf)
        hq_s[b] = hq
        hk_s[b] = hk
        cum_s[b] = cum
        t8 = lax.broadcasted_iota(jnp.int32, (8, LANE), 0)

        def diag_products(p):
            ls = slice(p * LANE, (p + 1) * LANE)
            for s_ in range(SUB):
                pieces = []
                for i in range(NSUB):
                    kb = jnp.broadcast_to(hk_s[b, i * SUB + s_:i * SUB + s_ + 1, ls], (8, LANE))
                    cb = jnp.broadcast_to(cum_s[b, i * SUB + s_:i * SUB + s_ + 1, ls], (8, LANE))
                    for half in range(SUB // 8):
                        rows = slice(i * SUB + half * 8, i * SUB + half * 8 + 8)
                        if half * 8 + 7 < s_:
                            pieces.append(jnp.zeros((8, LANE), F32))
                        elif half * 8 >= s_:
                            pieces.append(hq[rows, ls] * kb * jnp.exp(cum[rows, ls] - cb))
                        else:
                            e = jnp.exp(jnp.minimum(cum[rows, ls] - cb, 0.0))
                            pieces.append(jnp.where(t8 + half * 8 >= s_, hq[rows, ls] * kb * e, 0.0))
                pv = jnp.concatenate(pieces, axis=0)
                p_s[p, pl.ds(r0, TB), s_ * LANE:(s_ + 1) * LANE] = pv.astype(BF16)
                if s_ % 2 == 1:
                    yield

        def finish():
            my = jnp.concatenate(ys, axis=1) * _silu(pj(C_MZ, W))
            om = my * lax.rsqrt(jnp.mean(my * my, axis=-1, keepdims=True) + EPS) * mnw_ref[...]
            mixed_s[pl.ds(r0, TB), 3 * W:4 * W] = om.astype(BF16)

        gens = ([s5_input_piece(b)] + [ret_pair(p) for p in range(W // LANE)]
                + [m2_pair(p) for p in range(W // LANE)] + [diag_products(p) for p in range(W // LANE)])
        return gens, finish

    _for_sequences(nb, phase1_parts, 2)

    for p in range(W // LANE):
        dg_s[p] = jnp.dot(p_s[p], sel_ref[...], preferred_element_type=F32)

    ar = jnp.broadcast_to(sA_ref[0:1, :], (nb, S5N))
    ai = jnp.broadcast_to(sA_ref[1:2, :], (nb, S5N))

    def scan_body(t, carry):
        hr, hi = carry
        row = pl.multiple_of(t * nb, nb)
        half = NBC // 2
        nr = ar * hr - ai * hi + jnp.concatenate([bu_s[c_, pl.ds(row, nb), :] for c_ in range(half)], axis=1)
        ni = ar * hi + ai * hr + jnp.concatenate([bu_s[half + c_, pl.ds(row, nb), :] for c_ in range(half)],
                                                 axis=1)
        for c_ in range(half):
            bu_s[c_, pl.ds(row, nb), :] = nr[:, c_ * BCW:(c_ + 1) * BCW]
            bu_s[half + c_, pl.ds(row, nb), :] = ni[:, c_ * BCW:(c_ + 1) * BCW]
        return nr, ni
    hr, hi = lax.fori_loop(0, TB, scan_body, (s5_ref[:, 0:S5N], s5_ref[:, S5N:2 * S5N]))
    s5_ref[:, 0:S5N] = hr
    s5_ref[:, S5N:2 * S5N] = hi
    ch_tb = _dot(bu_s[0], sC_ref[0:BCW, :])
    for c_ in range(1, NBC):
        ch_tb = ch_tb + _dot(bu_s[c_], sC_ref[c_ * BCW:(c_ + 1) * BCW, :])
    for c_ in range(W // LANE):
        u_s[c_] = ch_tb[:, c_ * LANE:(c_ + 1) * LANE]

    def phase2_parts(b):
        r0 = pl.multiple_of(b * TB, TB)

        def pj(c0, w):
            return proj_s[pl.ds(r0, TB), c0:c0 + w]

        def s5_out():
            chs = jnp.concatenate([u_s[c_, pl.ds(b, TB, stride=nb), :] for c_ in range(W // LANE)],
                                  axis=1)
            gy = _gelu_tanh(chs + sD_ref[...] * pj(C_SU, W))
            glu = _dot(gy, gluw_ref[...])
            yield
            os5 = gy * jax.nn.sigmoid(glu + glub_ref[...]) * _silu(pj(C_SG, W))
            mixed_s[pl.ds(r0, TB), 2 * W:3 * W] = os5.astype(BF16)

        cum = cum_s[b]
        rr = jnp.concatenate(
            [jnp.zeros((SUB, W), F32)]
            + [jnp.broadcast_to(cum_s[b, i * SUB - 1:i * SUB, :], (SUB, W))
               for i in range(1, NSUB)], axis=0)
        ee = jnp.concatenate(
            [jnp.broadcast_to(cum_s[b, i * SUB + SUB - 1:i * SUB + SUB, :], (SUB, W))
             for i in range(NSUB)], axis=0)
        totc = jnp.broadcast_to(cum[TB - 1:TB, :], (LANE, W)).T
        hq = hq_s[b]
        hk = hk_s[b]
        hv = pj(C_GI, W)
        gg = pj(C_GG, W)
        qt = hq * jnp.exp(cum - rr)
        kh_ = hk * jnp.exp(ee - cum)
        qe = hq * jnp.exp(cum)
        tot = cum[TB - 1:TB, :]
        kend = kh_ * jnp.exp(tot - ee)
        trow = lax.broadcasted_iota(jnp.int32, (TB, W), 0) // SUB
        qx = []
        for jb in range(NSUB - 1):
            eb = jnp.broadcast_to(ee[jb * SUB:jb * SUB + 1, :], (TB, W))
            qx.append(jnp.where(trow > jb, qt * jnp.exp(jnp.minimum(rr - eb, 0.0)), 0.0))

        def hg_pair(p):
            ls = slice(p * LANE, (p + 1) * LANE)
            hb = hhg_s[b, p]
            hv2 = hv[:, ls]
            kt = dup_t(kh_[:, ls])
            off = _dot(qx[0][:, ls], jnp.where(srcblock[0], kt, 0.0))
            for jb in range(1, NSUB - 1):
                off = off + _dot(qx[jb][:, ls], jnp.where(srcblock[jb], kt, 0.0))
            oi = _dot(qe[:, ls], hb)
            kv = _dot(dup_t(kend[:, ls])[:, 0:HD], hv2)
            yield
            o = _dot(jnp.where(subdiag2, dg_s[p, pl.ds(r0, TB), :], 0.0) + off, blockdiag2(hv2))
            yield
            hhg_s[b, p] = jnp.where(pairmask, jnp.exp(totc[ls, :]) * hb + kv, 0.0)
            o = o + oi
            ms = _dot(o * o, bones_ref[...])
            yield
            o = o * lax.rsqrt(ms + EPS) * hnw_ref[:, ls] * _silu(gg[:, ls])
            mixed_s[pl.ds(r0, TB), 1 * W + p * LANE:1 * W + (p + 1) * LANE] = o.astype(BF16)

        return [s5_out()] + [hg_pair(p) for p in range(W // LANE)], None

    _for_sequences(nb, phase2_parts, 4)

    @pl.when(j == pl.num_programs(0) - 1)
    def _emit_states():
        for b in range(nb):
            for h in range(NH):
                p, h2 = divmod(h, 2)
                blk = (slice(h2 * HD, (h2 + 1) * HD),) * 2
                ret_ref[b, h] = hret_s[(b, p) + blk]
                hg_ref[b, h] = hhg_s[(b, p) + blk]
                m2_ref[b, h] = hm2_s[(b, p) + blk]

    res = jnp.dot(mixed_s[...], wout_ref[...], preferred_element_type=F32)
    for b in range(nb):
        xo = x_ref[b] + res[b * TB:(b + 1) * TB]
        if last:
            xo = xo * lax.rsqrt(jnp.mean(xo * xo, axis=-1, keepdims=True) + EPS) * fnw_ref[...]
        act_ref[b] = xo


def _const_spec(shape):
    nd = len(shape)
    return pl.BlockSpec(shape, lambda j: (0,) * nd)


def _prompt_layer(x, l, last, w):
    nb, L, _ = x.shape
    nblk = L // TB
    rows = nb * TB
    xspec = pl.BlockSpec((nb, TB, D), lambda j: (0, j, 0))
    tspec = pl.BlockSpec((TB, LANE), lambda j: (j, 0))

    def per_layer(a):
        nd = a.ndim
        return pl.BlockSpec((None,) + a.shape[1:], lambda j: (l,) + (0,) * (nd - 1))

    stacked = [w[k] for k in ("norm_w", "w_in", "w_out")]
    consts = [w[k] for k in ("rdec", "recum", "rkdec", "retot")]
    ret_nw = [w["ret_norm_w"], w["lb"], w["hgrn_norm_w"]]
    s5 = [w[k] for k in ("s5_A", "s5_Bblk_cols", "s5_Cblk", "s5_D", "s5_glu_w", "s5_glu_b")]
    m2 = [w[k] for k in ("m2_conv_w", "m2_conv_b", "m2_dt_bias", "m2_A_log", "m2_D", "m2_norm_w")]
    pair_consts = [w["bones"], w["hexp"]]
    args = [x, w["cos_p"], w["sin_p"]] + stacked + consts + ret_nw + [w["sel"]] + s5 + m2 + pair_consts
    in_specs = ([xspec, tspec, tspec] + [per_layer(a) for a in stacked] + [_const_spec(a.shape) for a in consts]
                + [per_layer(a) for a in ret_nw] + [_const_spec(w["sel"].shape)]
                + [per_layer(a) for a in s5 + m2] + [_const_spec(a.shape) for a in pair_consts])
    if last:
        args.append(w["final_norm_w"])
        in_specs.append(_const_spec(w["final_norm_w"].shape))
    state_shapes = [jax.ShapeDtypeStruct((nb, NH, HD, HD), F32),
                    jax.ShapeDtypeStruct((nb, NH, HD, HD), F32),
                    jax.ShapeDtypeStruct((nb, 2 * S5N), F32),
                    jax.ShapeDtypeStruct((nb, NH, HD, HD), F32),
                    jax.ShapeDtypeStruct((nb, CONV_K - 1, CONV_CH), F32)]
    out_shape = [jax.ShapeDtypeStruct((nb, L, D), F32)] + state_shapes
    out_specs = [xspec] + [_const_spec(s.shape) for s in state_shapes]
    scratch = [pltpu.VMEM((rows, D), BF16),
               pltpu.VMEM((rows, P_PAD), F32),
               pltpu.VMEM((rows, D), BF16),
               pltpu.VMEM((NBC, rows, BCW), F32),
               pltpu.VMEM((W // LANE, rows, LANE), F32),
               pltpu.VMEM((rows, W), BF16),
               pltpu.VMEM((nb, TB + 8, CONV_CH), F32),
               pltpu.VMEM((W // LANE, rows, SUB * LANE), BF16),
               pltpu.VMEM((W // LANE, rows, LANE), F32),
               pltpu.VMEM((nb, TB, W), F32),
               pltpu.VMEM((nb, TB, W), F32),
               pltpu.VMEM((nb, TB, W), F32)] + [pltpu.VMEM((nb, W // LANE, LANE, LANE), F32)] * 3
    return pl.pallas_call(
        functools.partial(_prompt_layer_body, last),
        grid=(nblk,),
        in_specs=in_specs,
        out_specs=out_specs,
        out_shape=out_shape,
        scratch_shapes=scratch,
        compiler_params=pltpu.CompilerParams(dimension_semantics=("arbitrary",),
                                             vmem_limit_bytes=VMEM_LIMIT),
        name="prompt_layer",
    )(*args)


def _rope_tables(pos):
    half = HD // 2
    inv = 1.0 / (ROPE_BASE ** (np.arange(half, dtype=np.float64) / half))
    ang = pos.astype(np.float64)[:, None] * inv[None, :]
    cos, sin = np.cos(ang), np.sin(ang)
    cos_t = np.tile(cos, (1, LANE // half))
    sin_t = np.tile(np.concatenate([-sin, sin], axis=1), (1, LANE // HD))
    return cos_t.astype(np.float32), sin_t.astype(np.float32)


def _retention_tables():
    log_gamma = np.log1p(-(2.0 ** (-5.0 - np.arange(NH, dtype=np.float64))))
    cum = np.cumsum(np.broadcast_to(log_gamma, (TB, NH)), axis=0)
    total = cum[-1]
    causal = np.tril(np.ones((TB, TB), dtype=bool))
    diff = cum[:, None, :] - cum[None, :, :]
    dec = np.where(causal[:, :, None], np.exp(np.where(causal[:, :, None], diff, 0.0)), 0.0)
    rdec = np.moveaxis(dec, 2, 0)
    recum = np.broadcast_to(np.exp(cum).T[:, :, None], (NH, TB, HD))
    rkdec = np.broadcast_to(np.exp(total[None, :] - cum).T[:, :, None], (NH, TB, HD))
    retot = np.broadcast_to(np.exp(total)[:, None, None], (NH, 1, HD))
    pair = lambda t: np.concatenate([t[0::2], t[1::2]], axis=-1).astype(np.float32)
    return pair(rdec), pair(recum), pair(rkdec), pair(retot), log_gamma


def _sel_matrix():
    sel = np.zeros((SUB, 2, HD, 2, TB), np.float32)
    for s_ in range(SUB):
        for h2 in range(2):
            sel[s_, h2, :, h2, s_::SUB] = 1.0
    return jnp.asarray(sel.reshape(SUB * LANE, LANE), dtype=BF16)


def _rows(v, width=None):
    v = v.astype(F32)
    if width is not None and v.shape[-1] < width:
        v = jnp.pad(v, ((0, 0), (0, width - v.shape[-1])))
    return v[:, None, :]


def _block_diag(blocks):
    g, r, c = blocks.shape
    eye = jnp.eye(g, dtype=blocks.dtype)
    return jnp.einsum('grc,gh->grhc', blocks, eye).reshape(g * r, g * c)


def _s5_discretise(A_re, A_im, log_dt, B_re, B_im):
    A_re, A_im = A_re.astype(F32), A_im.astype(F32)
    dt = jnp.exp(log_dt.astype(F32))[:, None]
    mag = jnp.exp(A_re * dt)
    ab_re, ab_im = mag * jnp.cos(A_im * dt), mag * jnp.sin(A_im * dt)
    nr, ni = ab_re - 1.0, ab_im
    den = A_re * A_re + A_im * A_im
    f_re = (nr * A_re + ni * A_im) / den
    f_im = (ni * A_re - nr * A_im) / den
    B_re, B_im = B_re.astype(F32), B_im.astype(F32)
    bb_re = f_re[..., None] * B_re - f_im[..., None] * B_im
    bb_im = f_re[..., None] * B_im + f_im[..., None] * B_re
    return ab_re, ab_im, bb_re, bb_im


def _s5_matrices(A_re, A_im, log_dt, B_re, B_im, C_re, C_im):
    ab_re, ab_im, bb_re, bb_im = _s5_discretise(A_re, A_im, log_dt, B_re, B_im)
    bblk = jnp.concatenate([_block_diag(jnp.swapaxes(bb_re, 1, 2)),
                            _block_diag(jnp.swapaxes(bb_im, 1, 2))], axis=1)
    cblk = jnp.concatenate([_block_diag(jnp.swapaxes(C_re.astype(F32), 1, 2)),
                            _block_diag(jnp.swapaxes(-C_im.astype(F32), 1, 2))], axis=0)
    return jnp.stack([ab_re.reshape(-1), ab_im.reshape(-1)], axis=0), bblk, cblk


R_COS, R_SIN, R_RETNW, R_LB, R_HNW, R_S5D, R_GLUB, R_MNW = [i * W for i in range(8)]
R_AR = 8 * W
R_AI = R_AR + S5N
R_DTB = R_AI + S5N
R_ALOG = R_DTB + 8
R_MD = R_ALOG + 8
N_COLP = R_MD + 8


def _prepare(p, lb_all, prompt_len):
    depth = p["norm_w"].shape[0]
    s5_A, bblk, cblk = jax.vmap(_s5_matrices)(p["s5_A_re"], p["s5_A_im"], p["s5_log_dt"], p["s5_B_re"],
                                              p["s5_B_im"], p["s5_C_re"], p["s5_C_im"])
    cos_p, sin_p = _rope_tables(np.arange(prompt_len, dtype=np.float32))
    rdec, recum, rkdec, retot, log_gamma = _retention_tables()
    cos_s, sin_s = _rope_tables(np.float32(PAST_LEN) + np.arange(1, dtype=np.float32))
    w = dict(
        norm_w=_rows(p["norm_w"]),
        w_in=jnp.pad(p["w_in"].astype(BF16), ((0, 0), (0, 0), (0, P_PAD - P_TOTAL))),
        w_out=p["w_out"].astype(BF16),
        ret_norm_w=_rows(p["ret_norm_w"]),
        lb=_rows(lb_all),
        hgrn_norm_w=_rows(p["hgrn_norm_w"]),
        s5_A=s5_A,
        s5_Bblk=bblk.astype(BF16),
        s5_Cblk=cblk.astype(BF16),
        s5_D=_rows(p["s5_D"]),
        s5_glu_w=p["s5_glu_w"].astype(BF16),
        s5_glu_b=_rows(p["s5_glu_b"]),
        m2_conv_w=p["m2_conv_w"].astype(F32),
        m2_conv_b=_rows(p["m2_conv_b"]),
        m2_dt_bias=_rows(p["m2_dt_bias"], LANE),
        m2_A_log=_rows(jnp.repeat(p["m2_A_log"], HD, axis=1)),
        m2_D=_rows(jnp.repeat(p["m2_D"], HD, axis=1)),
        m2_norm_w=_rows(p["m2_norm_w"]),
        final_norm_w=p["final_norm_w"].astype(F32).reshape(1, D),
    )
    tile2 = lambda t: jnp.asarray(np.broadcast_to(np.tile(t[0], W // LANE), (depth, W)))
    colp = jnp.concatenate(
        [tile2(cos_s), tile2(sin_s), w["ret_norm_w"][:, 0], w["lb"][:, 0], w["hgrn_norm_w"][:, 0],
         w["s5_D"][:, 0], w["s5_glu_b"][:, 0], w["m2_norm_w"][:, 0], s5_A[:, 0], s5_A[:, 1],
         _rows(p["m2_dt_bias"], 8)[:, 0], _rows(p["m2_A_log"], 8)[:, 0], _rows(p["m2_D"], 8)[:, 0]], axis=1)
    w.update(
        colp=jnp.broadcast_to(colp[:, :, None], (depth, N_COLP, LANE)),
        s5_Bblk_cols=jnp.moveaxis(w["s5_Bblk"].reshape(depth, W, NBC, BCW), 2, 1),
        s5_BblkT=jnp.swapaxes(w["s5_Bblk"], 1, 2),
        s5_CblkT=jnp.swapaxes(w["s5_Cblk"], 1, 2),
        s5_glu_wT=jnp.swapaxes(w["s5_glu_w"], 1, 2),
        rgam=jnp.asarray(np.broadcast_to(np.exp(log_gamma)[:, None, None], (NH, 8, LANE)), dtype=F32),
        cos_p=jnp.asarray(cos_p), sin_p=jnp.asarray(sin_p), rdec=jnp.asarray(rdec), recum=jnp.asarray(recum),
        rkdec=jnp.asarray(rkdec), retot=jnp.asarray(retot), sel=_sel_matrix(),
        bones=jnp.asarray(np.kron(np.eye(LANE // HD), np.full((HD, HD), 1.0 / HD)), dtype=BF16),
        hexp=jnp.asarray(np.kron(np.eye(LANE, NH), np.ones((1, HD))), dtype=F32),
    )
    return w


KC = 16
NKC = HD // KC
STEPS = NH * NKC


def _rotary_cols(x, cos, sin_signed):
    half = HD // 2
    parts = []
    for h in range(NH):
        parts += [x[h * HD + half:(h + 1) * HD], x[h * HD:h * HD + half]]
    return x * cos + jnp.concatenate(parts, axis=0) * sin_signed


def _expand_rows(dst, x):
    for c in range(x.shape[0]):
        dst[c] = jnp.broadcast_to(x[c:c + 1, :], (8, LANE))


def _sample_body(n_steps, x_ref, normw_ref, win_ref, wout_ref, fnw_ref, colp_ref, sbt_ref, sct_ref,
                 gluwt_ref, cw_ref, cb_ref, rgam_ref,
                 ret_in, hg_in, m2_in, s5re_in, s5im_in, buf_in,
                 y_ref, ret_out, hg_out, m2_out, s5re_out, s5im_out, buf_out,
                 xs_s, pt_s, vt_s, ot_s, mixt_s, o_s, hp_s,
                 kret_s, qret_s, khg_s, qhg_s, ahg_s, km2_s, qm2_s):
    i = pl.program_id(0)
    r = i % STEPS
    h = r // NKC
    kc = r % NKC

    def cp(r0, n=W):
        return colp_ref[r0:r0 + n, :]

    def pc(c0, w):
        return pt_s[c0:c0 + w, :]

    @pl.when(i == 0)
    def _load():
        xs_s[...] = x_ref[...]

    @pl.when(r == 0)
    def _prep():
        x = xs_s[...]
        hn = x * lax.rsqrt(jnp.mean(x * x, axis=-1, keepdims=True) + EPS) * normw_ref[...]
        proj = jnp.dot(hn.astype(BF16), win_ref[...], preferred_element_type=F32)

        xnew = proj[:, C_XBC:C_XBC + CONV_CH]
        acc = cb_ref[...] + xnew * cw_ref[CONV_K - 1:CONV_K, :]
        for t in range(CONV_K - 1):
            acc = acc + buf_in[t] * cw_ref[t:t + 1, :]
        for t in range(CONV_K - 2):
            buf_out[t] = buf_in[t + 1]
        buf_out[CONV_K - 2] = xnew
        xbc = _silu(acc)

        for t in range(P_PAD // LANE):
            c0 = t * LANE
            if C_XBC <= c0 < C_XBC + CONV_CH:
                tile = xbc[:, c0 - C_XBC:c0 - C_XBC + LANE]
            else:
                tile = proj[:, c0:c0 + LANE]
            pt_s[c0:c0 + LANE, :] = tile.T

        cos, sin = cp(R_COS), cp(R_SIN)
        _expand_rows(kret_s, _rotary_cols(pc(C_RK, W), cos, sin) * (HD ** -0.5))
        _expand_rows(qret_s, _rotary_cols(pc(C_RQ, W), cos, sin))
        vt_s[0] = pc(C_RV, W)

        fr = pc(C_GF, W)
        lb = cp(R_LB)
        logf = _log_sigmoid(fr) + jnp.log(1.0 + lb * jnp.exp(jnp.minimum(-fr, EXP_CLIP)))
        _expand_rows(ahg_s, jnp.exp(logf))
        _expand_rows(khg_s, (1.0 - lb) * jax.nn.sigmoid(-fr))
        _expand_rows(qhg_s, _silu(pc(C_GQ, W)))
        vt_s[1] = pc(C_GI, W)

        dt8 = _softplus(pc(C_DT, 8) + cp(R_DTB, 8))
        adec8 = jnp.exp(dt8 * (-jnp.exp(cp(R_ALOG, 8))))
        for hh in range(NH):
            hp_s[hh] = jnp.broadcast_to(adec8[hh:hh + 1, :], (8, LANE))
            vt_s[2, hh * HD:(hh + 1) * HD, :] = pc(C_XBC + hh * HD, HD) * dt8[hh:hh + 1, :]
        _expand_rows(km2_s, pc(C_XBC + W, 2 * HD))
        _expand_rows(qm2_s, pc(C_XBC + W + 2 * HD, 2 * HD))

        u = pc(C_SU, W)
        bu = jnp.dot(sbt_ref[...], u.astype(BF16), preferred_element_type=F32)
        hr, hi = s5re_in[...], s5im_in[...]
        ar, ai = cp(R_AR, S5N), cp(R_AI, S5N)
        nr = ar * hr - ai * hi + bu[0:S5N]
        ni = ar * hi + ai * hr + bu[S5N:2 * S5N]
        s5re_out[...] = nr
        s5im_out[...] = ni
        hcat = jnp.concatenate([nr, ni], axis=0).astype(BF16)
        sy = jnp.dot(sct_ref[...], hcat, preferred_element_type=F32) + cp(R_S5D) * u
        gy = _gelu_tanh(sy)
        glu = jnp.dot(gluwt_ref[...], gy.astype(BF16), preferred_element_type=F32) + cp(R_GLUB)
        mixt_s[2 * W:3 * W, :] = gy * jax.nn.sigmoid(glu) * _silu(pc(C_SG, W))

    @pl.when(kc == 0)
    def _zero():
        o_s[...] = jnp.zeros(o_s.shape, F32)

    hrow = pl.multiple_of(h * HD, HD)
    cbase = h * HD + kc * KC
    gbase = (h // 2) * HD + kc * KC

    def update(m, st_in, st_out, kx, qx, base, decay):
        v3 = vt_s[m, pl.ds(hrow, HD), :].reshape(HD // 8, 8, LANE)

        def body(kk, o):
            s_new = decay(kk) * st_in[kk].reshape(HD // 8, 8, LANE) + kx[base + kk] * v3
            st_out[kk] = s_new.reshape(HD, LANE)
            return o + qx[base + kk] * s_new
        o_s[m] = lax.fori_loop(0, KC, body, o_s[m], unroll=2)

    gam = rgam_ref[h]
    update(0, ret_in, ret_out, kret_s, qret_s, cbase, lambda kk: gam)
    update(1, hg_in, hg_out, khg_s, qhg_s, cbase, lambda kk: ahg_s[cbase + kk])
    adec = hp_s[h]
    update(2, m2_in, m2_out, km2_s, qm2_s, gbase, lambda kk: adec)

    @pl.when(kc == NKC - 1)
    def _head_done():
        for m in range(3):
            ot_s[m, pl.ds(hrow, HD), :] = o_s[m].reshape(HD, LANE)

    @pl.when(r == STEPS - 1)
    def _finish():
        def head_rms_cols(o):
            parts = []
            for hh in range(NH):
                seg = o[hh * HD:(hh + 1) * HD]
                parts.append(seg * lax.rsqrt(jnp.mean(seg * seg, axis=0, keepdims=True) + EPS))
            return jnp.concatenate(parts, axis=0)

        mixt_s[0:W, :] = head_rms_cols(ot_s[0]) * cp(R_RETNW) * _silu(pc(C_RG, W))
        mixt_s[W:2 * W, :] = head_rms_cols(ot_s[1]) * cp(R_HNW) * _silu(pc(C_GG, W))
        md8 = cp(R_MD, 8)
        ym = jnp.concatenate([ot_s[2, hh * HD:(hh + 1) * HD, :] + md8[hh:hh + 1, :] * pc(C_XBC + hh * HD, HD)
                              for hh in range(NH)], axis=0)
        my = ym * _silu(pc(C_MZ, W))
        mixt_s[3 * W:4 * W, :] = my * lax.rsqrt(jnp.mean(my * my, axis=0, keepdims=True) + EPS) * cp(R_MNW)
        mixed = jnp.concatenate([mixt_s[t * LANE:(t + 1) * LANE, :].T for t in range(D // LANE)], axis=1)
        xo = xs_s[...] + jnp.dot(mixed.astype(BF16), wout_ref[...], preferred_element_type=F32)
        xs_s[...] = xo

        @pl.when(i == n_steps - 1)
        def _final_norm():
            y_ref[...] = xo * lax.rsqrt(jnp.mean(xo * xo, axis=-1, keepdims=True) + EPS) * fnw_ref[...]


def _sample_step(x, w, ret, hg, m2, s5re, s5im, buf):
    depth = ret.shape[0]
    n = x.shape[0]
    n_steps = depth * STEPS
    lay = lambda i: i // STEPS

    def per_layer(a):
        nd = a.ndim
        return pl.BlockSpec((None,) + a.shape[1:], lambda i: (lay(i),) + (0,) * (nd - 1))

    st_spec = pl.BlockSpec((None, None, KC, HD, LANE),
                           lambda i: (lay(i), (i % STEPS) // NKC, i % NKC, 0, 0))
    weights = [w["norm_w"], w["w_in"], w["w_out"]]
    tables = [w["colp"], w["s5_BblkT"], w["s5_CblkT"], w["s5_glu_wT"], w["m2_conv_w"], w["m2_conv_b"]]
    in_specs = ([_const_spec(x.shape)] + [per_layer(a) for a in weights] + [_const_spec(w["final_norm_w"].shape)]
                + [per_layer(a) for a in tables] + [_const_spec(w["rgam"].shape)]
                + [st_spec, st_spec, st_spec, per_layer(s5re), per_layer(s5im), per_layer(buf)])
    out_shape = [jax.ShapeDtypeStruct((n, D), F32)] + [jax.ShapeDtypeStruct(a.shape, F32)
                                                       for a in (ret, hg, m2, s5re, s5im, buf)]
    out_specs = [_const_spec((n, D)), st_spec, st_spec, st_spec, per_layer(s5re), per_layer(s5im),
                 per_layer(buf)]
    expand = lambda c: pltpu.VMEM((c, 8, LANE), F32)
    scratch = [pltpu.VMEM((n, D), F32),
               pltpu.VMEM((P_PAD, LANE), F32),
               pltpu.VMEM((3, W, LANE), F32),
               pltpu.VMEM((3, W, LANE), F32),
               pltpu.VMEM((D, LANE), F32),
               pltpu.VMEM((3, HD // 8, 8, LANE), F32),
               pltpu.VMEM((NH, 8, LANE), F32),
               expand(W), expand(W), expand(W), expand(W), expand(W), expand(2 * HD), expand(2 * HD)]
    return pl.pallas_call(
        functools.partial(_sample_body, n_steps),
        grid=(n_steps,),
        in_specs=in_specs,
        out_specs=out_specs,
        out_shape=out_shape,
        scratch_shapes=scratch,
        compiler_params=pltpu.CompilerParams(dimension_semantics=("arbitrary",),
                                             vmem_limit_bytes=VMEM_LIMIT),
        name="sample_step",
    )(x, *weights, w["final_norm_w"], *tables, w["rgam"], ret, hg, m2, s5re, s5im, buf)


def kernel(x_prompt, x_sample, state_ret, state_hgrn, state_s5_re, state_s5_im, state_m2_ssm,
           state_m2_conv, norm_w, w_in, ret_norm_w, hgrn_lb_logits, hgrn_norm_w, s5_A_re, s5_A_im,
           s5_log_dt, s5_B_re, s5_B_im, s5_C_re, s5_C_im, s5_D, s5_glu_w, s5_glu_b, m2_conv_w,
           m2_conv_b, m2_dt_bias, m2_A_log, m2_D, m2_norm_w, w_out, final_norm_w):
    p = dict(norm_w=norm_w, w_in=w_in, ret_norm_w=ret_norm_w, hgrn_norm_w=hgrn_norm_w,
             s5_A_re=s5_A_re, s5_A_im=s5_A_im, s5_log_dt=s5_log_dt, s5_B_re=s5_B_re, s5_B_im=s5_B_im,
             s5_C_re=s5_C_re, s5_C_im=s5_C_im, s5_D=s5_D, s5_glu_w=s5_glu_w, s5_glu_b=s5_glu_b,
             m2_conv_w=m2_conv_w, m2_conv_b=m2_conv_b, m2_dt_bias=m2_dt_bias, m2_A_log=m2_A_log,
             m2_D=m2_D, m2_norm_w=m2_norm_w, w_out=w_out, final_norm_w=final_norm_w)
    depth = norm_w.shape[0]
    nbp, lp, _ = x_prompt.shape
    nbs = x_sample.shape[0]

    lb_sm = jax.nn.softmax(hgrn_lb_logits.astype(F32), axis=0)
    lb_all = jnp.clip(jnp.cumsum(lb_sm, axis=0) - lb_sm[0], 0.0, 1.0)

    w = _prepare(p, lb_all, lp)

    xp = x_prompt
    pst = []
    for l in range(depth):
        outs = _prompt_layer(xp, l, l == depth - 1, w)
        xp = outs[0]
        ret, hg, s5, m2, buf = outs[-5:]
        pst.append((ret, hg, s5[:, :S5N].reshape(nbp, S5G, S5P), s5[:, S5N:].reshape(nbp, S5G, S5P),
                    m2, buf))
    yp = xp

    seq_last = lambda a: jnp.moveaxis(a.astype(F32), 1, -1)
    ys, ret, hg, m2, s5re, s5im, buf = _sample_step(
        x_sample.reshape(nbs, D), w,
        seq_last(state_ret), seq_last(state_hgrn), seq_last(state_m2_ssm),
        seq_last(state_s5_re).reshape(depth, S5N, nbs), seq_last(state_s5_im).reshape(depth, S5N, nbs),
        jnp.swapaxes(state_m2_conv.astype(F32), 1, 2))
    seq_second = lambda a: jnp.moveaxis(a, -1, 1)

    stk = lambda i: jnp.stack([s[i] for s in pst], axis=0)
    return (yp, ys.reshape(nbs, 1, D),
            stk(0), stk(1), stk(2), stk(3), stk(4), stk(5),
            seq_second(ret), seq_second(hg), seq_second(s5re.reshape(depth, S5G, S5P, nbs)),
            seq_second(s5im.reshape(depth, S5G, S5P, nbs)), seq_second(m2), jnp.swapaxes(buf, 1, 2))
```

```python
import functools
import math

import numpy as np
import jax
import jax.numpy as jnp
from jax import lax
from jax.experimental import pallas as pl
from jax.experimental.pallas import tpu as pltpu

F32 = jnp.float32
BF16 = jnp.bfloat16

D = 1024
W = 256
NH = 4
HD = 64
S5G = 16
S5C = 16
S5P = 64
S5N = S5G * S5P
NBC = 8
BCW = 2 * S5N // NBC
CONV_CH = 512
CONV_K = 4
TB = 64
SUB = 16
NSUB = TB // SUB
EPS = 1e-6
EXP_CLIP = 60.0
ROPE_BASE = 10000.0
PAST_LEN = 16384

C_RQ, C_RK, C_RV, C_RG = 0, 256, 512, 768
C_GQ, C_GF, C_GI, C_GG = 1024, 1280, 1536, 1792
C_SU, C_SG = 2048, 2304
C_MZ, C_XBC, C_DT = 2560, 2816, 3328
P_TOTAL = 3332
PCH = 1152
NPC = 3
P_PAD = NPC * PCH
LANE = 128
VMEM_LIMIT = 56 * 1024 * 1024


def _silu(x):
    return x * jax.nn.sigmoid(x)


def _softplus(x):
    return jnp.maximum(x, 0.0) + jnp.log(1.0 + jnp.exp(-jnp.abs(x)))


def _log_sigmoid(x):
    return jnp.minimum(x, 0.0) - jnp.log(1.0 + jnp.exp(-jnp.abs(x)))


def _round_robin(gens):
    gens = list(gens)
    while gens:
        alive = []
        for g in gens:
            try:
                next(g)
                alive.append(g)
            except StopIteration:
                pass
        gens = alive


def _for_sequences(nb, parts, group):
    def body(i, c):
        built = [parts(i * group + k) for k in range(group)]
        _round_robin([g for gens, _ in built for g in gens])
        for _, finish in built:
            if finish is not None:
                finish()
        return c
    lax.fori_loop(0, nb // group, body, 0)


def _gelu_tanh(x):
    c = math.sqrt(2.0 / math.pi)
    return 0.5 * x * (1.0 + jnp.tanh(c * (x + 0.044715 * (x * x * x))))


def _dot(a, b):
    return jnp.dot(a.astype(BF16), b.astype(BF16), preferred_element_type=F32)


def _split3(x):
    hi = x.astype(BF16)
    rest = x - hi.astype(F32)
    mid = rest.astype(BF16)
    return hi, mid, (rest - mid.astype(F32)).astype(BF16)


def _select_rows(m01, x):
    m = m01.astype(BF16)
    hi, mid, lo = _split3(x)
    return (jnp.dot(m, hi, preferred_element_type=F32) + jnp.dot(m, mid, preferred_element_type=F32)
            + jnp.dot(m, lo, preferred_element_type=F32))


def _select_cols(x, m01):
    m = m01.astype(BF16)
    hi, mid, lo = _split3(x)
    return (jnp.dot(hi, m, preferred_element_type=F32) + jnp.dot(mid, m, preferred_element_type=F32)
            + jnp.dot(lo, m, preferred_element_type=F32))


def _rot_half_partner(x):
    lane = lax.broadcasted_iota(jnp.int32, x.shape, 1)
    first = (lane % HD) < (HD // 2)
    return jnp.where(first, pltpu.roll(x, LANE - HD // 2, 1), pltpu.roll(x, HD // 2, 1))


def _rotary(x, cos, sin_signed):
    parts = []
    for i in range(W // LANE):
        xi = x[:, i * LANE:(i + 1) * LANE]
        parts.append(xi * cos + _rot_half_partner(xi) * sin_signed)
    return jnp.concatenate(parts, axis=1)


def _prompt_layer_body(last, *refs):
    (x_ref, cos_ref, sin_ref, normw_ref, win_ref, wout_ref,
     rdec_ref, recum_ref, rkdec_ref, retot_ref, retnw_ref,
     lb_ref, hnw_ref, sel_ref,
     sA_ref, sB_ref, sC_ref, sD_ref, gluw_ref, glub_ref,
     cw_ref, cb_ref, dtb_ref, alog_ref, md_ref, mnw_ref, bones_ref, hexp_ref) = refs[:28]
    refs = refs[28:]
    if last:
        fnw_ref = refs[0]
        refs = refs[1:]
    act_ref = refs[0]
    refs = refs[1:]
    (ret_ref, hg_ref, s5_ref, m2_ref, m2buf_ref,
     hn_s, proj_s, mixed_s, bu_s, u_s, ub_s, cv_s, p_s, dg_s, hq_s, hk_s, cum_s, hret_s, hhg_s, hm2_s) = refs
    j = pl.program_id(0)
    nb = x_ref.shape[0]

    @pl.when(j == 0)
    def _init():
        hret_s[...] = jnp.zeros(hret_s.shape, F32)
        hhg_s[...] = jnp.zeros(hhg_s.shape, F32)
        s5_ref[...] = jnp.zeros(s5_ref.shape, F32)
        hm2_s[...] = jnp.zeros(hm2_s.shape, F32)
        cv_s[...] = jnp.zeros(cv_s.shape, F32)

    ti = lax.broadcasted_iota(jnp.int32, (TB, TB), 0)
    si = lax.broadcasted_iota(jnp.int32, (TB, TB), 1)
    causal = si <= ti
    tri_l = causal.astype(F32)
    pr = lax.broadcasted_iota(jnp.int32, (LANE, LANE), 0)
    pc_ = lax.broadcasted_iota(jnp.int32, (LANE, LANE), 1)
    pairmask = (pr // HD) == (pc_ // HD)
    t2 = lax.broadcasted_iota(jnp.int32, (TB, LANE), 0)
    l2 = lax.broadcasted_iota(jnp.int32, (TB, LANE), 1)
    causal2 = (l2 % HD) <= t2
    first_head = l2 < HD
    subdiag2 = (t2 // SUB) == ((l2 % HD) // SUB)
    srcblock = [pairmask & (((pc_ % HD) // SUB) == jb) for jb in range(NSUB - 1)]

    def dup_t(x2):
        return jnp.concatenate([x2, x2], axis=0).T

    def blockdiag2(x2):
        return jnp.where(pairmask, jnp.concatenate([x2, x2], axis=0), 0.0)

    def norm_body(b, c):
        xb = x_ref[b]
        hn = xb * lax.rsqrt(jnp.mean(xb * xb, axis=-1, keepdims=True) + EPS) * normw_ref[...]
        hn_s[pl.ds(pl.multiple_of(b * TB, TB), TB), :] = hn.astype(BF16)
        return c
    lax.fori_loop(0, nb, norm_body, 0, unroll=4)
    for c in range(NPC):
        cs = slice(c * PCH, (c + 1) * PCH)
        proj_s[:, cs] = jnp.dot(hn_s[...], win_ref[:, cs], preferred_element_type=F32)

    cos = cos_ref[...]
    sin = sin_ref[...]

    def reorder_u(b, c):
        u = proj_s[pl.ds(pl.multiple_of(b * TB, TB), TB), C_SU:C_SU + W]
        for c_ in range(W // LANE):
            u_s[c_, pl.ds(b, TB, stride=nb), :] = u[:, c_ * LANE:(c_ + 1) * LANE]
        return c
    lax.fori_loop(0, nb, reorder_u, 0)
    ub_s[...] = jnp.concatenate([u_s[c_] for c_ in range(W // LANE)], axis=1).astype(BF16)

    def s5_input_piece(cb):
        bu_s[cb] = jnp.dot(ub_s[...], sB_ref[cb], preferred_element_type=F32)
        yield

    def phase1_parts(b):
        r0 = pl.multiple_of(b * TB, TB)

        def pj(c0, w):
            return proj_s[pl.ds(r0, TB), c0:c0 + w]

        rq = _rotary(pj(C_RQ, W), cos, sin)
        rk = _rotary(pj(C_RK, W), cos, sin) * (HD ** -0.5)
        rv = pj(C_RV, W)
        rg = pj(C_RG, W)

        def ret_pair(p):
            ls = slice(p * LANE, (p + 1) * LANE)
            q2, k2, v2 = rq[:, ls], rk[:, ls], rv[:, ls]
            hb = hret_s[b, p]
            kt = dup_t(k2)
            s_raw = _dot(q2, jnp.where(pairmask, kt, 0.0))
            oi = _dot(q2, hb)
            kv = _dot(kt[:, 0:HD], v2 * rkdec_ref[p])
            yield
            o = _dot(s_raw * rdec_ref[p], blockdiag2(v2))
            yield
            o = o + oi * recum_ref[p]
            hret_s[b, p] = jnp.where(pairmask, retot_ref[p] * hb + kv, 0.0)
            ms = _dot(o * o, bones_ref[...])
            yield
            o = o * lax.rsqrt(ms + EPS) * retnw_ref[:, ls] * _silu(rg[:, ls])
            mixed_s[pl.ds(r0, TB), 0 * W + p * LANE:0 * W + (p + 1) * LANE] = o.astype(BF16)

        cv_s[b, 8:8 + TB, :] = pj(C_XBC, CONV_CH)
        acc = cb_ref[...] + cv_s[b, 5:5 + TB, :] * cw_ref[0:1, :]
        for i in range(1, CONV_K):
            acc = acc + cv_s[b, 5 + i:5 + i + TB, :] * cw_ref[i:i + 1, :]
        tail = cv_s[b, TB + 5:TB + 8, :]
        cv_s[b, 5:8, :] = tail
        m2buf_ref[b] = tail
        xbc = _silu(acc)
        xm = xbc[:, 0:W]
        bm = xbc[:, W:W + 2 * HD]
        cm = xbc[:, W + 2 * HD:W + 4 * HD]
        bm_sw = pltpu.roll(bm, HD, 1)
        cm_sw = pltpu.roll(cm, HD, 1)
        dt_b = _select_cols(_softplus(pj(C_DT, LANE) + dtb_ref[...]), hexp_ref[...])
        la_b = dt_b * (-jnp.exp(alog_ref[...]))
        cum_b = _select_rows(tri_l, la_b)
        ys = [None] * (W // LANE)

        def m2_pair(p):
            ls = slice(p * LANE, (p + 1) * LANE)
            b2 = jnp.where(first_head, bm, bm_sw) if p == 0 else jnp.where(first_head, bm_sw, bm)
            c2 = jnp.where(first_head, cm, cm_sw) if p == 0 else jnp.where(first_head, cm_sw, cm)
            x2, dt2, cum2 = xm[:, ls], dt_b[:, ls], cum_b[:, ls]
            xdt2 = x2 * dt2
            ct = dup_t(cum2)
            r2 = jnp.where(first_head, ct[0:TB], ct[HD:HD + TB])
            tot2 = cum2[TB - 1:TB, :]
            hb = hm2_s[b, p]
            bt = dup_t(b2)
            s_raw = _dot(c2, jnp.where(pairmask, bt, 0.0))
            oi = _dot(c2, hb)
            kv = _dot(bt[:, 0:HD], xdt2 * jnp.exp(tot2 - cum2))
            decay = jnp.where(causal2, jnp.exp(jnp.minimum(cum2 - r2, 0.0)), 0.0)
            yield
            o = _dot(s_raw * decay, blockdiag2(xdt2))
            yield
            hm2_s[b, p] = jnp.where(pairmask, jnp.exp(tot2) * hb + kv, 0.0)
            ys[p] = o + oi * jnp.exp(cum2) + md_ref[:, ls] * x2

        fr = pj(C_GF, W)
        lb = lb_ref[...]
        logf = _log_sigmoid(fr) + jnp.log(1.0 + lb * jnp.exp(jnp.minimum(-fr, EXP_CLIP)))
        hq = _silu(pj(C_GQ, W))
        hk = (1.0 - lb) * jax.nn.sigmoid(-fr)
        cum = _select_rows(tri_l, logf)
        hq_s[b] = hq
        hk_s[b] = hk
        cum_s[b] = cum
        t8 = lax.broadcasted_iota(jnp.int32, (8, LANE), 0)

        def diag_products(p):
            ls = slice(p * LANE, (p + 1) * LANE)
            for s_ in range(SUB):
                pieces = []
                for i in range(NSUB):
                    kb = jnp.broadcast_to(hk_s[b, i * SUB + s_:i * SUB + s_ + 1, ls], (8, LANE))
                    cb = jnp.broadcast_to(cum_s[b, i * SUB + s_:i * SUB + s_ + 1, ls], (8, LANE))
                    for half in range(SUB // 8):
                        rows = slice(i * SUB + half * 8, i * SUB + half * 8 + 8)
                        if half * 8 + 7 < s_:
                            pieces.append(jnp.zeros((8, LANE), F32))
                        elif half * 8 >= s_:
                            pieces.append(hq[rows, ls] * kb * jnp.exp(cum[rows, ls] - cb))
                        else:
                            e = jnp.exp(jnp.minimum(cum[rows, ls] - cb, 0.0))
                            pieces.append(jnp.where(t8 + half * 8 >= s_, hq[rows, ls] * kb * e, 0.0))
                pv = jnp.concatenate(pieces, axis=0)
                p_s[p, pl.ds(r0, TB), s_ * LANE:(s_ + 1) * LANE] = pv.astype(BF16)
                if s_ % 2 == 1:
                    yield

        def finish():
            my = jnp.concatenate(ys, axis=1) * _silu(pj(C_MZ, W))
            om = my * lax.rsqrt(jnp.mean(my * my, axis=-1, keepdims=True) + EPS) * mnw_ref[...]
            mixed_s[pl.ds(r0, TB), 3 * W:4 * W] = om.astype(BF16)

        gens = ([s5_input_piece(b)] + [ret_pair(p) for p in range(W // LANE)]
                + [m2_pair(p) for p in range(W // LANE)] + [diag_products(p) for p in range(W // LANE)])
        return gens, finish

    _for_sequences(nb, phase1_parts, 2)

    for p in range(W // LANE):
        dg_s[p] = jnp.dot(p_s[p], sel_ref[...], preferred_element_type=F32)

    ar = jnp.broadcast_to(sA_ref[0:1, :], (nb, S5N))
    ai = jnp.broadcast_to(sA_ref[1:2, :], (nb, S5N))

    def scan_body(t, carry):
        hr, hi = carry
        row = pl.multiple_of(t * nb, nb)
        half = NBC // 2
        nr = ar * hr - ai * hi + jnp.concatenate([bu_s[c_, pl.ds(row, nb), :] for c_ in range(half)], axis=1)
        ni = ar * hi + ai * hr + jnp.concatenate([bu_s[half + c_, pl.ds(row, nb), :] for c_ in range(half)],
                                                 axis=1)
        for c_ in range(half):
            bu_s[c_, pl.ds(row, nb), :] = nr[:, c_ * BCW:(c_ + 1) * BCW]
            bu_s[half + c_, pl.ds(row, nb), :] = ni[:, c_ * BCW:(c_ + 1) * BCW]
        return nr, ni
    hr, hi = lax.fori_loop(0, TB, scan_body, (s5_ref[:, 0:S5N], s5_ref[:, S5N:2 * S5N]))
    s5_ref[:, 0:S5N] = hr
    s5_ref[:, S5N:2 * S5N] = hi
    ch_tb = _dot(bu_s[0], sC_ref[0:BCW, :])
    for c_ in range(1, NBC):
        ch_tb = ch_tb + _dot(bu_s[c_], sC_ref[c_ * BCW:(c_ + 1) * BCW, :])
    for c_ in range(W // LANE):
        u_s[c_] = ch_tb[:, c_ * LANE:(c_ + 1) * LANE]

    def phase2_parts(b):
        r0 = pl.multiple_of(b * TB, TB)

        def pj(c0, w):
            return proj_s[pl.ds(r0, TB), c0:c0 + w]

        def s5_out():
            chs = jnp.concatenate([u_s[c_, pl.ds(b, TB, stride=nb), :] for c_ in range(W // LANE)],
                                  axis=1)
            gy = _gelu_tanh(chs + sD_ref[...] * pj(C_SU, W))
            glu = _dot(gy, gluw_ref[...])
            yield
            os5 = gy * jax.nn.sigmoid(glu + glub_ref[...]) * _silu(pj(C_SG, W))
            mixed_s[pl.ds(r0, TB), 2 * W:3 * W] = os5.astype(BF16)

        cum = cum_s[b]
        rr = jnp.concatenate(
            [jnp.zeros((SUB, W), F32)]
            + [jnp.broadcast_to(cum_s[b, i * SUB - 1:i * SUB, :], (SUB, W))
               for i in range(1, NSUB)], axis=0)
        ee = jnp.concatenate(
            [jnp.broadcast_to(cum_s[b, i * SUB + SUB - 1:i * SUB + SUB, :], (SUB, W))
             for i in range(NSUB)], axis=0)
        totc = jnp.broadcast_to(cum[TB - 1:TB, :], (LANE, W)).T
        hq = hq_s[b]
        hk = hk_s[b]
        hv = pj(C_GI, W)
        gg = pj(C_GG, W)
        qt = hq * jnp.exp(cum - rr)
        kh_ = hk * jnp.exp(ee - cum)
        qe = hq * jnp.exp(cum)
        tot = cum[TB - 1:TB, :]
        kend = kh_ * jnp.exp(tot - ee)
        trow = lax.broadcasted_iota(jnp.int32, (TB, W), 0) // SUB
        qx = []
        for jb in range(NSUB - 1):
            eb = jnp.broadcast_to(ee[jb * SUB:jb * SUB + 1, :], (TB, W))
            qx.append(jnp.where(trow > jb, qt * jnp.exp(jnp.minimum(rr - eb, 0.0)), 0.0))

        def hg_pair(p):
            ls = slice(p * LANE, (p + 1) * LANE)
            hb = hhg_s[b, p]
            hv2 = hv[:, ls]
            kt = dup_t(kh_[:, ls])
            off = _dot(qx[0][:, ls], jnp.where(srcblock[0], kt, 0.0))
            for jb in range(1, NSUB - 1):
                off = off + _dot(qx[jb][:, ls], jnp.where(srcblock[jb], kt, 0.0))
            oi = _dot(qe[:, ls], hb)
            kv = _dot(dup_t(kend[:, ls])[:, 0:HD], hv2)
            yield
            o = _dot(jnp.where(subdiag2, dg_s[p, pl.ds(r0, TB), :], 0.0) + off, blockdiag2(hv2))
            yield
            hhg_s[b, p] = jnp.where(pairmask, jnp.exp(totc[ls, :]) * hb + kv, 0.0)
            o = o + oi
            ms = _dot(o * o, bones_ref[...])
            yield
            o = o * lax.rsqrt(ms + EPS) * hnw_ref[:, ls] * _silu(gg[:, ls])
            mixed_s[pl.ds(r0, TB), 1 * W + p * LANE:1 * W + (p + 1) * LANE] = o.astype(BF16)

        return [s5_out()] + [hg_pair(p) for p in range(W // LANE)], None

    _for_sequences(nb, phase2_parts, 4)

    @pl.when(j == pl.num_programs(0) - 1)
    def _emit_states():
        for b in range(nb):
            for h in range(NH):
                p, h2 = divmod(h, 2)
                blk = (slice(h2 * HD, (h2 + 1) * HD),) * 2
                ret_ref[b, h] = hret_s[(b, p) + blk]
                hg_ref[b, h] = hhg_s[(b, p) + blk]
                m2_ref[b, h] = hm2_s[(b, p) + blk]

    res = jnp.dot(mixed_s[...], wout_ref[...], preferred_element_type=F32)
    for b in range(nb):
        xo = x_ref[b] + res[b * TB:(b + 1) * TB]
        if last:
            xo = xo * lax.rsqrt(jnp.mean(xo * xo, axis=-1, keepdims=True) + EPS) * fnw_ref[...]
        act_ref[b] = xo


def _const_spec(shape):
    nd = len(shape)
    return pl.BlockSpec(shape, lambda j: (0,) * nd)


def _prompt_layer(x, l, last, w):
    nb, L, _ = x.shape
    nblk = L // TB
    rows = nb * TB
    xspec = pl.BlockSpec((nb, TB, D), lambda j: (0, j, 0))
    tspec = pl.BlockSpec((TB, LANE), lambda j: (j, 0))

    def per_layer(a):
        nd = a.ndim
        return pl.BlockSpec((None,) + a.shape[1:], lambda j: (l,) + (0,) * (nd - 1))

    stacked = [w[k] for k in ("norm_w", "w_in", "w_out")]
    consts = [w[k] for k in ("rdec", "recum", "rkdec", "retot")]
    ret_nw = [w["ret_norm_w"], w["lb"], w["hgrn_norm_w"]]
    s5 = [w[k] for k in ("s5_A", "s5_Bblk_cols", "s5_Cblk", "s5_D", "s5_glu_w", "s5_glu_b")]
    m2 = [w[k] for k in ("m2_conv_w", "m2_conv_b", "m2_dt_bias", "m2_A_log", "m2_D", "m2_norm_w")]
    pair_consts = [w["bones"], w["hexp"]]
    args = [x, w["cos_p"], w["sin_p"]] + stacked + consts + ret_nw + [w["sel"]] + s5 + m2 + pair_consts
    in_specs = ([xspec, tspec, tspec] + [per_layer(a) for a in stacked] + [_const_spec(a.shape) for a in consts]
                + [per_layer(a) for a in ret_nw] + [_const_spec(w["sel"].shape)]
                + [per_layer(a) for a in s5 + m2] + [_const_spec(a.shape) for a in pair_consts])
    if last:
        args.append(w["final_norm_w"])
        in_specs.append(_const_spec(w["final_norm_w"].shape))
    state_shapes = [jax.ShapeDtypeStruct((nb, NH, HD, HD), F32),
                    jax.ShapeDtypeStruct((nb, NH, HD, HD), F32),
                    jax.ShapeDtypeStruct((nb, 2 * S5N), F32),
                    jax.ShapeDtypeStruct((nb, NH, HD, HD), F32),
                    jax.ShapeDtypeStruct((nb, CONV_K - 1, CONV_CH), F32)]
    out_shape = [jax.ShapeDtypeStruct((nb, L, D), F32)] + state_shapes
    out_specs = [xspec] + [_const_spec(s.shape) for s in state_shapes]
    scratch = [pltpu.VMEM((rows, D), BF16),
               pltpu.VMEM((rows, P_PAD), F32),
               pltpu.VMEM((rows, D), BF16),
               pltpu.VMEM((NBC, rows, BCW), F32),
               pltpu.VMEM((W // LANE, rows, LANE), F32),
               pltpu.VMEM((rows, W), BF16),
               pltpu.VMEM((nb, TB + 8, CONV_CH), F32),
               pltpu.VMEM((W // LANE, rows, SUB * LANE), BF16),
               pltpu.VMEM((W // LANE, rows, LANE), F32),
               pltpu.VMEM((nb, TB, W), F32),
               pltpu.VMEM((nb, TB, W), F32),
               pltpu.VMEM((nb, TB, W), F32)] + [pltpu.VMEM((nb, W // LANE, LANE, LANE), F32)] * 3
    return pl.pallas_call(
        functools.partial(_prompt_layer_body, last),
        grid=(nblk,),
        in_specs=in_specs,
        out_specs=out_specs,
        out_shape=out_shape,
        scratch_shapes=scratch,
        compiler_params=pltpu.CompilerParams(dimension_semantics=("arbitrary",),
                                             vmem_limit_bytes=VMEM_LIMIT),
        name="prompt_layer",
    )(*args)


def _rope_tables(pos):
    half = HD // 2
    inv = 1.0 / (ROPE_BASE ** (np.arange(half, dtype=np.float64) / half))
    ang = pos.astype(np.float64)[:, None] * inv[None, :]
    cos, sin = np.cos(ang), np.sin(ang)
    cos_t = np.tile(cos, (1, LANE // half))
    sin_t = np.tile(np.concatenate([-sin, sin], axis=1), (1, LANE // HD))
    return cos_t.astype(np.float32), sin_t.astype(np.float32)


def _retention_tables():
    log_gamma = np.log1p(-(2.0 ** (-5.0 - np.arange(NH, dtype=np.float64))))
    cum = np.cumsum(np.broadcast_to(log_gamma, (TB, NH)), axis=0)
    total = cum[-1]
    causal = np.tril(np.ones((TB, TB), dtype=bool))
    diff = cum[:, None, :] - cum[None, :, :]
    dec = np.where(causal[:, :, None], np.exp(np.where(causal[:, :, None], diff, 0.0)), 0.0)
    rdec = np.moveaxis(dec, 2, 0)
    recum = np.broadcast_to(np.exp(cum).T[:, :, None], (NH, TB, HD))
    rkdec = np.broadcast_to(np.exp(total[None, :] - cum).T[:, :, None], (NH, TB, HD))
    retot = np.broadcast_to(np.exp(total)[:, None, None], (NH, 1, HD))
    pair = lambda t: np.concatenate([t[0::2], t[1::2]], axis=-1).astype(np.float32)
    return pair(rdec), pair(recum), pair(rkdec), pair(retot), log_gamma


def _sel_matrix():
    sel = np.zeros((SUB, 2, HD, 2, TB), np.float32)
    for s_ in range(SUB):
        for h2 in range(2):
            sel[s_, h2, :, h2, s_::SUB] = 1.0
    return jnp.asarray(sel.reshape(SUB * LANE, LANE), dtype=BF16)


def _rows(v, width=None):
    v = v.astype(F32)
    if width is not None and v.shape[-1] < width:
        v = jnp.pad(v, ((0, 0), (0, width - v.shape[-1])))
    return v[:, None, :]


def _block_diag(blocks):
    g, r, c = blocks.shape
    eye = jnp.eye(g, dtype=blocks.dtype)
    return jnp.einsum('grc,gh->grhc', blocks, eye).reshape(g * r, g * c)


def _s5_discretise(A_re, A_im, log_dt, B_re, B_im):
    A_re, A_im = A_re.astype(F32), A_im.astype(F32)
    dt = jnp.exp(log_dt.astype(F32))[:, None]
    mag = jnp.exp(A_re * dt)
    ab_re, ab_im = mag * jnp.cos(A_im * dt), mag * jnp.sin(A_im * dt)
    nr, ni = ab_re - 1.0, ab_im
    den = A_re * A_re + A_im * A_im
    f_re = (nr * A_re + ni * A_im) / den
    f_im = (ni * A_re - nr * A_im) / den
    B_re, B_im = B_re.astype(F32), B_im.astype(F32)
    bb_re = f_re[..., None] * B_re - f_im[..., None] * B_im
    bb_im = f_re[..., None] * B_im + f_im[..., None] * B_re
    return ab_re, ab_im, bb_re, bb_im


def _s5_matrices(A_re, A_im, log_dt, B_re, B_im, C_re, C_im):
    ab_re, ab_im, bb_re, bb_im = _s5_discretise(A_re, A_im, log_dt, B_re, B_im)
    bblk = jnp.concatenate([_block_diag(jnp.swapaxes(bb_re, 1, 2)),
                            _block_diag(jnp.swapaxes(bb_im, 1, 2))], axis=1)
    cblk = jnp.concatenate([_block_diag(jnp.swapaxes(C_re.astype(F32), 1, 2)),
                            _block_diag(jnp.swapaxes(-C_im.astype(F32), 1, 2))], axis=0)
    return jnp.stack([ab_re.reshape(-1), ab_im.reshape(-1)], axis=0), bblk, cblk


R_COS, R_SIN, R_RETNW, R_LB, R_HNW, R_S5D, R_GLUB, R_MNW = [i * W for i in range(8)]
R_AR = 8 * W
R_AI = R_AR + S5N
R_DTB = R_AI + S5N
R_ALOG = R_DTB + 8
R_MD = R_ALOG + 8
N_COLP = R_MD + 8


def _prepare(p, lb_all, prompt_len):
    depth = p["norm_w"].shape[0]
    s5_A, bblk, cblk = jax.vmap(_s5_matrices)(p["s5_A_re"], p["s5_A_im"], p["s5_log_dt"], p["s5_B_re"],
                                              p["s5_B_im"], p["s5_C_re"], p["s5_C_im"])
    cos_p, sin_p = _rope_tables(np.arange(prompt_len, dtype=np.float32))
    rdec, recum, rkdec, retot, log_gamma = _retention_tables()
    cos_s, sin_s = _rope_tables(np.float32(PAST_LEN) + np.arange(1, dtype=np.float32))
    w = dict(
        norm_w=_rows(p["norm_w"]),
        w_in=jnp.pad(p["w_in"].astype(BF16), ((0, 0), (0, 0), (0, P_PAD - P_TOTAL))),
        w_out=p["w_out"].astype(BF16),
        ret_norm_w=_rows(p["ret_norm_w"]),
        lb=_rows(lb_all),
        hgrn_norm_w=_rows(p["hgrn_norm_w"]),
        s5_A=s5_A,
        s5_Bblk=bblk.astype(BF16),
        s5_Cblk=cblk.astype(BF16),
        s5_D=_rows(p["s5_D"]),
        s5_glu_w=p["s5_glu_w"].astype(BF16),
        s5_glu_b=_rows(p["s5_glu_b"]),
        m2_conv_w=p["m2_conv_w"].astype(F32),
        m2_conv_b=_rows(p["m2_conv_b"]),
        m2_dt_bias=_rows(p["m2_dt_bias"], LANE),
        m2_A_log=_rows(jnp.repeat(p["m2_A_log"], HD, axis=1)),
        m2_D=_rows(jnp.repeat(p["m2_D"], HD, axis=1)),
        m2_norm_w=_rows(p["m2_norm_w"]),
        final_norm_w=p["final_norm_w"].astype(F32).reshape(1, D),
    )
    tile2 = lambda t: jnp.asarray(np.broadcast_to(np.tile(t[0], W // LANE), (depth, W)))
    colp = jnp.concatenate(
        [tile2(cos_s), tile2(sin_s), w["ret_norm_w"][:, 0], w["lb"][:, 0], w["hgrn_norm_w"][:, 0],
         w["s5_D"][:, 0], w["s5_glu_b"][:, 0], w["m2_norm_w"][:, 0], s5_A[:, 0], s5_A[:, 1],
         _rows(p["m2_dt_bias"], 8)[:, 0], _rows(p["m2_A_log"], 8)[:, 0], _rows(p["m2_D"], 8)[:, 0]], axis=1)
    w.update(
        colp=jnp.broadcast_to(colp[:, :, None], (depth, N_COLP, LANE)),
        s5_Bblk_cols=jnp.moveaxis(w["s5_Bblk"].reshape(depth, W, NBC, BCW), 2, 1),
        s5_BblkT=jnp.swapaxes(w["s5_Bblk"], 1, 2),
        s5_CblkT=jnp.swapaxes(w["s5_Cblk"], 1, 2),
        s5_glu_wT=jnp.swapaxes(w["s5_glu_w"], 1, 2),
        rgam=jnp.asarray(np.broadcast_to(np.exp(log_gamma)[:, None, None], (NH, 8, LANE)), dtype=F32),
        cos_p=jnp.asarray(cos_p), sin_p=jnp.asarray(sin_p), rdec=jnp.asarray(rdec), recum=jnp.asarray(recum),
        rkdec=jnp.asarray(rkdec), retot=jnp.asarray(retot), sel=_sel_matrix(),
        bones=jnp.asarray(np.kron(np.eye(LANE // HD), np.full((HD, HD), 1.0 / HD)), dtype=BF16),
        hexp=jnp.asarray(np.kron(np.eye(LANE, NH), np.ones((1, HD))), dtype=F32),
    )
    return w


KC = 16
NKC = HD // KC
STEPS = NH * NKC


def _rotary_cols(x, cos, sin_signed):
    half = HD // 2
    parts = []
    for h in range(NH):
        parts += [x[h * HD + half:(h + 1) * HD], x[h * HD:h * HD + half]]
    return x * cos + jnp.concatenate(parts, axis=0) * sin_signed


def _expand_rows(dst, x):
    for c in range(x.shape[0]):
        dst[c] = jnp.broadcast_to(x[c:c + 1, :], (8, LANE))


def _sample_body(n_steps, x_ref, normw_ref, win_ref, wout_ref, fnw_ref, colp_ref, sbt_ref, sct_ref,
                 gluwt_ref, cw_ref, cb_ref, rgam_ref,
                 ret_in, hg_in, m2_in, s5re_in, s5im_in, buf_in,
                 y_ref, ret_out, hg_out, m2_out, s5re_out, s5im_out, buf_out,
                 xs_s, pt_s, vt_s, ot_s, mixt_s, o_s, hp_s,
                 kret_s, qret_s, khg_s, qhg_s, ahg_s, km2_s, qm2_s):
    i = pl.program_id(0)
    r = i % STEPS
    h = r // NKC
    kc = r % NKC

    def cp(r0, n=W):
        return colp_ref[r0:r0 + n, :]

    def pc(c0, w):
        return pt_s[c0:c0 + w, :]

    @pl.when(i == 0)
    def _load():
        xs_s[...] = x_ref[...]

    @pl.when(r == 0)
    def _prep():
        x = xs_s[...]
        hn = x * lax.rsqrt(jnp.mean(x * x, axis=-1, keepdims=True) + EPS) * normw_ref[...]
        proj = jnp.dot(hn.astype(BF16), win_ref[...], preferred_element_type=F32)

        xnew = proj[:, C_XBC:C_XBC + CONV_CH]
        acc = cb_ref[...] + xnew * cw_ref[CONV_K - 1:CONV_K, :]
        for t in range(CONV_K - 1):
            acc = acc + buf_in[t] * cw_ref[t:t + 1, :]
        for t in range(CONV_K - 2):
            buf_out[t] = buf_in[t + 1]
        buf_out[CONV_K - 2] = xnew
        xbc = _silu(acc)

        for t in range(P_PAD // LANE):
            c0 = t * LANE
            if C_XBC <= c0 < C_XBC + CONV_CH:
                tile = xbc[:, c0 - C_XBC:c0 - C_XBC + LANE]
            else:
                tile = proj[:, c0:c0 + LANE]
            pt_s[c0:c0 + LANE, :] = tile.T

        cos, sin = cp(R_COS), cp(R_SIN)
        _expand_rows(kret_s, _rotary_cols(pc(C_RK, W), cos, sin) * (HD ** -0.5))
        _expand_rows(qret_s, _rotary_cols(pc(C_RQ, W), cos, sin))
        vt_s[0] = pc(C_RV, W)

        fr = pc(C_GF, W)
        lb = cp(R_LB)
        logf = _log_sigmoid(fr) + jnp.log(1.0 + lb * jnp.exp(jnp.minimum(-fr, EXP_CLIP)))
        _expand_rows(ahg_s, jnp.exp(logf))
        _expand_rows(khg_s, (1.0 - lb) * jax.nn.sigmoid(-fr))
        _expand_rows(qhg_s, _silu(pc(C_GQ, W)))
        vt_s[1] = pc(C_GI, W)

        dt8 = _softplus(pc(C_DT, 8) + cp(R_DTB, 8))
        adec8 = jnp.exp(dt8 * (-jnp.exp(cp(R_ALOG, 8))))
        for hh in range(NH):
            hp_s[hh] = jnp.broadcast_to(adec8[hh:hh + 1, :], (8, LANE))
            vt_s[2, hh * HD:(hh + 1) * HD, :] = pc(C_XBC + hh * HD, HD) * dt8[hh:hh + 1, :]
        _expand_rows(km2_s, pc(C_XBC + W, 2 * HD))
        _expand_rows(qm2_s, pc(C_XBC + W + 2 * HD, 2 * HD))

        u = pc(C_SU, W)
        bu = jnp.dot(sbt_ref[...], u.astype(BF16), preferred_element_type=F32)
        hr, hi = s5re_in[...], s5im_in[...]
        ar, ai = cp(R_AR, S5N), cp(R_AI, S5N)
        nr = ar * hr - ai * hi + bu[0:S5N]
        ni = ar * hi + ai * hr + bu[S5N:2 * S5N]
        s5re_out[...] = nr
        s5im_out[...] = ni
        hcat = jnp.concatenate([nr, ni], axis=0).astype(BF16)
        sy = jnp.dot(sct_ref[...], hcat, preferred_element_type=F32) + cp(R_S5D) * u
        gy = _gelu_tanh(sy)
        glu = jnp.dot(gluwt_ref[...], gy.astype(BF16), preferred_element_type=F32) + cp(R_GLUB)
        mixt_s[2 * W:3 * W, :] = gy * jax.nn.sigmoid(glu) * _silu(pc(C_SG, W))

    @pl.when(kc == 0)
    def _zero():
        o_s[...] = jnp.zeros(o_s.shape, F32)

    hrow = pl.multiple_of(h * HD, HD)
    cbase = h * HD + kc * KC
    gbase = (h // 2) * HD + kc * KC

    def update(m, st_in, st_out, kx, qx, base, decay):
        v3 = vt_s[m, pl.ds(hrow, HD), :].reshape(HD // 8, 8, LANE)

        def body(kk, o):
            s_new = decay(kk) * st_in[kk].reshape(HD // 8, 8, LANE) + kx[base + kk] * v3
            st_out[kk] = s_new.reshape(HD, LANE)
            return o + qx[base + kk] * s_new
        o_s[m] = lax.fori_loop(0, KC, body, o_s[m], unroll=2)

    gam = rgam_ref[h]
    update(0, ret_in, ret_out, kret_s, qret_s, cbase, lambda kk: gam)
    update(1, hg_in, hg_out, khg_s, qhg_s, cbase, lambda kk: ahg_s[cbase + kk])
    adec = hp_s[h]
    update(2, m2_in, m2_out, km2_s, qm2_s, gbase, lambda kk: adec)

    @pl.when(kc == NKC - 1)
    def _head_done():
        for m in range(3):
            ot_s[m, pl.ds(hrow, HD), :] = o_s[m].reshape(HD, LANE)

    @pl.when(r == STEPS - 1)
    def _finish():
        def head_rms_cols(o):
            parts = []
            for hh in range(NH):
                seg = o[hh * HD:(hh + 1) * HD]
                parts.append(seg * lax.rsqrt(jnp.mean(seg * seg, axis=0, keepdims=True) + EPS))
            return jnp.concatenate(parts, axis=0)

        mixt_s[0:W, :] = head_rms_cols(ot_s[0]) * cp(R_RETNW) * _silu(pc(C_RG, W))
        mixt_s[W:2 * W, :] = head_rms_cols(ot_s[1]) * cp(R_HNW) * _silu(pc(C_GG, W))
        md8 = cp(R_MD, 8)
        ym = jnp.concatenate([ot_s[2, hh * HD:(hh + 1) * HD, :] + md8[hh:hh + 1, :] * pc(C_XBC + hh * HD, HD)
                              for hh in range(NH)], axis=0)
        my = ym * _silu(pc(C_MZ, W))
        mixt_s[3 * W:4 * W, :] = my * lax.rsqrt(jnp.mean(my * my, axis=0, keepdims=True) + EPS) * cp(R_MNW)
        mixed = jnp.concatenate([mixt_s[t * LANE:(t + 1) * LANE, :].T for t in range(D // LANE)], axis=1)
        xo = xs_s[...] + jnp.dot(mixed.astype(BF16), wout_ref[...], preferred_element_type=F32)
        xs_s[...] = xo

        @pl.when(i == n_steps - 1)
        def _final_norm():
            y_ref[...] = xo * lax.rsqrt(jnp.mean(xo * xo, axis=-1, keepdims=True) + EPS) * fnw_ref[...]


def _sample_step(x, w, ret, hg, m2, s5re, s5im, buf):
    depth = ret.shape[0]
    n = x.shape[0]
    n_steps = depth * STEPS
    lay = lambda i: i // STEPS

    def per_layer(a):
        nd = a.ndim
        return pl.BlockSpec((None,) + a.shape[1:], lambda i: (lay(i),) + (0,) * (nd - 1))

    st_spec = pl.BlockSpec((None, None, KC, HD, LANE),
                           lambda i: (lay(i), (i % STEPS) // NKC, i % NKC, 0, 0))
    weights = [w["norm_w"], w["w_in"], w["w_out"]]
    tables = [w["colp"], w["s5_BblkT"], w["s5_CblkT"], w["s5_glu_wT"], w["m2_conv_w"], w["m2_conv_b"]]
    in_specs = ([_const_spec(x.shape)] + [per_layer(a) for a in weights] + [_const_spec(w["final_norm_w"].shape)]
                + [per_layer(a) for a in tables] + [_const_spec(w["rgam"].shape)]
                + [st_spec, st_spec, st_spec, per_layer(s5re), per_layer(s5im), per_layer(buf)])
    out_shape = [jax.ShapeDtypeStruct((n, D), F32)] + [jax.ShapeDtypeStruct(a.shape, F32)
                                                       for a in (ret, hg, m2, s5re, s5im, buf)]
    out_specs = [_const_spec((n, D)), st_spec, st_spec, st_spec, per_layer(s5re), per_layer(s5im),
                 per_layer(buf)]
    expand = lambda c: pltpu.VMEM((c, 8, LANE), F32)
    scratch = [pltpu.VMEM((n, D), F32),
               pltpu.VMEM((P_PAD, LANE), F32),
               pltpu.VMEM((3, W, LANE), F32),
               pltpu.VMEM((3, W, LANE), F32),
               pltpu.VMEM((D, LANE), F32),
               pltpu.VMEM((3, HD // 8, 8, LANE), F32),
               pltpu.VMEM((NH, 8, LANE), F32),
               expand(W), expand(W), expand(W), expand(W), expand(W), expand(2 * HD), expand(2 * HD)]
    return pl.pallas_call(
        functools.partial(_sample_body, n_steps),
        grid=(n_steps,),
        in_specs=in_specs,
        out_specs=out_specs,
        out_shape=out_shape,
        scratch_shapes=scratch,
        compiler_params=pltpu.CompilerParams(dimension_semantics=("arbitrary",),
                                             vmem_limit_bytes=VMEM_LIMIT),
        name="sample_step",
    )(x, *weights, w["final_norm_w"], *tables, w["rgam"], ret, hg, m2, s5re, s5im, buf)


def kernel(x_prompt, x_sample, state_ret, state_hgrn, state_s5_re, state_s5_im, state_m2_ssm,
           state_m2_conv, norm_w, w_in, ret_norm_w, hgrn_lb_logits, hgrn_norm_w, s5_A_re, s5_A_im,
           s5_log_dt, s5_B_re, s5_B_im, s5_C_re, s5_C_im, s5_D, s5_glu_w, s5_glu_b, m2_conv_w,
           m2_conv_b, m2_dt_bias, m2_A_log, m2_D, m2_norm_w, w_out, final_norm_w):
    p = dict(norm_w=norm_w, w_in=w_in, ret_norm_w=ret_norm_w, hgrn_norm_w=hgrn_norm_w,
             s5_A_re=s5_A_re, s5_A_im=s5_A_im, s5_log_dt=s5_log_dt, s5_B_re=s5_B_re, s5_B_im=s5_B_im,
             s5_C_re=s5_C_re, s5_C_im=s5_C_im, s5_D=s5_D, s5_glu_w=s5_glu_w, s5_glu_b=s5_glu_b,
             m2_conv_w=m2_conv_w, m2_conv_b=m2_conv_b, m2_dt_bias=m2_dt_bias, m2_A_log=m2_A_log,
             m2_D=m2_D, m2_norm_w=m2_norm_w, w_out=w_out, final_norm_w=final_norm_w)
    depth = norm_w.shape[0]
    nbp, lp, _ = x_prompt.shape
    nbs = x_sample.shape[0]

    lb_sm = jax.nn.softmax(hgrn_lb_logits.astype(F32), axis=0)
    lb_all = jnp.clip(jnp.cumsum(lb_sm, axis=0) - lb_sm[0], 0.0, 1.0)

    w = _prepare(p, lb_all, lp)

    xp = x_prompt
    pst = []
    for l in range(depth):
        outs = _prompt_layer(xp, l, l == depth - 1, w)
        xp = outs[0]
        ret, hg, s5, m2, buf = outs[-5:]
        pst.append((ret, hg, s5[:, :S5N].reshape(nbp, S5G, S5P), s5[:, S5N:].reshape(nbp, S5G, S5P),
                    m2, buf))
    yp = xp

    seq_last = lambda a: jnp.moveaxis(a.astype(F32), 1, -1)
    ys, ret, hg, m2, s5re, s5im, buf = _sample_step(
        x_sample.reshape(nbs, D), w,
        seq_last(state_ret), seq_last(state_hgrn), seq_last(state_m2_ssm),
        seq_last(state_s5_re).reshape(depth, S5N, nbs), seq_last(state_s5_im).reshape(depth, S5N, nbs),
        jnp.swapaxes(state_m2_conv.astype(F32), 1, 2))
    seq_second = lambda a: jnp.moveaxis(a, -1, 1)

    stk = lambda i: jnp.stack([s[i] for s in pst], axis=0)
    return (yp, ys.reshape(nbs, 1, D),
            stk(0), stk(1), stk(2), stk(3), stk(4), stk(5),
            seq_second(ret), seq_second(hg), seq_second(s5re.reshape(depth, S5G, S5P, nbs)),
            seq_second(s5im.reshape(depth, S5G, S5P, nbs)), seq_second(m2), jnp.swapaxes(buf, 1, 2))
```

```python
import functools
import math

import numpy as np
import jax
import jax.numpy as jnp
from jax import lax
from jax.experimental import pallas as pl
from jax.experimental.pallas import tpu as pltpu

F32 = jnp.float32
BF16 = jnp.bfloat16

D = 1024
W = 256
NH = 4
HD = 64
S5G = 16
S5C = 16
S5P = 64
S5N = S5G * S5P
NBC = 8
BCW = 2 * S5N // NBC
CONV_CH = 512
CONV_K = 4
TB = 64
SUB = 16
NSUB = TB // SUB
EPS = 1e-6
EXP_CLIP = 60.0
ROPE_BASE = 10000.0
PAST_LEN = 16384

C_RQ, C_RK, C_RV, C_RG = 0, 256, 512, 768
C_GQ, C_GF, C_GI, C_GG = 1024, 1280, 1536, 1792
C_SU, C_SG = 2048, 2304
C_MZ, C_XBC, C_DT = 2560, 2816, 3328
P_TOTAL = 3332
PCH = 1024
P_PAD = 3456
LANE = 128
VMEM_LIMIT = 56 * 1024 * 1024


def _silu(x):
    return x * jax.nn.sigmoid(x)


def _softplus(x):
    return jnp.maximum(x, 0.0) + jnp.log(1.0 + jnp.exp(-jnp.abs(x)))


def _log_sigmoid(x):
    return jnp.minimum(x, 0.0) - jnp.log(1.0 + jnp.exp(-jnp.abs(x)))


def _round_robin(gens):
    gens = list(gens)
    while gens:
        alive = []
        for g in gens:
            try:
                next(g)
                alive.append(g)
            except StopIteration:
                pass
        gens = alive


def _for_sequences(nb, parts, group):
    def body(i, c):
        built = [parts(i * group + k) for k in range(group)]
        _round_robin([g for gens, _ in built for g in gens])
        for _, finish in built:
            if finish is not None:
                finish()
        return c
    lax.fori_loop(0, nb // group, body, 0)


def _gelu_tanh(x):
    c = math.sqrt(2.0 / math.pi)
    return 0.5 * x * (1.0 + jnp.tanh(c * (x + 0.044715 * (x * x * x))))


def _dot(a, b):
    return jnp.dot(a.astype(BF16), b.astype(BF16), preferred_element_type=F32)


def _split3(x):
    hi = x.astype(BF16)
    rest = x - hi.astype(F32)
    mid = rest.astype(BF16)
    return hi, mid, (rest - mid.astype(F32)).astype(BF16)


def _select_rows(m01, x):
    m = m01.astype(BF16)
    hi, mid, lo = _split3(x)
    return (jnp.dot(m, hi, preferred_element_type=F32) + jnp.dot(m, mid, preferred_element_type=F32)
            + jnp.dot(m, lo, preferred_element_type=F32))


def _select_cols(x, m01):
    m = m01.astype(BF16)
    hi, mid, lo = _split3(x)
    return (jnp.dot(hi, m, preferred_element_type=F32) + jnp.dot(mid, m, preferred_element_type=F32)
            + jnp.dot(lo, m, preferred_element_type=F32))


def _rot_half_partner(x):
    lane = lax.broadcasted_iota(jnp.int32, x.shape, 1)
    first = (lane % HD) < (HD // 2)
    return jnp.where(first, pltpu.roll(x, LANE - HD // 2, 1), pltpu.roll(x, HD // 2, 1))


def _rotary(x, cos, sin_signed):
    parts = []
    for i in range(W // LANE):
        xi = x[:, i * LANE:(i + 1) * LANE]
        parts.append(xi * cos + _rot_half_partner(xi) * sin_signed)
    return jnp.concatenate(parts, axis=1)


def _prompt_layer_body(last, *refs):
    (x_ref, cos_ref, sin_ref, normw_ref, win_ref, wout_ref,
     rdec_ref, recum_ref, rkdec_ref, retot_ref, retnw_ref,
     lb_ref, hnw_ref, sel_ref,
     sA_ref, sB_ref, sC_ref, sD_ref, gluw_ref, glub_ref,
     cw_ref, cb_ref, dtb_ref, alog_ref, md_ref, mnw_ref, bones_ref, hexp_ref) = refs[:28]
    refs = refs[28:]
    if last:
        fnw_ref = refs[0]
        refs = refs[1:]
    act_ref = refs[0]
    refs = refs[1:]
    (ret_ref, hg_ref, s5_ref, m2_ref, m2buf_ref,
     hn_s, proj_s, mixed_s, bu_s, u_s, ub_s, cv_s, p_s, dg_s, hq_s, hk_s, cum_s, hret_s, hhg_s, hm2_s) = refs
    j = pl.program_id(0)
    nb = x_ref.shape[0]

    @pl.when(j == 0)
    def _init():
        hret_s[...] = jnp.zeros(hret_s.shape, F32)
        hhg_s[...] = jnp.zeros(hhg_s.shape, F32)
        s5_ref[...] = jnp.zeros(s5_ref.shape, F32)
        hm2_s[...] = jnp.zeros(hm2_s.shape, F32)
        cv_s[...] = jnp.zeros(cv_s.shape, F32)

    ti = lax.broadcasted_iota(jnp.int32, (TB, TB), 0)
    si = lax.broadcasted_iota(jnp.int32, (TB, TB), 1)
    causal = si <= ti
    tri_l = causal.astype(F32)
    pr = lax.broadcasted_iota(jnp.int32, (LANE, LANE), 0)
    pc_ = lax.broadcasted_iota(jnp.int32, (LANE, LANE), 1)
    pairmask = (pr // HD) == (pc_ // HD)
    t2 = lax.broadcasted_iota(jnp.int32, (TB, LANE), 0)
    l2 = lax.broadcasted_iota(jnp.int32, (TB, LANE), 1)
    causal2 = (l2 % HD) <= t2
    first_head = l2 < HD
    subdiag2 = (t2 // SUB) == ((l2 % HD) // SUB)
    srcblock = [pairmask & (((pc_ % HD) // SUB) == jb) for jb in range(NSUB - 1)]

    def dup_t(x2):
        return jnp.concatenate([x2, x2], axis=0).T

    def blockdiag2(x2):
        return jnp.where(pairmask, jnp.concatenate([x2, x2], axis=0), 0.0)

    def norm_body(b, c):
        xb = x_ref[b]
        hn = xb * lax.rsqrt(jnp.mean(xb * xb, axis=-1, keepdims=True) + EPS) * normw_ref[...]
        hn_s[pl.ds(pl.multiple_of(b * TB, TB), TB), :] = hn.astype(BF16)
        return c
    lax.fori_loop(0, nb, norm_body, 0, unroll=4)
    for c0 in range(0, P_PAD, PCH):
        cs = slice(c0, min(c0 + PCH, P_PAD))
        proj_s[:, cs] = jnp.dot(hn_s[...], win_ref[:, cs], preferred_element_type=F32)

    cos = cos_ref[...]
    sin = sin_ref[...]

    def reorder_u(b, c):
        u = proj_s[pl.ds(pl.multiple_of(b * TB, TB), TB), C_SU:C_SU + W]
        for c_ in range(W // LANE):
            u_s[c_, pl.ds(b, TB, stride=nb), :] = u[:, c_ * LANE:(c_ + 1) * LANE]
        return c
    lax.fori_loop(0, nb, reorder_u, 0)
    ub_s[...] = jnp.concatenate([u_s[c_] for c_ in range(W // LANE)], axis=1).astype(BF16)

    def s5_input_piece(cb):
        bu_s[cb] = jnp.dot(ub_s[...], sB_ref[cb], preferred_element_type=F32)
        yield

    def phase1_parts(b):
        r0 = pl.multiple_of(b * TB, TB)

        def pj(c0, w):
            return proj_s[pl.ds(r0, TB), c0:c0 + w]

        rq = _rotary(pj(C_RQ, W), cos, sin)
        rk = _rotary(pj(C_RK, W), cos, sin) * (HD ** -0.5)
        rv = pj(C_RV, W)
        rg = pj(C_RG, W)

        def ret_pair(p):
            ls = slice(p * LANE, (p + 1) * LANE)
            q2, k2, v2 = rq[:, ls], rk[:, ls], rv[:, ls]
            hb = hret_s[b, p]
            kt = dup_t(k2)
            s_raw = _dot(q2, jnp.where(pairmask, kt, 0.0))
            oi = _dot(q2, hb)
            kv = _dot(kt[:, 0:HD], v2 * rkdec_ref[p])
            yield
            o = _dot(s_raw * rdec_ref[p], blockdiag2(v2))
            yield
            o = o + oi * recum_ref[p]
            hret_s[b, p] = jnp.where(pairmask, retot_ref[p] * hb + kv, 0.0)
            ms = _dot(o * o, bones_ref[...])
            yield
            o = o * lax.rsqrt(ms + EPS) * retnw_ref[:, ls] * _silu(rg[:, ls])
            mixed_s[pl.ds(r0, TB), 0 * W + p * LANE:0 * W + (p + 1) * LANE] = o.astype(BF16)

        cv_s[b, 8:8 + TB, :] = pj(C_XBC, CONV_CH)
        acc = cb_ref[...] + cv_s[b, 5:5 + TB, :] * cw_ref[0:1, :]
        for i in range(1, CONV_K):
            acc = acc + cv_s[b, 5 + i:5 + i + TB, :] * cw_ref[i:i + 1, :]
        tail = cv_s[b, TB + 5:TB + 8, :]
        cv_s[b, 5:8, :] = tail
        m2buf_ref[b] = tail
        xbc = _silu(acc)
        xm = xbc[:, 0:W]
        bm = xbc[:, W:W + 2 * HD]
        cm = xbc[:, W + 2 * HD:W + 4 * HD]
        bm_sw = pltpu.roll(bm, HD, 1)
        cm_sw = pltpu.roll(cm, HD, 1)
        dt_b = _select_cols(_softplus(pj(C_DT, LANE) + dtb_ref[...]), hexp_ref[...])
        la_b = dt_b * (-jnp.exp(alog_ref[...]))
        cum_b = _select_rows(tri_l, la_b)
        ys = [None] * (W // LANE)

        def m2_pair(p):
            ls = slice(p * LANE, (p + 1) * LANE)
            b2 = jnp.where(first_head, bm, bm_sw) if p == 0 else jnp.where(first_head, bm_sw, bm)
            c2 = jnp.where(first_head, cm, cm_sw) if p == 0 else jnp.where(first_head, cm_sw, cm)
            x2, dt2, cum2 = xm[:, ls], dt_b[:, ls], cum_b[:, ls]
            xdt2 = x2 * dt2
            ct = dup_t(cum2)
            r2 = jnp.where(first_head, ct[0:TB], ct[HD:HD + TB])
            tot2 = cum2[TB - 1:TB, :]
            hb = hm2_s[b, p]
            bt = dup_t(b2)
            s_raw = _dot(c2, jnp.where(pairmask, bt, 0.0))
            oi = _dot(c2, hb)
            kv = _dot(bt[:, 0:HD], xdt2 * jnp.exp(tot2 - cum2))
            decay = jnp.where(causal2, jnp.exp(jnp.minimum(cum2 - r2, 0.0)), 0.0)
            yield
            o = _dot(s_raw * decay, blockdiag2(xdt2))
            yield
            hm2_s[b, p] = jnp.where(pairmask, jnp.exp(tot2) * hb + kv, 0.0)
            ys[p] = o + oi * jnp.exp(cum2) + md_ref[:, ls] * x2

        fr = pj(C_GF, W)
        lb = lb_ref[...]
        logf = _log_sigmoid(fr) + jnp.log(1.0 + lb * jnp.exp(jnp.minimum(-fr, EXP_CLIP)))
        hq = _silu(pj(C_GQ, W))
        hk = (1.0 - lb) * jax.nn.sigmoid(-fr)
        cum = _select_rows(tri_l, logf)
        hq_s[b] = hq
        hk_s[b] = hk
        cum_s[b] = cum
        t8 = lax.broadcasted_iota(jnp.int32, (8, LANE), 0)

        def diag_products(p):
            ls = slice(p * LANE, (p + 1) * LANE)
            for s_ in range(SUB):
                pieces = []
                for i in range(NSUB):
                    kb = jnp.broadcast_to(hk_s[b, i * SUB + s_:i * SUB + s_ + 1, ls], (8, LANE))
                    cb = jnp.broadcast_to(cum_s[b, i * SUB + s_:i * SUB + s_ + 1, ls], (8, LANE))
                    for half in range(SUB // 8):
                        rows = slice(i * SUB + half * 8, i * SUB + half * 8 + 8)
                        if half * 8 + 7 < s_:
                            pieces.append(jnp.zeros((8, LANE), F32))
                        elif half * 8 >= s_:
                            pieces.append(hq[rows, ls] * kb * jnp.exp(cum[rows, ls] - cb))
                        else:
                            e = jnp.exp(jnp.minimum(cum[rows, ls] - cb, 0.0))
                            pieces.append(jnp.where(t8 + half * 8 >= s_, hq[rows, ls] * kb * e, 0.0))
                pv = jnp.concatenate(pieces, axis=0)
                p_s[p, pl.ds(r0, TB), s_ * LANE:(s_ + 1) * LANE] = pv.astype(BF16)
                if s_ % 2 == 1:
                    yield

        def finish():
            my = jnp.concatenate(ys, axis=1) * _silu(pj(C_MZ, W))
            om = my * lax.rsqrt(jnp.mean(my * my, axis=-1, keepdims=True) + EPS) * mnw_ref[...]
            mixed_s[pl.ds(r0, TB), 3 * W:4 * W] = om.astype(BF16)

        gens = ([s5_input_piece(b)] + [ret_pair(p) for p in range(W // LANE)]
                + [m2_pair(p) for p in range(W // LANE)] + [diag_products(p) for p in range(W // LANE)])
        return gens, finish

    _for_sequences(nb, phase1_parts, 2)

    for p in range(W // LANE):
        dg_s[p] = jnp.dot(p_s[p], sel_ref[...], preferred_element_type=F32)

    ar = jnp.broadcast_to(sA_ref[0:1, :], (nb, S5N))
    ai = jnp.broadcast_to(sA_ref[1:2, :], (nb, S5N))

    def scan_body(t, carry):
        hr, hi = carry
        row = pl.multiple_of(t * nb, nb)
        half = NBC // 2
        nr = ar * hr - ai * hi + jnp.concatenate([bu_s[c_, pl.ds(row, nb), :] for c_ in range(half)], axis=1)
        ni = ar * hi + ai * hr + jnp.concatenate([bu_s[half + c_, pl.ds(row, nb), :] for c_ in range(half)],
                                                 axis=1)
        for c_ in range(half):
            bu_s[c_, pl.ds(row, nb), :] = nr[:, c_ * BCW:(c_ + 1) * BCW]
            bu_s[half + c_, pl.ds(row, nb), :] = ni[:, c_ * BCW:(c_ + 1) * BCW]
        return nr, ni
    hr, hi = lax.fori_loop(0, TB, scan_body, (s5_ref[:, 0:S5N], s5_ref[:, S5N:2 * S5N]))
    s5_ref[:, 0:S5N] = hr
    s5_ref[:, S5N:2 * S5N] = hi
    ch_tb = _dot(bu_s[0], sC_ref[0:BCW, :])
    for c_ in range(1, NBC):
        ch_tb = ch_tb + _dot(bu_s[c_], sC_ref[c_ * BCW:(c_ + 1) * BCW, :])
    for c_ in range(W // LANE):
        u_s[c_] = ch_tb[:, c_ * LANE:(c_ + 1) * LANE]

    def phase2_parts(b):
        r0 = pl.multiple_of(b * TB, TB)

        def pj(c0, w):
            return proj_s[pl.ds(r0, TB), c0:c0 + w]

        def s5_out():
            chs = jnp.concatenate([u_s[c_, pl.ds(b, TB, stride=nb), :] for c_ in range(W // LANE)],
                                  axis=1)
            gy = _gelu_tanh(chs + sD_ref[...] * pj(C_SU, W))
            glu = _dot(gy, gluw_ref[...])
            yield
            os5 = gy * jax.nn.sigmoid(glu + glub_ref[...]) * _silu(pj(C_SG, W))
            mixed_s[pl.ds(r0, TB), 2 * W:3 * W] = os5.astype(BF16)

        cum = cum_s[b]
        rr = jnp.concatenate(
            [jnp.zeros((SUB, W), F32)]
            + [jnp.broadcast_to(cum_s[b, i * SUB - 1:i * SUB, :], (SUB, W))
               for i in range(1, NSUB)], axis=0)
        ee = jnp.concatenate(
            [jnp.broadcast_to(cum_s[b, i * SUB + SUB - 1:i * SUB + SUB, :], (SUB, W))
             for i in range(NSUB)], axis=0)
        totc = jnp.broadcast_to(cum[TB - 1:TB, :], (LANE, W)).T
        hq = hq_s[b]
        hk = hk_s[b]
        hv = pj(C_GI, W)
        gg = pj(C_GG, W)
        qt = hq * jnp.exp(cum - rr)
        kh_ = hk * jnp.exp(ee - cum)
        qe = hq * jnp.exp(cum)
        tot = cum[TB - 1:TB, :]
        kend = kh_ * jnp.exp(tot - ee)
        trow = lax.broadcasted_iota(jnp.int32, (TB, W), 0) // SUB
        qx = []
        for jb in range(NSUB - 1):
            eb = jnp.broadcast_to(ee[jb * SUB:jb * SUB + 1, :], (TB, W))
            qx.append(jnp.where(trow > jb, qt * jnp.exp(jnp.minimum(rr - eb, 0.0)), 0.0))

        def hg_pair(p):
            ls = slice(p * LANE, (p + 1) * LANE)
            hb = hhg_s[b, p]
            hv2 = hv[:, ls]
            kt = dup_t(kh_[:, ls])
            off = _dot(qx[0][:, ls], jnp.where(srcblock[0], kt, 0.0))
            for jb in range(1, NSUB - 1):
                off = off + _dot(qx[jb][:, ls], jnp.where(srcblock[jb], kt, 0.0))
            oi = _dot(qe[:, ls], hb)
            kv = _dot(dup_t(kend[:, ls])[:, 0:HD], hv2)
            yield
            o = _dot(jnp.where(subdiag2, dg_s[p, pl.ds(r0, TB), :], 0.0) + off, blockdiag2(hv2))
            yield
            hhg_s[b, p] = jnp.where(pairmask, jnp.exp(totc[ls, :]) * hb + kv, 0.0)
            o = o + oi
            ms = _dot(o * o, bones_ref[...])
            yield
            o = o * lax.rsqrt(ms + EPS) * hnw_ref[:, ls] * _silu(gg[:, ls])
            mixed_s[pl.ds(r0, TB), 1 * W + p * LANE:1 * W + (p + 1) * LANE] = o.astype(BF16)

        return [s5_out()] + [hg_pair(p) for p in range(W // LANE)], None

    _for_sequences(nb, phase2_parts, 4)

    @pl.when(j == pl.num_programs(0) - 1)
    def _emit_states():
        for b in range(nb):
            for h in range(NH):
                p, h2 = divmod(h, 2)
                blk = (slice(h2 * HD, (h2 + 1) * HD),) * 2
                ret_ref[b, h] = hret_s[(b, p) + blk]
                hg_ref[b, h] = hhg_s[(b, p) + blk]
                m2_ref[b, h] = hm2_s[(b, p) + blk]

    res = jnp.dot(mixed_s[...], wout_ref[...], preferred_element_type=F32)
    for b in range(nb):
        xo = x_ref[b] + res[b * TB:(b + 1) * TB]
        if last:
            xo = xo * lax.rsqrt(jnp.mean(xo * xo, axis=-1, keepdims=True) + EPS) * fnw_ref[...]
        act_ref[b] = xo


def _const_spec(shape):
    nd = len(shape)
    return pl.BlockSpec(shape, lambda j: (0,) * nd)


def _prompt_layer(x, l, last, w):
    nb, L, _ = x.shape
    nblk = L // TB
    rows = nb * TB
    xspec = pl.BlockSpec((nb, TB, D), lambda j: (0, j, 0))
    tspec = pl.BlockSpec((TB, LANE), lambda j: (j, 0))

    def per_layer(a):
        nd = a.ndim
        return pl.BlockSpec((None,) + a.shape[1:], lambda j: (l,) + (0,) * (nd - 1))

    stacked = [w[k] for k in ("norm_w", "w_in", "w_out")]
    consts = [w[k] for k in ("rdec", "recum", "rkdec", "retot")]
    ret_nw = [w["ret_norm_w"], w["lb"], w["hgrn_norm_w"]]
    s5 = [w[k] for k in ("s5_A", "s5_Bblk_cols", "s5_Cblk", "s5_D", "s5_glu_w", "s5_glu_b")]
    m2 = [w[k] for k in ("m2_conv_w", "m2_conv_b", "m2_dt_bias", "m2_A_log", "m2_D", "m2_norm_w")]
    pair_consts = [w["bones"], w["hexp"]]
    args = [x, w["cos_p"], w["sin_p"]] + stacked + consts + ret_nw + [w["sel"]] + s5 + m2 + pair_consts
    in_specs = ([xspec, tspec, tspec] + [per_layer(a) for a in stacked] + [_const_spec(a.shape) for a in consts]
                + [per_layer(a) for a in ret_nw] + [_const_spec(w["sel"].shape)]
                + [per_layer(a) for a in s5 + m2] + [_const_spec(a.shape) for a in pair_consts])
    if last:
        args.append(w["final_norm_w"])
        in_specs.append(_const_spec(w["final_norm_w"].shape))
    state_shapes = [jax.ShapeDtypeStruct((nb, NH, HD, HD), F32),
                    jax.ShapeDtypeStruct((nb, NH, HD, HD), F32),
                    jax.ShapeDtypeStruct((nb, 2 * S5N), F32),
                    jax.ShapeDtypeStruct((nb, NH, HD, HD), F32),
                    jax.ShapeDtypeStruct((nb, CONV_K - 1, CONV_CH), F32)]
    out_shape = [jax.ShapeDtypeStruct((nb, L, D), F32)] + state_shapes
    out_specs = [xspec] + [_const_spec(s.shape) for s in state_shapes]
    scratch = [pltpu.VMEM((rows, D), BF16),
               pltpu.VMEM((rows, P_PAD), F32),
               pltpu.VMEM((rows, D), BF16),
               pltpu.VMEM((NBC, rows, BCW), F32),
               pltpu.VMEM((W // LANE, rows, LANE), F32),
               pltpu.VMEM((rows, W), BF16),
               pltpu.VMEM((nb, TB + 8, CONV_CH), F32),
               pltpu.VMEM((W // LANE, rows, SUB * LANE), BF16),
               pltpu.VMEM((W // LANE, rows, LANE), F32),
               pltpu.VMEM((nb, TB, W), F32),
               pltpu.VMEM((nb, TB, W), F32),
               pltpu.VMEM((nb, TB, W), F32)] + [pltpu.VMEM((nb, W // LANE, LANE, LANE), F32)] * 3
    return pl.pallas_call(
        functools.partial(_prompt_layer_body, last),
        grid=(nblk,),
        in_specs=in_specs,
        out_specs=out_specs,
        out_shape=out_shape,
        scratch_shapes=scratch,
        compiler_params=pltpu.CompilerParams(dimension_semantics=("arbitrary",),
                                             vmem_limit_bytes=VMEM_LIMIT),
        name="prompt_layer",
    )(*args)


def _rope_tables(pos):
    half = HD // 2
    inv = 1.0 / (ROPE_BASE ** (np.arange(half, dtype=np.float64) / half))
    ang = pos.astype(np.float64)[:, None] * inv[None, :]
    cos, sin = np.cos(ang), np.sin(ang)
    cos_t = np.tile(cos, (1, LANE // half))
    sin_t = np.tile(np.concatenate([-sin, sin], axis=1), (1, LANE // HD))
    return cos_t.astype(np.float32), sin_t.astype(np.float32)


def _retention_tables():
    log_gamma = np.log1p(-(2.0 ** (-5.0 - np.arange(NH, dtype=np.float64))))
    cum = np.cumsum(np.broadcast_to(log_gamma, (TB, NH)), axis=0)
    total = cum[-1]
    causal = np.tril(np.ones((TB, TB), dtype=bool))
    diff = cum[:, None, :] - cum[None, :, :]
    dec = np.where(causal[:, :, None], np.exp(np.where(causal[:, :, None], diff, 0.0)), 0.0)
    rdec = np.moveaxis(dec, 2, 0)
    recum = np.broadcast_to(np.exp(cum).T[:, :, None], (NH, TB, HD))
    rkdec = np.broadcast_to(np.exp(total[None, :] - cum).T[:, :, None], (NH, TB, HD))
    retot = np.broadcast_to(np.exp(total)[:, None, None], (NH, 1, HD))
    pair = lambda t: np.concatenate([t[0::2], t[1::2]], axis=-1).astype(np.float32)
    return pair(rdec), pair(recum), pair(rkdec), pair(retot), log_gamma


def _sel_matrix():
    sel = np.zeros((SUB, 2, HD, 2, TB), np.float32)
    for s_ in range(SUB):
        for h2 in range(2):
            sel[s_, h2, :, h2, s_::SUB] = 1.0
    return jnp.asarray(sel.reshape(SUB * LANE, LANE), dtype=BF16)


def _rows(v, width=None):
    v = v.astype(F32)
    if width is not None and v.shape[-1] < width:
        v = jnp.pad(v, ((0, 0), (0, width - v.shape[-1])))
    return v[:, None, :]


def _block_diag(blocks):
    g, r, c = blocks.shape
    eye = jnp.eye(g, dtype=blocks.dtype)
    return jnp.einsum('grc,gh->grhc', blocks, eye).reshape(g * r, g * c)


def _s5_discretise(A_re, A_im, log_dt, B_re, B_im):
    A_re, A_im = A_re.astype(F32), A_im.astype(F32)
    dt = jnp.exp(log_dt.astype(F32))[:, None]
    mag = jnp.exp(A_re * dt)
    ab_re, ab_im = mag * jnp.cos(A_im * dt), mag * jnp.sin(A_im * dt)
    nr, ni = ab_re - 1.0, ab_im
    den = A_re * A_re + A_im * A_im
    f_re = (nr * A_re + ni * A_im) / den
    f_im = (ni * A_re - nr * A_im) / den
    B_re, B_im = B_re.astype(F32), B_im.astype(F32)
    bb_re = f_re[..., None] * B_re - f_im[..., None] * B_im
    bb_im = f_re[..., None] * B_im + f_im[..., None] * B_re
    return ab_re, ab_im, bb_re, bb_im


def _s5_matrices(A_re, A_im, log_dt, B_re, B_im, C_re, C_im):
    ab_re, ab_im, bb_re, bb_im = _s5_discretise(A_re, A_im, log_dt, B_re, B_im)
    bblk = jnp.concatenate([_block_diag(jnp.swapaxes(bb_re, 1, 2)),
                            _block_diag(jnp.swapaxes(bb_im, 1, 2))], axis=1)
    cblk = jnp.concatenate([_block_diag(jnp.swapaxes(C_re.astype(F32), 1, 2)),
                            _block_diag(jnp.swapaxes(-C_im.astype(F32), 1, 2))], axis=0)
    return jnp.stack([ab_re.reshape(-1), ab_im.reshape(-1)], axis=0), bblk, cblk


R_COS, R_SIN, R_RETNW, R_LB, R_HNW, R_S5D, R_GLUB, R_MNW = [i * W for i in range(8)]
R_AR = 8 * W
R_AI = R_AR + S5N
R_DTB = R_AI + S5N
R_ALOG = R_DTB + 8
R_MD = R_ALOG + 8
N_COLP = R_MD + 8


def _prepare(p, lb_all, prompt_len):
    depth = p["norm_w"].shape[0]
    s5_A, bblk, cblk = jax.vmap(_s5_matrices)(p["s5_A_re"], p["s5_A_im"], p["s5_log_dt"], p["s5_B_re"],
                                              p["s5_B_im"], p["s5_C_re"], p["s5_C_im"])
    cos_p, sin_p = _rope_tables(np.arange(prompt_len, dtype=np.float32))
    rdec, recum, rkdec, retot, log_gamma = _retention_tables()
    cos_s, sin_s = _rope_tables(np.float32(PAST_LEN) + np.arange(1, dtype=np.float32))
    w = dict(
        norm_w=_rows(p["norm_w"]),
        w_in=jnp.pad(p["w_in"].astype(BF16), ((0, 0), (0, 0), (0, P_PAD - P_TOTAL))),
        w_out=p["w_out"].astype(BF16),
        ret_norm_w=_rows(p["ret_norm_w"]),
        lb=_rows(lb_all),
        hgrn_norm_w=_rows(p["hgrn_norm_w"]),
        s5_A=s5_A,
        s5_Bblk=bblk.astype(BF16),
        s5_Cblk=cblk.astype(BF16),
        s5_D=_rows(p["s5_D"]),
        s5_glu_w=p["s5_glu_w"].astype(BF16),
        s5_glu_b=_rows(p["s5_glu_b"]),
        m2_conv_w=p["m2_conv_w"].astype(F32),
        m2_conv_b=_rows(p["m2_conv_b"]),
        m2_dt_bias=_rows(p["m2_dt_bias"], LANE),
        m2_A_log=_rows(jnp.repeat(p["m2_A_log"], HD, axis=1)),
        m2_D=_rows(jnp.repeat(p["m2_D"], HD, axis=1)),
        m2_norm_w=_rows(p["m2_norm_w"]),
        final_norm_w=p["final_norm_w"].astype(F32).reshape(1, D),
    )
    tile2 = lambda t: jnp.asarray(np.broadcast_to(np.tile(t[0], W // LANE), (depth, W)))
    colp = jnp.concatenate(
        [tile2(cos_s), tile2(sin_s), w["ret_norm_w"][:, 0], w["lb"][:, 0], w["hgrn_norm_w"][:, 0],
         w["s5_D"][:, 0], w["s5_glu_b"][:, 0], w["m2_norm_w"][:, 0], s5_A[:, 0], s5_A[:, 1],
         _rows(p["m2_dt_bias"], 8)[:, 0], _rows(p["m2_A_log"], 8)[:, 0], _rows(p["m2_D"], 8)[:, 0]], axis=1)
    w.update(
        colp=jnp.broadcast_to(colp[:, :, None], (depth, N_COLP, LANE)),
        s5_Bblk_cols=jnp.moveaxis(w["s5_Bblk"].reshape(depth, W, NBC, BCW), 2, 1),
        s5_BblkT=jnp.swapaxes(w["s5_Bblk"], 1, 2),
        s5_CblkT=jnp.swapaxes(w["s5_Cblk"], 1, 2),
        s5_glu_wT=jnp.swapaxes(w["s5_glu_w"], 1, 2),
        rgam=jnp.asarray(np.broadcast_to(np.exp(log_gamma)[:, None, None], (NH, 8, LANE)), dtype=F32),
        cos_p=jnp.asarray(cos_p), sin_p=jnp.asarray(sin_p), rdec=jnp.asarray(rdec), recum=jnp.asarray(recum),
        rkdec=jnp.asarray(rkdec), retot=jnp.asarray(retot), sel=_sel_matrix(),
        bones=jnp.asarray(np.kron(np.eye(LANE // HD), np.full((HD, HD), 1.0 / HD)), dtype=BF16),
        hexp=jnp.asarray(np.kron(np.eye(LANE, NH), np.ones((1, HD))), dtype=F32),
    )
    return w


KC = 16
NKC = HD // KC
STEPS = NH * NKC


def _rotary_cols(x, cos, sin_signed):
    half = HD // 2
    parts = []
    for h in range(NH):
        parts += [x[h * HD + half:(h + 1) * HD], x[h * HD:h * HD + half]]
    return x * cos + jnp.concatenate(parts, axis=0) * sin_signed


def _expand_rows(dst, x):
    for c in range(x.shape[0]):
        dst[c] = jnp.broadcast_to(x[c:c + 1, :], (8, LANE))


def _sample_body(n_steps, x_ref, normw_ref, win_ref, wout_ref, fnw_ref, colp_ref, sbt_ref, sct_ref,
                 gluwt_ref, cw_ref, cb_ref, rgam_ref,
                 ret_in, hg_in, m2_in, s5re_in, s5im_in, buf_in,
                 y_ref, ret_out, hg_out, m2_out, s5re_out, s5im_out, buf_out,
                 xs_s, pt_s, vt_s, ot_s, mixt_s, o_s, hp_s,
                 kret_s, qret_s, khg_s, qhg_s, ahg_s, km2_s, qm2_s):
    i = pl.program_id(0)
    r = i % STEPS
    h = r // NKC
    kc = r % NKC

    def cp(r0, n=W):
        return colp_ref[r0:r0 + n, :]

    def pc(c0, w):
        return pt_s[c0:c0 + w, :]

    @pl.when(i == 0)
    def _load():
        xs_s[...] = x_ref[...]

    @pl.when(r == 0)
    def _prep():
        x = xs_s[...]
        hn = x * lax.rsqrt(jnp.mean(x * x, axis=-1, keepdims=True) + EPS) * normw_ref[...]
        proj = jnp.dot(hn.astype(BF16), win_ref[...], preferred_element_type=F32)

        xnew = proj[:, C_XBC:C_XBC + CONV_CH]
        acc = cb_ref[...] + xnew * cw_ref[CONV_K - 1:CONV_K, :]
        for t in range(CONV_K - 1):
            acc = acc + buf_in[t] * cw_ref[t:t + 1, :]
        for t in range(CONV_K - 2):
            buf_out[t] = buf_in[t + 1]
        buf_out[CONV_K - 2] = xnew
        xbc = _silu(acc)

        for t in range(P_PAD // LANE):
            c0 = t * LANE
            if C_XBC <= c0 < C_XBC + CONV_CH:
                tile = xbc[:, c0 - C_XBC:c0 - C_XBC + LANE]
            else:
                tile = proj[:, c0:c0 + LANE]
            pt_s[c0:c0 + LANE, :] = tile.T

        cos, sin = cp(R_COS), cp(R_SIN)
        _expand_rows(kret_s, _rotary_cols(pc(C_RK, W), cos, sin) * (HD ** -0.5))
        _expand_rows(qret_s, _rotary_cols(pc(C_RQ, W), cos, sin))
        vt_s[0] = pc(C_RV, W)

        fr = pc(C_GF, W)
        lb = cp(R_LB)
        logf = _log_sigmoid(fr) + jnp.log(1.0 + lb * jnp.exp(jnp.minimum(-fr, EXP_CLIP)))
        _expand_rows(ahg_s, jnp.exp(logf))
        _expand_rows(khg_s, (1.0 - lb) * jax.nn.sigmoid(-fr))
        _expand_rows(qhg_s, _silu(pc(C_GQ, W)))
        vt_s[1] = pc(C_GI, W)

        dt8 = _softplus(pc(C_DT, 8) + cp(R_DTB, 8))
        adec8 = jnp.exp(dt8 * (-jnp.exp(cp(R_ALOG, 8))))
        for hh in range(NH):
            hp_s[hh] = jnp.broadcast_to(adec8[hh:hh + 1, :], (8, LANE))
            vt_s[2, hh * HD:(hh + 1) * HD, :] = pc(C_XBC + hh * HD, HD) * dt8[hh:hh + 1, :]
        _expand_rows(km2_s, pc(C_XBC + W, 2 * HD))
        _expand_rows(qm2_s, pc(C_XBC + W + 2 * HD, 2 * HD))

        u = pc(C_SU, W)
        bu = jnp.dot(sbt_ref[...], u.astype(BF16), preferred_element_type=F32)
        hr, hi = s5re_in[...], s5im_in[...]
        ar, ai = cp(R_AR, S5N), cp(R_AI, S5N)
        nr = ar * hr - ai * hi + bu[0:S5N]
        ni = ar * hi + ai * hr + bu[S5N:2 * S5N]
        s5re_out[...] = nr
        s5im_out[...] = ni
        hcat = jnp.concatenate([nr, ni], axis=0).astype(BF16)
        sy = jnp.dot(sct_ref[...], hcat, preferred_element_type=F32) + cp(R_S5D) * u
        gy = _gelu_tanh(sy)
        glu = jnp.dot(gluwt_ref[...], gy.astype(BF16), preferred_element_type=F32) + cp(R_GLUB)
        mixt_s[2 * W:3 * W, :] = gy * jax.nn.sigmoid(glu) * _silu(pc(C_SG, W))

    @pl.when(kc == 0)
    def _zero():
        o_s[...] = jnp.zeros(o_s.shape, F32)

    hrow = pl.multiple_of(h * HD, HD)
    cbase = h * HD + kc * KC
    gbase = (h // 2) * HD + kc * KC

    def update(m, st_in, st_out, kx, qx, base, decay):
        v3 = vt_s[m, pl.ds(hrow, HD), :].reshape(HD // 8, 8, LANE)

        def body(kk, o):
            s_new = decay(kk) * st_in[kk].reshape(HD // 8, 8, LANE) + kx[base + kk] * v3
            st_out[kk] = s_new.reshape(HD, LANE)
            return o + qx[base + kk] * s_new
        o_s[m] = lax.fori_loop(0, KC, body, o_s[m], unroll=2)

    gam = rgam_ref[h]
    update(0, ret_in, ret_out, kret_s, qret_s, cbase, lambda kk: gam)
    update(1, hg_in, hg_out, khg_s, qhg_s, cbase, lambda kk: ahg_s[cbase + kk])
    adec = hp_s[h]
    update(2, m2_in, m2_out, km2_s, qm2_s, gbase, lambda kk: adec)

    @pl.when(kc == NKC - 1)
    def _head_done():
        for m in range(3):
            ot_s[m, pl.ds(hrow, HD), :] = o_s[m].reshape(HD, LANE)

    @pl.when(r == STEPS - 1)
    def _finish():
        def head_rms_cols(o):
            parts = []
            for hh in range(NH):
                seg = o[hh * HD:(hh + 1) * HD]
                parts.append(seg * lax.rsqrt(jnp.mean(seg * seg, axis=0, keepdims=True) + EPS))
            return jnp.concatenate(parts, axis=0)

        mixt_s[0:W, :] = head_rms_cols(ot_s[0]) * cp(R_RETNW) * _silu(pc(C_RG, W))
        mixt_s[W:2 * W, :] = head_rms_cols(ot_s[1]) * cp(R_HNW) * _silu(pc(C_GG, W))
        md8 = cp(R_MD, 8)
        ym = jnp.concatenate([ot_s[2, hh * HD:(hh + 1) * HD, :] + md8[hh:hh + 1, :] * pc(C_XBC + hh * HD, HD)
                              for hh in range(NH)], axis=0)
        my = ym * _silu(pc(C_MZ, W))
        mixt_s[3 * W:4 * W, :] = my * lax.rsqrt(jnp.mean(my * my, axis=0, keepdims=True) + EPS) * cp(R_MNW)
        mixed = jnp.concatenate([mixt_s[t * LANE:(t + 1) * LANE, :].T for t in range(D // LANE)], axis=1)
        xo = xs_s[...] + jnp.dot(mixed.astype(BF16), wout_ref[...], preferred_element_type=F32)
        xs_s[...] = xo

        @pl.when(i == n_steps - 1)
        def _final_norm():
            y_ref[...] = xo * lax.rsqrt(jnp.mean(xo * xo, axis=-1, keepdims=True) + EPS) * fnw_ref[...]


def _sample_step(x, w, ret, hg, m2, s5re, s5im, buf):
    depth = ret.shape[0]
    n = x.shape[0]
    n_steps = depth * STEPS
    lay = lambda i: i // STEPS

    def per_layer(a):
        nd = a.ndim
        return pl.BlockSpec((None,) + a.shape[1:], lambda i: (lay(i),) + (0,) * (nd - 1))

    st_spec = pl.BlockSpec((None, None, KC, HD, LANE),
                           lambda i: (lay(i), (i % STEPS) // NKC, i % NKC, 0, 0))
    weights = [w["norm_w"], w["w_in"], w["w_out"]]
    tables = [w["colp"], w["s5_BblkT"], w["s5_CblkT"], w["s5_glu_wT"], w["m2_conv_w"], w["m2_conv_b"]]
    in_specs = ([_const_spec(x.shape)] + [per_layer(a) for a in weights] + [_const_spec(w["final_norm_w"].shape)]
                + [per_layer(a) for a in tables] + [_const_spec(w["rgam"].shape)]
                + [st_spec, st_spec, st_spec, per_layer(s5re), per_layer(s5im), per_layer(buf)])
    out_shape = [jax.ShapeDtypeStruct((n, D), F32)] + [jax.ShapeDtypeStruct(a.shape, F32)
                                                       for a in (ret, hg, m2, s5re, s5im, buf)]
    out_specs = [_const_spec((n, D)), st_spec, st_spec, st_spec, per_layer(s5re), per_layer(s5im),
                 per_layer(buf)]
    expand = lambda c: pltpu.VMEM((c, 8, LANE), F32)
    scratch = [pltpu.VMEM((n, D), F32),
               pltpu.VMEM((P_PAD, LANE), F32),
               pltpu.VMEM((3, W, LANE), F32),
               pltpu.VMEM((3, W, LANE), F32),
               pltpu.VMEM((D, LANE), F32),
               pltpu.VMEM((3, HD // 8, 8, LANE), F32),
               pltpu.VMEM((NH, 8, LANE), F32),
               expand(W), expand(W), expand(W), expand(W), expand(W), expand(2 * HD), expand(2 * HD)]
    return pl.pallas_call(
        functools.partial(_sample_body, n_steps),
        grid=(n_steps,),
        in_specs=in_specs,
        out_specs=out_specs,
        out_shape=out_shape,
        scratch_shapes=scratch,
        compiler_params=pltpu.CompilerParams(dimension_semantics=("arbitrary",),
                                             vmem_limit_bytes=VMEM_LIMIT),
        name="sample_step",
    )(x, *weights, w["final_norm_w"], *tables, w["rgam"], ret, hg, m2, s5re, s5im, buf)


def kernel(x_prompt, x_sample, state_ret, state_hgrn, state_s5_re, state_s5_im, state_m2_ssm,
           state_m2_conv, norm_w, w_in, ret_norm_w, hgrn_lb_logits, hgrn_norm_w, s5_A_re, s5_A_im,
           s5_log_dt, s5_B_re, s5_B_im, s5_C_re, s5_C_im, s5_D, s5_glu_w, s5_glu_b, m2_conv_w,
           m2_conv_b, m2_dt_bias, m2_A_log, m2_D, m2_norm_w, w_out, final_norm_w):
    p = dict(norm_w=norm_w, w_in=w_in, ret_norm_w=ret_norm_w, hgrn_norm_w=hgrn_norm_w,
             s5_A_re=s5_A_re, s5_A_im=s5_A_im, s5_log_dt=s5_log_dt, s5_B_re=s5_B_re, s5_B_im=s5_B_im,
             s5_C_re=s5_C_re, s5_C_im=s5_C_im, s5_D=s5_D, s5_glu_w=s5_glu_w, s5_glu_b=s5_glu_b,
             m2_conv_w=m2_conv_w, m2_conv_b=m2_conv_b, m2_dt_bias=m2_dt_bias, m2_A_log=m2_A_log,
             m2_D=m2_D, m2_norm_w=m2_norm_w, w_out=w_out, final_norm_w=final_norm_w)
    depth = norm_w.shape[0]
    nbp, lp, _ = x_prompt.shape
    nbs = x_sample.shape[0]

    lb_sm = jax.nn.softmax(hgrn_lb_logits.astype(F32), axis=0)
    lb_all = jnp.clip(jnp.cumsum(lb_sm, axis=0) - lb_sm[0], 0.0, 1.0)

    w = _prepare(p, lb_all, lp)

    xp = x_prompt
    pst = []
    for l in range(depth):
        outs = _prompt_layer(xp, l, l == depth - 1, w)
        xp = outs[0]
        ret, hg, s5, m2, buf = outs[-5:]
        pst.append((ret, hg, s5[:, :S5N].reshape(nbp, S5G, S5P), s5[:, S5N:].reshape(nbp, S5G, S5P),
                    m2, buf))
    yp = xp

    seq_last = lambda a: jnp.moveaxis(a.astype(F32), 1, -1)
    ys, ret, hg, m2, s5re, s5im, buf = _sample_step(
        x_sample.reshape(nbs, D), w,
        seq_last(state_ret), seq_last(state_hgrn), seq_last(state_m2_ssm),
        seq_last(state_s5_re).reshape(depth, S5N, nbs), seq_last(state_s5_im).reshape(depth, S5N, nbs),
        jnp.swapaxes(state_m2_conv.astype(F32), 1, 2))
    seq_second = lambda a: jnp.moveaxis(a, -1, 1)

    stk = lambda i: jnp.stack([s[i] for s in pst], axis=0)
    return (yp, ys.reshape(nbs, 1, D),
            stk(0), stk(1), stk(2), stk(3), stk(4), stk(5),
            seq_second(ret), seq_second(hg), seq_second(s5re.reshape(depth, S5G, S5P, nbs)),
            seq_second(s5im.reshape(depth, S5G, S5P, nbs)), seq_second(m2), jnp.swapaxes(buf, 1, 2))
```

```python
import functools
import math

import numpy as np
import jax
import jax.numpy as jnp
from jax import lax
from jax.experimental import pallas as pl
from jax.experimental.pallas import tpu as pltpu

F32 = jnp.float32
BF16 = jnp.bfloat16

D = 1024
W = 256
NH = 4
HD = 64
S5G = 16
S5C = 16
S5P = 64
S5N = S5G * S5P
NBC = 8
BCW = 2 * S5N // NBC
CONV_CH = 512
CONV_K = 4
TB = 64
SUB = 16
NSUB = TB // SUB
EPS = 1e-6
EXP_CLIP = 60.0
ROPE_BASE = 10000.0
PAST_LEN = 16384

C_RQ, C_RK, C_RV, C_RG = 0, 256, 512, 768
C_GQ, C_GF, C_GI, C_GG = 1024, 1280, 1536, 1792
C_SU, C_SG = 2048, 2304
C_MZ, C_XBC, C_DT = 2560, 2816, 3328
P_TOTAL = 3332
PCH = 1024
P_PAD = 3456
LANE = 128
VMEM_LIMIT = 56 * 1024 * 1024


def _silu(x):
    return x * jax.nn.sigmoid(x)


def _softplus(x):
    return jnp.maximum(x, 0.0) + jnp.log(1.0 + jnp.exp(-jnp.abs(x)))


def _log_sigmoid(x):
    return jnp.minimum(x, 0.0) - jnp.log(1.0 + jnp.exp(-jnp.abs(x)))


def _round_robin(gens):
    gens = list(gens)
    while gens:
        alive = []
        for g in gens:
            try:
                next(g)
                alive.append(g)
            except StopIteration:
                pass
        gens = alive


def _for_sequences(nb, parts, group):
    def body(i, c):
        built = [parts(i * group + k) for k in range(group)]
        _round_robin([g for gens, _ in built for g in gens])
        for _, finish in built:
            if finish is not None:
                finish()
        return c
    lax.fori_loop(0, nb // group, body, 0)


def _gelu_tanh(x):
    c = math.sqrt(2.0 / math.pi)
    return 0.5 * x * (1.0 + jnp.tanh(c * (x + 0.044715 * (x * x * x))))


def _dot(a, b):
    return jnp.dot(a.astype(BF16), b.astype(BF16), preferred_element_type=F32)


def _split3(x):
    hi = x.astype(BF16)
    rest = x - hi.astype(F32)
    mid = rest.astype(BF16)
    return hi, mid, (rest - mid.astype(F32)).astype(BF16)


def _select_rows(m01, x):
    m = m01.astype(BF16)
    hi, mid, lo = _split3(x)
    return (jnp.dot(m, hi, preferred_element_type=F32) + jnp.dot(m, mid, preferred_element_type=F32)
            + jnp.dot(m, lo, preferred_element_type=F32))


def _select_cols(x, m01):
    m = m01.astype(BF16)
    hi, mid, lo = _split3(x)
    return (jnp.dot(hi, m, preferred_element_type=F32) + jnp.dot(mid, m, preferred_element_type=F32)
            + jnp.dot(lo, m, preferred_element_type=F32))


def _rot_half_partner(x):
    lane = lax.broadcasted_iota(jnp.int32, x.shape, 1)
    first = (lane % HD) < (HD // 2)
    return jnp.where(first, pltpu.roll(x, LANE - HD // 2, 1), pltpu.roll(x, HD // 2, 1))


def _rotary(x, cos, sin_signed):
    parts = []
    for i in range(W // LANE):
        xi = x[:, i * LANE:(i + 1) * LANE]
        parts.append(xi * cos + _rot_half_partner(xi) * sin_signed)
    return jnp.concatenate(parts, axis=1)


def _prompt_layer_body(last, *refs):
    (x_ref, cos_ref, sin_ref, normw_ref, win_ref, wout_ref,
     rdec_ref, recum_ref, rkdec_ref, retot_ref, retnw_ref,
     lb_ref, hnw_ref, sel_ref,
     sA_ref, sB_ref, sC_ref, sD_ref, gluw_ref, glub_ref,
     cw_ref, cb_ref, dtb_ref, alog_ref, md_ref, mnw_ref, bones_ref, hexp_ref) = refs[:28]
    refs = refs[28:]
    if last:
        fnw_ref = refs[0]
        refs = refs[1:]
    act_ref = refs[0]
    refs = refs[1:]
    (ret_ref, hg_ref, s5_ref, m2_ref, m2buf_ref,
     hn_s, proj_s, mixed_s, bu_s, u_s, ub_s, cv_s, p_s, dg_s, hq_s, hk_s, cum_s, hret_s, hhg_s, hm2_s) = refs
    j = pl.program_id(0)
    nb = x_ref.shape[0]

    @pl.when(j == 0)
    def _init():
        hret_s[...] = jnp.zeros(hret_s.shape, F32)
        hhg_s[...] = jnp.zeros(hhg_s.shape, F32)
        s5_ref[...] = jnp.zeros(s5_ref.shape, F32)
        hm2_s[...] = jnp.zeros(hm2_s.shape, F32)
        cv_s[...] = jnp.zeros(cv_s.shape, F32)

    ti = lax.broadcasted_iota(jnp.int32, (TB, TB), 0)
    si = lax.broadcasted_iota(jnp.int32, (TB, TB), 1)
    causal = si <= ti
    tri_l = causal.astype(F32)
    pr = lax.broadcasted_iota(jnp.int32, (LANE, LANE), 0)
    pc_ = lax.broadcasted_iota(jnp.int32, (LANE, LANE), 1)
    pairmask = (pr // HD) == (pc_ // HD)
    t2 = lax.broadcasted_iota(jnp.int32, (TB, LANE), 0)
    l2 = lax.broadcasted_iota(jnp.int32, (TB, LANE), 1)
    causal2 = (l2 % HD) <= t2
    first_head = l2 < HD
    subdiag2 = (t2 // SUB) == ((l2 % HD) // SUB)
    srcblock = [pairmask & (((pc_ % HD) // SUB) == jb) for jb in range(NSUB - 1)]

    def dup_t(x2):
        return jnp.concatenate([x2, x2], axis=0).T

    def blockdiag2(x2):
        return jnp.where(pairmask, jnp.concatenate([x2, x2], axis=0), 0.0)

    def norm_body(b, c):
        xb = x_ref[b]
        hn = xb * lax.rsqrt(jnp.mean(xb * xb, axis=-1, keepdims=True) + EPS) * normw_ref[...]
        hn_s[pl.ds(pl.multiple_of(b * TB, TB), TB), :] = hn.astype(BF16)
        return c
    lax.fori_loop(0, nb, norm_body, 0, unroll=4)
    for c0 in range(0, P_PAD, PCH):
        cs = slice(c0, min(c0 + PCH, P_PAD))
        proj_s[:, cs] = jnp.dot(hn_s[...], win_ref[:, cs], preferred_element_type=F32)

    cos = cos_ref[...]
    sin = sin_ref[...]

    def reorder_u(b, c):
        u = proj_s[pl.ds(pl.multiple_of(b * TB, TB), TB), C_SU:C_SU + W]
        for c_ in range(W // LANE):
            u_s[c_, pl.ds(b, TB, stride=nb), :] = u[:, c_ * LANE:(c_ + 1) * LANE]
        return c
    lax.fori_loop(0, nb, reorder_u, 0)
    ub_s[...] = jnp.concatenate([u_s[c_] for c_ in range(W // LANE)], axis=1).astype(BF16)

    def s5_input_piece(cb):
        bu_s[cb] = jnp.dot(ub_s[...], sB_ref[cb], preferred_element_type=F32)
        yield

    def phase1_parts(b):
        r0 = pl.multiple_of(b * TB, TB)

        def pj(c0, w):
            return proj_s[pl.ds(r0, TB), c0:c0 + w]

        rq = _rotary(pj(C_RQ, W), cos, sin)
        rk = _rotary(pj(C_RK, W), cos, sin) * (HD ** -0.5)
        rv = pj(C_RV, W)
        rg = pj(C_RG, W)

        def ret_pair(p):
            ls = slice(p * LANE, (p + 1) * LANE)
            q2, k2, v2 = rq[:, ls], rk[:, ls], rv[:, ls]
            hb = hret_s[b, p]
            kt = dup_t(k2)
            s_raw = _dot(q2, jnp.where(pairmask, kt, 0.0))
            oi = _dot(q2, hb)
            kv = _dot(kt[:, 0:HD], v2 * rkdec_ref[p])
            yield
            o = _dot(s_raw * rdec_ref[p], blockdiag2(v2))
            yield
            o = o + oi * recum_ref[p]
            hret_s[b, p] = jnp.where(pairmask, retot_ref[p] * hb + kv, 0.0)
            ms = _dot(o * o, bones_ref[...])
            yield
            o = o * lax.rsqrt(ms + EPS) * retnw_ref[:, ls] * _silu(rg[:, ls])
            mixed_s[pl.ds(r0, TB), 0 * W + p * LANE:0 * W + (p + 1) * LANE] = o.astype(BF16)

        cv_s[b, 8:8 + TB, :] = pj(C_XBC, CONV_CH)
        acc = cb_ref[...] + cv_s[b, 5:5 + TB, :] * cw_ref[0:1, :]
        for i in range(1, CONV_K):
            acc = acc + cv_s[b, 5 + i:5 + i + TB, :] * cw_ref[i:i + 1, :]
        tail = cv_s[b, TB + 5:TB + 8, :]
        cv_s[b, 5:8, :] = tail
        m2buf_ref[b] = tail
        xbc = _silu(acc)
        xm = xbc[:, 0:W]
        bm = xbc[:, W:W + 2 * HD]
        cm = xbc[:, W + 2 * HD:W + 4 * HD]
        bm_sw = pltpu.roll(bm, HD, 1)
        cm_sw = pltpu.roll(cm, HD, 1)
        dt_b = _select_cols(_softplus(pj(C_DT, LANE) + dtb_ref[...]), hexp_ref[...])
        la_b = dt_b * (-jnp.exp(alog_ref[...]))
        cum_b = _select_rows(tri_l, la_b)
        ys = [None] * (W // LANE)

        def m2_pair(p):
            ls = slice(p * LANE, (p + 1) * LANE)
            b2 = jnp.where(first_head, bm, bm_sw) if p == 0 else jnp.where(first_head, bm_sw, bm)
            c2 = jnp.where(first_head, cm, cm_sw) if p == 0 else jnp.where(first_head, cm_sw, cm)
            x2, dt2, cum2 = xm[:, ls], dt_b[:, ls], cum_b[:, ls]
            xdt2 = x2 * dt2
            ct = dup_t(cum2)
            r2 = jnp.where(first_head, ct[0:TB], ct[HD:HD + TB])
            tot2 = cum2[TB - 1:TB, :]
            hb = hm2_s[b, p]
            bt = dup_t(b2)
            s_raw = _dot(c2, jnp.where(pairmask, bt, 0.0))
            oi = _dot(c2, hb)
            kv = _dot(bt[:, 0:HD], xdt2 * jnp.exp(tot2 - cum2))
            decay = jnp.where(causal2, jnp.exp(jnp.minimum(cum2 - r2, 0.0)), 0.0)
            yield
            o = _dot(s_raw * decay, blockdiag2(xdt2))
            yield
            hm2_s[b, p] = jnp.where(pairmask, jnp.exp(tot2) * hb + kv, 0.0)
            ys[p] = o + oi * jnp.exp(cum2) + md_ref[:, ls] * x2

        fr = pj(C_GF, W)
        lb = lb_ref[...]
        logf = _log_sigmoid(fr) + jnp.log(1.0 + lb * jnp.exp(jnp.minimum(-fr, EXP_CLIP)))
        hq = _silu(pj(C_GQ, W))
        hk = (1.0 - lb) * jax.nn.sigmoid(-fr)
        cum = _select_rows(tri_l, logf)
        hq_s[b] = hq
        hk_s[b] = hk
        cum_s[b] = cum
        t8 = lax.broadcasted_iota(jnp.int32, (8, LANE), 0)

        def diag_products(p):
            ls = slice(p * LANE, (p + 1) * LANE)
            for s_ in range(SUB):
                pieces = []
                for i in range(NSUB):
                    kb = jnp.broadcast_to(hk_s[b, i * SUB + s_:i * SUB + s_ + 1, ls], (8, LANE))
                    cb = jnp.broadcast_to(cum_s[b, i * SUB + s_:i * SUB + s_ + 1, ls], (8, LANE))
                    for half in range(SUB // 8):
                        rows = slice(i * SUB + half * 8, i * SUB + half * 8 + 8)
                        if half * 8 + 7 < s_:
                            pieces.append(jnp.zeros((8, LANE), F32))
                        elif half * 8 >= s_:
                            pieces.append(hq[rows, ls] * kb * jnp.exp(cum[rows, ls] - cb))
                        else:
                            e = jnp.exp(jnp.minimum(cum[rows, ls] - cb, 0.0))
                            pieces.append(jnp.where(t8 + half * 8 >= s_, hq[rows, ls] * kb * e, 0.0))
                pv = jnp.concatenate(pieces, axis=0)
                p_s[p, pl.ds(r0, TB), s_ * LANE:(s_ + 1) * LANE] = pv.astype(BF16)
                if s_ % 2 == 1:
                    yield

        def finish():
            my = jnp.concatenate(ys, axis=1) * _silu(pj(C_MZ, W))
            om = my * lax.rsqrt(jnp.mean(my * my, axis=-1, keepdims=True) + EPS) * mnw_ref[...]
            mixed_s[pl.ds(r0, TB), 3 * W:4 * W] = om.astype(BF16)

        gens = ([s5_input_piece(b)] + [ret_pair(p) for p in range(W // LANE)]
                + [m2_pair(p) for p in range(W // LANE)] + [diag_products(p) for p in range(W // LANE)])
        return gens, finish

    _for_sequences(nb, phase1_parts, 2)

    half_rows = nb * TB // 2
    for p in range(W // LANE):
        for r_ in range(2):
            rs = slice(r_ * half_rows, (r_ + 1) * half_rows)
            dg_s[p, rs, :] = jnp.dot(p_s[p, rs, :], sel_ref[...], preferred_element_type=F32)

    ar = jnp.broadcast_to(sA_ref[0:1, :], (nb, S5N))
    ai = jnp.broadcast_to(sA_ref[1:2, :], (nb, S5N))

    def scan_body(t, carry):
        hr, hi = carry
        row = pl.multiple_of(t * nb, nb)
        half = NBC // 2
        nr = ar * hr - ai * hi + jnp.concatenate([bu_s[c_, pl.ds(row, nb), :] for c_ in range(half)], axis=1)
        ni = ar * hi + ai * hr + jnp.concatenate([bu_s[half + c_, pl.ds(row, nb), :] for c_ in range(half)],
                                                 axis=1)
        for c_ in range(half):
            bu_s[c_, pl.ds(row, nb), :] = nr[:, c_ * BCW:(c_ + 1) * BCW]
            bu_s[half + c_, pl.ds(row, nb), :] = ni[:, c_ * BCW:(c_ + 1) * BCW]
        return nr, ni
    hr, hi = lax.fori_loop(0, TB, scan_body, (s5_ref[:, 0:S5N], s5_ref[:, S5N:2 * S5N]))
    s5_ref[:, 0:S5N] = hr
    s5_ref[:, S5N:2 * S5N] = hi
    ch_tb = _dot(bu_s[0], sC_ref[0:BCW, :])
    for c_ in range(1, NBC):
        ch_tb = ch_tb + _dot(bu_s[c_], sC_ref[c_ * BCW:(c_ + 1) * BCW, :])
    for c_ in range(W // LANE):
        u_s[c_] = ch_tb[:, c_ * LANE:(c_ + 1) * LANE]

    def phase2_parts(b):
        r0 = pl.multiple_of(b * TB, TB)

        def pj(c0, w):
            return proj_s[pl.ds(r0, TB), c0:c0 + w]

        def s5_out():
            chs = jnp.concatenate([u_s[c_, pl.ds(b, TB, stride=nb), :] for c_ in range(W // LANE)],
                                  axis=1)
            gy = _gelu_tanh(chs + sD_ref[...] * pj(C_SU, W))
            glu = _dot(gy, gluw_ref[...])
            yield
            os5 = gy * jax.nn.sigmoid(glu + glub_ref[...]) * _silu(pj(C_SG, W))
            mixed_s[pl.ds(r0, TB), 2 * W:3 * W] = os5.astype(BF16)

        cum = cum_s[b]
        rr = jnp.concatenate(
            [jnp.zeros((SUB, W), F32)]
            + [jnp.broadcast_to(cum_s[b, i * SUB - 1:i * SUB, :], (SUB, W))
               for i in range(1, NSUB)], axis=0)
        ee = jnp.concatenate(
            [jnp.broadcast_to(cum_s[b, i * SUB + SUB - 1:i * SUB + SUB, :], (SUB, W))
             for i in range(NSUB)], axis=0)
        totc = jnp.broadcast_to(cum[TB - 1:TB, :], (LANE, W)).T
        hq = hq_s[b]
        hk = hk_s[b]
        hv = pj(C_GI, W)
        gg = pj(C_GG, W)
        qt = hq * jnp.exp(cum - rr)
        kh_ = hk * jnp.exp(ee - cum)
        qe = hq * jnp.exp(cum)
        tot = cum[TB - 1:TB, :]
        kend = kh_ * jnp.exp(tot - ee)
        trow = lax.broadcasted_iota(jnp.int32, (TB, W), 0) // SUB
        qx = []
        for jb in range(NSUB - 1):
            eb = jnp.broadcast_to(ee[jb * SUB:jb * SUB + 1, :], (TB, W))
            qx.append(jnp.where(trow > jb, qt * jnp.exp(jnp.minimum(rr - eb, 0.0)), 0.0))

        def hg_pair(p):
            ls = slice(p * LANE, (p + 1) * LANE)
            hb = hhg_s[b, p]
            hv2 = hv[:, ls]
            kt = dup_t(kh_[:, ls])
            off = _dot(qx[0][:, ls], jnp.where(srcblock[0], kt, 0.0))
            for jb in range(1, NSUB - 1):
                off = off + _dot(qx[jb][:, ls], jnp.where(srcblock[jb], kt, 0.0))
            oi = _dot(qe[:, ls], hb)
            kv = _dot(dup_t(kend[:, ls])[:, 0:HD], hv2)
            yield
            o = _dot(jnp.where(subdiag2, dg_s[p, pl.ds(r0, TB), :], 0.0) + off, blockdiag2(hv2))
            yield
            hhg_s[b, p] = jnp.where(pairmask, jnp.exp(totc[ls, :]) * hb + kv, 0.0)
            o = o + oi
            ms = _dot(o * o, bones_ref[...])
            yield
            o = o * lax.rsqrt(ms + EPS) * hnw_ref[:, ls] * _silu(gg[:, ls])
            mixed_s[pl.ds(r0, TB), 1 * W + p * LANE:1 * W + (p + 1) * LANE] = o.astype(BF16)

        return [s5_out()] + [hg_pair(p) for p in range(W // LANE)], None

    _for_sequences(nb, phase2_parts, 4)

    @pl.when(j == pl.num_programs(0) - 1)
    def _emit_states():
        for b in range(nb):
            for h in range(NH):
                p, h2 = divmod(h, 2)
                blk = (slice(h2 * HD, (h2 + 1) * HD),) * 2
                ret_ref[b, h] = hret_s[(b, p) + blk]
                hg_ref[b, h] = hhg_s[(b, p) + blk]
                m2_ref[b, h] = hm2_s[(b, p) + blk]

    res = jnp.dot(mixed_s[...], wout_ref[...], preferred_element_type=F32)
    for b in range(nb):
        xo = x_ref[b] + res[b * TB:(b + 1) * TB]
        if last:
            xo = xo * lax.rsqrt(jnp.mean(xo * xo, axis=-1, keepdims=True) + EPS) * fnw_ref[...]
        act_ref[b] = xo


def _const_spec(shape):
    nd = len(shape)
    return pl.BlockSpec(shape, lambda j: (0,) * nd)


def _prompt_layer(x, l, last, w):
    nb, L, _ = x.shape
    nblk = L // TB
    rows = nb * TB
    xspec = pl.BlockSpec((nb, TB, D), lambda j: (0, j, 0))
    tspec = pl.BlockSpec((TB, LANE), lambda j: (j, 0))

    def per_layer(a):
        nd = a.ndim
        return pl.BlockSpec((None,) + a.shape[1:], lambda j: (l,) + (0,) * (nd - 1))

    stacked = [w[k] for k in ("norm_w", "w_in", "w_out")]
    consts = [w[k] for k in ("rdec", "recum", "rkdec", "retot")]
    ret_nw = [w["ret_norm_w"], w["lb"], w["hgrn_norm_w"]]
    s5 = [w[k] for k in ("s5_A", "s5_Bblk_cols", "s5_Cblk", "s5_D", "s5_glu_w", "s5_glu_b")]
    m2 = [w[k] for k in ("m2_conv_w", "m2_conv_b", "m2_dt_bias", "m2_A_log", "m2_D", "m2_norm_w")]
    pair_consts = [w["bones"], w["hexp"]]
    args = [x, w["cos_p"], w["sin_p"]] + stacked + consts + ret_nw + [w["sel"]] + s5 + m2 + pair_consts
    in_specs = ([xspec, tspec, tspec] + [per_layer(a) for a in stacked] + [_const_spec(a.shape) for a in consts]
                + [per_layer(a) for a in ret_nw] + [_const_spec(w["sel"].shape)]
                + [per_layer(a) for a in s5 + m2] + [_const_spec(a.shape) for a in pair_consts])
    if last:
        args.append(w["final_norm_w"])
        in_specs.append(_const_spec(w["final_norm_w"].shape))
    state_shapes = [jax.ShapeDtypeStruct((nb, NH, HD, HD), F32),
                    jax.ShapeDtypeStruct((nb, NH, HD, HD), F32),
                    jax.ShapeDtypeStruct((nb, 2 * S5N), F32),
                    jax.ShapeDtypeStruct((nb, NH, HD, HD), F32),
                    jax.ShapeDtypeStruct((nb, CONV_K - 1, CONV_CH), F32)]
    out_shape = [jax.ShapeDtypeStruct((nb, L, D), F32)] + state_shapes
    out_specs = [xspec] + [_const_spec(s.shape) for s in state_shapes]
    scratch = [pltpu.VMEM((rows, D), BF16),
               pltpu.VMEM((rows, P_PAD), F32),
               pltpu.VMEM((rows, D), BF16),
               pltpu.VMEM((NBC, rows, BCW), F32),
               pltpu.VMEM((W // LANE, rows, LANE), F32),
               pltpu.VMEM((rows, W), BF16),
               pltpu.VMEM((nb, TB + 8, CONV_CH), F32),
               pltpu.VMEM((W // LANE, rows, SUB * LANE), BF16),
               pltpu.VMEM((W // LANE, rows, LANE), F32),
               pltpu.VMEM((nb, TB, W), F32),
               pltpu.VMEM((nb, TB, W), F32),
               pltpu.VMEM((nb, TB, W), F32)] + [pltpu.VMEM((nb, W // LANE, LANE, LANE), F32)] * 3
    return pl.pallas_call(
        functools.partial(_prompt_layer_body, last),
        grid=(nblk,),
        in_specs=in_specs,
        out_specs=out_specs,
        out_shape=out_shape,
        scratch_shapes=scratch,
        compiler_params=pltpu.CompilerParams(dimension_semantics=("arbitrary",),
                                             vmem_limit_bytes=VMEM_LIMIT),
        name="prompt_layer",
    )(*args)


def _rope_tables(pos):
    half = HD // 2
    inv = 1.0 / (ROPE_BASE ** (np.arange(half, dtype=np.float64) / half))
    ang = pos.astype(np.float64)[:, None] * inv[None, :]
    cos, sin = np.cos(ang), np.sin(ang)
    cos_t = np.tile(cos, (1, LANE // half))
    sin_t = np.tile(np.concatenate([-sin, sin], axis=1), (1, LANE // HD))
    return cos_t.astype(np.float32), sin_t.astype(np.float32)


def _retention_tables():
    log_gamma = np.log1p(-(2.0 ** (-5.0 - np.arange(NH, dtype=np.float64))))
    cum = np.cumsum(np.broadcast_to(log_gamma, (TB, NH)), axis=0)
    total = cum[-1]
    causal = np.tril(np.ones((TB, TB), dtype=bool))
    diff = cum[:, None, :] - cum[None, :, :]
    dec = np.where(causal[:, :, None], np.exp(np.where(causal[:, :, None], diff, 0.0)), 0.0)
    rdec = np.moveaxis(dec, 2, 0)
    recum = np.broadcast_to(np.exp(cum).T[:, :, None], (NH, TB, HD))
    rkdec = np.broadcast_to(np.exp(total[None, :] - cum).T[:, :, None], (NH, TB, HD))
    retot = np.broadcast_to(np.exp(total)[:, None, None], (NH, 1, HD))
    pair = lambda t: np.concatenate([t[0::2], t[1::2]], axis=-1).astype(np.float32)
    return pair(rdec), pair(recum), pair(rkdec), pair(retot), log_gamma


def _sel_matrix():
    sel = np.zeros((SUB, 2, HD, 2, TB), np.float32)
    for s_ in range(SUB):
        for h2 in range(2):
            sel[s_, h2, :, h2, s_::SUB] = 1.0
    return jnp.asarray(sel.reshape(SUB * LANE, LANE), dtype=BF16)


def _rows(v, width=None):
    v = v.astype(F32)
    if width is not None and v.shape[-1] < width:
        v = jnp.pad(v, ((0, 0), (0, width - v.shape[-1])))
    return v[:, None, :]


def _block_diag(blocks):
    g, r, c = blocks.shape
    eye = jnp.eye(g, dtype=blocks.dtype)
    return jnp.einsum('grc,gh->grhc', blocks, eye).reshape(g * r, g * c)


def _s5_discretise(A_re, A_im, log_dt, B_re, B_im):
    A_re, A_im = A_re.astype(F32), A_im.astype(F32)
    dt = jnp.exp(log_dt.astype(F32))[:, None]
    mag = jnp.exp(A_re * dt)
    ab_re, ab_im = mag * jnp.cos(A_im * dt), mag * jnp.sin(A_im * dt)
    nr, ni = ab_re - 1.0, ab_im
    den = A_re * A_re + A_im * A_im
    f_re = (nr * A_re + ni * A_im) / den
    f_im = (ni * A_re - nr * A_im) / den
    B_re, B_im = B_re.astype(F32), B_im.astype(F32)
    bb_re = f_re[..., None] * B_re - f_im[..., None] * B_im
    bb_im = f_re[..., None] * B_im + f_im[..., None] * B_re
    return ab_re, ab_im, bb_re, bb_im


def _s5_matrices(A_re, A_im, log_dt, B_re, B_im, C_re, C_im):
    ab_re, ab_im, bb_re, bb_im = _s5_discretise(A_re, A_im, log_dt, B_re, B_im)
    bblk = jnp.concatenate([_block_diag(jnp.swapaxes(bb_re, 1, 2)),
                            _block_diag(jnp.swapaxes(bb_im, 1, 2))], axis=1)
    cblk = jnp.concatenate([_block_diag(jnp.swapaxes(C_re.astype(F32), 1, 2)),
                            _block_diag(jnp.swapaxes(-C_im.astype(F32), 1, 2))], axis=0)
    return jnp.stack([ab_re.reshape(-1), ab_im.reshape(-1)], axis=0), bblk, cblk


R_COS, R_SIN, R_RETNW, R_LB, R_HNW, R_S5D, R_GLUB, R_MNW = [i * W for i in range(8)]
R_AR = 8 * W
R_AI = R_AR + S5N
R_DTB = R_AI + S5N
R_ALOG = R_DTB + 8
R_MD = R_ALOG + 8
N_COLP = R_MD + 8


def _prepare(p, lb_all, prompt_len):
    depth = p["norm_w"].shape[0]
    s5_A, bblk, cblk = jax.vmap(_s5_matrices)(p["s5_A_re"], p["s5_A_im"], p["s5_log_dt"], p["s5_B_re"],
                                              p["s5_B_im"], p["s5_C_re"], p["s5_C_im"])
    cos_p, sin_p = _rope_tables(np.arange(prompt_len, dtype=np.float32))
    rdec, recum, rkdec, retot, log_gamma = _retention_tables()
    cos_s, sin_s = _rope_tables(np.float32(PAST_LEN) + np.arange(1, dtype=np.float32))
    w = dict(
        norm_w=_rows(p["norm_w"]),
        w_in=jnp.pad(p["w_in"].astype(BF16), ((0, 0), (0, 0), (0, P_PAD - P_TOTAL))),
        w_out=p["w_out"].astype(BF16),
        ret_norm_w=_rows(p["ret_norm_w"]),
        lb=_rows(lb_all),
        hgrn_norm_w=_rows(p["hgrn_norm_w"]),
        s5_A=s5_A,
        s5_Bblk=bblk.astype(BF16),
        s5_Cblk=cblk.astype(BF16),
        s5_D=_rows(p["s5_D"]),
        s5_glu_w=p["s5_glu_w"].astype(BF16),
        s5_glu_b=_rows(p["s5_glu_b"]),
        m2_conv_w=p["m2_conv_w"].astype(F32),
        m2_conv_b=_rows(p["m2_conv_b"]),
        m2_dt_bias=_rows(p["m2_dt_bias"], LANE),
        m2_A_log=_rows(jnp.repeat(p["m2_A_log"], HD, axis=1)),
        m2_D=_rows(jnp.repeat(p["m2_D"], HD, axis=1)),
        m2_norm_w=_rows(p["m2_norm_w"]),
        final_norm_w=p["final_norm_w"].astype(F32).reshape(1, D),
    )
    tile2 = lambda t: jnp.asarray(np.broadcast_to(np.tile(t[0], W // LANE), (depth, W)))
    colp = jnp.concatenate(
        [tile2(cos_s), tile2(sin_s), w["ret_norm_w"][:, 0], w["lb"][:, 0], w["hgrn_norm_w"][:, 0],
         w["s5_D"][:, 0], w["s5_glu_b"][:, 0], w["m2_norm_w"][:, 0], s5_A[:, 0], s5_A[:, 1],
         _rows(p["m2_dt_bias"], 8)[:, 0], _rows(p["m2_A_log"], 8)[:, 0], _rows(p["m2_D"], 8)[:, 0]], axis=1)
    w.update(
        colp=jnp.broadcast_to(colp[:, :, None], (depth, N_COLP, LANE)),
        s5_Bblk_cols=jnp.moveaxis(w["s5_Bblk"].reshape(depth, W, NBC, BCW), 2, 1),
        s5_BblkT=jnp.swapaxes(w["s5_Bblk"], 1, 2),
        s5_CblkT=jnp.swapaxes(w["s5_Cblk"], 1, 2),
        s5_glu_wT=jnp.swapaxes(w["s5_glu_w"], 1, 2),
        rgam=jnp.asarray(np.broadcast_to(np.exp(log_gamma)[:, None, None], (NH, 8, LANE)), dtype=F32),
        cos_p=jnp.asarray(cos_p), sin_p=jnp.asarray(sin_p), rdec=jnp.asarray(rdec), recum=jnp.asarray(recum),
        rkdec=jnp.asarray(rkdec), retot=jnp.asarray(retot), sel=_sel_matrix(),
        bones=jnp.asarray(np.kron(np.eye(LANE // HD), np.full((HD, HD), 1.0 / HD)), dtype=BF16),
        hexp=jnp.asarray(np.kron(np.eye(LANE, NH), np.ones((1, HD))), dtype=F32),
    )
    return w


KC = 16
NKC = HD // KC
STEPS = NH * NKC


def _rotary_cols(x, cos, sin_signed):
    half = HD // 2
    parts = []
    for h in range(NH):
        parts += [x[h * HD + half:(h + 1) * HD], x[h * HD:h * HD + half]]
    return x * cos + jnp.concatenate(parts, axis=0) * sin_signed


def _expand_rows(dst, x):
    for c in range(x.shape[0]):
        dst[c] = jnp.broadcast_to(x[c:c + 1, :], (8, LANE))


def _sample_body(n_steps, x_ref, normw_ref, win_ref, wout_ref, fnw_ref, colp_ref, sbt_ref, sct_ref,
                 gluwt_ref, cw_ref, cb_ref, rgam_ref,
                 ret_in, hg_in, m2_in, s5re_in, s5im_in, buf_in,
                 y_ref, ret_out, hg_out, m2_out, s5re_out, s5im_out, buf_out,
                 xs_s, pt_s, vt_s, ot_s, mixt_s, o_s, hp_s,
                 kret_s, qret_s, khg_s, qhg_s, ahg_s, km2_s, qm2_s):
    i = pl.program_id(0)
    r = i % STEPS
    h = r // NKC
    kc = r % NKC

    def cp(r0, n=W):
        return colp_ref[r0:r0 + n, :]

    def pc(c0, w):
        return pt_s[c0:c0 + w, :]

    @pl.when(i == 0)
    def _load():
        xs_s[...] = x_ref[...]

    @pl.when(r == 0)
    def _prep():
        x = xs_s[...]
        hn = x * lax.rsqrt(jnp.mean(x * x, axis=-1, keepdims=True) + EPS) * normw_ref[...]
        proj = jnp.dot(hn.astype(BF16), win_ref[...], preferred_element_type=F32)

        xnew = proj[:, C_XBC:C_XBC + CONV_CH]
        acc = cb_ref[...] + xnew * cw_ref[CONV_K - 1:CONV_K, :]
        for t in range(CONV_K - 1):
            acc = acc + buf_in[t] * cw_ref[t:t + 1, :]
        for t in range(CONV_K - 2):
            buf_out[t] = buf_in[t + 1]
        buf_out[CONV_K - 2] = xnew
        xbc = _silu(acc)

        for t in range(P_PAD // LANE):
            c0 = t * LANE
            if C_XBC <= c0 < C_XBC + CONV_CH:
                tile = xbc[:, c0 - C_XBC:c0 - C_XBC + LANE]
            else:
                tile = proj[:, c0:c0 + LANE]
            pt_s[c0:c0 + LANE, :] = tile.T

        cos, sin = cp(R_COS), cp(R_SIN)
        _expand_rows(kret_s, _rotary_cols(pc(C_RK, W), cos, sin) * (HD ** -0.5))
        _expand_rows(qret_s, _rotary_cols(pc(C_RQ, W), cos, sin))
        vt_s[0] = pc(C_RV, W)

        fr = pc(C_GF, W)
        lb = cp(R_LB)
        logf = _log_sigmoid(fr) + jnp.log(1.0 + lb * jnp.exp(jnp.minimum(-fr, EXP_CLIP)))
        _expand_rows(ahg_s, jnp.exp(logf))
        _expand_rows(khg_s, (1.0 - lb) * jax.nn.sigmoid(-fr))
        _expand_rows(qhg_s, _silu(pc(C_GQ, W)))
        vt_s[1] = pc(C_GI, W)

        dt8 = _softplus(pc(C_DT, 8) + cp(R_DTB, 8))
        adec8 = jnp.exp(dt8 * (-jnp.exp(cp(R_ALOG, 8))))
        for hh in range(NH):
            hp_s[hh] = jnp.broadcast_to(adec8[hh:hh + 1, :], (8, LANE))
            vt_s[2, hh * HD:(hh + 1) * HD, :] = pc(C_XBC + hh * HD, HD) * dt8[hh:hh + 1, :]
        _expand_rows(km2_s, pc(C_XBC + W, 2 * HD))
        _expand_rows(qm2_s, pc(C_XBC + W + 2 * HD, 2 * HD))

        u = pc(C_SU, W)
        bu = jnp.dot(sbt_ref[...], u.astype(BF16), preferred_element_type=F32)
        hr, hi = s5re_in[...], s5im_in[...]
        ar, ai = cp(R_AR, S5N), cp(R_AI, S5N)
        nr = ar * hr - ai * hi + bu[0:S5N]
        ni = ar * hi + ai * hr + bu[S5N:2 * S5N]
        s5re_out[...] = nr
        s5im_out[...] = ni
        hcat = jnp.concatenate([nr, ni], axis=0).astype(BF16)
        sy = jnp.dot(sct_ref[...], hcat, preferred_element_type=F32) + cp(R_S5D) * u
        gy = _gelu_tanh(sy)
        glu = jnp.dot(gluwt_ref[...], gy.astype(BF16), preferred_element_type=F32) + cp(R_GLUB)
        mixt_s[2 * W:3 * W, :] = gy * jax.nn.sigmoid(glu) * _silu(pc(C_SG, W))

    @pl.when(kc == 0)
    def _zero():
        o_s[...] = jnp.zeros(o_s.shape, F32)

    hrow = pl.multiple_of(h * HD, HD)
    cbase = h * HD + kc * KC
    gbase = (h // 2) * HD + kc * KC

    def update(m, st_in, st_out, kx, qx, base, decay):
        v3 = vt_s[m, pl.ds(hrow, HD), :].reshape(HD // 8, 8, LANE)

        def body(kk, o):
            s_new = decay(kk) * st_in[kk].reshape(HD // 8, 8, LANE) + kx[base + kk] * v3
            st_out[kk] = s_new.reshape(HD, LANE)
            return o + qx[base + kk] * s_new
        o_s[m] = lax.fori_loop(0, KC, body, o_s[m], unroll=2)

    gam = rgam_ref[h]
    update(0, ret_in, ret_out, kret_s, qret_s, cbase, lambda kk: gam)
    update(1, hg_in, hg_out, khg_s, qhg_s, cbase, lambda kk: ahg_s[cbase + kk])
    adec = hp_s[h]
    update(2, m2_in, m2_out, km2_s, qm2_s, gbase, lambda kk: adec)

    @pl.when(kc == NKC - 1)
    def _head_done():
        for m in range(3):
            ot_s[m, pl.ds(hrow, HD), :] = o_s[m].reshape(HD, LANE)

    @pl.when(r == STEPS - 1)
    def _finish():
        def head_rms_cols(o):
            parts = []
            for hh in range(NH):
                seg = o[hh * HD:(hh + 1) * HD]
                parts.append(seg * lax.rsqrt(jnp.mean(seg * seg, axis=0, keepdims=True) + EPS))
            return jnp.concatenate(parts, axis=0)

        mixt_s[0:W, :] = head_rms_cols(ot_s[0]) * cp(R_RETNW) * _silu(pc(C_RG, W))
        mixt_s[W:2 * W, :] = head_rms_cols(ot_s[1]) * cp(R_HNW) * _silu(pc(C_GG, W))
        md8 = cp(R_MD, 8)
        ym = jnp.concatenate([ot_s[2, hh * HD:(hh + 1) * HD, :] + md8[hh:hh + 1, :] * pc(C_XBC + hh * HD, HD)
                              for hh in range(NH)], axis=0)
        my = ym * _silu(pc(C_MZ, W))
        mixt_s[3 * W:4 * W, :] = my * lax.rsqrt(jnp.mean(my * my, axis=0, keepdims=True) + EPS) * cp(R_MNW)
        mixed = jnp.concatenate([mixt_s[t * LANE:(t + 1) * LANE, :].T for t in range(D // LANE)], axis=1)
        xo = xs_s[...] + jnp.dot(mixed.astype(BF16), wout_ref[...], preferred_element_type=F32)
        xs_s[...] = xo

        @pl.when(i == n_steps - 1)
        def _final_norm():
            y_ref[...] = xo * lax.rsqrt(jnp.mean(xo * xo, axis=-1, keepdims=True) + EPS) * fnw_ref[...]


def _sample_step(x, w, ret, hg, m2, s5re, s5im, buf):
    depth = ret.shape[0]
    n = x.shape[0]
    n_steps = depth * STEPS
    lay = lambda i: i // STEPS

    def per_layer(a):
        nd = a.ndim
        return pl.BlockSpec((None,) + a.shape[1:], lambda i: (lay(i),) + (0,) * (nd - 1))

    st_spec = pl.BlockSpec((None, None, KC, HD, LANE),
                           lambda i: (lay(i), (i % STEPS) // NKC, i % NKC, 0, 0))
    weights = [w["norm_w"], w["w_in"], w["w_out"]]
    tables = [w["colp"], w["s5_BblkT"], w["s5_CblkT"], w["s5_glu_wT"], w["m2_conv_w"], w["m2_conv_b"]]
    in_specs = ([_const_spec(x.shape)] + [per_layer(a) for a in weights] + [_const_spec(w["final_norm_w"].shape)]
                + [per_layer(a) for a in tables] + [_const_spec(w["rgam"].shape)]
                + [st_spec, st_spec, st_spec, per_layer(s5re), per_layer(s5im), per_layer(buf)])
    out_shape = [jax.ShapeDtypeStruct((n, D), F32)] + [jax.ShapeDtypeStruct(a.shape, F32)
                                                       for a in (ret, hg, m2, s5re, s5im, buf)]
    out_specs = [_const_spec((n, D)), st_spec, st_spec, st_spec, per_layer(s5re), per_layer(s5im),
                 per_layer(buf)]
    expand = lambda c: pltpu.VMEM((c, 8, LANE), F32)
    scratch = [pltpu.VMEM((n, D), F32),
               pltpu.VMEM((P_PAD, LANE), F32),
               pltpu.VMEM((3, W, LANE), F32),
               pltpu.VMEM((3, W, LANE), F32),
               pltpu.VMEM((D, LANE), F32),
               pltpu.VMEM((3, HD // 8, 8, LANE), F32),
               pltpu.VMEM((NH, 8, LANE), F32),
               expand(W), expand(W), expand(W), expand(W), expand(W), expand(2 * HD), expand(2 * HD)]
    return pl.pallas_call(
        functools.partial(_sample_body, n_steps),
        grid=(n_steps,),
        in_specs=in_specs,
        out_specs=out_specs,
        out_shape=out_shape,
        scratch_shapes=scratch,
        compiler_params=pltpu.CompilerParams(dimension_semantics=("arbitrary",),
                                             vmem_limit_bytes=VMEM_LIMIT),
        name="sample_step",
    )(x, *weights, w["final_norm_w"], *tables, w["rgam"], ret, hg, m2, s5re, s5im, buf)


def kernel(x_prompt, x_sample, state_ret, state_hgrn, state_s5_re, state_s5_im, state_m2_ssm,
           state_m2_conv, norm_w, w_in, ret_norm_w, hgrn_lb_logits, hgrn_norm_w, s5_A_re, s5_A_im,
           s5_log_dt, s5_B_re, s5_B_im, s5_C_re, s5_C_im, s5_D, s5_glu_w, s5_glu_b, m2_conv_w,
           m2_conv_b, m2_dt_bias, m2_A_log, m2_D, m2_norm_w, w_out, final_norm_w):
    p = dict(norm_w=norm_w, w_in=w_in, ret_norm_w=ret_norm_w, hgrn_norm_w=hgrn_norm_w,
             s5_A_re=s5_A_re, s5_A_im=s5_A_im, s5_log_dt=s5_log_dt, s5_B_re=s5_B_re, s5_B_im=s5_B_im,
             s5_C_re=s5_C_re, s5_C_im=s5_C_im, s5_D=s5_D, s5_glu_w=s5_glu_w, s5_glu_b=s5_glu_b,
             m2_conv_w=m2_conv_w, m2_conv_b=m2_conv_b, m2_dt_bias=m2_dt_bias, m2_A_log=m2_A_log,
             m2_D=m2_D, m2_norm_w=m2_norm_w, w_out=w_out, final_norm_w=final_norm_w)
    depth = norm_w.shape[0]
    nbp, lp, _ = x_prompt.shape
    nbs = x_sample.shape[0]

    lb_sm = jax.nn.softmax(hgrn_lb_logits.astype(F32), axis=0)
    lb_all = jnp.clip(jnp.cumsum(lb_sm, axis=0) - lb_sm[0], 0.0, 1.0)

    w = _prepare(p, lb_all, lp)

    xp = x_prompt
    pst = []
    for l in range(depth):
        outs = _prompt_layer(xp, l, l == depth - 1, w)
        xp = outs[0]
        ret, hg, s5, m2, buf = outs[-5:]
        pst.append((ret, hg, s5[:, :S5N].reshape(nbp, S5G, S5P), s5[:, S5N:].reshape(nbp, S5G, S5P),
                    m2, buf))
    yp = xp

    seq_last = lambda a: jnp.moveaxis(a.astype(F32), 1, -1)
    ys, ret, hg, m2, s5re, s5im, buf = _sample_step(
        x_sample.reshape(nbs, D), w,
        seq_last(state_ret), seq_last(state_hgrn), seq_last(state_m2_ssm),
        seq_last(state_s5_re).reshape(depth, S5N, nbs), seq_last(state_s5_im).reshape(depth, S5N, nbs),
        jnp.swapaxes(state_m2_conv.astype(F32), 1, 2))
    seq_second = lambda a: jnp.moveaxis(a, -1, 1)

    stk = lambda i: jnp.stack([s[i] for s in pst], axis=0)
    return (yp, ys.reshape(nbs, 1, D),
            stk(0), stk(1), stk(2), stk(3), stk(4), stk(5),
            seq_second(ret), seq_second(hg), seq_second(s5re.reshape(depth, S5G, S5P, nbs)),
            seq_second(s5im.reshape(depth, S5G, S5P, nbs)), seq_second(m2), jnp.swapaxes(buf, 1, 2))
```

```python
import functools
import math

import numpy as np
import jax
import jax.numpy as jnp
from jax import lax
from jax.experimental import pallas as pl
from jax.experimental.pallas import tpu as pltpu

F32 = jnp.float32
BF16 = jnp.bfloat16

D = 1024
W = 256
NH = 4
HD = 64
S5G = 16
S5C = 16
S5P = 64
S5N = S5G * S5P
NBC = 8
BCW = 2 * S5N // NBC
CONV_CH = 512
CONV_K = 4
TB = 64
SUB = 16
NSUB = TB // SUB
EPS = 1e-6
EXP_CLIP = 60.0
ROPE_BASE = 10000.0
PAST_LEN = 16384

C_RQ, C_RK, C_RV, C_RG = 0, 256, 512, 768
C_GQ, C_GF, C_GI, C_GG = 1024, 1280, 1536, 1792
C_SU, C_SG = 2048, 2304
C_MZ, C_XBC, C_DT = 2560, 2816, 3328
P_TOTAL = 3332
PCH = 1024
P_PAD = 3456
LANE = 128
VMEM_LIMIT = 56 * 1024 * 1024


def _silu(x):
    return x * jax.nn.sigmoid(x)


def _softplus(x):
    return jnp.maximum(x, 0.0) + jnp.log(1.0 + jnp.exp(-jnp.abs(x)))


def _log_sigmoid(x):
    return jnp.minimum(x, 0.0) - jnp.log(1.0 + jnp.exp(-jnp.abs(x)))


def _round_robin(gens):
    gens = list(gens)
    while gens:
        alive = []
        for g in gens:
            try:
                next(g)
                alive.append(g)
            except StopIteration:
                pass
        gens = alive


def _for_sequences(nb, parts, group):
    def body(i, c):
        built = [parts(i * group + k) for k in range(group)]
        _round_robin([g for gens, _ in built for g in gens])
        for _, finish in built:
            if finish is not None:
                finish()
        return c
    lax.fori_loop(0, nb // group, body, 0)


def _gelu_tanh(x):
    c = math.sqrt(2.0 / math.pi)
    return 0.5 * x * (1.0 + jnp.tanh(c * (x + 0.044715 * (x * x * x))))


def _dot(a, b):
    return jnp.dot(a.astype(BF16), b.astype(BF16), preferred_element_type=F32)


def _split3(x):
    hi = x.astype(BF16)
    rest = x - hi.astype(F32)
    mid = rest.astype(BF16)
    return hi, mid, (rest - mid.astype(F32)).astype(BF16)


def _select_rows(m01, x):
    m = m01.astype(BF16)
    hi, mid, lo = _split3(x)
    return (jnp.dot(m, hi, preferred_element_type=F32) + jnp.dot(m, mid, preferred_element_type=F32)
            + jnp.dot(m, lo, preferred_element_type=F32))


def _select_cols(x, m01):
    m = m01.astype(BF16)
    hi, mid, lo = _split3(x)
    return (jnp.dot(hi, m, preferred_element_type=F32) + jnp.dot(mid, m, preferred_element_type=F32)
            + jnp.dot(lo, m, preferred_element_type=F32))


def _rot_half_partner(x):
    lane = lax.broadcasted_iota(jnp.int32, x.shape, 1)
    first = (lane % HD) < (HD // 2)
    return jnp.where(first, pltpu.roll(x, LANE - HD // 2, 1), pltpu.roll(x, HD // 2, 1))


def _rotary(x, cos, sin_signed):
    parts = []
    for i in range(W // LANE):
        xi = x[:, i * LANE:(i + 1) * LANE]
        parts.append(xi * cos + _rot_half_partner(xi) * sin_signed)
    return jnp.concatenate(parts, axis=1)


def _prompt_layer_body(last, *refs):
    (x_ref, cos_ref, sin_ref, normw_ref, win_ref, wout_ref,
     rdec_ref, recum_ref, rkdec_ref, retot_ref, retnw_ref,
     lb_ref, hnw_ref, sel_ref,
     sA_ref, sB_ref, sC_ref, sD_ref, gluw_ref, glub_ref,
     cw_ref, cb_ref, dtb_ref, alog_ref, md_ref, mnw_ref, bones_ref, hexp_ref) = refs[:28]
    refs = refs[28:]
    if last:
        fnw_ref = refs[0]
        refs = refs[1:]
    act_ref = refs[0]
    refs = refs[1:]
    (ret_ref, hg_ref, s5_ref, m2_ref, m2buf_ref,
     hn_s, proj_s, mixed_s, bu_s, u_s, ub_s, cv_s, p_s, dg_s, hq_s, hk_s, cum_s, hret_s, hhg_s, hm2_s) = refs
    j = pl.program_id(0)
    nb = x_ref.shape[0]

    @pl.when(j == 0)
    def _init():
        hret_s[...] = jnp.zeros(hret_s.shape, F32)
        hhg_s[...] = jnp.zeros(hhg_s.shape, F32)
        s5_ref[...] = jnp.zeros(s5_ref.shape, F32)
        hm2_s[...] = jnp.zeros(hm2_s.shape, F32)
        cv_s[...] = jnp.zeros(cv_s.shape, F32)

    ti = lax.broadcasted_iota(jnp.int32, (TB, TB), 0)
    si = lax.broadcasted_iota(jnp.int32, (TB, TB), 1)
    causal = si <= ti
    tri_l = causal.astype(F32)
    pr = lax.broadcasted_iota(jnp.int32, (LANE, LANE), 0)
    pc_ = lax.broadcasted_iota(jnp.int32, (LANE, LANE), 1)
    pairmask = (pr // HD) == (pc_ // HD)
    t2 = lax.broadcasted_iota(jnp.int32, (TB, LANE), 0)
    l2 = lax.broadcasted_iota(jnp.int32, (TB, LANE), 1)
    causal2 = (l2 % HD) <= t2
    first_head = l2 < HD
    subdiag2 = (t2 // SUB) == ((l2 % HD) // SUB)
    srcblock = [pairmask & (((pc_ % HD) // SUB) == jb) for jb in range(NSUB - 1)]

    def dup_t(x2):
        return jnp.concatenate([x2, x2], axis=0).T

    def blockdiag2(x2):
        return jnp.where(pairmask, jnp.concatenate([x2, x2], axis=0), 0.0)

    def norm_body(b, c):
        xb = x_ref[b]
        hn = xb * lax.rsqrt(jnp.mean(xb * xb, axis=-1, keepdims=True) + EPS) * normw_ref[...]
        hn_s[pl.ds(pl.multiple_of(b * TB, TB), TB), :] = hn.astype(BF16)
        return c
    lax.fori_loop(0, nb, norm_body, 0, unroll=4)
    for c0 in range(0, P_PAD, PCH):
        cs = slice(c0, min(c0 + PCH, P_PAD))
        proj_s[:, cs] = jnp.dot(hn_s[...], win_ref[:, cs], preferred_element_type=F32)

    cos = cos_ref[...]
    sin = sin_ref[...]

    def reorder_u(b, c):
        u = proj_s[pl.ds(pl.multiple_of(b * TB, TB), TB), C_SU:C_SU + W]
        for c_ in range(W // LANE):
            u_s[c_, pl.ds(b, TB, stride=nb), :] = u[:, c_ * LANE:(c_ + 1) * LANE]
        return c
    lax.fori_loop(0, nb, reorder_u, 0)
    ub_s[...] = jnp.concatenate([u_s[c_] for c_ in range(W // LANE)], axis=1).astype(BF16)

    def s5_input_piece(cb):
        bu_s[cb] = jnp.dot(ub_s[...], sB_ref[cb], preferred_element_type=F32)
        yield

    def phase1_parts(b):
        r0 = pl.multiple_of(b * TB, TB)

        def pj(c0, w):
            return proj_s[pl.ds(r0, TB), c0:c0 + w]

        rq = _rotary(pj(C_RQ, W), cos, sin)
        rk = _rotary(pj(C_RK, W), cos, sin) * (HD ** -0.5)
        rv = pj(C_RV, W)
        rg = pj(C_RG, W)

        def ret_pair(p):
            ls = slice(p * LANE, (p + 1) * LANE)
            q2, k2, v2 = rq[:, ls], rk[:, ls], rv[:, ls]
            hb = hret_s[b, p]
            kt = dup_t(k2)
            s_raw = _dot(q2, jnp.where(pairmask, kt, 0.0))
            oi = _dot(q2, hb)
            kv = _dot(kt[:, 0:HD], v2 * rkdec_ref[p])
            yield
            o = _dot(s_raw * rdec_ref[p], blockdiag2(v2))
            yield
            o = o + oi * recum_ref[p]
            hret_s[b, p] = jnp.where(pairmask, retot_ref[p] * hb + kv, 0.0)
            ms = _dot(o * o, bones_ref[...])
            yield
            o = o * lax.rsqrt(ms + EPS) * retnw_ref[:, ls] * _silu(rg[:, ls])
            mixed_s[pl.ds(r0, TB), 0 * W + p * LANE:0 * W + (p + 1) * LANE] = o.astype(BF16)

        cv_s[b, 8:8 + TB, :] = pj(C_XBC, CONV_CH)
        acc = cb_ref[...] + cv_s[b, 5:5 + TB, :] * cw_ref[0:1, :]
        for i in range(1, CONV_K):
            acc = acc + cv_s[b, 5 + i:5 + i + TB, :] * cw_ref[i:i + 1, :]
        tail = cv_s[b, TB + 5:TB + 8, :]
        cv_s[b, 5:8, :] = tail
        m2buf_ref[b] = tail
        xbc = _silu(acc)
        xm = xbc[:, 0:W]
        bm = xbc[:, W:W + 2 * HD]
        cm = xbc[:, W + 2 * HD:W + 4 * HD]
        bm_sw = pltpu.roll(bm, HD, 1)
        cm_sw = pltpu.roll(cm, HD, 1)
        dt_b = _select_cols(_softplus(pj(C_DT, LANE) + dtb_ref[...]), hexp_ref[...])
        la_b = dt_b * (-jnp.exp(alog_ref[...]))
        cum_b = _select_rows(tri_l, la_b)
        ys = [None] * (W // LANE)

        def m2_pair(p):
            ls = slice(p * LANE, (p + 1) * LANE)
            b2 = jnp.where(first_head, bm, bm_sw) if p == 0 else jnp.where(first_head, bm_sw, bm)
            c2 = jnp.where(first_head, cm, cm_sw) if p == 0 else jnp.where(first_head, cm_sw, cm)
            x2, dt2, cum2 = xm[:, ls], dt_b[:, ls], cum_b[:, ls]
            xdt2 = x2 * dt2
            ct = dup_t(cum2)
            r2 = jnp.where(first_head, ct[0:TB], ct[HD:HD + TB])
            tot2 = cum2[TB - 1:TB, :]
            hb = hm2_s[b, p]
            bt = dup_t(b2)
            s_raw = _dot(c2, jnp.where(pairmask, bt, 0.0))
            oi = _dot(c2, hb)
            kv = _dot(bt[:, 0:HD], xdt2 * jnp.exp(tot2 - cum2))
            decay = jnp.where(causal2, jnp.exp(jnp.minimum(cum2 - r2, 0.0)), 0.0)
            yield
            o = _dot(s_raw * decay, blockdiag2(xdt2))
            yield
            hm2_s[b, p] = jnp.where(pairmask, jnp.exp(tot2) * hb + kv, 0.0)
            ys[p] = o + oi * jnp.exp(cum2) + md_ref[:, ls] * x2

        fr = pj(C_GF, W)
        lb = lb_ref[...]
        logf = _log_sigmoid(fr) + jnp.log(1.0 + lb * jnp.exp(jnp.minimum(-fr, EXP_CLIP)))
        hq = _silu(pj(C_GQ, W))
        hk = (1.0 - lb) * jax.nn.sigmoid(-fr)
        cum = _select_rows(tri_l, logf)
        hq_s[b] = hq
        hk_s[b] = hk
        cum_s[b] = cum
        t8 = lax.broadcasted_iota(jnp.int32, (8, LANE), 0)

        def diag_products(p):
            ls = slice(p * LANE, (p + 1) * LANE)
            for s_ in range(SUB):
                pieces = []
                for i in range(NSUB):
                    kb = jnp.broadcast_to(hk_s[b, i * SUB + s_:i * SUB + s_ + 1, ls], (8, LANE))
                    cb = jnp.broadcast_to(cum_s[b, i * SUB + s_:i * SUB + s_ + 1, ls], (8, LANE))
                    for half in range(SUB // 8):
                        rows = slice(i * SUB + half * 8, i * SUB + half * 8 + 8)
                        if half * 8 + 7 < s_:
                            pieces.append(jnp.zeros((8, LANE), F32))
                        elif half * 8 >= s_:
                            pieces.append(hq[rows, ls] * kb * jnp.exp(cum[rows, ls] - cb))
                        else:
                            e = jnp.exp(jnp.minimum(cum[rows, ls] - cb, 0.0))
                            pieces.append(jnp.where(t8 + half * 8 >= s_, hq[rows, ls] * kb * e, 0.0))
                pv = jnp.concatenate(pieces, axis=0)
                p_s[p, pl.ds(r0, TB), s_ * LANE:(s_ + 1) * LANE] = pv.astype(BF16)
                if s_ % 2 == 1:
                    yield

        def finish():
            my = jnp.concatenate(ys, axis=1) * _silu(pj(C_MZ, W))
            om = my * lax.rsqrt(jnp.mean(my * my, axis=-1, keepdims=True) + EPS) * mnw_ref[...]
            mixed_s[pl.ds(r0, TB), 3 * W:4 * W] = om.astype(BF16)

        gens = ([s5_input_piece(b)] + [ret_pair(p) for p in range(W // LANE)]
                + [m2_pair(p) for p in range(W // LANE)] + [diag_products(p) for p in range(W // LANE)])
        return gens, finish

    _for_sequences(nb, phase1_parts, 2)

    half_rows = nb * TB // 2
    for p in range(W // LANE):
        for r_ in range(2):
            rs = slice(r_ * half_rows, (r_ + 1) * half_rows)
            dg_s[p, rs, :] = jnp.dot(p_s[p, rs, :], sel_ref[...], preferred_element_type=F32)

    ar = jnp.broadcast_to(sA_ref[0:1, :], (nb, S5N))
    ai = jnp.broadcast_to(sA_ref[1:2, :], (nb, S5N))

    def scan_body(t, carry):
        hr, hi = carry
        row = pl.multiple_of(t * nb, nb)
        half = NBC // 2
        nr = ar * hr - ai * hi + jnp.concatenate([bu_s[c_, pl.ds(row, nb), :] for c_ in range(half)], axis=1)
        ni = ar * hi + ai * hr + jnp.concatenate([bu_s[half + c_, pl.ds(row, nb), :] for c_ in range(half)],
                                                 axis=1)
        for c_ in range(half):
            bu_s[c_, pl.ds(row, nb), :] = nr[:, c_ * BCW:(c_ + 1) * BCW]
            bu_s[half + c_, pl.ds(row, nb), :] = ni[:, c_ * BCW:(c_ + 1) * BCW]
        return nr, ni
    hr, hi = lax.fori_loop(0, TB, scan_body, (s5_ref[:, 0:S5N], s5_ref[:, S5N:2 * S5N]))
    s5_ref[:, 0:S5N] = hr
    s5_ref[:, S5N:2 * S5N] = hi
    ch_tb = _dot(bu_s[0], sC_ref[0:BCW, :])
    for c_ in range(1, NBC):
        ch_tb = ch_tb + _dot(bu_s[c_], sC_ref[c_ * BCW:(c_ + 1) * BCW, :])
    for c_ in range(W // LANE):
        u_s[c_] = ch_tb[:, c_ * LANE:(c_ + 1) * LANE]

    def phase2_parts(b):
        r0 = pl.multiple_of(b * TB, TB)

        def pj(c0, w):
            return proj_s[pl.ds(r0, TB), c0:c0 + w]

        def s5_out():
            chs = jnp.concatenate([u_s[c_, pl.ds(b, TB, stride=nb), :] for c_ in range(W // LANE)],
                                  axis=1)
            gy = _gelu_tanh(chs + sD_ref[...] * pj(C_SU, W))
            glu = _dot(gy, gluw_ref[...])
            yield
            os5 = gy * jax.nn.sigmoid(glu + glub_ref[...]) * _silu(pj(C_SG, W))
            mixed_s[pl.ds(r0, TB), 2 * W:3 * W] = os5.astype(BF16)

        cum = cum_s[b]
        rr = jnp.concatenate(
            [jnp.zeros((SUB, W), F32)]
            + [jnp.broadcast_to(cum_s[b, i * SUB - 1:i * SUB, :], (SUB, W))
               for i in range(1, NSUB)], axis=0)
        ee = jnp.concatenate(
            [jnp.broadcast_to(cum_s[b, i * SUB + SUB - 1:i * SUB + SUB, :], (SUB, W))
             for i in range(NSUB)], axis=0)
        totc = jnp.broadcast_to(cum[TB - 1:TB, :], (LANE, W)).T
        hq = hq_s[b]
        hk = hk_s[b]
        hv = pj(C_GI, W)
        gg = pj(C_GG, W)
        qt = hq * jnp.exp(cum - rr)
        kh_ = hk * jnp.exp(ee - cum)
        qe = hq * jnp.exp(cum)
        tot = cum[TB - 1:TB, :]
        kend = kh_ * jnp.exp(tot - ee)
        trow = lax.broadcasted_iota(jnp.int32, (TB, W), 0) // SUB
        qx = []
        for jb in range(NSUB - 1):
            eb = jnp.broadcast_to(ee[jb * SUB:jb * SUB + 1, :], (TB, W))
            qx.append(jnp.where(trow > jb, qt * jnp.exp(jnp.minimum(rr - eb, 0.0)), 0.0))

        def hg_pair(p):
            ls = slice(p * LANE, (p + 1) * LANE)
            hb = hhg_s[b, p]
            hv2 = hv[:, ls]
            kt = dup_t(kh_[:, ls])
            off = _dot(qx[0][:, ls], jnp.where(srcblock[0], kt, 0.0))
            for jb in range(1, NSUB - 1):
                off = off + _dot(qx[jb][:, ls], jnp.where(srcblock[jb], kt, 0.0))
            oi = _dot(qe[:, ls], hb)
            kv = _dot(dup_t(kend[:, ls])[:, 0:HD], hv2)
            yield
            o = _dot(jnp.where(subdiag2, dg_s[p, pl.ds(r0, TB), :], 0.0) + off, blockdiag2(hv2))
            yield
            hhg_s[b, p] = jnp.where(pairmask, jnp.exp(totc[ls, :]) * hb + kv, 0.0)
            o = o + oi
            ms = _dot(o * o, bones_ref[...])
            yield
            o = o * lax.rsqrt(ms + EPS) * hnw_ref[:, ls] * _silu(gg[:, ls])
            mixed_s[pl.ds(r0, TB), 1 * W + p * LANE:1 * W + (p + 1) * LANE] = o.astype(BF16)

        return [s5_out()] + [hg_pair(p) for p in range(W // LANE)], None

    _for_sequences(nb, phase2_parts, 8)

    @pl.when(j == pl.num_programs(0) - 1)
    def _emit_states():
        for b in range(nb):
            for h in range(NH):
                p, h2 = divmod(h, 2)
                blk = (slice(h2 * HD, (h2 + 1) * HD),) * 2
                ret_ref[b, h] = hret_s[(b, p) + blk]
                hg_ref[b, h] = hhg_s[(b, p) + blk]
                m2_ref[b, h] = hm2_s[(b, p) + blk]

    res = jnp.dot(mixed_s[...], wout_ref[...], preferred_element_type=F32)
    for b in range(nb):
        xo = x_ref[b] + res[b * TB:(b + 1) * TB]
        if last:
            xo = xo * lax.rsqrt(jnp.mean(xo * xo, axis=-1, keepdims=True) + EPS) * fnw_ref[...]
        act_ref[b] = xo


def _const_spec(shape):
    nd = len(shape)
    return pl.BlockSpec(shape, lambda j: (0,) * nd)


def _prompt_layer(x, l, last, w):
    nb, L, _ = x.shape
    nblk = L // TB
    rows = nb * TB
    xspec = pl.BlockSpec((nb, TB, D), lambda j: (0, j, 0))
    tspec = pl.BlockSpec((TB, LANE), lambda j: (j, 0))

    def per_layer(a):
        nd = a.ndim
        return pl.BlockSpec((None,) + a.shape[1:], lambda j: (l,) + (0,) * (nd - 1))

    stacked = [w[k] for k in ("norm_w", "w_in", "w_out")]
    consts = [w[k] for k in ("rdec", "recum", "rkdec", "retot")]
    ret_nw = [w["ret_norm_w"], w["lb"], w["hgrn_norm_w"]]
    s5 = [w[k] for k in ("s5_A", "s5_Bblk_cols", "s5_Cblk", "s5_D", "s5_glu_w", "s5_glu_b")]
    m2 = [w[k] for k in ("m2_conv_w", "m2_conv_b", "m2_dt_bias", "m2_A_log", "m2_D", "m2_norm_w")]
    pair_consts = [w["bones"], w["hexp"]]
    args = [x, w["cos_p"], w["sin_p"]] + stacked + consts + ret_nw + [w["sel"]] + s5 + m2 + pair_consts
    in_specs = ([xspec, tspec, tspec] + [per_layer(a) for a in stacked] + [_const_spec(a.shape) for a in consts]
                + [per_layer(a) for a in ret_nw] + [_const_spec(w["sel"].shape)]
                + [per_layer(a) for a in s5 + m2] + [_const_spec(a.shape) for a in pair_consts])
    if last:
        args.append(w["final_norm_w"])
        in_specs.append(_const_spec(w["final_norm_w"].shape))
    state_shapes = [jax.ShapeDtypeStruct((nb, NH, HD, HD), F32),
                    jax.ShapeDtypeStruct((nb, NH, HD, HD), F32),
                    jax.ShapeDtypeStruct((nb, 2 * S5N), F32),
                    jax.ShapeDtypeStruct((nb, NH, HD, HD), F32),
                    jax.ShapeDtypeStruct((nb, CONV_K - 1, CONV_CH), F32)]
    out_shape = [jax.ShapeDtypeStruct((nb, L, D), F32)] + state_shapes
    out_specs = [xspec] + [_const_spec(s.shape) for s in state_shapes]
    scratch = [pltpu.VMEM((rows, D), BF16),
               pltpu.VMEM((rows, P_PAD), F32),
               pltpu.VMEM((rows, D), BF16),
               pltpu.VMEM((NBC, rows, BCW), F32),
               pltpu.VMEM((W // LANE, rows, LANE), F32),
               pltpu.VMEM((rows, W), BF16),
               pltpu.VMEM((nb, TB + 8, CONV_CH), F32),
               pltpu.VMEM((W // LANE, rows, SUB * LANE), BF16),
               pltpu.VMEM((W // LANE, rows, LANE), F32),
               pltpu.VMEM((nb, TB, W), F32),
               pltpu.VMEM((nb, TB, W), F32),
               pltpu.VMEM((nb, TB, W), F32)] + [pltpu.VMEM((nb, W // LANE, LANE, LANE), F32)] * 3
    return pl.pallas_call(
        functools.partial(_prompt_layer_body, last),
        grid=(nblk,),
        in_specs=in_specs,
        out_specs=out_specs,
        out_shape=out_shape,
        scratch_shapes=scratch,
        compiler_params=pltpu.CompilerParams(dimension_semantics=("arbitrary",),
                                             vmem_limit_bytes=VMEM_LIMIT),
        name="prompt_layer",
    )(*args)


def _rope_tables(pos):
    half = HD // 2
    inv = 1.0 / (ROPE_BASE ** (np.arange(half, dtype=np.float64) / half))
    ang = pos.astype(np.float64)[:, None] * inv[None, :]
    cos, sin = np.cos(ang), np.sin(ang)
    cos_t = np.tile(cos, (1, LANE // half))
    sin_t = np.tile(np.concatenate([-sin, sin], axis=1), (1, LANE // HD))
    return cos_t.astype(np.float32), sin_t.astype(np.float32)


def _retention_tables():
    log_gamma = np.log1p(-(2.0 ** (-5.0 - np.arange(NH, dtype=np.float64))))
    cum = np.cumsum(np.broadcast_to(log_gamma, (TB, NH)), axis=0)
    total = cum[-1]
    causal = np.tril(np.ones((TB, TB), dtype=bool))
    diff = cum[:, None, :] - cum[None, :, :]
    dec = np.where(causal[:, :, None], np.exp(np.where(causal[:, :, None], diff, 0.0)), 0.0)
    rdec = np.moveaxis(dec, 2, 0)
    recum = np.broadcast_to(np.exp(cum).T[:, :, None], (NH, TB, HD))
    rkdec = np.broadcast_to(np.exp(total[None, :] - cum).T[:, :, None], (NH, TB, HD))
    retot = np.broadcast_to(np.exp(total)[:, None, None], (NH, 1, HD))
    pair = lambda t: np.concatenate([t[0::2], t[1::2]], axis=-1).astype(np.float32)
    return pair(rdec), pair(recum), pair(rkdec), pair(retot), log_gamma


def _sel_matrix():
    sel = np.zeros((SUB, 2, HD, 2, TB), np.float32)
    for s_ in range(SUB):
        for h2 in range(2):
            sel[s_, h2, :, h2, s_::SUB] = 1.0
    return jnp.asarray(sel.reshape(SUB * LANE, LANE), dtype=BF16)


def _rows(v, width=None):
    v = v.astype(F32)
    if width is not None and v.shape[-1] < width:
        v = jnp.pad(v, ((0, 0), (0, width - v.shape[-1])))
    return v[:, None, :]


def _block_diag(blocks):
    g, r, c = blocks.shape
    eye = jnp.eye(g, dtype=blocks.dtype)
    return jnp.einsum('grc,gh->grhc', blocks, eye).reshape(g * r, g * c)


def _s5_discretise(A_re, A_im, log_dt, B_re, B_im):
    A_re, A_im = A_re.astype(F32), A_im.astype(F32)
    dt = jnp.exp(log_dt.astype(F32))[:, None]
    mag = jnp.exp(A_re * dt)
    ab_re, ab_im = mag * jnp.cos(A_im * dt), mag * jnp.sin(A_im * dt)
    nr, ni = ab_re - 1.0, ab_im
    den = A_re * A_re + A_im * A_im
    f_re = (nr * A_re + ni * A_im) / den
    f_im = (ni * A_re - nr * A_im) / den
    B_re, B_im = B_re.astype(F32), B_im.astype(F32)
    bb_re = f_re[..., None] * B_re - f_im[..., None] * B_im
    bb_im = f_re[..., None] * B_im + f_im[..., None] * B_re
    return ab_re, ab_im, bb_re, bb_im


def _s5_matrices(A_re, A_im, log_dt, B_re, B_im, C_re, C_im):
    ab_re, ab_im, bb_re, bb_im = _s5_discretise(A_re, A_im, log_dt, B_re, B_im)
    bblk = jnp.concatenate([_block_diag(jnp.swapaxes(bb_re, 1, 2)),
                            _block_diag(jnp.swapaxes(bb_im, 1, 2))], axis=1)
    cblk = jnp.concatenate([_block_diag(jnp.swapaxes(C_re.astype(F32), 1, 2)),
                            _block_diag(jnp.swapaxes(-C_im.astype(F32), 1, 2))], axis=0)
    return jnp.stack([ab_re.reshape(-1), ab_im.reshape(-1)], axis=0), bblk, cblk


R_COS, R_SIN, R_RETNW, R_LB, R_HNW, R_S5D, R_GLUB, R_MNW = [i * W for i in range(8)]
R_AR = 8 * W
R_AI = R_AR + S5N
R_DTB = R_AI + S5N
R_ALOG = R_DTB + 8
R_MD = R_ALOG + 8
N_COLP = R_MD + 8


def _prepare(p, lb_all, prompt_len):
    depth = p["norm_w"].shape[0]
    s5_A, bblk, cblk = jax.vmap(_s5_matrices)(p["s5_A_re"], p["s5_A_im"], p["s5_log_dt"], p["s5_B_re"],
                                              p["s5_B_im"], p["s5_C_re"], p["s5_C_im"])
    cos_p, sin_p = _rope_tables(np.arange(prompt_len, dtype=np.float32))
    rdec, recum, rkdec, retot, log_gamma = _retention_tables()
    cos_s, sin_s = _rope_tables(np.float32(PAST_LEN) + np.arange(1, dtype=np.float32))
    w = dict(
        norm_w=_rows(p["norm_w"]),
        w_in=jnp.pad(p["w_in"].astype(BF16), ((0, 0), (0, 0), (0, P_PAD - P_TOTAL))),
        w_out=p["w_out"].astype(BF16),
        ret_norm_w=_rows(p["ret_norm_w"]),
        lb=_rows(lb_all),
        hgrn_norm_w=_rows(p["hgrn_norm_w"]),
        s5_A=s5_A,
        s5_Bblk=bblk.astype(BF16),
        s5_Cblk=cblk.astype(BF16),
        s5_D=_rows(p["s5_D"]),
        s5_glu_w=p["s5_glu_w"].astype(BF16),
        s5_glu_b=_rows(p["s5_glu_b"]),
        m2_conv_w=p["m2_conv_w"].astype(F32),
        m2_conv_b=_rows(p["m2_conv_b"]),
        m2_dt_bias=_rows(p["m2_dt_bias"], LANE),
        m2_A_log=_rows(jnp.repeat(p["m2_A_log"], HD, axis=1)),
        m2_D=_rows(jnp.repeat(p["m2_D"], HD, axis=1)),
        m2_norm_w=_rows(p["m2_norm_w"]),
        final_norm_w=p["final_norm_w"].astype(F32).reshape(1, D),
    )
    tile2 = lambda t: jnp.asarray(np.broadcast_to(np.tile(t[0], W // LANE), (depth, W)))
    colp = jnp.concatenate(
        [tile2(cos_s), tile2(sin_s), w["ret_norm_w"][:, 0], w["lb"][:, 0], w["hgrn_norm_w"][:, 0],
         w["s5_D"][:, 0], w["s5_glu_b"][:, 0], w["m2_norm_w"][:, 0], s5_A[:, 0], s5_A[:, 1],
         _rows(p["m2_dt_bias"], 8)[:, 0], _rows(p["m2_A_log"], 8)[:, 0], _rows(p["m2_D"], 8)[:, 0]], axis=1)
    w.update(
        colp=jnp.broadcast_to(colp[:, :, None], (depth, N_COLP, LANE)),
        s5_Bblk_cols=jnp.moveaxis(w["s5_Bblk"].reshape(depth, W, NBC, BCW), 2, 1),
        s5_BblkT=jnp.swapaxes(w["s5_Bblk"], 1, 2),
        s5_CblkT=jnp.swapaxes(w["s5_Cblk"], 1, 2),
        s5_glu_wT=jnp.swapaxes(w["s5_glu_w"], 1, 2),
        rgam=jnp.asarray(np.broadcast_to(np.exp(log_gamma)[:, None, None], (NH, 8, LANE)), dtype=F32),
        cos_p=jnp.asarray(cos_p), sin_p=jnp.asarray(sin_p), rdec=jnp.asarray(rdec), recum=jnp.asarray(recum),
        rkdec=jnp.asarray(rkdec), retot=jnp.asarray(retot), sel=_sel_matrix(),
        bones=jnp.asarray(np.kron(np.eye(LANE // HD), np.full((HD, HD), 1.0 / HD)), dtype=BF16),
        hexp=jnp.asarray(np.kron(np.eye(LANE, NH), np.ones((1, HD))), dtype=F32),
    )
    return w


KC = 16
NKC = HD // KC
STEPS = NH * NKC


def _rotary_cols(x, cos, sin_signed):
    half = HD // 2
    parts = []
    for h in range(NH):
        parts += [x[h * HD + half:(h + 1) * HD], x[h * HD:h * HD + half]]
    return x * cos + jnp.concatenate(parts, axis=0) * sin_signed


def _expand_rows(dst, x):
    for c in range(x.shape[0]):
        dst[c] = jnp.broadcast_to(x[c:c + 1, :], (8, LANE))


def _sample_body(n_steps, x_ref, normw_ref, win_ref, wout_ref, fnw_ref, colp_ref, sbt_ref, sct_ref,
                 gluwt_ref, cw_ref, cb_ref, rgam_ref,
                 ret_in, hg_in, m2_in, s5re_in, s5im_in, buf_in,
                 y_ref, ret_out, hg_out, m2_out, s5re_out, s5im_out, buf_out,
                 xs_s, pt_s, vt_s, ot_s, mixt_s, o_s, hp_s,
                 kret_s, qret_s, khg_s, qhg_s, ahg_s, km2_s, qm2_s):
    i = pl.program_id(0)
    r = i % STEPS
    h = r // NKC
    kc = r % NKC

    def cp(r0, n=W):
        return colp_ref[r0:r0 + n, :]

    def pc(c0, w):
        return pt_s[c0:c0 + w, :]

    @pl.when(i == 0)
    def _load():
        xs_s[...] = x_ref[...]

    @pl.when(r == 0)
    def _prep():
        x = xs_s[...]
        hn = x * lax.rsqrt(jnp.mean(x * x, axis=-1, keepdims=True) + EPS) * normw_ref[...]
        proj = jnp.dot(hn.astype(BF16), win_ref[...], preferred_element_type=F32)

        xnew = proj[:, C_XBC:C_XBC + CONV_CH]
        acc = cb_ref[...] + xnew * cw_ref[CONV_K - 1:CONV_K, :]
        for t in range(CONV_K - 1):
            acc = acc + buf_in[t] * cw_ref[t:t + 1, :]
        for t in range(CONV_K - 2):
            buf_out[t] = buf_in[t + 1]
        buf_out[CONV_K - 2] = xnew
        xbc = _silu(acc)

        for t in range(P_PAD // LANE):
            c0 = t * LANE
            if C_XBC <= c0 < C_XBC + CONV_CH:
                tile = xbc[:, c0 - C_XBC:c0 - C_XBC + LANE]
            else:
                tile = proj[:, c0:c0 + LANE]
            pt_s[c0:c0 + LANE, :] = tile.T

        cos, sin = cp(R_COS), cp(R_SIN)
        _expand_rows(kret_s, _rotary_cols(pc(C_RK, W), cos, sin) * (HD ** -0.5))
        _expand_rows(qret_s, _rotary_cols(pc(C_RQ, W), cos, sin))
        vt_s[0] = pc(C_RV, W)

        fr = pc(C_GF, W)
        lb = cp(R_LB)
        logf = _log_sigmoid(fr) + jnp.log(1.0 + lb * jnp.exp(jnp.minimum(-fr, EXP_CLIP)))
        _expand_rows(ahg_s, jnp.exp(logf))
        _expand_rows(khg_s, (1.0 - lb) * jax.nn.sigmoid(-fr))
        _expand_rows(qhg_s, _silu(pc(C_GQ, W)))
        vt_s[1] = pc(C_GI, W)

        dt8 = _softplus(pc(C_DT, 8) + cp(R_DTB, 8))
        adec8 = jnp.exp(dt8 * (-jnp.exp(cp(R_ALOG, 8))))
        for hh in range(NH):
            hp_s[hh] = jnp.broadcast_to(adec8[hh:hh + 1, :], (8, LANE))
            vt_s[2, hh * HD:(hh + 1) * HD, :] = pc(C_XBC + hh * HD, HD) * dt8[hh:hh + 1, :]
        _expand_rows(km2_s, pc(C_XBC + W, 2 * HD))
        _expand_rows(qm2_s, pc(C_XBC + W + 2 * HD, 2 * HD))

        u = pc(C_SU, W)
        bu = jnp.dot(sbt_ref[...], u.astype(BF16), preferred_element_type=F32)
        hr, hi = s5re_in[...], s5im_in[...]
        ar, ai = cp(R_AR, S5N), cp(R_AI, S5N)
        nr = ar * hr - ai * hi + bu[0:S5N]
        ni = ar * hi + ai * hr + bu[S5N:2 * S5N]
        s5re_out[...] = nr
        s5im_out[...] = ni
        hcat = jnp.concatenate([nr, ni], axis=0).astype(BF16)
        sy = jnp.dot(sct_ref[...], hcat, preferred_element_type=F32) + cp(R_S5D) * u
        gy = _gelu_tanh(sy)
        glu = jnp.dot(gluwt_ref[...], gy.astype(BF16), preferred_element_type=F32) + cp(R_GLUB)
        mixt_s[2 * W:3 * W, :] = gy * jax.nn.sigmoid(glu) * _silu(pc(C_SG, W))

    @pl.when(kc == 0)
    def _zero():
        o_s[...] = jnp.zeros(o_s.shape, F32)

    hrow = pl.multiple_of(h * HD, HD)
    cbase = h * HD + kc * KC
    gbase = (h // 2) * HD + kc * KC

    def update(m, st_in, st_out, kx, qx, base, decay):
        v3 = vt_s[m, pl.ds(hrow, HD), :].reshape(HD // 8, 8, LANE)

        def body(kk, o):
            s_new = decay(kk) * st_in[kk].reshape(HD // 8, 8, LANE) + kx[base + kk] * v3
            st_out[kk] = s_new.reshape(HD, LANE)
            return o + qx[base + kk] * s_new
        o_s[m] = lax.fori_loop(0, KC, body, o_s[m], unroll=2)

    gam = rgam_ref[h]
    update(0, ret_in, ret_out, kret_s, qret_s, cbase, lambda kk: gam)
    update(1, hg_in, hg_out, khg_s, qhg_s, cbase, lambda kk: ahg_s[cbase + kk])
    adec = hp_s[h]
    update(2, m2_in, m2_out, km2_s, qm2_s, gbase, lambda kk: adec)

    @pl.when(kc == NKC - 1)
    def _head_done():
        for m in range(3):
            ot_s[m, pl.ds(hrow, HD), :] = o_s[m].reshape(HD, LANE)

    @pl.when(r == STEPS - 1)
    def _finish():
        def head_rms_cols(o):
            parts = []
            for hh in range(NH):
                seg = o[hh * HD:(hh + 1) * HD]
                parts.append(seg * lax.rsqrt(jnp.mean(seg * seg, axis=0, keepdims=True) + EPS))
            return jnp.concatenate(parts, axis=0)

        mixt_s[0:W, :] = head_rms_cols(ot_s[0]) * cp(R_RETNW) * _silu(pc(C_RG, W))
        mixt_s[W:2 * W, :] = head_rms_cols(ot_s[1]) * cp(R_HNW) * _silu(pc(C_GG, W))
        md8 = cp(R_MD, 8)
        ym = jnp.concatenate([ot_s[2, hh * HD:(hh + 1) * HD, :] + md8[hh:hh + 1, :] * pc(C_XBC + hh * HD, HD)
                              for hh in range(NH)], axis=0)
        my = ym * _silu(pc(C_MZ, W))
        mixt_s[3 * W:4 * W, :] = my * lax.rsqrt(jnp.mean(my * my, axis=0, keepdims=True) + EPS) * cp(R_MNW)
        mixed = jnp.concatenate([mixt_s[t * LANE:(t + 1) * LANE, :].T for t in range(D // LANE)], axis=1)
        xo = xs_s[...] + jnp.dot(mixed.astype(BF16), wout_ref[...], preferred_element_type=F32)
        xs_s[...] = xo

        @pl.when(i == n_steps - 1)
        def _final_norm():
            y_ref[...] = xo * lax.rsqrt(jnp.mean(xo * xo, axis=-1, keepdims=True) + EPS) * fnw_ref[...]


def _sample_step(x, w, ret, hg, m2, s5re, s5im, buf):
    depth = ret.shape[0]
    n = x.shape[0]
    n_steps = depth * STEPS
    lay = lambda i: i // STEPS

    def per_layer(a):
        nd = a.ndim
        return pl.BlockSpec((None,) + a.shape[1:], lambda i: (lay(i),) + (0,) * (nd - 1))

    st_spec = pl.BlockSpec((None, None, KC, HD, LANE),
                           lambda i: (lay(i), (i % STEPS) // NKC, i % NKC, 0, 0))
    weights = [w["norm_w"], w["w_in"], w["w_out"]]
    tables = [w["colp"], w["s5_BblkT"], w["s5_CblkT"], w["s5_glu_wT"], w["m2_conv_w"], w["m2_conv_b"]]
    in_specs = ([_const_spec(x.shape)] + [per_layer(a) for a in weights] + [_const_spec(w["final_norm_w"].shape)]
                + [per_layer(a) for a in tables] + [_const_spec(w["rgam"].shape)]
                + [st_spec, st_spec, st_spec, per_layer(s5re), per_layer(s5im), per_layer(buf)])
    out_shape = [jax.ShapeDtypeStruct((n, D), F32)] + [jax.ShapeDtypeStruct(a.shape, F32)
                                                       for a in (ret, hg, m2, s5re, s5im, buf)]
    out_specs = [_const_spec((n, D)), st_spec, st_spec, st_spec, per_layer(s5re), per_layer(s5im),
                 per_layer(buf)]
    expand = lambda c: pltpu.VMEM((c, 8, LANE), F32)
    scratch = [pltpu.VMEM((n, D), F32),
               pltpu.VMEM((P_PAD, LANE), F32),
               pltpu.VMEM((3, W, LANE), F32),
               pltpu.VMEM((3, W, LANE), F32),
               pltpu.VMEM((D, LANE), F32),
               pltpu.VMEM((3, HD // 8, 8, LANE), F32),
               pltpu.VMEM((NH, 8, LANE), F32),
               expand(W), expand(W), expand(W), expand(W), expand(W), expand(2 * HD), expand(2 * HD)]
    return pl.pallas_call(
        functools.partial(_sample_body, n_steps),
        grid=(n_steps,),
        in_specs=in_specs,
        out_specs=out_specs,
        out_shape=out_shape,
        scratch_shapes=scratch,
        compiler_params=pltpu.CompilerParams(dimension_semantics=("arbitrary",),
                                             vmem_limit_bytes=VMEM_LIMIT),
        name="sample_step",
    )(x, *weights, w["final_norm_w"], *tables, w["rgam"], ret, hg, m2, s5re, s5im, buf)


def kernel(x_prompt, x_sample, state_ret, state_hgrn, state_s5_re, state_s5_im, state_m2_ssm,
           state_m2_conv, norm_w, w_in, ret_norm_w, hgrn_lb_logits, hgrn_norm_w, s5_A_re, s5_A_im,
           s5_log_dt, s5_B_re, s5_B_im, s5_C_re, s5_C_im, s5_D, s5_glu_w, s5_glu_b, m2_conv_w,
           m2_conv_b, m2_dt_bias, m2_A_log, m2_D, m2_norm_w, w_out, final_norm_w):
    p = dict(norm_w=norm_w, w_in=w_in, ret_norm_w=ret_norm_w, hgrn_norm_w=hgrn_norm_w,
             s5_A_re=s5_A_re, s5_A_im=s5_A_im, s5_log_dt=s5_log_dt, s5_B_re=s5_B_re, s5_B_im=s5_B_im,
             s5_C_re=s5_C_re, s5_C_im=s5_C_im, s5_D=s5_D, s5_glu_w=s5_glu_w, s5_glu_b=s5_glu_b,
             m2_conv_w=m2_conv_w, m2_conv_b=m2_conv_b, m2_dt_bias=m2_dt_bias, m2_A_log=m2_A_log,
             m2_D=m2_D, m2_norm_w=m2_norm_w, w_out=w_out, final_norm_w=final_norm_w)
    depth = norm_w.shape[0]
    nbp, lp, _ = x_prompt.shape
    nbs = x_sample.shape[0]

    lb_sm = jax.nn.softmax(hgrn_lb_logits.astype(F32), axis=0)
    lb_all = jnp.clip(jnp.cumsum(lb_sm, axis=0) - lb_sm[0], 0.0, 1.0)

    w = _prepare(p, lb_all, lp)

    xp = x_prompt
    pst = []
    for l in range(depth):
        outs = _prompt_layer(xp, l, l == depth - 1, w)
        xp = outs[0]
        ret, hg, s5, m2, buf = outs[-5:]
        pst.append((ret, hg, s5[:, :S5N].reshape(nbp, S5G, S5P), s5[:, S5N:].reshape(nbp, S5G, S5P),
                    m2, buf))
    yp = xp

    seq_last = lambda a: jnp.moveaxis(a.astype(F32), 1, -1)
    ys, ret, hg, m2, s5re, s5im, buf = _sample_step(
        x_sample.reshape(nbs, D), w,
        seq_last(state_ret), seq_last(state_hgrn), seq_last(state_m2_ssm),
        seq_last(state_s5_re).reshape(depth, S5N, nbs), seq_last(state_s5_im).reshape(depth, S5N, nbs),
        jnp.swapaxes(state_m2_conv.astype(F32), 1, 2))
    seq_second = lambda a: jnp.moveaxis(a, -1, 1)

    stk = lambda i: jnp.stack([s[i] for s in pst], axis=0)
    return (yp, ys.reshape(nbs, 1, D),
            stk(0), stk(1), stk(2), stk(3), stk(4), stk(5),
            seq_second(ret), seq_second(hg), seq_second(s5re.reshape(depth, S5G, S5P, nbs)),
            seq_second(s5im.reshape(depth, S5G, S5P, nbs)), seq_second(m2), jnp.swapaxes(buf, 1, 2))
```

```python
import functools
import math

import numpy as np
import jax
import jax.numpy as jnp
from jax import lax
from jax.experimental import pallas as pl
from jax.experimental.pallas import tpu as pltpu

F32 = jnp.float32
BF16 = jnp.bfloat16

D = 1024
W = 256
NH = 4
HD = 64
S5G = 16
S5C = 16
S5P = 64
S5N = S5G * S5P
NBC = 8
BCW = 2 * S5N // NBC
CONV_CH = 512
CONV_K = 4
TB = 64
SUB = 16
NSUB = TB // SUB
EPS = 1e-6
EXP_CLIP = 60.0
ROPE_BASE = 10000.0
PAST_LEN = 16384

C_RQ, C_RK, C_RV, C_RG = 0, 256, 512, 768
C_GQ, C_GF, C_GI, C_GG = 1024, 1280, 1536, 1792
C_SU, C_SG = 2048, 2304
C_MZ, C_XBC, C_DT = 2560, 2816, 3328
P_TOTAL = 3332
PCH = 1024
P_PAD = 3456
LANE = 128
VMEM_LIMIT = 56 * 1024 * 1024


def _silu(x):
    return x * jax.nn.sigmoid(x)


def _softplus(x):
    return jnp.maximum(x, 0.0) + jnp.log(1.0 + jnp.exp(-jnp.abs(x)))


def _log_sigmoid(x):
    return jnp.minimum(x, 0.0) - jnp.log(1.0 + jnp.exp(-jnp.abs(x)))


def _round_robin(gens):
    gens = list(gens)
    while gens:
        alive = []
        for g in gens:
            try:
                next(g)
                alive.append(g)
            except StopIteration:
                pass
        gens = alive


def _for_sequences(nb, parts, group):
    def body(i, c):
        built = [parts(i * group + k) for k in range(group)]
        _round_robin([g for gens, _ in built for g in gens])
        for _, finish in built:
            if finish is not None:
                finish()
        return c
    lax.fori_loop(0, nb // group, body, 0, unroll=2)


def _gelu_tanh(x):
    c = math.sqrt(2.0 / math.pi)
    return 0.5 * x * (1.0 + jnp.tanh(c * (x + 0.044715 * (x * x * x))))


def _dot(a, b):
    return jnp.dot(a.astype(BF16), b.astype(BF16), preferred_element_type=F32)


def _split3(x):
    hi = x.astype(BF16)
    rest = x - hi.astype(F32)
    mid = rest.astype(BF16)
    return hi, mid, (rest - mid.astype(F32)).astype(BF16)


def _select_rows(m01, x):
    m = m01.astype(BF16)
    hi, mid, lo = _split3(x)
    return (jnp.dot(m, hi, preferred_element_type=F32) + jnp.dot(m, mid, preferred_element_type=F32)
            + jnp.dot(m, lo, preferred_element_type=F32))


def _select_cols(x, m01):
    m = m01.astype(BF16)
    hi, mid, lo = _split3(x)
    return (jnp.dot(hi, m, preferred_element_type=F32) + jnp.dot(mid, m, preferred_element_type=F32)
            + jnp.dot(lo, m, preferred_element_type=F32))


def _rot_half_partner(x):
    lane = lax.broadcasted_iota(jnp.int32, x.shape, 1)
    first = (lane % HD) < (HD // 2)
    return jnp.where(first, pltpu.roll(x, LANE - HD // 2, 1), pltpu.roll(x, HD // 2, 1))


def _rotary(x, cos, sin_signed):
    parts = []
    for i in range(W // LANE):
        xi = x[:, i * LANE:(i + 1) * LANE]
        parts.append(xi * cos + _rot_half_partner(xi) * sin_signed)
    return jnp.concatenate(parts, axis=1)


def _prompt_layer_body(last, *refs):
    (x_ref, cos_ref, sin_ref, normw_ref, win_ref, wout_ref,
     rdec_ref, recum_ref, rkdec_ref, retot_ref, retnw_ref,
     lb_ref, hnw_ref, sel_ref,
     sA_ref, sB_ref, sC_ref, sD_ref, gluw_ref, glub_ref,
     cw_ref, cb_ref, dtb_ref, alog_ref, md_ref, mnw_ref, bones_ref, hexp_ref) = refs[:28]
    refs = refs[28:]
    if last:
        fnw_ref = refs[0]
        refs = refs[1:]
    act_ref = refs[0]
    refs = refs[1:]
    (ret_ref, hg_ref, s5_ref, m2_ref, m2buf_ref,
     hn_s, proj_s, mixed_s, bu_s, u_s, ub_s, cv_s, p_s, dg_s, hq_s, hk_s, cum_s, hret_s, hhg_s, hm2_s) = refs
    j = pl.program_id(0)
    nb = x_ref.shape[0]

    @pl.when(j == 0)
    def _init():
        hret_s[...] = jnp.zeros(hret_s.shape, F32)
        hhg_s[...] = jnp.zeros(hhg_s.shape, F32)
        s5_ref[...] = jnp.zeros(s5_ref.shape, F32)
        hm2_s[...] = jnp.zeros(hm2_s.shape, F32)
        cv_s[...] = jnp.zeros(cv_s.shape, F32)

    ti = lax.broadcasted_iota(jnp.int32, (TB, TB), 0)
    si = lax.broadcasted_iota(jnp.int32, (TB, TB), 1)
    causal = si <= ti
    tri_l = causal.astype(F32)
    pr = lax.broadcasted_iota(jnp.int32, (LANE, LANE), 0)
    pc_ = lax.broadcasted_iota(jnp.int32, (LANE, LANE), 1)
    pairmask = (pr // HD) == (pc_ // HD)
    t2 = lax.broadcasted_iota(jnp.int32, (TB, LANE), 0)
    l2 = lax.broadcasted_iota(jnp.int32, (TB, LANE), 1)
    causal2 = (l2 % HD) <= t2
    first_head = l2 < HD
    subdiag2 = (t2 // SUB) == ((l2 % HD) // SUB)
    srcblock = [pairmask & (((pc_ % HD) // SUB) == jb) for jb in range(NSUB - 1)]

    def dup_t(x2):
        return jnp.concatenate([x2, x2], axis=0).T

    def blockdiag2(x2):
        return jnp.where(pairmask, jnp.concatenate([x2, x2], axis=0), 0.0)

    def norm_body(b, c):
        xb = x_ref[b]
        hn = xb * lax.rsqrt(jnp.mean(xb * xb, axis=-1, keepdims=True) + EPS) * normw_ref[...]
        hn_s[pl.ds(pl.multiple_of(b * TB, TB), TB), :] = hn.astype(BF16)
        return c
    lax.fori_loop(0, nb, norm_body, 0, unroll=True)
    for c0 in range(0, P_PAD, PCH):
        cs = slice(c0, min(c0 + PCH, P_PAD))
        proj_s[:, cs] = jnp.dot(hn_s[...], win_ref[:, cs], preferred_element_type=F32)

    cos = cos_ref[...]
    sin = sin_ref[...]

    def reorder_u(b, c):
        u = proj_s[pl.ds(pl.multiple_of(b * TB, TB), TB), C_SU:C_SU + W]
        for c_ in range(W // LANE):
            u_s[c_, pl.ds(b, TB, stride=nb), :] = u[:, c_ * LANE:(c_ + 1) * LANE]
        return c
    lax.fori_loop(0, nb, reorder_u, 0)
    ub_s[...] = jnp.concatenate([u_s[c_] for c_ in range(W // LANE)], axis=1).astype(BF16)

    def s5_input_piece(cb):
        bu_s[cb] = jnp.dot(ub_s[...], sB_ref[cb], preferred_element_type=F32)
        yield

    def phase1_parts(b):
        r0 = pl.multiple_of(b * TB, TB)

        def pj(c0, w):
            return proj_s[pl.ds(r0, TB), c0:c0 + w]

        rq = _rotary(pj(C_RQ, W), cos, sin)
        rk = _rotary(pj(C_RK, W), cos, sin) * (HD ** -0.5)
        rv = pj(C_RV, W)
        rg = pj(C_RG, W)

        def ret_pair(p):
            ls = slice(p * LANE, (p + 1) * LANE)
            q2, k2, v2 = rq[:, ls], rk[:, ls], rv[:, ls]
            hb = hret_s[b, p]
            kt = dup_t(k2)
            s_raw = _dot(q2, jnp.where(pairmask, kt, 0.0))
            oi = _dot(q2, hb)
            kv = _dot(kt[:, 0:HD], v2 * rkdec_ref[p])
            yield
            o = _dot(s_raw * rdec_ref[p], blockdiag2(v2))
            yield
            o = o + oi * recum_ref[p]
            hret_s[b, p] = jnp.where(pairmask, retot_ref[p] * hb + kv, 0.0)
            ms = _dot(o * o, bones_ref[...])
            yield
            o = o * lax.rsqrt(ms + EPS) * retnw_ref[:, ls] * _silu(rg[:, ls])
            mixed_s[pl.ds(r0, TB), 0 * W + p * LANE:0 * W + (p + 1) * LANE] = o.astype(BF16)

        cv_s[b, 8:8 + TB, :] = pj(C_XBC, CONV_CH)
        acc = cb_ref[...] + cv_s[b, 5:5 + TB, :] * cw_ref[0:1, :]
        for i in range(1, CONV_K):
            acc = acc + cv_s[b, 5 + i:5 + i + TB, :] * cw_ref[i:i + 1, :]
        tail = cv_s[b, TB + 5:TB + 8, :]
        cv_s[b, 5:8, :] = tail
        m2buf_ref[b] = tail
        xbc = _silu(acc)
        xm = xbc[:, 0:W]
        bm = xbc[:, W:W + 2 * HD]
        cm = xbc[:, W + 2 * HD:W + 4 * HD]
        bm_sw = pltpu.roll(bm, HD, 1)
        cm_sw = pltpu.roll(cm, HD, 1)
        dt_b = _select_cols(_softplus(pj(C_DT, LANE) + dtb_ref[...]), hexp_ref[...])
        la_b = dt_b * (-jnp.exp(alog_ref[...]))
        cum_b = _select_rows(tri_l, la_b)
        ys = [None] * (W // LANE)

        def m2_pair(p):
            ls = slice(p * LANE, (p + 1) * LANE)
            b2 = jnp.where(first_head, bm, bm_sw) if p == 0 else jnp.where(first_head, bm_sw, bm)
            c2 = jnp.where(first_head, cm, cm_sw) if p == 0 else jnp.where(first_head, cm_sw, cm)
            x2, dt2, cum2 = xm[:, ls], dt_b[:, ls], cum_b[:, ls]
            xdt2 = x2 * dt2
            ct = dup_t(cum2)
            r2 = jnp.where(first_head, ct[0:TB], ct[HD:HD + TB])
            tot2 = cum2[TB - 1:TB, :]
            hb = hm2_s[b, p]
            bt = dup_t(b2)
            s_raw = _dot(c2, jnp.where(pairmask, bt, 0.0))
            oi = _dot(c2, hb)
            kv = _dot(bt[:, 0:HD], xdt2 * jnp.exp(tot2 - cum2))
            decay = jnp.where(causal2, jnp.exp(jnp.minimum(cum2 - r2, 0.0)), 0.0)
            yield
            o = _dot(s_raw * decay, blockdiag2(xdt2))
            yield
            hm2_s[b, p] = jnp.where(pairmask, jnp.exp(tot2) * hb + kv, 0.0)
            ys[p] = o + oi * jnp.exp(cum2) + md_ref[:, ls] * x2

        fr = pj(C_GF, W)
        lb = lb_ref[...]
        logf = _log_sigmoid(fr) + jnp.log(1.0 + lb * jnp.exp(jnp.minimum(-fr, EXP_CLIP)))
        hq = _silu(pj(C_GQ, W))
        hk = (1.0 - lb) * jax.nn.sigmoid(-fr)
        cum = _select_rows(tri_l, logf)
        hq_s[b] = hq
        hk_s[b] = hk
        cum_s[b] = cum
        t8 = lax.broadcasted_iota(jnp.int32, (8, LANE), 0)

        def diag_products(p):
            ls = slice(p * LANE, (p + 1) * LANE)
            for s_ in range(SUB):
                pieces = []
                for i in range(NSUB):
                    kb = jnp.broadcast_to(hk_s[b, i * SUB + s_:i * SUB + s_ + 1, ls], (8, LANE))
                    cb = jnp.broadcast_to(cum_s[b, i * SUB + s_:i * SUB + s_ + 1, ls], (8, LANE))
                    for half in range(SUB // 8):
                        rows = slice(i * SUB + half * 8, i * SUB + half * 8 + 8)
                        if half * 8 + 7 < s_:
                            pieces.append(jnp.zeros((8, LANE), F32))
                        elif half * 8 >= s_:
                            pieces.append(hq[rows, ls] * kb * jnp.exp(cum[rows, ls] - cb))
                        else:
                            e = jnp.exp(jnp.minimum(cum[rows, ls] - cb, 0.0))
                            pieces.append(jnp.where(t8 + half * 8 >= s_, hq[rows, ls] * kb * e, 0.0))
                pv = jnp.concatenate(pieces, axis=0)
                p_s[p, pl.ds(r0, TB), s_ * LANE:(s_ + 1) * LANE] = pv.astype(BF16)
                if s_ % 2 == 1:
                    yield

        def finish():
            my = jnp.concatenate(ys, axis=1) * _silu(pj(C_MZ, W))
            om = my * lax.rsqrt(jnp.mean(my * my, axis=-1, keepdims=True) + EPS) * mnw_ref[...]
            mixed_s[pl.ds(r0, TB), 3 * W:4 * W] = om.astype(BF16)

        gens = ([s5_input_piece(b)] + [ret_pair(p) for p in range(W // LANE)]
                + [m2_pair(p) for p in range(W // LANE)] + [diag_products(p) for p in range(W // LANE)])
        return gens, finish

    _for_sequences(nb, phase1_parts, 2)

    half_rows = nb * TB // 2
    for p in range(W // LANE):
        for r_ in range(2):
            rs = slice(r_ * half_rows, (r_ + 1) * half_rows)
            dg_s[p, rs, :] = jnp.dot(p_s[p, rs, :], sel_ref[...], preferred_element_type=F32)

    ar = jnp.broadcast_to(sA_ref[0:1, :], (nb, S5N))
    ai = jnp.broadcast_to(sA_ref[1:2, :], (nb, S5N))

    def scan_body(t, carry):
        hr, hi = carry
        row = pl.multiple_of(t * nb, nb)
        half = NBC // 2
        nr = ar * hr - ai * hi + jnp.concatenate([bu_s[c_, pl.ds(row, nb), :] for c_ in range(half)], axis=1)
        ni = ar * hi + ai * hr + jnp.concatenate([bu_s[half + c_, pl.ds(row, nb), :] for c_ in range(half)],
                                                 axis=1)
        for c_ in range(half):
            bu_s[c_, pl.ds(row, nb), :] = nr[:, c_ * BCW:(c_ + 1) * BCW]
            bu_s[half + c_, pl.ds(row, nb), :] = ni[:, c_ * BCW:(c_ + 1) * BCW]
        return nr, ni
    hr, hi = lax.fori_loop(0, TB, scan_body, (s5_ref[:, 0:S5N], s5_ref[:, S5N:2 * S5N]), unroll=2)
    s5_ref[:, 0:S5N] = hr
    s5_ref[:, S5N:2 * S5N] = hi
    ch_tb = _dot(bu_s[0], sC_ref[0:BCW, :])
    for c_ in range(1, NBC):
        ch_tb = ch_tb + _dot(bu_s[c_], sC_ref[c_ * BCW:(c_ + 1) * BCW, :])
    for c_ in range(W // LANE):
        u_s[c_] = ch_tb[:, c_ * LANE:(c_ + 1) * LANE]

    def phase2_parts(b):
        r0 = pl.multiple_of(b * TB, TB)

        def pj(c0, w):
            return proj_s[pl.ds(r0, TB), c0:c0 + w]

        def s5_out():
            chs = jnp.concatenate([u_s[c_, pl.ds(b, TB, stride=nb), :] for c_ in range(W // LANE)],
                                  axis=1)
            gy = _gelu_tanh(chs + sD_ref[...] * pj(C_SU, W))
            glu = _dot(gy, gluw_ref[...])
            yield
            os5 = gy * jax.nn.sigmoid(glu + glub_ref[...]) * _silu(pj(C_SG, W))
            mixed_s[pl.ds(r0, TB), 2 * W:3 * W] = os5.astype(BF16)

        cum = cum_s[b]
        rr = jnp.concatenate(
            [jnp.zeros((SUB, W), F32)]
            + [jnp.broadcast_to(cum_s[b, i * SUB - 1:i * SUB, :], (SUB, W))
               for i in range(1, NSUB)], axis=0)
        ee = jnp.concatenate(
            [jnp.broadcast_to(cum_s[b, i * SUB + SUB - 1:i * SUB + SUB, :], (SUB, W))
             for i in range(NSUB)], axis=0)
        totc = jnp.broadcast_to(cum[TB - 1:TB, :], (LANE, W)).T
        hq = hq_s[b]
        hk = hk_s[b]
        hv = pj(C_GI, W)
        gg = pj(C_GG, W)
        qt = hq * jnp.exp(cum - rr)
        kh_ = hk * jnp.exp(ee - cum)
        qe = hq * jnp.exp(cum)
        tot = cum[TB - 1:TB, :]
        kend = kh_ * jnp.exp(tot - ee)
        trow = lax.broadcasted_iota(jnp.int32, (TB, W), 0) // SUB
        qx = []
        for jb in range(NSUB - 1):
            eb = jnp.broadcast_to(ee[jb * SUB:jb * SUB + 1, :], (TB, W))
            qx.append(jnp.where(trow > jb, qt * jnp.exp(jnp.minimum(rr - eb, 0.0)), 0.0))

        def hg_pair(p):
            ls = slice(p * LANE, (p + 1) * LANE)
            hb = hhg_s[b, p]
            hv2 = hv[:, ls]
            kt = dup_t(kh_[:, ls])
            off = _dot(qx[0][:, ls], jnp.where(srcblock[0], kt, 0.0))
            for jb in range(1, NSUB - 1):
                off = off + _dot(qx[jb][:, ls], jnp.where(srcblock[jb], kt, 0.0))
            oi = _dot(qe[:, ls], hb)
            kv = _dot(dup_t(kend[:, ls])[:, 0:HD], hv2)
            yield
            o = _dot(jnp.where(subdiag2, dg_s[p, pl.ds(r0, TB), :], 0.0) + off, blockdiag2(hv2))
            yield
            hhg_s[b, p] = jnp.where(pairmask, jnp.exp(totc[ls, :]) * hb + kv, 0.0)
            o = o + oi
            ms = _dot(o * o, bones_ref[...])
            yield
            o = o * lax.rsqrt(ms + EPS) * hnw_ref[:, ls] * _silu(gg[:, ls])
            mixed_s[pl.ds(r0, TB), 1 * W + p * LANE:1 * W + (p + 1) * LANE] = o.astype(BF16)

        return [s5_out()] + [hg_pair(p) for p in range(W // LANE)], None

    _for_sequences(nb, phase2_parts, 8)

    @pl.when(j == pl.num_programs(0) - 1)
    def _emit_states():
        for b in range(nb):
            for h in range(NH):
                p, h2 = divmod(h, 2)
                blk = (slice(h2 * HD, (h2 + 1) * HD),) * 2
                ret_ref[b, h] = hret_s[(b, p) + blk]
                hg_ref[b, h] = hhg_s[(b, p) + blk]
                m2_ref[b, h] = hm2_s[(b, p) + blk]

    res = jnp.dot(mixed_s[...], wout_ref[...], preferred_element_type=F32)
    for b in range(nb):
        xo = x_ref[b] + res[b * TB:(b + 1) * TB]
        if last:
            xo = xo * lax.rsqrt(jnp.mean(xo * xo, axis=-1, keepdims=True) + EPS) * fnw_ref[...]
        act_ref[b] = xo


def _const_spec(shape):
    nd = len(shape)
    return pl.BlockSpec(shape, lambda j: (0,) * nd)


def _prompt_layer(x, l, last, w):
    nb, L, _ = x.shape
    nblk = L // TB
    rows = nb * TB
    xspec = pl.BlockSpec((nb, TB, D), lambda j: (0, j, 0))
    tspec = pl.BlockSpec((TB, LANE), lambda j: (j, 0))

    def per_layer(a):
        nd = a.ndim
        return pl.BlockSpec((None,) + a.shape[1:], lambda j: (l,) + (0,) * (nd - 1))

    stacked = [w[k] for k in ("norm_w", "w_in", "w_out")]
    consts = [w[k] for k in ("rdec", "recum", "rkdec", "retot")]
    ret_nw = [w["ret_norm_w"], w["lb"], w["hgrn_norm_w"]]
    s5 = [w[k] for k in ("s5_A", "s5_Bblk_cols", "s5_Cblk", "s5_D", "s5_glu_w", "s5_glu_b")]
    m2 = [w[k] for k in ("m2_conv_w", "m2_conv_b", "m2_dt_bias", "m2_A_log", "m2_D", "m2_norm_w")]
    pair_consts = [w["bones"], w["hexp"]]
    args = [x, w["cos_p"], w["sin_p"]] + stacked + consts + ret_nw + [w["sel"]] + s5 + m2 + pair_consts
    in_specs = ([xspec, tspec, tspec] + [per_layer(a) for a in stacked] + [_const_spec(a.shape) for a in consts]
                + [per_layer(a) for a in ret_nw] + [_const_spec(w["sel"].shape)]
                + [per_layer(a) for a in s5 + m2] + [_const_spec(a.shape) for a in pair_consts])
    if last:
        args.append(w["final_norm_w"])
        in_specs.append(_const_spec(w["final_norm_w"].shape))
    state_shapes = [jax.ShapeDtypeStruct((nb, NH, HD, HD), F32),
                    jax.ShapeDtypeStruct((nb, NH, HD, HD), F32),
                    jax.ShapeDtypeStruct((nb, 2 * S5N), F32),
                    jax.ShapeDtypeStruct((nb, NH, HD, HD), F32),
                    jax.ShapeDtypeStruct((nb, CONV_K - 1, CONV_CH), F32)]
    out_shape = [jax.ShapeDtypeStruct((nb, L, D), F32)] + state_shapes
    out_specs = [xspec] + [_const_spec(s.shape) for s in state_shapes]
    scratch = [pltpu.VMEM((rows, D), BF16),
               pltpu.VMEM((rows, P_PAD), F32),
               pltpu.VMEM((rows, D), BF16),
               pltpu.VMEM((NBC, rows, BCW), F32),
               pltpu.VMEM((W // LANE, rows, LANE), F32),
               pltpu.VMEM((rows, W), BF16),
               pltpu.VMEM((nb, TB + 8, CONV_CH), F32),
               pltpu.VMEM((W // LANE, rows, SUB * LANE), BF16),
               pltpu.VMEM((W // LANE, rows, LANE), F32),
               pltpu.VMEM((nb, TB, W), F32),
               pltpu.VMEM((nb, TB, W), F32),
               pltpu.VMEM((nb, TB, W), F32)] + [pltpu.VMEM((nb, W // LANE, LANE, LANE), F32)] * 3
    return pl.pallas_call(
        functools.partial(_prompt_layer_body, last),
        grid=(nblk,),
        in_specs=in_specs,
        out_specs=out_specs,
        out_shape=out_shape,
        scratch_shapes=scratch,
        compiler_params=pltpu.CompilerParams(dimension_semantics=("arbitrary",),
                                             vmem_limit_bytes=VMEM_LIMIT),
        name="prompt_layer",
    )(*args)


def _rope_tables(pos):
    half = HD // 2
    inv = 1.0 / (ROPE_BASE ** (np.arange(half, dtype=np.float64) / half))
    ang = pos.astype(np.float64)[:, None] * inv[None, :]
    cos, sin = np.cos(ang), np.sin(ang)
    cos_t = np.tile(cos, (1, LANE // half))
    sin_t = np.tile(np.concatenate([-sin, sin], axis=1), (1, LANE // HD))
    return cos_t.astype(np.float32), sin_t.astype(np.float32)


def _retention_tables():
    log_gamma = np.log1p(-(2.0 ** (-5.0 - np.arange(NH, dtype=np.float64))))
    cum = np.cumsum(np.broadcast_to(log_gamma, (TB, NH)), axis=0)
    total = cum[-1]
    causal = np.tril(np.ones((TB, TB), dtype=bool))
    diff = cum[:, None, :] - cum[None, :, :]
    dec = np.where(causal[:, :, None], np.exp(np.where(causal[:, :, None], diff, 0.0)), 0.0)
    rdec = np.moveaxis(dec, 2, 0)
    recum = np.broadcast_to(np.exp(cum).T[:, :, None], (NH, TB, HD))
    rkdec = np.broadcast_to(np.exp(total[None, :] - cum).T[:, :, None], (NH, TB, HD))
    retot = np.broadcast_to(np.exp(total)[:, None, None], (NH, 1, HD))
    pair = lambda t: np.concatenate([t[0::2], t[1::2]], axis=-1).astype(np.float32)
    return pair(rdec), pair(recum), pair(rkdec), pair(retot), log_gamma


def _sel_matrix():
    sel = np.zeros((SUB, 2, HD, 2, TB), np.float32)
    for s_ in range(SUB):
        for h2 in range(2):
            sel[s_, h2, :, h2, s_::SUB] = 1.0
    return jnp.asarray(sel.reshape(SUB * LANE, LANE), dtype=BF16)


def _rows(v, width=None):
    v = v.astype(F32)
    if width is not None and v.shape[-1] < width:
        v = jnp.pad(v, ((0, 0), (0, width - v.shape[-1])))
    return v[:, None, :]


def _block_diag(blocks):
    g, r, c = blocks.shape
    eye = jnp.eye(g, dtype=blocks.dtype)
    return jnp.einsum('grc,gh->grhc', blocks, eye).reshape(g * r, g * c)


def _s5_discretise(A_re, A_im, log_dt, B_re, B_im):
    A_re, A_im = A_re.astype(F32), A_im.astype(F32)
    dt = jnp.exp(log_dt.astype(F32))[:, None]
    mag = jnp.exp(A_re * dt)
    ab_re, ab_im = mag * jnp.cos(A_im * dt), mag * jnp.sin(A_im * dt)
    nr, ni = ab_re - 1.0, ab_im
    den = A_re * A_re + A_im * A_im
    f_re = (nr * A_re + ni * A_im) / den
    f_im = (ni * A_re - nr * A_im) / den
    B_re, B_im = B_re.astype(F32), B_im.astype(F32)
    bb_re = f_re[..., None] * B_re - f_im[..., None] * B_im
    bb_im = f_re[..., None] * B_im + f_im[..., None] * B_re
    return ab_re, ab_im, bb_re, bb_im


def _s5_matrices(A_re, A_im, log_dt, B_re, B_im, C_re, C_im):
    ab_re, ab_im, bb_re, bb_im = _s5_discretise(A_re, A_im, log_dt, B_re, B_im)
    bblk = jnp.concatenate([_block_diag(jnp.swapaxes(bb_re, 1, 2)),
                            _block_diag(jnp.swapaxes(bb_im, 1, 2))], axis=1)
    cblk = jnp.concatenate([_block_diag(jnp.swapaxes(C_re.astype(F32), 1, 2)),
                            _block_diag(jnp.swapaxes(-C_im.astype(F32), 1, 2))], axis=0)
    return jnp.stack([ab_re.reshape(-1), ab_im.reshape(-1)], axis=0), bblk, cblk


R_COS, R_SIN, R_RETNW, R_LB, R_HNW, R_S5D, R_GLUB, R_MNW = [i * W for i in range(8)]
R_AR = 8 * W
R_AI = R_AR + S5N
R_DTB = R_AI + S5N
R_ALOG = R_DTB + 8
R_MD = R_ALOG + 8
N_COLP = R_MD + 8


def _prepare(p, lb_all, prompt_len):
    depth = p["norm_w"].shape[0]
    s5_A, bblk, cblk = jax.vmap(_s5_matrices)(p["s5_A_re"], p["s5_A_im"], p["s5_log_dt"], p["s5_B_re"],
                                              p["s5_B_im"], p["s5_C_re"], p["s5_C_im"])
    cos_p, sin_p = _rope_tables(np.arange(prompt_len, dtype=np.float32))
    rdec, recum, rkdec, retot, log_gamma = _retention_tables()
    cos_s, sin_s = _rope_tables(np.float32(PAST_LEN) + np.arange(1, dtype=np.float32))
    w = dict(
        norm_w=_rows(p["norm_w"]),
        w_in=jnp.pad(p["w_in"].astype(BF16), ((0, 0), (0, 0), (0, P_PAD - P_TOTAL))),
        w_out=p["w_out"].astype(BF16),
        ret_norm_w=_rows(p["ret_norm_w"]),
        lb=_rows(lb_all),
        hgrn_norm_w=_rows(p["hgrn_norm_w"]),
        s5_A=s5_A,
        s5_Bblk=bblk.astype(BF16),
        s5_Cblk=cblk.astype(BF16),
        s5_D=_rows(p["s5_D"]),
        s5_glu_w=p["s5_glu_w"].astype(BF16),
        s5_glu_b=_rows(p["s5_glu_b"]),
        m2_conv_w=p["m2_conv_w"].astype(F32),
        m2_conv_b=_rows(p["m2_conv_b"]),
        m2_dt_bias=_rows(p["m2_dt_bias"], LANE),
        m2_A_log=_rows(jnp.repeat(p["m2_A_log"], HD, axis=1)),
        m2_D=_rows(jnp.repeat(p["m2_D"], HD, axis=1)),
        m2_norm_w=_rows(p["m2_norm_w"]),
        final_norm_w=p["final_norm_w"].astype(F32).reshape(1, D),
    )
    tile2 = lambda t: jnp.asarray(np.broadcast_to(np.tile(t[0], W // LANE), (depth, W)))
    colp = jnp.concatenate(
        [tile2(cos_s), tile2(sin_s), w["ret_norm_w"][:, 0], w["lb"][:, 0], w["hgrn_norm_w"][:, 0],
         w["s5_D"][:, 0], w["s5_glu_b"][:, 0], w["m2_norm_w"][:, 0], s5_A[:, 0], s5_A[:, 1],
         _rows(p["m2_dt_bias"], 8)[:, 0], _rows(p["m2_A_log"], 8)[:, 0], _rows(p["m2_D"], 8)[:, 0]], axis=1)
    w.update(
        colp=jnp.broadcast_to(colp[:, :, None], (depth, N_COLP, LANE)),
        s5_Bblk_cols=jnp.moveaxis(w["s5_Bblk"].reshape(depth, W, NBC, BCW), 2, 1),
        s5_BblkT=jnp.swapaxes(w["s5_Bblk"], 1, 2),
        s5_CblkT=jnp.swapaxes(w["s5_Cblk"], 1, 2),
        s5_glu_wT=jnp.swapaxes(w["s5_glu_w"], 1, 2),
        rgam=jnp.asarray(np.broadcast_to(np.exp(log_gamma)[:, None, None], (NH, 8, LANE)), dtype=F32),
        cos_p=jnp.asarray(cos_p), sin_p=jnp.asarray(sin_p), rdec=jnp.asarray(rdec), recum=jnp.asarray(recum),
        rkdec=jnp.asarray(rkdec), retot=jnp.asarray(retot), sel=_sel_matrix(),
        bones=jnp.asarray(np.kron(np.eye(LANE // HD), np.full((HD, HD), 1.0 / HD)), dtype=BF16),
        hexp=jnp.asarray(np.kron(np.eye(LANE, NH), np.ones((1, HD))), dtype=F32),
    )
    return w


KC = 16
NKC = HD // KC
STEPS = NH * NKC


def _rotary_cols(x, cos, sin_signed):
    half = HD // 2
    parts = []
    for h in range(NH):
        parts += [x[h * HD + half:(h + 1) * HD], x[h * HD:h * HD + half]]
    return x * cos + jnp.concatenate(parts, axis=0) * sin_signed


def _expand_rows(dst, x):
    for c in range(x.shape[0]):
        dst[c] = jnp.broadcast_to(x[c:c + 1, :], (8, LANE))


def _sample_body(n_steps, x_ref, normw_ref, win_ref, wout_ref, fnw_ref, colp_ref, sbt_ref, sct_ref,
                 gluwt_ref, cw_ref, cb_ref, rgam_ref,
                 ret_in, hg_in, m2_in, s5re_in, s5im_in, buf_in,
                 y_ref, ret_out, hg_out, m2_out, s5re_out, s5im_out, buf_out,
                 xs_s, pt_s, vt_s, ot_s, mixt_s, o_s, hp_s,
                 kret_s, qret_s, khg_s, qhg_s, ahg_s, km2_s, qm2_s):
    i = pl.program_id(0)
    r = i % STEPS
    h = r // NKC
    kc = r % NKC

    def cp(r0, n=W):
        return colp_ref[r0:r0 + n, :]

    def pc(c0, w):
        return pt_s[c0:c0 + w, :]

    @pl.when(i == 0)
    def _load():
        xs_s[...] = x_ref[...]

    @pl.when(r == 0)
    def _prep():
        x = xs_s[...]
        hn = x * lax.rsqrt(jnp.mean(x * x, axis=-1, keepdims=True) + EPS) * normw_ref[...]
        proj = jnp.dot(hn.astype(BF16), win_ref[...], preferred_element_type=F32)

        xnew = proj[:, C_XBC:C_XBC + CONV_CH]
        acc = cb_ref[...] + xnew * cw_ref[CONV_K - 1:CONV_K, :]
        for t in range(CONV_K - 1):
            acc = acc + buf_in[t] * cw_ref[t:t + 1, :]
        for t in range(CONV_K - 2):
            buf_out[t] = buf_in[t + 1]
        buf_out[CONV_K - 2] = xnew
        xbc = _silu(acc)

        for t in range(P_PAD // LANE):
            c0 = t * LANE
            if C_XBC <= c0 < C_XBC + CONV_CH:
                tile = xbc[:, c0 - C_XBC:c0 - C_XBC + LANE]
            else:
                tile = proj[:, c0:c0 + LANE]
            pt_s[c0:c0 + LANE, :] = tile.T

        cos, sin = cp(R_COS), cp(R_SIN)
        _expand_rows(kret_s, _rotary_cols(pc(C_RK, W), cos, sin) * (HD ** -0.5))
        _expand_rows(qret_s, _rotary_cols(pc(C_RQ, W), cos, sin))
        vt_s[0] = pc(C_RV, W)

        fr = pc(C_GF, W)
        lb = cp(R_LB)
        logf = _log_sigmoid(fr) + jnp.log(1.0 + lb * jnp.exp(jnp.minimum(-fr, EXP_CLIP)))
        _expand_rows(ahg_s, jnp.exp(logf))
        _expand_rows(khg_s, (1.0 - lb) * jax.nn.sigmoid(-fr))
        _expand_rows(qhg_s, _silu(pc(C_GQ, W)))
        vt_s[1] = pc(C_GI, W)

        dt8 = _softplus(pc(C_DT, 8) + cp(R_DTB, 8))
        adec8 = jnp.exp(dt8 * (-jnp.exp(cp(R_ALOG, 8))))
        for hh in range(NH):
            hp_s[hh] = jnp.broadcast_to(adec8[hh:hh + 1, :], (8, LANE))
            vt_s[2, hh * HD:(hh + 1) * HD, :] = pc(C_XBC + hh * HD, HD) * dt8[hh:hh + 1, :]
        _expand_rows(km2_s, pc(C_XBC + W, 2 * HD))
        _expand_rows(qm2_s, pc(C_XBC + W + 2 * HD, 2 * HD))

        u = pc(C_SU, W)
        bu = jnp.dot(sbt_ref[...], u.astype(BF16), preferred_element_type=F32)
        hr, hi = s5re_in[...], s5im_in[...]
        ar, ai = cp(R_AR, S5N), cp(R_AI, S5N)
        nr = ar * hr - ai * hi + bu[0:S5N]
        ni = ar * hi + ai * hr + bu[S5N:2 * S5N]
        s5re_out[...] = nr
        s5im_out[...] = ni
        hcat = jnp.concatenate([nr, ni], axis=0).astype(BF16)
        sy = jnp.dot(sct_ref[...], hcat, preferred_element_type=F32) + cp(R_S5D) * u
        gy = _gelu_tanh(sy)
        glu = jnp.dot(gluwt_ref[...], gy.astype(BF16), preferred_element_type=F32) + cp(R_GLUB)
        mixt_s[2 * W:3 * W, :] = gy * jax.nn.sigmoid(glu) * _silu(pc(C_SG, W))

    @pl.when(kc == 0)
    def _zero():
        o_s[...] = jnp.zeros(o_s.shape, F32)

    hrow = pl.multiple_of(h * HD, HD)
    cbase = h * HD + kc * KC
    gbase = (h // 2) * HD + kc * KC

    def update(m, st_in, st_out, kx, qx, base, decay):
        v3 = vt_s[m, pl.ds(hrow, HD), :].reshape(HD // 8, 8, LANE)

        def body(kk, o):
            s_new = decay(kk) * st_in[kk].reshape(HD // 8, 8, LANE) + kx[base + kk] * v3
            st_out[kk] = s_new.reshape(HD, LANE)
            return o + qx[base + kk] * s_new
        o_s[m] = lax.fori_loop(0, KC, body, o_s[m], unroll=2)

    gam = rgam_ref[h]
    update(0, ret_in, ret_out, kret_s, qret_s, cbase, lambda kk: gam)
    update(1, hg_in, hg_out, khg_s, qhg_s, cbase, lambda kk: ahg_s[cbase + kk])
    adec = hp_s[h]
    update(2, m2_in, m2_out, km2_s, qm2_s, gbase, lambda kk: adec)

    @pl.when(kc == NKC - 1)
    def _head_done():
        for m in range(3):
            ot_s[m, pl.ds(hrow, HD), :] = o_s[m].reshape(HD, LANE)

    @pl.when(r == STEPS - 1)
    def _finish():
        def head_rms_cols(o):
            parts = []
            for hh in range(NH):
                seg = o[hh * HD:(hh + 1) * HD]
                parts.append(seg * lax.rsqrt(jnp.mean(seg * seg, axis=0, keepdims=True) + EPS))
            return jnp.concatenate(parts, axis=0)

        mixt_s[0:W, :] = head_rms_cols(ot_s[0]) * cp(R_RETNW) * _silu(pc(C_RG, W))
        mixt_s[W:2 * W, :] = head_rms_cols(ot_s[1]) * cp(R_HNW) * _silu(pc(C_GG, W))
        md8 = cp(R_MD, 8)
        ym = jnp.concatenate([ot_s[2, hh * HD:(hh + 1) * HD, :] + md8[hh:hh + 1, :] * pc(C_XBC + hh * HD, HD)
                              for hh in range(NH)], axis=0)
        my = ym * _silu(pc(C_MZ, W))
        mixt_s[3 * W:4 * W, :] = my * lax.rsqrt(jnp.mean(my * my, axis=0, keepdims=True) + EPS) * cp(R_MNW)
        mixed = jnp.concatenate([mixt_s[t * LANE:(t + 1) * LANE, :].T for t in range(D // LANE)], axis=1)
        xo = xs_s[...] + jnp.dot(mixed.astype(BF16), wout_ref[...], preferred_element_type=F32)
        xs_s[...] = xo

        @pl.when(i == n_steps - 1)
        def _final_norm():
            y_ref[...] = xo * lax.rsqrt(jnp.mean(xo * xo, axis=-1, keepdims=True) + EPS) * fnw_ref[...]


def _sample_step(x, w, ret, hg, m2, s5re, s5im, buf):
    depth = ret.shape[0]
    n = x.shape[0]
    n_steps = depth * STEPS
    lay = lambda i: i // STEPS

    def per_layer(a):
        nd = a.ndim
        return pl.BlockSpec((None,) + a.shape[1:], lambda i: (lay(i),) + (0,) * (nd - 1))

    st_spec = pl.BlockSpec((None, None, KC, HD, LANE),
                           lambda i: (lay(i), (i % STEPS) // NKC, i % NKC, 0, 0))
    weights = [w["norm_w"], w["w_in"], w["w_out"]]
    tables = [w["colp"], w["s5_BblkT"], w["s5_CblkT"], w["s5_glu_wT"], w["m2_conv_w"], w["m2_conv_b"]]
    in_specs = ([_const_spec(x.shape)] + [per_layer(a) for a in weights] + [_const_spec(w["final_norm_w"].shape)]
                + [per_layer(a) for a in tables] + [_const_spec(w["rgam"].shape)]
                + [st_spec, st_spec, st_spec, per_layer(s5re), per_layer(s5im), per_layer(buf)])
    out_shape = [jax.ShapeDtypeStruct((n, D), F32)] + [jax.ShapeDtypeStruct(a.shape, F32)
                                                       for a in (ret, hg, m2, s5re, s5im, buf)]
    out_specs = [_const_spec((n, D)), st_spec, st_spec, st_spec, per_layer(s5re), per_layer(s5im),
                 per_layer(buf)]
    expand = lambda c: pltpu.VMEM((c, 8, LANE), F32)
    scratch = [pltpu.VMEM((n, D), F32),
               pltpu.VMEM((P_PAD, LANE), F32),
               pltpu.VMEM((3, W, LANE), F32),
               pltpu.VMEM((3, W, LANE), F32),
               pltpu.VMEM((D, LANE), F32),
               pltpu.VMEM((3, HD // 8, 8, LANE), F32),
               pltpu.VMEM((NH, 8, LANE), F32),
               expand(W), expand(W), expand(W), expand(W), expand(W), expand(2 * HD), expand(2 * HD)]
    return pl.pallas_call(
        functools.partial(_sample_body, n_steps),
        grid=(n_steps,),
        in_specs=in_specs,
        out_specs=out_specs,
        out_shape=out_shape,
        scratch_shapes=scratch,
        compiler_params=pltpu.CompilerParams(dimension_semantics=("arbitrary",),
                                             vmem_limit_bytes=VMEM_LIMIT),
        name="sample_step",
    )(x, *weights, w["final_norm_w"], *tables, w["rgam"], ret, hg, m2, s5re, s5im, buf)


def kernel(x_prompt, x_sample, state_ret, state_hgrn, state_s5_re, state_s5_im, state_m2_ssm,
           state_m2_conv, norm_w, w_in, ret_norm_w, hgrn_lb_logits, hgrn_norm_w, s5_A_re, s5_A_im,
           s5_log_dt, s5_B_re, s5_B_im, s5_C_re, s5_C_im, s5_D, s5_glu_w, s5_glu_b, m2_conv_w,
           m2_conv_b, m2_dt_bias, m2_A_log, m2_D, m2_norm_w, w_out, final_norm_w):
    p = dict(norm_w=norm_w, w_in=w_in, ret_norm_w=ret_norm_w, hgrn_norm_w=hgrn_norm_w,
             s5_A_re=s5_A_re, s5_A_im=s5_A_im, s5_log_dt=s5_log_dt, s5_B_re=s5_B_re, s5_B_im=s5_B_im,
             s5_C_re=s5_C_re, s5_C_im=s5_C_im, s5_D=s5_D, s5_glu_w=s5_glu_w, s5_glu_b=s5_glu_b,
             m2_conv_w=m2_conv_w, m2_conv_b=m2_conv_b, m2_dt_bias=m2_dt_bias, m2_A_log=m2_A_log,
             m2_D=m2_D, m2_norm_w=m2_norm_w, w_out=w_out, final_norm_w=final_norm_w)
    depth = norm_w.shape[0]
    nbp, lp, _ = x_prompt.shape
    nbs = x_sample.shape[0]

    lb_sm = jax.nn.softmax(hgrn_lb_logits.astype(F32), axis=0)
    lb_all = jnp.clip(jnp.cumsum(lb_sm, axis=0) - lb_sm[0], 0.0, 1.0)

    w = _prepare(p, lb_all, lp)

    xp = x_prompt
    pst = []
    for l in range(depth):
        outs = _prompt_layer(xp, l, l == depth - 1, w)
        xp = outs[0]
        ret, hg, s5, m2, buf = outs[-5:]
        pst.append((ret, hg, s5[:, :S5N].reshape(nbp, S5G, S5P), s5[:, S5N:].reshape(nbp, S5G, S5P),
                    m2, buf))
    yp = xp

    seq_last = lambda a: jnp.moveaxis(a.astype(F32), 1, -1)
    ys, ret, hg, m2, s5re, s5im, buf = _sample_step(
        x_sample.reshape(nbs, D), w,
        seq_last(state_ret), seq_last(state_hgrn), seq_last(state_m2_ssm),
        seq_last(state_s5_re).reshape(depth, S5N, nbs), seq_last(state_s5_im).reshape(depth, S5N, nbs),
        jnp.swapaxes(state_m2_conv.astype(F32), 1, 2))
    seq_second = lambda a: jnp.moveaxis(a, -1, 1)

    stk = lambda i: jnp.stack([s[i] for s in pst], axis=0)
    return (yp, ys.reshape(nbs, 1, D),
            stk(0), stk(1), stk(2), stk(3), stk(4), stk(5),
            seq_second(ret), seq_second(hg), seq_second(s5re.reshape(depth, S5G, S5P, nbs)),
            seq_second(s5im.reshape(depth, S5G, S5P, nbs)), seq_second(m2), jnp.swapaxes(buf, 1, 2))
```
